```python
import jax, jax.numpy as jnp
from jax import lax
import numpy as np

D_MODEL = 1024
BATCH = 32
SEQ = 256
DEPTH = 2
DEC_BATCH = 2
DEC_SEQ = 4096
PAST_LEN = 512

GRID_W = 64
HEAD_DIM = 64
A_HEADS = 8
A_KV_HEADS = 2
WINDOW = 128
ATTN_BLOCK = 128
B_HEADS = 4
CONV_W = 5
GDN_CHUNK = 64
C_HEADS = 4
HGRN_CHUNK = 64
D_A = A_HEADS * HEAD_DIM
D_KV = A_KV_HEADS * HEAD_DIM
D_B = B_HEADS * HEAD_DIM
D_C = C_HEADS * HEAD_DIM
D_MIX = D_A + D_B + D_C
D_IN = (D_A + 2 * D_KV) + (4 * D_B + 4 * B_HEADS) + 5 * D_C
D_FF = 2816
N_EXPERTS = 8
TOP_K = 2
EXPERT_FF = 1024
ROPE_THETA = 10000.0
NORM_EPS = 1e-5
MASK_VALUE = -1e30
TINY = 1e-30
F32 = jnp.float32

kernel_name = 'hybrid_flow_backbone_step'


def layer_norm(x, g, b):
    xf = x.astype(F32)
    mu = jnp.mean(xf, -1, keepdims=True)
    var = jnp.mean(jnp.square(xf - mu), -1, keepdims=True)
    return ((xf - mu) * lax.rsqrt(var + NORM_EPS) * g.astype(F32) + b.astype(F32)).astype(x.dtype)


def l2_normalize(x):
    return x * lax.rsqrt(jnp.sum(x * x, -1, keepdims=True) + 1e-6)


def gated_rms_readout(o, g, w):
    o = o * lax.rsqrt(jnp.mean(o * o, -1, keepdims=True) + NORM_EPS) * w.astype(F32)
    o = o * jax.nn.silu(g.astype(F32))
    return o.reshape(o.shape[0], o.shape[1], -1)


def modulation(cvec, w_mod, b_mod):
    m = jax.nn.silu(cvec) @ w_mod + b_mod
    return jnp.split(m[..., None, :], 6, axis=-1)


def rope_2d(x):
    T = x.shape[1]
    t = jnp.arange(T)
    row = (t // GRID_W).astype(F32)
    col = (t % GRID_W).astype(F32)
    half = HEAD_DIM // 2
    quarter = half // 2
    inv_freq = ROPE_THETA ** (-jnp.arange(quarter, dtype=F32) * 2.0 / half)

    def rotate(xa, pos):
        ang = pos[:, None] * inv_freq[None, :]
        cos = jnp.cos(ang)[None, :, None, :]
        sin = jnp.sin(ang)[None, :, None, :]
        x1, x2 = xa[..., :quarter], xa[..., quarter:]
        return jnp.concatenate([x1 * cos - x2 * sin, x2 * cos + x1 * sin], -1)

    xf = x.astype(F32)
    return jnp.concatenate([rotate(xf[..., :half], row), rotate(xf[..., half:], col)], -1).astype(x.dtype)


def short_conv(x, w):
    return lax.conv_general_dilated(x, w.astype(x.dtype)[:, None, :], window_strides=(1,),
                                    padding=[(CONV_W // 2, CONV_W // 2)],
                                    dimension_numbers=('NWC', 'WIO', 'NWC'),
                                    feature_group_count=x.shape[-1])


def to_chunks(x, c):
    B, T = x.shape[:2]
    x = x.reshape(B, T // c, c, *x.shape[2:])
    return jnp.moveaxis(x, (1, 3), (0, 2))


def from_chunks(o):
    o = jnp.moveaxis(o, (0, 2), (1, 3))
    return o.reshape(o.shape[0], -1, *o.shape[3:])


def masked_exp(mask, d):
    return jnp.where(mask, jnp.exp(jnp.where(mask, d, 0.0)), 0.0)


def gdn_scan(q, k, v, beta, log_alpha, s0):
    C = GDN_CHUNK
    tri_incl = jnp.tril(jnp.ones((C, C), bool))
    tri_strict = jnp.tril(jnp.ones((C, C), bool), -1)
    eye = jnp.eye(C, dtype=F32)

    def step(S, inp):
        qc, kc, vc, bc, ac = inp
        g = jnp.cumsum(ac, axis=-1)
        decay = masked_exp(tri_incl, g[..., :, None] - g[..., None, :])
        kb = kc * bc[..., None]
        L = jnp.where(tri_strict, jnp.einsum('bhik,bhjk->bhij', kb, kc) * decay, 0.0)
        Tm = lax.linalg.triangular_solve(eye + L, jnp.broadcast_to(eye, L.shape), left_side=True, lower=True)
        u = Tm @ (vc * bc[..., None])
        w = Tm @ (kb * jnp.exp(g)[..., None])
        v_new = u - w @ S
        attn = jnp.einsum('bhik,bhjk->bhij', qc, kc) * decay
        o = (qc * jnp.exp(g)[..., None]) @ S + attn @ v_new
        g_last = g[..., -1]
        S = S * jnp.exp(g_last)[..., None, None] + jnp.einsum(
            'bhck,bhcv->bhkv', kc * jnp.exp(g_last[..., None] - g)[..., None], v_new)
        return S, o

    xs = tuple(to_chunks(t.astype(F32), C) for t in (q, k, v, beta, log_alpha))
    S, o = lax.scan(step, s0.astype(F32), xs)
    return from_chunks(o), S


def hgrn_scan(q, k, v, log_f, s0):
    C = HGRN_CHUNK
    tri = jnp.tril(jnp.ones((C, C), bool))[:, :, None]

    def step(S, inp):
        qc, kc, vc, lf = inp
        b = jnp.cumsum(lf, axis=2)
        dec = masked_exp(tri, b[:, :, :, None, :] - b[:, :, None, :, :])
        A = jnp.einsum('bhik,bhjk,bhijk->bhij', qc, kc, dec)
        o = jnp.einsum('bhik,bhkv->bhiv', qc * jnp.exp(b), S) + A @ vc
        bl = b[:, :, -1]
        S = S * jnp.exp(bl)[..., None] + jnp.einsum('bhck,bhcv->bhkv', kc * jnp.exp(bl[:, :, None] - b), vc)
        return S, o

    xs = tuple(to_chunks(t.astype(F32), C) for t in (q, k, v, log_f))
    S, o = lax.scan(step, s0.astype(F32), xs)
    return from_chunks(o), S


def bidirectional(scan_fn, fwd_args, bwd_args, s0_fwd, s0_bwd):
    o_f, s_f = scan_fn(*fwd_args, s0_fwd)
    o_b, s_b = scan_fn(*[jnp.flip(t, axis=1) for t in bwd_args], s0_bwd)
    return o_f + jnp.flip(o_b, axis=1), jnp.stack([s_f, s_b], axis=1)


def softmax_with_sink(s, sink):
    snk = sink.astype(F32).reshape(A_KV_HEADS, A_HEADS // A_KV_HEADS)[:, :, None, None]
    m = jnp.maximum(jnp.max(s, -1, keepdims=True), snk)
    e = jnp.exp(s - m)
    return e / (jnp.sum(e, -1, keepdims=True) + jnp.exp(snk - m))


def attn_context(q, k, v, sink):
    B, L = q.shape[:2]
    R = A_HEADS // A_KV_HEADS
    nqb = L // ATTN_BLOCK
    qb = jnp.moveaxis(q.reshape(B, nqb, ATTN_BLOCK, A_KV_HEADS, R, HEAD_DIM), 1, 0)
    scale = HEAD_DIM ** -0.5

    def one_block(qblk):
        s = jnp.einsum('bqgrd,bkgd->bgrqk', qblk, k, preferred_element_type=F32) * scale
        p = softmax_with_sink(s, sink)
        return jnp.einsum('bgrqk,bkgd->bqgrd', p.astype(v.dtype), v)

    o = lax.map(one_block, qb)
    return jnp.moveaxis(o, 0, 1).reshape(B, L, D_A)


def attn_latent(q, k, v, ck, cv, sink):
    B, T = q.shape[:2]
    R = A_HEADS // A_KV_HEADS
    nb = T // ATTN_BLOCK
    pad = ((0, 0), (ATTN_BLOCK, ATTN_BLOCK), (0, 0), (0, 0))
    kp = jnp.pad(k, pad).reshape(B, nb + 2, ATTN_BLOCK, A_KV_HEADS, HEAD_DIM)
    vp = jnp.pad(v, pad).reshape(B, nb + 2, ATTN_BLOCK, A_KV_HEADS, HEAD_DIM)
    kband = jnp.concatenate([kp[:, :-2], kp[:, 1:-1], kp[:, 2:]], axis=2)
    vband = jnp.concatenate([vp[:, :-2], vp[:, 1:-1], vp[:, 2:]], axis=2)
    qb = q.reshape(B, nb, ATTN_BLOCK, A_KV_HEADS, R, HEAD_DIM)
    scale = HEAD_DIM ** -0.5
    s_loc = jnp.einsum('bnqgrd,bnkgd->bngrqk', qb, kband, preferred_element_type=F32) * scale
    s_ctx = jnp.einsum('bnqgrd,bkgd->bngrqk', qb, ck, preferred_element_type=F32) * scale
    blk = jnp.arange(nb)[:, None] * ATTN_BLOCK
    qpos = blk + jnp.arange(ATTN_BLOCK)[None, :]
    kpos = blk - ATTN_BLOCK + jnp.arange(3 * ATTN_BLOCK)[None, :]
    valid = ((kpos[:, None, :] >= 0) & (kpos[:, None, :] < T)
             & (jnp.abs(qpos[:, :, None] - kpos[:, None, :]) <= WINDOW))
    s_loc = jnp.where(valid[None, :, None, None], s_loc, MASK_VALUE)
    p = softmax_with_sink(jnp.concatenate([s_loc, s_ctx], -1), sink).astype(v.dtype)
    nk = 3 * ATTN_BLOCK
    o = (jnp.einsum('bngrqk,bnkgd->bnqgrd', p[..., :nk], vband)
         + jnp.einsum('bngrqk,bkgd->bnqgrd', p[..., nk:], cv))
    return o.reshape(B, T, D_A)


def prepare(h, w_in, conv_w, a_log, dt_bias, lb):
    B, T, _ = h.shape
    sizes = [D_A, D_KV, D_KV, 3 * D_B, D_B, 2 * B_HEADS, 2 * B_HEADS, D_C, D_C, D_C, D_C, D_C]
    bounds, acc = [], 0
    for s in sizes[:-1]:
        acc += s
        bounds.append(acc)
    qa, ka, va, qkv_b, g_b, beta_b, alpha_b, f_fw, f_bw, i_c, q_c, g_c = jnp.split(h @ w_in, bounds, axis=-1)

    def heads(t, n):
        return t.reshape(B, T, n, HEAD_DIM)

    attn = (heads(qa, A_HEADS), heads(ka, A_KV_HEADS), heads(va, A_KV_HEADS))
    qkv_b = jax.nn.silu(short_conv(qkv_b, conv_w))
    q_b, k_b, v_b = [heads(t, B_HEADS).astype(F32) for t in jnp.split(qkv_b, 3, axis=-1)]
    q_b = l2_normalize(q_b) * HEAD_DIM ** -0.5
    k_b = l2_normalize(k_b)
    beta = jax.nn.sigmoid(beta_b.astype(F32)).reshape(B, T, 2, B_HEADS)
    log_alpha = -jnp.exp(a_log.astype(F32)) * jax.nn.softplus(
        alpha_b.astype(F32).reshape(B, T, 2, B_HEADS) + dt_bias.astype(F32))
    gdn = (q_b, k_b, v_b, beta, log_alpha, heads(g_b, B_HEADS))
    z = jnp.stack([f_fw, f_bw], axis=2).astype(F32)
    f = lb + (1.0 - lb) * jax.nn.sigmoid(z)
    log_f = jnp.log(jnp.maximum(f, TINY)).reshape(B, T, 2, C_HEADS, HEAD_DIM)
    k_c = (1.0 - f).reshape(B, T, 2, C_HEADS, HEAD_DIM)
    hgrn = (heads(q_c, C_HEADS).astype(F32), k_c, heads(i_c, C_HEADS).astype(F32), log_f, heads(g_c, C_HEADS))
    return attn, gdn, hgrn


def recurrent_groups(gdn, hgrn, s_gdn, s_hgrn, gdn_norm_w, hgrn_norm_w, dtype):
    q_b, k_b, v_b, beta, log_alpha, g_b = gdn
    o_b, st_b = bidirectional(gdn_scan, (q_b, k_b, v_b, beta[:, :, 0], log_alpha[:, :, 0]),
                              (q_b, k_b, v_b, beta[:, :, 1], log_alpha[:, :, 1]), s_gdn[:, 0], s_gdn[:, 1])
    q_c, k_c, i_c, log_f, g_c = hgrn
    o_c, st_c = bidirectional(hgrn_scan, (q_c, k_c[:, :, 0], i_c, log_f[:, :, 0]),
                              (q_c, k_c[:, :, 1], i_c, log_f[:, :, 1]), s_hgrn[:, 0], s_hgrn[:, 1])
    out_b = gated_rms_readout(o_b, g_b, gdn_norm_w).astype(dtype)
    out_c = gated_rms_readout(o_c, g_c, hgrn_norm_w).astype(dtype)
    return out_b, out_c, st_b, st_c


def mixer_context(h, w_in, conv_w, a_log, dt_bias, lb, sink, gdn_norm_w, hgrn_norm_w, w_out):
    (qa, ka, va), gdn, hgrn = prepare(h, w_in, conv_w, a_log, dt_bias, lb)
    B = h.shape[0]
    o_a = attn_context(qa, ka, va, sink)
    z_b = jnp.zeros((B, 2, B_HEADS, HEAD_DIM, HEAD_DIM), F32)
    z_c = jnp.zeros((B, 2, C_HEADS, HEAD_DIM, HEAD_DIM), F32)
    o_b, o_c, st_b, st_c = recurrent_groups(gdn, hgrn, z_b, z_c, gdn_norm_w, hgrn_norm_w, h.dtype)
    y = jnp.concatenate([o_a, o_b, o_c], -1) @ w_out
    return y, ka, va, st_b, st_c


def mixer_latent(h, ck, cv, s_gdn, s_hgrn, w_in, conv_w, a_log, dt_bias, lb, sink, gdn_norm_w, hgrn_norm_w, w_out):
    (qa, ka, va), gdn, hgrn = prepare(h, w_in, conv_w, a_log, dt_bias, lb)
    o_a = attn_latent(rope_2d(qa), rope_2d(ka), va, ck, cv, sink)
    o_b, o_c, _, _ = recurrent_groups(gdn, hgrn, s_gdn, s_hgrn, gdn_norm_w, hgrn_norm_w, h.dtype)
    return jnp.concatenate([o_a, o_b, o_c], -1) @ w_out


def swiglu(h, w1, w3, w2):
    return (jax.nn.silu(h @ w1) * (h @ w3)) @ w2


def moe(h, w_r, b_r, w1, w3, w2):
    logits = (h @ w_r).astype(F32) + b_r.astype(F32)
    top_v, top_i = lax.top_k(logits, TOP_K)
    gates = jax.nn.softmax(top_v, axis=-1)
    combine = jnp.sum(jax.nn.one_hot(top_i, N_EXPERTS, dtype=F32) * gates[..., None], axis=-2).astype(h.dtype)
    y = jnp.zeros_like(h)
    for e in range(N_EXPERTS):
        y = y + combine[..., e:e + 1] * swiglu(h, w1[e], w3[e], w2[e])
    return y


def channel_mixer(l, h, ffn_w1, ffn_w3, ffn_w2, moe_router, moe_router_b, moe_w1, moe_w3, moe_w2):
    i = l // 2
    if l % 2 == 0:
        return swiglu(h, ffn_w1[i], ffn_w3[i], ffn_w2[i])
    return moe(h, moe_router[i], moe_router_b[i], moe_w1[i], moe_w3[i], moe_w2[i])


def hgrn_lower_bounds(lb_param):
    s = jax.nn.softmax(lb_param.astype(F32), axis=0)
    return jnp.cumsum(s, axis=0) - s[0:1]


def setup_inputs(seed: int = 0) -> dict:
    key = jax.random.key(seed)
    ks = jax.random.split(key, 32)

    def nrm(k, shape, s):
        return jax.random.normal(k, shape, F32) * s

    deep = (8.0 * DEPTH) ** -0.25
    n_dense = (DEPTH + 1) // 2
    n_moe = DEPTH // 2
    dt = jnp.exp(jax.random.uniform(ks[10], (DEPTH, 2, B_HEADS), F32, np.log(1e-3), np.log(1e-1)))
    return {
        'x_prompt': nrm(ks[0], (BATCH, SEQ, D_MODEL), 1.0),
        'x_sample': nrm(ks[1], (DEC_BATCH, DEC_SEQ, D_MODEL), 1.0),
        'cache_k': nrm(ks[2], (DEC_BATCH, DEPTH, PAST_LEN, A_KV_HEADS, HEAD_DIM), 1.0),
        'cache_v': nrm(ks[3], (DEC_BATCH, DEPTH, PAST_LEN, A_KV_HEADS, HEAD_DIM), 1.0),
        'state_gdn': nrm(ks[4], (DEC_BATCH, DEPTH, 2, B_HEADS, HEAD_DIM, HEAD_DIM), 0.1),
        'state_hgrn': nrm(ks[5], (DEC_BATCH, DEPTH, 2, C_HEADS, HEAD_DIM, HEAD_DIM), 0.5),
        'c': nrm(ks[6], (DEC_BATCH, D_MODEL), 1.0),
        'c_ctx': nrm(ks[7], (D_MODEL,), 1.0),
        'w_mod': nrm(ks[8], (DEPTH, D_MODEL, 6 * D_MODEL), 0.5 * D_MODEL ** -0.5),
        'b_mod': nrm(ks[9], (DEPTH, 6 * D_MODEL), 0.02),
        'w_in': nrm(ks[11], (DEPTH, D_MODEL, D_IN), D_MODEL ** -0.5),
        'conv_w': nrm(ks[12], (DEPTH, CONV_W, 3 * D_B), CONV_W ** -0.5),
        'attn_sink': nrm(ks[13], (DEPTH, A_HEADS), 0.5),
        'gdn_a_log': jnp.log(jax.random.uniform(ks[14], (DEPTH, 2, B_HEADS), F32, 1.0, 16.0)),
        'gdn_dt_bias': dt + jnp.log(-jnp.expm1(-dt)),
        'gdn_norm_w': 1.0 + nrm(ks[15], (DEPTH, HEAD_DIM), 0.02),
        'hgrn_lb': nrm(ks[16], (DEPTH, 2, D_C), 1.0),
        'hgrn_norm_w': 1.0 + nrm(ks[17], (DEPTH, HEAD_DIM), 0.02),
        'w_out': nrm(ks[18], (DEPTH, D_MIX, D_MODEL), deep * D_MIX ** -0.5),
        'ln_g': 1.0 + nrm(ks[19], (DEPTH, 2, D_MODEL), 0.02),
        'ln_b': nrm(ks[20], (DEPTH, 2, D_MODEL), 0.02),
        'ffn_w1': nrm(ks[21], (n_dense, D_MODEL, D_FF), D_MODEL ** -0.5),
        'ffn_w3': nrm(ks[22], (n_dense, D_MODEL, D_FF), D_MODEL ** -0.5),
        'ffn_w2': nrm(ks[23], (n_dense, D_FF, D_MODEL), deep * D_FF ** -0.5),
        'moe_router': nrm(ks[24], (n_moe, D_MODEL, N_EXPERTS), D_MODEL ** -0.5),
        'moe_router_b': nrm(ks[25], (n_moe, N_EXPERTS), 0.01),
        'moe_w1': nrm(ks[26], (n_moe, N_EXPERTS, D_MODEL, EXPERT_FF), D_MODEL ** -0.5),
        'moe_w3': nrm(ks[27], (n_moe, N_EXPERTS, D_MODEL, EXPERT_FF), D_MODEL ** -0.5),
        'moe_w2': nrm(ks[28], (n_moe, N_EXPERTS, EXPERT_FF, D_MODEL), deep * EXPERT_FF ** -0.5),
    }


def reference(x_prompt, x_sample, cache_k, cache_v, state_gdn, state_hgrn, c, c_ctx,
              w_mod, b_mod, w_in, conv_w, attn_sink, gdn_a_log, gdn_dt_bias, gdn_norm_w,
              hgrn_lb, hgrn_norm_w, w_out, ln_g, ln_b, ffn_w1, ffn_w3, ffn_w2,
              moe_router, moe_router_b, moe_w1, moe_w3, moe_w2):
    alpha = (2.0 * DEPTH) ** 0.25
    lower_bounds = hgrn_lower_bounds(hgrn_lb)
    xp, xs = x_prompt, x_sample
    new_k, new_v, new_sg, new_sh = [], [], [], []
    for l in range(DEPTH):
        mix_w = (w_in[l], conv_w[l], gdn_a_log[l], gdn_dt_bias[l], lower_bounds[l], attn_sink[l],
                 gdn_norm_w[l], hgrn_norm_w[l], w_out[l])
        ffn_w = (ffn_w1, ffn_w3, ffn_w2, moe_router, moe_router_b, moe_w1, moe_w3, moe_w2)
        sh1, sc1, g1, sh2, sc2, g2 = modulation(c_ctx, w_mod[l], b_mod[l])
        mix, k_l, v_l, s_b, s_c = mixer_context(xp * (1 + sc1) + sh1, *mix_w)
        xp = layer_norm(alpha * xp + g1 * mix, ln_g[l, 0], ln_b[l, 0])
        xp = layer_norm(alpha * xp + g2 * channel_mixer(l, xp * (1 + sc2) + sh2, *ffn_w), ln_g[l, 1], ln_b[l, 1])
        new_k.append(k_l)
        new_v.append(v_l)
        new_sg.append(s_b.astype(xp.dtype))
        new_sh.append(s_c.astype(xp.dtype))
        sh1, sc1, g1, sh2, sc2, g2 = modulation(c, w_mod[l], b_mod[l])
        mix = mixer_latent(xs * (1 + sc1) + sh1, cache_k[:, l], cache_v[:, l], state_gdn[:, l], state_hgrn[:, l], *mix_w)
        xs = layer_norm(alpha * xs + g1 * mix, ln_g[l, 0], ln_b[l, 0])
        xs = layer_norm(alpha * xs + g2 * channel_mixer(l, xs * (1 + sc2) + sh2, *ffn_w), ln_g[l, 1], ln_b[l, 1])
    new_cache_k = jnp.stack(new_k, axis=1)
    new_cache_v = jnp.stack(new_v, axis=1)
    new_state_gdn = jnp.stack(new_sg, axis=1)
    new_state_hgrn = jnp.stack(new_sh, axis=1)
    return (xp, xs, new_cache_k, new_cache_v, new_state_gdn, new_state_hgrn)
```

```python
import functools

import numpy as np
import jax
import jax.numpy as jnp
from jax import lax
from jax.experimental import pallas as pl
from jax.experimental.pallas import tpu as pltpu

F32 = jnp.float32
BF16 = jnp.bfloat16

D_MODEL = 1024
HEAD_DIM = 64
A_HEADS = 8
A_KV_HEADS = 2
GRID_W = 64
ATTN_BLOCK = 128
N_HEADS_REC = 4
D_REC = N_HEADS_REC * HEAD_DIM
CONV_W = 5
CHUNK = 64
SUB = 16
D_A = A_HEADS * HEAD_DIM
D_KV = A_KV_HEADS * HEAD_DIM
D_FF = 2816
N_EXPERTS = 8
EXPERT_FF = 1024
ROPE_THETA = 10000.0
NORM_EPS = 1e-5
TINY = 1e-30
NEG_BIG = -1e30

C_QA, C_KA, C_VA, C_QKVB, C_GB, C_HG, C_BA, C_END = 0, 512, 640, 768, 1536, 1792, 3072, 3200
D_HG = 1280
HALO = 8

TM = 256
TM_FFN = 512
FF_CHUNK = 256
TM_MOE = 1024
VMEM_LIMIT = 56 * 1024 * 1024

NN = (((1,), (0,)), ((), ()))
NT = (((1,), (1,)), ((), ()))
TN = (((0,), (0,)), ((), ()))


def _dot(a, b, dims=NN):
    return lax.dot_general(a.astype(BF16), b.astype(BF16), dims, preferred_element_type=F32)


def _split2(x):
    hi = x.astype(BF16)
    lo = (x - hi.astype(F32)).astype(BF16)
    return hi, lo


def _split3(x):
    hi = x.astype(BF16)
    r = x - hi.astype(F32)
    mid = r.astype(BF16)
    lo = (r - mid.astype(F32)).astype(BF16)
    return hi, mid, lo


def _dot_sel(sel, x, dims=NN, sel_is_lhs=True):
    out = None
    for part in _split3(x):
        a, b = (sel, part) if sel_is_lhs else (part, sel)
        t = lax.dot_general(a, b, dims, preferred_element_type=F32)
        out = t if out is None else out + t
    return out


def _dot3(a, b_hi, b_lo, dims=NN):
    a_hi, a_lo = _split2(a)
    out = lax.dot_general(a_hi, b_hi, dims, preferred_element_type=F32)
    out = out + lax.dot_general(a_lo, b_hi, dims, preferred_element_type=F32)
    return out + lax.dot_general(a_hi, b_lo, dims, preferred_element_type=F32)


def _sigmoid(x):
    return 1.0 / (1.0 + jnp.exp(-x))


def _silu(x):
    return x * _sigmoid(x)


def _layer_norm(x, g, b):
    mu = jnp.mean(x, -1, keepdims=True)
    xc = x - mu
    var = jnp.mean(xc * xc, -1, keepdims=True)
    return xc * lax.rsqrt(var + NORM_EPS) * g + b


def _iota(shape, dim):
    return lax.broadcasted_iota(jnp.int32, shape, dim)


def _head_ones():
    return jnp.where(_iota((D_REC, D_REC), 0) // HEAD_DIM == _iota((D_REC, D_REC), 1) // HEAD_DIM, 1.0, 0.0).astype(BF16)


def _block_diag(x):
    head = _iota(x.shape, 1) // HEAD_DIM
    return jnp.concatenate([jnp.where(head == h, x, jnp.zeros_like(x)) for h in range(N_HEADS_REC)], axis=0)


def _block_diag_mask():
    return _iota((D_REC, D_REC), 0) // HEAD_DIM == _iota((D_REC, D_REC), 1) // HEAD_DIM


def _collapse_block_diag(s):
    out = s[0:HEAD_DIM]
    for h in range(1, N_HEADS_REC):
        out = out + s[h * HEAD_DIM:(h + 1) * HEAD_DIM]
    return out


def _mod_kernel(ct_ref, w_ref, b_ref, o_ref):
    s = _silu(ct_ref[...])
    w = w_ref[0]
    rows = [jnp.sum(s[:, r:r + 1] * w, axis=0, keepdims=True) for r in range(o_ref.shape[1])]
    o_ref[0] = jnp.concatenate(rows, axis=0) + b_ref[0]


def _modulation(cvecs, w_mod, b_mod):
    depth, d, n6 = w_mod.shape
    r = cvecs.shape[0]
    tn = 1024
    ct = jnp.zeros((d, 8), F32).at[:, :r].set(cvecs.T)
    return pl.pallas_call(
        _mod_kernel,
        out_shape=jax.ShapeDtypeStruct((depth, r, n6), F32),
        grid=(depth, n6 // tn),
        in_specs=[pl.BlockSpec((d, 8), lambda l, j: (0, 0)),
                  pl.BlockSpec((1, d, tn), lambda l, j: (l, 0, j)),
                  pl.BlockSpec((1, 1, tn), lambda l, j: (l, 0, j))],
        out_specs=pl.BlockSpec((1, r, tn), lambda l, j: (l, 0, j)),
        compiler_params=pltpu.CompilerParams(dimension_semantics=("arbitrary", "arbitrary"),
                                             vmem_limit_bytes=VMEM_LIMIT),
        name="modulation",
    )(ct, w_mod, b_mod.reshape(depth, 1, n6))


def _rope(x, cos, sin):
    outs = []
    lane = _iota(cos.shape, 1)
    first = (lane % 32) < 16
    for m in range(x.shape[1] // 128):
        xs = x[:, m * 128:(m + 1) * 128]
        swapped = jnp.where(first, pltpu.roll(xs, 128 - 16, 1), pltpu.roll(xs, 16, 1))
        outs.append(xs * cos + swapped * sin)
    return outs[0] if len(outs) == 1 else jnp.concatenate(outs, axis=1)


def _in_proj_kernel(x_ref, mod_ref, w_ref, cos_ref, sin_ref,
                    qa_ref, ka_ref, va_ref, qkvb_ref, gb_ref, hg_ref, ba_ref, *, n_ctx_tiles):
    i = pl.program_id(0)
    m = mod_ref[0]
    h = (x_ref[...] * (1.0 + m[1:2]) + m[0:1]).astype(BF16)

    def proj(lo, hi):
        return jnp.dot(h, w_ref[:, lo:hi], preferred_element_type=F32)

    qa = proj(C_QA, C_KA)
    ka = proj(C_KA, C_VA)

    @pl.when(i < n_ctx_tiles)
    def _():
        qa_ref[...] = qa
        ka_ref[...] = ka

    @pl.when(i >= n_ctx_tiles)
    def _():
        cos, sin = cos_ref[...], sin_ref[...]
        qa_ref[...] = _rope(qa, cos, sin)
        ka_ref[...] = _rope(ka, cos, sin)

    va_ref[...] = proj(C_VA, C_QKVB)
    qkvb_ref[...] = proj(C_QKVB, C_GB)
    gb_ref[...] = proj(C_GB, C_HG)
    hg_ref[...] = proj(C_HG, C_BA)
    ba_ref[...] = proj(C_BA, C_END)


def _in_proj(x, mod_l, w_in_r, cos_tab, sin_tab, mod_index, n_ctx_tiles, lat_tiles_per_seq):
    n = x.shape[0]

    def tab_index(i):
        return (jnp.where(i < n_ctx_tiles, 0, (i - n_ctx_tiles) % lat_tiles_per_seq), 0)

    def out(width):
        return jax.ShapeDtypeStruct((n, width), F32), pl.BlockSpec((TM, width), lambda i: (i, 0))

    outs = [out(D_A), out(D_KV), out(D_KV), out(3 * D_REC), out(D_REC), out(D_HG), out(128)]
    return pl.pallas_call(
        functools.partial(_in_proj_kernel, n_ctx_tiles=n_ctx_tiles),
        out_shape=[o[0] for o in outs],
        grid=(n // TM,),
        in_specs=[pl.BlockSpec((TM, D_MODEL), lambda i: (i, 0)),
                  pl.BlockSpec((1, 6, D_MODEL), lambda i: (mod_index(i), 0, 0)),
                  pl.BlockSpec((D_MODEL, C_END), lambda i: (0, 0)),
                  pl.BlockSpec((TM, 128), tab_index),
                  pl.BlockSpec((TM, 128), tab_index)],
        out_specs=[o[1] for o in outs],
        compiler_params=pltpu.CompilerParams(dimension_semantics=("arbitrary",), vmem_limit_bytes=VMEM_LIMIT),
        name="in_proj",
    )(x, mod_l, w_in_r, cos_tab, sin_tab)


def _attn_heads(q_ref, k, v, bias, sink_ref, o_ref):
    k_same, k_swap = k.astype(BF16), pltpu.roll(k, HEAD_DIM, 1).astype(BF16)
    v_same, v_swap = v.astype(BF16), pltpu.roll(v, HEAD_DIM, 1).astype(BF16)
    nq = q_ref.shape[0]
    low_half = _iota((nq, 128), 1) < HEAD_DIM
    rep = A_HEADS // A_KV_HEADS
    for p in range(A_HEADS // 2):
        q2 = q_ref[:, p * 128:(p + 1) * 128]
        halves = []
        for e in range(2):
            hq = 2 * p + e
            g = hq // rep
            qm = jnp.where(low_half if e == 0 else jnp.logical_not(low_half), q2, 0.0)
            s = _dot(qm, k_same if e == g else k_swap, NT) * (HEAD_DIM ** -0.5)
            if bias is not None:
                s = s + bias
            snk = sink_ref[hq]
            mx = jnp.maximum(jnp.max(s, -1, keepdims=True), snk)
            ex = jnp.exp(s - mx)
            den = jnp.sum(ex, -1, keepdims=True) + jnp.exp(snk - mx)
            halves.append(_dot(ex, v_same if e == g else v_swap) / den)
        o_ref[:, p * 128:(p + 1) * 128] = jnp.where(low_half, halves[0], halves[1])


def _attn_ctx_kernel(q_ref, k_ref, v_ref, sink_ref, o_ref):
    _attn_heads(q_ref, k_ref[...], v_ref[...], None, sink_ref, o_ref)


def _attn_lat_kernel(q_ref, kp_ref, kc_ref, kn_ref, vp_ref, vc_ref, vn_ref, ck_ref, cv_ref, sink_ref, o_ref, *, n_blocks):
    n = pl.program_id(1)
    k = jnp.concatenate([kp_ref[...], kc_ref[...], kn_ref[...], ck_ref[0]], axis=0)
    v = jnp.concatenate([vp_ref[...], vc_ref[...], vn_ref[...], cv_ref[0]], axis=0)
    nk = k.shape[0]
    qi = _iota((ATTN_BLOCK, nk), 0)
    kj = _iota((ATTN_BLOCK, nk), 1)
    seg = kj // ATTN_BLOCK
    jj = kj % ATTN_BLOCK
    bias_prev = jnp.where(jj >= qi, 0.0, NEG_BIG) + jnp.where(n > 0, 0.0, NEG_BIG)
    bias_next = jnp.where(jj <= qi, 0.0, NEG_BIG) + jnp.where(n < n_blocks - 1, 0.0, NEG_BIG)
    bias = jnp.where(seg == 0, bias_prev, jnp.where(seg == 2, bias_next, 0.0))
    _attn_heads(q_ref, k, v, bias, sink_ref, o_ref)


def _attention(qa, ka, va, ck, cv, sink, n_ctx, ctx_len, lat_batch, lat_len):
    n = qa.shape[0]
    smem = pl.BlockSpec(memory_space=pltpu.SMEM)
    params = pltpu.CompilerParams(dimension_semantics=("arbitrary",), vmem_limit_bytes=VMEM_LIMIT)
    o_ctx = pl.pallas_call(
        _attn_ctx_kernel,
        out_shape=jax.ShapeDtypeStruct((n_ctx * ctx_len, D_A), F32),
        grid=(n_ctx,),
        in_specs=[pl.BlockSpec((ctx_len, D_A), lambda b: (b, 0)),
                  pl.BlockSpec((ctx_len, D_KV), lambda b: (b, 0)),
                  pl.BlockSpec((ctx_len, D_KV), lambda b: (b, 0)),
                  smem],
        out_specs=pl.BlockSpec((ctx_len, D_A), lambda b: (b, 0)),
        compiler_params=params,
        name="attn_context",
    )(qa, ka, va, sink)

    nb = lat_len // ATTN_BLOCK
    base = n_ctx * ctx_len // ATTN_BLOCK
    past = ck.shape[1]

    def cur(b, j):
        return (base + b * nb + j, 0)

    def prev(b, j):
        return (base + b * nb + jnp.maximum(j - 1, 0), 0)

    def nxt(b, j):
        return (base + b * nb + jnp.minimum(j + 1, nb - 1), 0)

    kv = lambda f: pl.BlockSpec((ATTN_BLOCK, D_KV), f)
    o_lat = pl.pallas_call(
        functools.partial(_attn_lat_kernel, n_blocks=nb),
        out_shape=jax.ShapeDtypeStruct((lat_batch * lat_len, D_A), F32),
        grid=(lat_batch, nb),
        in_specs=[pl.BlockSpec((ATTN_BLOCK, D_A), cur),
                  kv(prev), kv(cur), kv(nxt), kv(prev), kv(cur), kv(nxt),
                  pl.BlockSpec((1, past, D_KV), lambda b, j: (b, 0, 0)),
                  pl.BlockSpec((1, past, D_KV), lambda b, j: (b, 0, 0)),
                  smem],
        out_specs=pl.BlockSpec((ATTN_BLOCK, D_A), lambda b, j: (b * nb + j, 0)),
        compiler_params=pltpu.CompilerParams(dimension_semantics=("arbitrary", "arbitrary"),
                                             vmem_limit_bytes=VMEM_LIMIT),
        name="attn_latent",
    )(qa, ka, ka, ka, va, va, va, ck, cv, sink)
    return jnp.concatenate([o_ctx, o_lat], axis=0)


def _tri(reverse, strict=False):
    i, j = _iota((CHUNK, CHUNK), 0), _iota((CHUNK, CHUNK), 1)
    if strict:
        return (i < j) if reverse else (i > j)
    return (i <= j) if reverse else (i >= j)


def _pair_masks(reverse):
    i = _iota((CHUNK, D_REC), 0)
    j = _iota((CHUNK, D_REC), 1) % HEAD_DIM
    if reverse:
        return i <= j, i < j, i == j
    return i >= j, i > j, i == j


def _gdn_kernel(cur_ref, prev_ref, next_ref, ba_ref, convw_ref, ebeta_ref, ealpha_ref, alog_ref, dtb_ref, s0_ref,
                o_ref, sfin_ref, s_scr, *, reverse, n_chunks):
    step = pl.program_id(1)
    pos = (n_chunks - 1 - step) if reverse else step

    @pl.when(step == 0)
    def _():
        s_scr[...] = _block_diag(s0_ref[0])

    has_prev = (pos > 0).astype(F32)
    has_next = (pos < n_chunks - 1).astype(F32)
    xh = jnp.concatenate([prev_ref[...] * has_prev, cur_ref[...], next_ref[...] * has_next], axis=0)
    cw = convw_ref[...]
    y = None
    for t in range(CONV_W):
        lo = HALO - CONV_W // 2 + t
        term = xh[lo:lo + CHUNK] * cw[t:t + 1]
        y = term if y is None else y + term
    y = _silu(y)
    q, k, v = y[:, :D_REC], y[:, D_REC:2 * D_REC], y[:, 2 * D_REC:]
    ones = _head_ones()
    q = q * lax.rsqrt(_dot_sel(ones, q * q, sel_is_lhs=False) + 1e-6) * (HEAD_DIM ** -0.5)
    k = k * lax.rsqrt(_dot_sel(ones, k * k, sel_is_lhs=False) + 1e-6)

    ba = ba_ref[...]
    beta = _sigmoid(_dot_sel(ebeta_ref[...], ba, sel_is_lhs=False))
    za = _dot_sel(ealpha_ref[...], ba, sel_is_lhs=False) + dtb_ref[...]
    softplus = jnp.maximum(za, 0.0) + jnp.log(1.0 + jnp.exp(-jnp.abs(za)))
    la = -jnp.exp(alog_ref[...]) * softplus

    incl, strict, diag = _pair_masks(reverse)
    tri = jnp.where(_tri(reverse), 1.0, 0.0).astype(BF16)
    g_i = _dot_sel(tri, la)
    t_row, j_col = _iota((CHUNK, D_REC), 0), _iota((CHUNK, D_REC), 1) % HEAD_DIM
    t_before_j = (t_row >= j_col) if reverse else (t_row <= j_col)
    g_j = _dot_sel(jnp.ones((CHUNK, CHUNK), BF16), jnp.where(t_before_j, la, 0.0))
    last = 0 if reverse else CHUNK - 1
    g_tot = g_i[last:last + 1]
    decay = jnp.where(incl, jnp.exp(jnp.where(incl, g_i - g_j, 0.0)), 0.0)

    k_bd = _block_diag(k).astype(BF16)
    kk = _dot(k, k_bd, NT)
    qk = _dot(q, k_bd, NT)
    low = jnp.where(strict, beta * kk * decay, 0.0)
    attn = qk * decay

    t_mat = jnp.where(diag, 1.0, 0.0) - low
    power = low
    for _ in range(5):
        p_hi, p_lo = _split2(power)
        p_hi, p_lo = _block_diag(p_hi), _block_diag(p_lo)
        power = _dot3(power, p_hi, p_lo)
        q_hi, q_lo = _split2(power)
        t_mat = t_mat + _dot3(t_mat, _block_diag(q_hi), _block_diag(q_lo))

    eg = jnp.exp(g_i)
    vb = v * beta
    kbg = k * beta * eg
    u = _dot(t_mat, _block_diag(vb))
    w = _dot(t_mat, _block_diag(kbg))

    s = s_scr[...]
    v_new = u - _dot(w, s)
    o_ref[...] = _dot(q * eg, s) + _dot(attn, _block_diag(v_new))
    k_dec = k * jnp.exp(g_tot - g_i)
    s_new = s * jnp.exp(g_tot) + jnp.where(_block_diag_mask(), _dot(k_dec, v_new, TN), 0.0)
    s_scr[...] = s_new

    @pl.when(step == n_chunks - 1)
    def _():
        sfin_ref[0] = _collapse_block_diag(s_new)


def _chunk_specs(width, reverse, n_chunks, chunk_base, n_rows):
    per = CHUNK // HALO

    def chunk(s, t):
        return chunk_base + s * n_chunks + ((n_chunks - 1 - t) if reverse else t)

    cur = pl.BlockSpec((CHUNK, width), lambda s, t: (chunk(s, t), 0))
    prev = pl.BlockSpec((HALO, width), lambda s, t: (jnp.maximum(chunk(s, t) * per - 1, 0), 0))
    nxt = pl.BlockSpec((HALO, width), lambda s, t: (jnp.minimum((chunk(s, t) + 1) * per, n_rows // HALO - 1), 0))
    return cur, prev, nxt


def _gdn_scan(qkvb, ba, conv_w, e_beta, e_alpha, alog_e, dtb_e, s0, reverse, n_seq, n_chunks, chunk_base):
    n_rows = qkvb.shape[0]
    cur, prev, nxt = _chunk_specs(3 * D_REC, reverse, n_chunks, chunk_base, n_rows)
    ba_spec, _, _ = _chunk_specs(128, reverse, n_chunks, chunk_base, n_rows)
    const = lambda shape: pl.BlockSpec(shape, lambda s, t: (0,) * len(shape))

    def out_chunk(s, t):
        return (s * n_chunks + ((n_chunks - 1 - t) if reverse else t), 0)

    return pl.pallas_call(
        functools.partial(_gdn_kernel, reverse=reverse, n_chunks=n_chunks),
        out_shape=[jax.ShapeDtypeStruct((n_seq * n_chunks * CHUNK, D_REC), F32),
                   jax.ShapeDtypeStruct((n_seq, HEAD_DIM, D_REC), F32)],
        grid=(n_seq, n_chunks),
        in_specs=[cur, prev, nxt, ba_spec, const((CONV_W, 3 * D_REC)), const((128, D_REC)), const((128, D_REC)),
                  const((1, D_REC)), const((1, D_REC)),
                  pl.BlockSpec((1, HEAD_DIM, D_REC), lambda s, t: (s, 0, 0))],
        out_specs=[pl.BlockSpec((CHUNK, D_REC), out_chunk),
                   pl.BlockSpec((1, HEAD_DIM, D_REC), lambda s, t: (s, 0, 0))],
        scratch_shapes=[pltpu.VMEM((D_REC, D_REC), F32)],
        compiler_params=pltpu.CompilerParams(dimension_semantics=("arbitrary", "arbitrary"),
                                             vmem_limit_bytes=VMEM_LIMIT),
        name="gdn_scan_bwd" if reverse else "gdn_scan_fwd",
    )(qkvb, qkvb, qkvb, ba, conv_w, e_beta, e_alpha, alog_e, dtb_e, s0)


def _hgrn_kernel(f_ref, i_ref, q_ref, lb_ref, s0_ref, o_ref, sfin_ref, s_scr, *, reverse, n_chunks):
    step = pl.program_id(1)

    @pl.when(step == 0)
    def _():
        s_scr[...] = _block_diag(s0_ref[0])

    lb = lb_ref[...]
    f = lb + (1.0 - lb) * _sigmoid(f_ref[...])
    lf = jnp.log(jnp.maximum(f, TINY))
    kk = 1.0 - f
    q = q_ref[...]
    v = i_ref[...]

    tri = jnp.where(_tri(reverse), 1.0, 0.0).astype(BF16)
    b = _dot_sel(tri, lf)

    row = _iota((CHUNK, D_REC), 0)
    col_tok = _iota((CHUNK, D_REC), 1) % HEAD_DIM
    n_sub = CHUNK // SUB

    ends = []
    for p in range(n_sub):
        r = p * SUB if reverse else p * SUB + SUB - 1
        ends.append(jnp.broadcast_to(b[r:r + 1], (SUB, D_REC)))
    b_end = jnp.concatenate(ends, axis=0)
    k_hat = kk * jnp.exp(b_end - b)
    k_hat_bd = _block_diag(k_hat).astype(BF16)

    blocks = range(1, n_sub) if reverse else range(n_sub - 1)
    lhs = []
    for p in blocks:
        r = p * SUB if reverse else p * SUB + SUB - 1
        after = (row < p * SUB) if reverse else (row >= (p + 1) * SUB)
        lhs.append(jnp.where(after, q * jnp.exp(jnp.where(after, b - b[r:r + 1], 0.0)), 0.0))
    res = _dot(jnp.concatenate(lhs, axis=0), k_hat_bd, NT)
    a_mat = jnp.zeros((CHUNK, D_REC), F32)
    for n, p in enumerate(blocks):
        a_mat = a_mat + jnp.where(col_tok // SUB == p, res[n * CHUNK:(n + 1) * CHUNK], 0.0)

    ones = _head_ones()
    r_in = row % SUB
    for d in range(SUB):
        if d == 0:
            k_s, b_s = kk, b
            ok = r_in >= 0
        elif reverse:
            k_s, b_s = pltpu.roll(kk, CHUNK - d, 0), pltpu.roll(b, CHUNK - d, 0)
            ok = r_in <= SUB - 1 - d
        else:
            k_s, b_s = pltpu.roll(kk, d, 0), pltpu.roll(b, d, 0)
            ok = r_in >= d
        e_d = jnp.where(ok, q * k_s * jnp.exp(jnp.where(ok, b - b_s, 0.0)), 0.0)
        r_d = _dot(e_d, ones)
        partner = (row + d) if reverse else (row - d)
        a_mat = a_mat + jnp.where(jnp.logical_and(col_tok == partner, ok), r_d, 0.0)

    s = s_scr[...]
    o_ref[...] = _dot(q * jnp.exp(b), s, NT) + _dot(a_mat, _block_diag(v))
    last = 0 if reverse else CHUNK - 1
    b_last = b[last:last + 1]
    k_dec = kk * jnp.exp(b_last - b)
    s_new = s * jnp.exp(b_last) + jnp.where(_block_diag_mask(), _dot(v, k_dec, TN), 0.0)
    s_scr[...] = s_new

    @pl.when(step == n_chunks - 1)
    def _():
        sfin_ref[0] = _collapse_block_diag(s_new)


def _hgrn_scan(hg, lb_e, s0, reverse, n_seq, n_chunks, chunk_base):
    def chunk(s, t):
        return chunk_base + s * n_chunks + ((n_chunks - 1 - t) if reverse else t)

    col = lambda c: pl.BlockSpec((CHUNK, D_REC), lambda s, t: (chunk(s, t), c))

    def out_chunk(s, t):
        return (s * n_chunks + ((n_chunks - 1 - t) if reverse else t), 0)

    return pl.pallas_call(
        functools.partial(_hgrn_kernel, reverse=reverse, n_chunks=n_chunks),
        out_shape=[jax.ShapeDtypeStruct((n_seq * n_chunks * CHUNK, D_REC), F32),
                   jax.ShapeDtypeStruct((n_seq, HEAD_DIM, D_REC), F32)],
        grid=(n_seq, n_chunks),
        in_specs=[col(1 if reverse else 0), col(2), col(3),
                  pl.BlockSpec((1, D_REC), lambda s, t: (0, 0)),
                  pl.BlockSpec((1, HEAD_DIM, D_REC), lambda s, t: (s, 0, 0))],
        out_specs=[pl.BlockSpec((CHUNK, D_REC), out_chunk),
                   pl.BlockSpec((1, HEAD_DIM, D_REC), lambda s, t: (s, 0, 0))],
        scratch_shapes=[pltpu.VMEM((D_REC, D_REC), F32)],
        compiler_params=pltpu.CompilerParams(dimension_semantics=("arbitrary", "arbitrary"),
                                             vmem_limit_bytes=VMEM_LIMIT),
        name="hgrn_scan_bwd" if reverse else "hgrn_scan_fwd",
    )(hg, hg, hg, lb_e, s0)


def _gated_readout(o, gate, w_e, ones):
    ms = _dot_sel(ones, o * o, sel_is_lhs=False) * (1.0 / HEAD_DIM)
    return o * lax.rsqrt(ms + NORM_EPS) * w_e * _silu(gate)


def _out_proj_kernel(x_ref, mod_ref, oa_ref, bf_ref, bb_ref, cf_ref, cb_ref, gb_ref, gc_ref, nwb_ref, nwc_ref,
                     w_ref, lng_ref, lnb_ref, y_ref, *, alpha):
    m = mod_ref[0]
    ones = _head_ones()
    o_b = _gated_readout(bf_ref[...] + bb_ref[...], gb_ref[...], nwb_ref[...], ones)
    o_c = _gated_readout(cf_ref[...] + cb_ref[...], gc_ref[...], nwc_ref[...], ones)
    mix = jnp.dot(oa_ref[...].astype(BF16), w_ref[0:D_A], preferred_element_type=F32)
    mix = mix + jnp.dot(o_b.astype(BF16), w_ref[D_A:D_A + D_REC], preferred_element_type=F32)
    mix = mix + jnp.dot(o_c.astype(BF16), w_ref[D_A + D_REC:], preferred_element_type=F32)
    y_ref[...] = _layer_norm(alpha * x_ref[...] + m[2:3] * mix, lng_ref[...], lnb_ref[...])


def _out_proj(x, mod_l, o_a, ob_f, ob_b, oc_f, oc_b, g_b, hg, nw_b, nw_c, w_out, ln_g, ln_b, mod_index, alpha):
    n = x.shape[0]
    tile = lambda w: pl.BlockSpec((TM, w), lambda i: (i, 0))
    row = lambda w: pl.BlockSpec((1, w), lambda i: (0, 0))
    return pl.pallas_call(
        functools.partial(_out_proj_kernel, alpha=alpha),
        out_shape=jax.ShapeDtypeStruct((n, D_MODEL), F32),
        grid=(n // TM,),
        in_specs=[tile(D_MODEL),
                  pl.BlockSpec((1, 6, D_MODEL), lambda i: (mod_index(i), 0, 0)),
                  tile(D_A), tile(D_REC), tile(D_REC), tile(D_REC), tile(D_REC), tile(D_REC),
                  pl.BlockSpec((TM, D_REC), lambda i: (i, 4)),
                  row(D_REC), row(D_REC),
                  pl.BlockSpec((D_MODEL, D_MODEL), lambda i: (0, 0)),
                  row(D_MODEL), row(D_MODEL)],
        out_specs=tile(D_MODEL),
        compiler_params=pltpu.CompilerParams(dimension_semantics=("arbitrary",), vmem_limit_bytes=VMEM_LIMIT),
        name="out_proj_ln",
    )(x, mod_l, o_a, ob_f, ob_b, oc_f, oc_b, g_b, hg, nw_b, nw_c, w_out, ln_g, ln_b)


def _ffn_kernel(x_ref, mod_ref, w1_ref, w3_ref, w2_ref, lng_ref, lnb_ref, y_ref, *, alpha):
    m = mod_ref[0]
    x = x_ref[...]
    h = (x * (1.0 + m[4:5]) + m[3:4]).astype(BF16)
    acc = jnp.zeros((x.shape[0], D_MODEL), F32)
    for c in range(D_FF // FF_CHUNK):
        sl = slice(c * FF_CHUNK, (c + 1) * FF_CHUNK)
        a = jnp.dot(h, w1_ref[:, sl], preferred_element_type=F32)
        g = jnp.dot(h, w3_ref[:, sl], preferred_element_type=F32)
        acc = acc + jnp.dot((_silu(a) * g).astype(BF16), w2_ref[sl, :], preferred_element_type=F32)
    y_ref[...] = _layer_norm(alpha * x + m[5:6] * acc, lng_ref[...], lnb_ref[...])


def _ffn(x, mod_l, w1, w3, w2, ln_g, ln_b, mod_index_ffn, alpha):
    n = x.shape[0]
    whole = lambda a: pl.BlockSpec(a.shape, lambda i: (0, 0), pipeline_mode=pl.Buffered(1))
    row = pl.BlockSpec((1, D_MODEL), lambda i: (0, 0))
    return pl.pallas_call(
        functools.partial(_ffn_kernel, alpha=alpha),
        out_shape=jax.ShapeDtypeStruct((n, D_MODEL), F32),
        grid=(n // TM_FFN,),
        in_specs=[pl.BlockSpec((TM_FFN, D_MODEL), lambda i: (i, 0)),
                  pl.BlockSpec((1, 6, D_MODEL), lambda i: (mod_index_ffn(i), 0, 0)),
                  whole(w1), whole(w3), whole(w2), row, row],
        out_specs=pl.BlockSpec((TM_FFN, D_MODEL), lambda i: (i, 0)),
        compiler_params=pltpu.CompilerParams(dimension_semantics=("arbitrary",), vmem_limit_bytes=VMEM_LIMIT),
        name="ffn_ln",
    )(x, mod_l, w1, w3, w2, ln_g, ln_b)


def _moe_kernel(x_ref, mod_ref, wr_ref, br_ref, w1_ref, w3_ref, w2_ref, lng_ref, lnb_ref, y_ref,
                h_scr, comb_scr, acc_scr, *, alpha):
    e = pl.program_id(1)
    m = mod_ref[0]

    @pl.when(e == 0)
    def _():
        h = x_ref[...] * (1.0 + m[4:5]) + m[3:4]
        h_scr[...] = h.astype(BF16)
        logits = jnp.dot(h, wr_ref[...], preferred_element_type=F32, precision=lax.Precision.HIGHEST) + br_ref[...]
        lane = _iota(logits.shape, 1)
        m1 = jnp.max(logits, -1, keepdims=True)
        i1 = jnp.min(jnp.where(logits == m1, lane, 128), -1, keepdims=True)
        rest = jnp.where(lane == i1, NEG_BIG, logits)
        m2 = jnp.max(rest, -1, keepdims=True)
        i2 = jnp.min(jnp.where(rest == m2, lane, 128), -1, keepdims=True)
        e2 = jnp.exp(m2 - m1)
        g1 = 1.0 / (1.0 + e2)
        comb_scr[...] = jnp.where(lane == i1, g1, 0.0) + jnp.where(lane == i2, e2 * g1, 0.0)
        acc_scr[...] = jnp.zeros_like(acc_scr)

    h = h_scr[...]
    a = jnp.dot(h, w1_ref[0], preferred_element_type=F32)
    g = jnp.dot(h, w3_ref[0], preferred_element_type=F32)
    y = jnp.dot((_silu(a) * g).astype(BF16), w2_ref[0], preferred_element_type=F32)
    comb = comb_scr[...]
    gate = jnp.sum(jnp.where(_iota(comb.shape, 1) == e, comb, 0.0), -1, keepdims=True)
    acc_scr[...] += gate * y

    @pl.when(e == N_EXPERTS - 1)
    def _():
        y_ref[...] = _layer_norm(alpha * x_ref[...] + m[5:6] * acc_scr[...], lng_ref[...], lnb_ref[...])


def _moe(x, mod_l, w_r, b_r, w1, w3, w2, ln_g, ln_b, mod_index_moe, alpha):
    n = x.shape[0]
    row = pl.BlockSpec((1, D_MODEL), lambda i, e: (0, 0))
    expert = lambda a: pl.BlockSpec((1,) + a.shape[1:], lambda i, e: (e, 0, 0))
    return pl.pallas_call(
        functools.partial(_moe_kernel, alpha=alpha),
        out_shape=jax.ShapeDtypeStruct((n, D_MODEL), F32),
        grid=(n // TM_MOE, N_EXPERTS),
        in_specs=[pl.BlockSpec((TM_MOE, D_MODEL), lambda i, e: (i, 0)),
                  pl.BlockSpec((1, 6, D_MODEL), lambda i, e: (mod_index_moe(i), 0, 0)),
                  pl.BlockSpec((D_MODEL, 128), lambda i, e: (0, 0)),
                  pl.BlockSpec((1, 128), lambda i, e: (0, 0)),
                  expert(w1), expert(w3), expert(w2), row, row],
        out_specs=pl.BlockSpec((TM_MOE, D_MODEL), lambda i, e: (i, 0)),
        scratch_shapes=[pltpu.VMEM((TM_MOE, D_MODEL), BF16), pltpu.VMEM((TM_MOE, 128), F32),
                        pltpu.VMEM((TM_MOE, D_MODEL), F32)],
        compiler_params=pltpu.CompilerParams(dimension_semantics=("arbitrary", "arbitrary"),
                                             vmem_limit_bytes=VMEM_LIMIT),
        name="moe_ln",
    )(x, mod_l, w_r, b_r, w1, w3, w2, ln_g, ln_b)


def _rope_tables(lat_len):
    t = jnp.arange(lat_len)
    pos = jnp.stack([t // GRID_W, t % GRID_W], axis=1).astype(F32)
    half, quarter = HEAD_DIM // 2, HEAD_DIM // 4
    inv_freq = ROPE_THETA ** (-jnp.arange(quarter, dtype=F32) * 2.0 / half)
    lane = np.arange(HEAD_DIM)
    ang = pos[:, lane // half] * inv_freq[lane % quarter][None, :]
    sign = jnp.asarray(np.where((lane % half) < quarter, -1.0, 1.0), F32)
    cos = jnp.tile(jnp.cos(ang), (1, 128 // HEAD_DIM))
    sin = jnp.tile(jnp.sin(ang) * sign, (1, 128 // HEAD_DIM))
    return cos, sin


def _expand_matrices():
    e_beta = np.zeros((2, 128, D_REC), np.float32)
    e_alpha = np.zeros((2, 128, D_REC), np.float32)
    for d in range(2):
        for h in range(N_HEADS_REC):
            e_beta[d, d * N_HEADS_REC + h, h * HEAD_DIM:(h + 1) * HEAD_DIM] = 1.0
            e_alpha[d, 2 * N_HEADS_REC + d * N_HEADS_REC + h, h * HEAD_DIM:(h + 1) * HEAD_DIM] = 1.0
    return jnp.asarray(e_beta, BF16), jnp.asarray(e_alpha, BF16)


def _per_head_lanes(p):
    return jnp.repeat(p, HEAD_DIM, axis=-1)


def kernel(x_prompt, x_sample, cache_k, cache_v, state_gdn, state_hgrn, c, c_ctx, w_mod, b_mod, w_in, conv_w, attn_sink, gdn_a_log, gdn_dt_bias, gdn_norm_w, hgrn_lb, hgrn_norm_w, w_out, ln_g, ln_b, ffn_w1, ffn_w3, ffn_w2, moe_router, moe_router_b, moe_w1, moe_w3, moe_w2):
    depth = w_in.shape[0]
    n_ctx, ctx_len, _ = x_prompt.shape
    lat_batch, lat_len, _ = x_sample.shape
    n_ctx_rows = n_ctx * ctx_len
    alpha = (2.0 * depth) ** 0.25

    x = jnp.concatenate([x_prompt.reshape(-1, D_MODEL), x_sample.reshape(-1, D_MODEL)], axis=0)

    def mod_index_for(tile):
        ctx_tiles, per_seq = n_ctx_rows // tile, lat_len // tile
        return lambda i: jnp.where(i < ctx_tiles, 0, 1 + (i - ctx_tiles) // per_seq)

    cvecs = jnp.concatenate([c_ctx[None], c], axis=0)
    mods = _modulation(cvecs, w_mod, b_mod).reshape(depth, cvecs.shape[0], 6, D_MODEL)

    cos_tab, sin_tab = _rope_tables(lat_len)
    e_beta, e_alpha = _expand_matrices()
    sm = jax.nn.softmax(hgrn_lb.astype(F32), axis=0)
    lower = jnp.cumsum(sm, axis=0) - sm[0:1]

    w_in_r = jnp.concatenate([w_in[:, :, :1792], w_in[:, :, 1808:], w_in[:, :, 1792:1808],
                              jnp.zeros((depth, D_MODEL, C_END - 3088), F32)], axis=-1).astype(BF16)
    w_out_b = w_out.astype(BF16)

    n_chunk_ctx, n_chunk_lat = ctx_len // CHUNK, lat_len // CHUNK
    zeros_state = jnp.zeros((n_ctx, HEAD_DIM, D_REC), F32)
    new_k, new_v, new_sg, new_sh = [], [], [], []

    for l in range(depth):
        qa, ka, va, qkvb, g_b, hg, ba = _in_proj(x, mods[l], w_in_r[l], cos_tab, sin_tab, mod_index_for(TM),
                                                 n_ctx_rows // TM, lat_len // TM)
        new_k.append(ka[:n_ctx_rows].reshape(n_ctx, ctx_len, A_KV_HEADS, HEAD_DIM))
        new_v.append(va[:n_ctx_rows].reshape(n_ctx, ctx_len, A_KV_HEADS, HEAD_DIM))

        o_a = _attention(qa, ka, va, cache_k[:, l].reshape(lat_batch, -1, D_KV), cache_v[:, l].reshape(lat_batch, -1, D_KV),
                         attn_sink[l], n_ctx, ctx_len, lat_batch, lat_len)

        sg0 = state_gdn[:, l].transpose(0, 1, 3, 2, 4).reshape(lat_batch, 2, HEAD_DIM, D_REC)
        sh0 = state_hgrn[:, l].transpose(0, 1, 4, 2, 3).reshape(lat_batch, 2, HEAD_DIM, D_REC)
        ob, oc, sg_fin, sh_fin = [], [], [], []
        for d, reverse in enumerate((False, True)):
            alog_e = _per_head_lanes(gdn_a_log[l, d])[None]
            dtb_e = _per_head_lanes(gdn_dt_bias[l, d])[None]
            gdn = functools.partial(_gdn_scan, qkvb, ba, conv_w[l], e_beta[d], e_alpha[d], alog_e, dtb_e)
            o_ctx, s_ctx = gdn(zeros_state, reverse, n_ctx, n_chunk_ctx, 0)
            o_lat, _ = gdn(sg0[:, d], reverse, lat_batch, n_chunk_lat, n_ctx_rows // CHUNK)
            ob.append(jnp.concatenate([o_ctx, o_lat], axis=0))
            sg_fin.append(s_ctx.reshape(n_ctx, HEAD_DIM, N_HEADS_REC, HEAD_DIM).transpose(0, 2, 1, 3))

            hgrn = functools.partial(_hgrn_scan, hg, lower[l, d][None])
            o_ctx, s_ctx = hgrn(zeros_state, reverse, n_ctx, n_chunk_ctx, 0)
            o_lat, _ = hgrn(sh0[:, d], reverse, lat_batch, n_chunk_lat, n_ctx_rows // CHUNK)
            oc.append(jnp.concatenate([o_ctx, o_lat], axis=0))
            sh_fin.append(s_ctx.reshape(n_ctx, HEAD_DIM, N_HEADS_REC, HEAD_DIM).transpose(0, 2, 3, 1))
        new_sg.append(jnp.stack(sg_fin, axis=1))
        new_sh.append(jnp.stack(sh_fin, axis=1))

        nw_b = jnp.tile(gdn_norm_w[l], N_HEADS_REC)[None]
        nw_c = jnp.tile(hgrn_norm_w[l], N_HEADS_REC)[None]
        x = _out_proj(x, mods[l], o_a, ob[0], ob[1], oc[0], oc[1], g_b, hg, nw_b, nw_c, w_out_b[l],
                      ln_g[l, 0][None], ln_b[l, 0][None], mod_index_for(TM), alpha)

        i = l // 2
        if l % 2 == 0:
            x = _ffn(x, mods[l], ffn_w1[i].astype(BF16), ffn_w3[i].astype(BF16), ffn_w2[i].astype(BF16),
                     ln_g[l, 1][None], ln_b[l, 1][None], mod_index_for(TM_FFN), alpha)
        else:
            w_r = jnp.zeros((D_MODEL, 128), F32).at[:, :N_EXPERTS].set(moe_router[i])
            b_r = jnp.full((1, 128), NEG_BIG, F32).at[0, :N_EXPERTS].set(moe_router_b[i])
            x = _moe(x, mods[l], w_r, b_r, moe_w1[i].astype(BF16), moe_w3[i].astype(BF16), moe_w2[i].astype(BF16),
                     ln_g[l, 1][None], ln_b[l, 1][None], mod_index_for(TM_MOE), alpha)

    y_prompt = x[:n_ctx_rows].reshape(x_prompt.shape)
    y_sample = x[n_ctx_rows:].reshape(x_sample.shape)
    return (y_prompt, y_sample, jnp.stack(new_k, axis=1), jnp.stack(new_v, axis=1),
            jnp.stack(new_sg, axis=1), jnp.stack(new_sh, axis=1))
```

```python
import functools

import numpy as np
import jax
import jax.numpy as jnp
from jax import lax
from jax.experimental import pallas as pl
from jax.experimental.pallas import tpu as pltpu

F32 = jnp.float32
BF16 = jnp.bfloat16

D_MODEL = 1024
HEAD_DIM = 64
A_HEADS = 8
A_KV_HEADS = 2
GRID_W = 64
ATTN_BLOCK = 128
N_HEADS_REC = 4
D_REC = N_HEADS_REC * HEAD_DIM
CONV_W = 5
CHUNK = 64
SUB = 8
D_A = A_HEADS * HEAD_DIM
D_KV = A_KV_HEADS * HEAD_DIM
D_FF = 2816
N_EXPERTS = 8
EXPERT_FF = 1024
ROPE_THETA = 10000.0
NORM_EPS = 1e-5
TINY = 1e-30
NEG_BIG = -1e30

C_QA, C_KA, C_VA, C_QKVB, C_GB, C_HG, C_BA, C_END = 0, 512, 640, 768, 1536, 1792, 3072, 3200
D_HG = 1280
HALO = 8

TM = 256
TS = 256
TM_FFN = 512
FF_CHUNK = 256
TM_MOE = 1024
VMEM_LIMIT = 56 * 1024 * 1024

NN = (((1,), (0,)), ((), ()))
NT = (((1,), (1,)), ((), ()))
TN = (((0,), (0,)), ((), ()))


def _dot(a, b, dims=NN):
    return lax.dot_general(a.astype(BF16), b.astype(BF16), dims, preferred_element_type=F32)


def _split2(x):
    hi = x.astype(BF16)
    lo = (x - hi.astype(F32)).astype(BF16)
    return hi, lo


def _split3(x):
    hi = x.astype(BF16)
    r = x - hi.astype(F32)
    mid = r.astype(BF16)
    lo = (r - mid.astype(F32)).astype(BF16)
    return hi, mid, lo


def _dot_sel(sel, x, dims=NN, sel_is_lhs=True):
    out = None
    for part in _split3(x):
        a, b = (sel, part) if sel_is_lhs else (part, sel)
        t = lax.dot_general(a, b, dims, preferred_element_type=F32)
        out = t if out is None else out + t
    return out


def _sigmoid(x):
    return 1.0 / (1.0 + jnp.exp(-x))


def _silu(x):
    return x * _sigmoid(x)


def _layer_norm(x, g, b):
    mu = jnp.mean(x, -1, keepdims=True)
    xc = x - mu
    var = jnp.mean(xc * xc, -1, keepdims=True)
    return xc * lax.rsqrt(var + NORM_EPS) * g + b


def _iota(shape, dim):
    return lax.broadcasted_iota(jnp.int32, shape, dim)


def _head_ones():
    return jnp.where(_iota((D_REC, D_REC), 0) // HEAD_DIM == _iota((D_REC, D_REC), 1) // HEAD_DIM, 1.0, 0.0).astype(BF16)


def _block_diag(x):
    head = _iota(x.shape, 1) // HEAD_DIM
    return jnp.concatenate([jnp.where(head == h, x, jnp.zeros_like(x)) for h in range(N_HEADS_REC)], axis=0)


def _block_diag_mask():
    return _iota((D_REC, D_REC), 0) // HEAD_DIM == _iota((D_REC, D_REC), 1) // HEAD_DIM


def _collapse_block_diag(s):
    out = s[0:HEAD_DIM]
    for h in range(1, N_HEADS_REC):
        out = out + s[h * HEAD_DIM:(h + 1) * HEAD_DIM]
    return out


def _mod_kernel(ct_ref, w_ref, b_ref, o_ref):
    s = _silu(ct_ref[...])
    w = w_ref[0]
    rows = [jnp.sum(s[:, r:r + 1] * w, axis=0, keepdims=True) for r in range(o_ref.shape[1])]
    o_ref[0] = jnp.concatenate(rows, axis=0) + b_ref[0]


def _modulation(cvecs, w_mod, b_mod):
    depth, d, n6 = w_mod.shape
    r = cvecs.shape[0]
    tn = 1024
    ct = jnp.zeros((d, 8), F32).at[:, :r].set(cvecs.T)
    return pl.pallas_call(
        _mod_kernel,
        out_shape=jax.ShapeDtypeStruct((depth, r, n6), F32),
        grid=(depth, n6 // tn),
        in_specs=[pl.BlockSpec((d, 8), lambda l, j: (0, 0)),
                  pl.BlockSpec((1, d, tn), lambda l, j: (l, 0, j)),
                  pl.BlockSpec((1, 1, tn), lambda l, j: (l, 0, j))],
        out_specs=pl.BlockSpec((1, r, tn), lambda l, j: (l, 0, j)),
        compiler_params=pltpu.CompilerParams(dimension_semantics=("arbitrary", "arbitrary"),
                                             vmem_limit_bytes=VMEM_LIMIT),
        name="modulation",
    )(ct, w_mod, b_mod.reshape(depth, 1, n6))


def _rope(x, cos, sin):
    outs = []
    lane = _iota(cos.shape, 1)
    first = (lane % 32) < 16
    for m in range(x.shape[1] // 128):
        xs = x[:, m * 128:(m + 1) * 128]
        swapped = jnp.where(first, pltpu.roll(xs, 128 - 16, 1), pltpu.roll(xs, 16, 1))
        outs.append(xs * cos + swapped * sin)
    return outs[0] if len(outs) == 1 else jnp.concatenate(outs, axis=1)


def _in_proj_kernel(x_ref, mod_ref, w_ref, cos_ref, sin_ref,
                    qa_ref, ka_ref, va_ref, qkvb_ref, gb_ref, hg_ref, ba_ref, *, n_ctx_tiles):
    i = pl.program_id(0)
    m = mod_ref[0]
    h = (x_ref[...] * (1.0 + m[1:2]) + m[0:1]).astype(BF16)

    def proj(lo, hi):
        return jnp.dot(h, w_ref[:, lo:hi], preferred_element_type=F32)

    qa = proj(C_QA, C_KA)
    ka = proj(C_KA, C_VA)

    @pl.when(i < n_ctx_tiles)
    def _():
        qa_ref[...] = qa
        ka_ref[...] = ka

    @pl.when(i >= n_ctx_tiles)
    def _():
        cos, sin = cos_ref[...], sin_ref[...]
        qa_ref[...] = _rope(qa, cos, sin)
        ka_ref[...] = _rope(ka, cos, sin)

    va_ref[...] = proj(C_VA, C_QKVB)
    qkvb_ref[...] = proj(C_QKVB, C_GB)
    gb_ref[...] = proj(C_GB, C_HG)
    hg_ref[...] = proj(C_HG, C_BA)
    ba_ref[...] = proj(C_BA, C_END)


def _in_proj(x, mod_l, w_in_r, cos_tab, sin_tab, mod_index, n_ctx_tiles, lat_tiles_per_seq):
    n = x.shape[0]

    def tab_index(i):
        return (jnp.where(i < n_ctx_tiles, 0, (i - n_ctx_tiles) % lat_tiles_per_seq), 0)

    def out(width):
        return jax.ShapeDtypeStruct((n, width), F32), pl.BlockSpec((TM, width), lambda i: (i, 0))

    outs = [out(D_A), out(D_KV), out(D_KV), out(3 * D_REC), out(D_REC), out(D_HG), out(128)]
    return pl.pallas_call(
        functools.partial(_in_proj_kernel, n_ctx_tiles=n_ctx_tiles),
        out_shape=[o[0] for o in outs],
        grid=(n // TM,),
        in_specs=[pl.BlockSpec((TM, D_MODEL), lambda i: (i, 0)),
                  pl.BlockSpec((1, 6, D_MODEL), lambda i: (mod_index(i), 0, 0)),
                  pl.BlockSpec((D_MODEL, C_END), lambda i: (0, 0)),
                  pl.BlockSpec((TM, 128), tab_index),
                  pl.BlockSpec((TM, 128), tab_index)],
        out_specs=[o[1] for o in outs],
        compiler_params=pltpu.CompilerParams(dimension_semantics=("arbitrary",), vmem_limit_bytes=VMEM_LIMIT),
        name="in_proj",
    )(x, mod_l, w_in_r, cos_tab, sin_tab)


def _attn_heads(q_ref, k, v, bias, sink_ref, o_ref):
    k_same, k_swap = k.astype(BF16), pltpu.roll(k, HEAD_DIM, 1).astype(BF16)
    v_same, v_swap = v.astype(BF16), pltpu.roll(v, HEAD_DIM, 1).astype(BF16)
    nq = q_ref.shape[0]
    low_half = _iota((nq, 128), 1) < HEAD_DIM
    rep = A_HEADS // A_KV_HEADS
    for p in range(A_HEADS // 2):
        q2 = q_ref[:, p * 128:(p + 1) * 128]
        halves = []
        for e in range(2):
            hq = 2 * p + e
            g = hq // rep
            qm = jnp.where(low_half if e == 0 else jnp.logical_not(low_half), q2, 0.0)
            s = _dot(qm, k_same if e == g else k_swap, NT) * (HEAD_DIM ** -0.5)
            if bias is not None:
                s = s + bias
            snk = sink_ref[hq]
            mx = jnp.maximum(jnp.max(s, -1, keepdims=True), snk)
            ex = jnp.exp(s - mx)
            den = jnp.sum(ex, -1, keepdims=True) + jnp.exp(snk - mx)
            halves.append(_dot(ex, v_same if e == g else v_swap) / den)
        o_ref[:, p * 128:(p + 1) * 128] = jnp.where(low_half, halves[0], halves[1])


def _attn_ctx_kernel(q_ref, k_ref, v_ref, sink_ref, o_ref):
    _attn_heads(q_ref, k_ref[...], v_ref[...], None, sink_ref, o_ref)


def _attn_lat_kernel(q_ref, kp_ref, kc_ref, kn_ref, vp_ref, vc_ref, vn_ref, ck_ref, cv_ref, sink_ref, o_all_ref,
                     o_ref, *, n_blocks):
    del o_all_ref
    n = pl.program_id(1)
    k = jnp.concatenate([kp_ref[...], kc_ref[...], kn_ref[...], ck_ref[0]], axis=0)
    v = jnp.concatenate([vp_ref[...], vc_ref[...], vn_ref[...], cv_ref[0]], axis=0)
    nk = k.shape[0]
    qi = _iota((ATTN_BLOCK, nk), 0)
    kj = _iota((ATTN_BLOCK, nk), 1)
    seg = kj // ATTN_BLOCK
    jj = kj % ATTN_BLOCK
    bias_prev = jnp.where(jj >= qi, 0.0, NEG_BIG) + jnp.where(n > 0, 0.0, NEG_BIG)
    bias_next = jnp.where(jj <= qi, 0.0, NEG_BIG) + jnp.where(n < n_blocks - 1, 0.0, NEG_BIG)
    bias = jnp.where(seg == 0, bias_prev, jnp.where(seg == 2, bias_next, 0.0))
    _attn_heads(q_ref, k, v, bias, sink_ref, o_ref)


def _attention(qa, ka, va, ck, cv, sink, n_ctx, ctx_len, lat_batch, lat_len):
    n = qa.shape[0]
    smem = pl.BlockSpec(memory_space=pltpu.SMEM)
    params = pltpu.CompilerParams(dimension_semantics=("arbitrary",), vmem_limit_bytes=VMEM_LIMIT)
    o_ctx = pl.pallas_call(
        _attn_ctx_kernel,
        out_shape=jax.ShapeDtypeStruct((n, D_A), F32),
        grid=(n_ctx,),
        in_specs=[pl.BlockSpec((ctx_len, D_A), lambda b: (b, 0)),
                  pl.BlockSpec((ctx_len, D_KV), lambda b: (b, 0)),
                  pl.BlockSpec((ctx_len, D_KV), lambda b: (b, 0)),
                  smem],
        out_specs=pl.BlockSpec((ctx_len, D_A), lambda b: (b, 0)),
        compiler_params=params,
        name="attn_context",
    )(qa, ka, va, sink)

    nb = lat_len // ATTN_BLOCK
    base = n_ctx * ctx_len // ATTN_BLOCK
    past = ck.shape[1]

    def cur(b, j):
        return (base + b * nb + j, 0)

    def prev(b, j):
        return (base + b * nb + jnp.maximum(j - 1, 0), 0)

    def nxt(b, j):
        return (base + b * nb + jnp.minimum(j + 1, nb - 1), 0)

    kv = lambda f: pl.BlockSpec((ATTN_BLOCK, D_KV), f)
    return pl.pallas_call(
        functools.partial(_attn_lat_kernel, n_blocks=nb),
        out_shape=jax.ShapeDtypeStruct((n, D_A), F32),
        grid=(lat_batch, nb),
        in_specs=[pl.BlockSpec((ATTN_BLOCK, D_A), cur),
                  kv(prev), kv(cur), kv(nxt), kv(prev), kv(cur), kv(nxt),
                  pl.BlockSpec((1, past, D_KV), lambda b, j: (b, 0, 0)),
                  pl.BlockSpec((1, past, D_KV), lambda b, j: (b, 0, 0)),
                  smem,
                  pl.BlockSpec(memory_space=pl.ANY)],
        out_specs=pl.BlockSpec((ATTN_BLOCK, D_A), cur),
        input_output_aliases={10: 0},
        compiler_params=pltpu.CompilerParams(dimension_semantics=("arbitrary", "arbitrary"),
                                             vmem_limit_bytes=VMEM_LIMIT),
        name="attn_latent",
    )(qa, ka, ka, ka, va, va, va, ck, cv, sink, o_ctx)


def _tri(reverse, strict=False):
    i, j = _iota((CHUNK, CHUNK), 0), _iota((CHUNK, CHUNK), 1)
    if strict:
        return (i < j) if reverse else (i > j)
    return (i <= j) if reverse else (i >= j)


def _gdn_prep_kernel(cur_ref, prev_ref, next_ref, ba_ref, convw_ref, alog_ref, dtb_ref, qkv_ref, bl_ref,
                     *, ctx_tiles, ctx_tiles_per_seq, lat_tiles_per_seq):
    i = pl.program_id(0)
    in_ctx = i < ctx_tiles
    t_ctx = i % ctx_tiles_per_seq
    t_lat = (i - ctx_tiles) % lat_tiles_per_seq
    is_first = jnp.where(in_ctx, t_ctx == 0, t_lat == 0)
    is_last = jnp.where(in_ctx, t_ctx == ctx_tiles_per_seq - 1, t_lat == lat_tiles_per_seq - 1)
    has_prev = jnp.where(is_first, 0.0, 1.0)
    has_next = jnp.where(is_last, 0.0, 1.0)
    xh = jnp.concatenate([prev_ref[...] * has_prev, cur_ref[...], next_ref[...] * has_next], axis=0)
    cw = convw_ref[...]
    rows = cur_ref.shape[0]
    y = None
    for t in range(CONV_W):
        lo = HALO - CONV_W // 2 + t
        term = xh[lo:lo + rows] * cw[t:t + 1]
        y = term if y is None else y + term
    y = _silu(y)
    q, k, v = y[:, :D_REC], y[:, D_REC:2 * D_REC], y[:, 2 * D_REC:]
    ones = _head_ones()
    qq_hi, qq_lo = _split2(q * q)
    kk_hi, kk_lo = _split2(k * k)
    sums = jnp.dot(jnp.concatenate([qq_hi, qq_lo, kk_hi, kk_lo], axis=0), ones, preferred_element_type=F32)
    q_ss = sums[0:rows] + sums[rows:2 * rows]
    k_ss = sums[2 * rows:3 * rows] + sums[3 * rows:]
    qkv_ref[:, :D_REC] = q * lax.rsqrt(q_ss + 1e-6) * (HEAD_DIM ** -0.5)
    qkv_ref[:, D_REC:2 * D_REC] = k * lax.rsqrt(k_ss + 1e-6)
    qkv_ref[:, 2 * D_REC:] = v

    ba = ba_ref[...]
    za = ba + dtb_ref[...]
    softplus = jnp.maximum(za, 0.0) + jnp.log(1.0 + jnp.exp(-jnp.abs(za)))
    la = -jnp.exp(alog_ref[...]) * softplus
    bl_ref[...] = jnp.where(_iota(ba.shape, 1) < 2 * N_HEADS_REC, _sigmoid(ba), la)


def _gdn_prep(qkvb, ba, conv_w, alog_row, dtb_row, ctx_tiles, ctx_tiles_per_seq, lat_tiles_per_seq):
    n = qkvb.shape[0]
    per = TM // HALO
    const = lambda shape: pl.BlockSpec(shape, lambda i: (0,) * len(shape))
    return pl.pallas_call(
        functools.partial(_gdn_prep_kernel, ctx_tiles=ctx_tiles, ctx_tiles_per_seq=ctx_tiles_per_seq,
                          lat_tiles_per_seq=lat_tiles_per_seq),
        out_shape=[jax.ShapeDtypeStruct((n, 3 * D_REC), F32), jax.ShapeDtypeStruct((n, 128), F32)],
        grid=(n // TM,),
        in_specs=[pl.BlockSpec((TM, 3 * D_REC), lambda i: (i, 0)),
                  pl.BlockSpec((HALO, 3 * D_REC), lambda i: (jnp.maximum(i * per - 1, 0), 0)),
                  pl.BlockSpec((HALO, 3 * D_REC), lambda i: (jnp.minimum((i + 1) * per, n // HALO - 1), 0)),
                  pl.BlockSpec((TM, 128), lambda i: (i, 0)),
                  const((CONV_W, 3 * D_REC)), const((1, 128)), const((1, 128))],
        out_specs=[pl.BlockSpec((TM, 3 * D_REC), lambda i: (i, 0)), pl.BlockSpec((TM, 128), lambda i: (i, 0))],
        compiler_params=pltpu.CompilerParams(dimension_semantics=("arbitrary",), vmem_limit_bytes=VMEM_LIMIT),
        name="gdn_prep",
    )(qkvb, qkvb, qkvb, ba, conv_w, alog_row, dtb_row)


def _scan_schedule(n_ctx, ctx_len, lat_batch, lat_len, reverse):
    tiles_ctx, tiles_lat = ctx_len // TS, lat_len // TS
    seqs = [[s * tiles_ctx + t for t in range(tiles_ctx)] for s in range(n_ctx)]
    seqs += [[n_ctx * tiles_ctx + b * tiles_lat + t for t in range(tiles_lat)] for b in range(lat_batch)]
    cols = []
    for s, tiles in enumerate(seqs):
        order = tiles[::-1] if reverse else tiles
        for n, t in enumerate(order):
            cols.append((t, s, int(n == 0), int(n == len(order) - 1)))
    return jnp.asarray(np.array(cols, np.int32).T)


def _bd2(x):
    low = _iota(x.shape, 1) < HEAD_DIM
    zero = jnp.zeros_like(x)
    return jnp.concatenate([jnp.where(low, x, zero), jnp.where(low, zero, x)], axis=0)


def _mm3_stack(lhs_list, rhs):
    r_hi, r_lo = _split2(rhs)
    w_hi, w_lo = _bd2(r_hi), _bd2(r_lo)
    parts = [_split2(a) for a in lhs_list]
    his = [p[0] for p in parts]
    los = [p[1] for p in parts]
    n = len(lhs_list) * CHUNK
    big = jnp.dot(jnp.concatenate(his + los, axis=0), w_hi, preferred_element_type=F32)
    small = jnp.dot(his[0] if len(his) == 1 else jnp.concatenate(his, axis=0), w_lo, preferred_element_type=F32)
    out = big[:n] + big[n:] + small
    return [out[m * CHUNK:(m + 1) * CHUNK] for m in range(len(lhs_list))]


def _pick2(a, b):
    shape = (CHUNK, 128)
    return jnp.where(_iota(shape, 1) < HEAD_DIM, jnp.broadcast_to(a, shape), jnp.broadcast_to(b, shape))


def _gdn_tile_kernel(sch_ref, qkv_ref, bl_ref, s0_ref, o_ref, sfin_ref, s_scr, *, reverse, direction):
    n = pl.program_id(0)
    n_groups = N_HEADS_REC // 2

    @pl.when(sch_ref[2, n] == 1)
    def _():
        for g in range(n_groups):
            s_scr[g] = _bd2(s0_ref[0, :, g * 128:(g + 1) * 128])

    row, lane = _iota((CHUNK, 128), 0), _iota((CHUNK, 128), 1) % HEAD_DIM
    if reverse:
        incl, strict = row <= lane, row < lane
    else:
        incl, strict = row >= lane, row > lane
    eye = jnp.where(row == lane, 1.0, 0.0)
    tri = jnp.where(_tri(reverse), 1.0, 0.0).astype(BF16)
    tri_t2 = jnp.where(_iota((CHUNK, 128), 0) >= lane, 1.0, 0.0) if reverse else \
        jnp.where(_iota((CHUNK, 128), 0) <= lane, 1.0, 0.0)
    tri_t2 = tri_t2.astype(BF16)
    bd_mask = (_iota((128, 128), 0) // HEAD_DIM) == (_iota((128, 128), 1) // HEAD_DIM)
    last = 0 if reverse else CHUNK - 1
    n_chunks = TS // CHUNK
    states = [s_scr[g] for g in range(n_groups)]

    chunk_order = list(range(n_chunks - 1, -1, -1) if reverse else range(n_chunks))
    cb, ca = direction * N_HEADS_REC, 2 * N_HEADS_REC + direction * N_HEADS_REC
    inst = []
    for c in chunk_order:
        rs = slice(c * CHUNK, (c + 1) * CHUNK)
        bl = bl_ref[rs, :]
        g16, gt2 = None, None
        for part in _split3(bl):
            a = jnp.dot(tri, part, preferred_element_type=F32)
            b = lax.dot_general(part, tri_t2, TN, preferred_element_type=F32)
            g16 = a if g16 is None else g16 + a
            gt2 = b if gt2 is None else gt2 + b
        for g in range(n_groups):
            h0, h1 = 2 * g, 2 * g + 1
            ls = slice(g * 128, (g + 1) * 128)
            d = dict(c=c, g=g, rs=rs, ls=ls)
            d["q"] = qkv_ref[rs, ls]
            d["k"] = qkv_ref[rs, D_REC + g * 128:D_REC + (g + 1) * 128]
            d["v"] = qkv_ref[rs, 2 * D_REC + g * 128:2 * D_REC + (g + 1) * 128]
            d["beta"] = _pick2(bl[:, cb + h0:cb + h0 + 1], bl[:, cb + h1:cb + h1 + 1])
            d["g_i"] = _pick2(g16[:, ca + h0:ca + h0 + 1], g16[:, ca + h1:ca + h1 + 1])
            g_j = _pick2(gt2[ca + h0:ca + h0 + 1, :], gt2[ca + h1:ca + h1 + 1, :])
            d["g_tot"] = d["g_i"][last:last + 1]
            d["decay"] = jnp.where(incl, jnp.exp(jnp.where(incl, d["g_i"] - g_j, 0.0)), 0.0)
            inst.append(d)

    for d in inst:
        gram = _dot(jnp.concatenate([d["k"], d["q"]], axis=0), _bd2(d["k"].astype(BF16)), NT)
        d["low"] = jnp.where(strict, d["beta"] * gram[:CHUNK] * d["decay"], 0.0)
        d["attn"] = gram[CHUNK:] * d["decay"]

    for d in inst:
        d["t"] = eye - d["low"]
        (d["p"],) = _mm3_stack([d["low"]], d["low"])
    for _ in range(4):
        for d in inst:
            p_next, tp = _mm3_stack([d["p"], d["t"]], d["p"])
            d["t"] = d["t"] + tp
            d["p"] = p_next
    for d in inst:
        t_mat = d["t"] + _mm3_stack([d["t"]], d["p"])[0]
        eg = jnp.exp(d["g_i"])
        vb = d["v"] * d["beta"]
        kbg = d["k"] * d["beta"] * eg
        d["u"] = _dot(t_mat, _bd2(vb.astype(BF16)))
        d["w_qg"] = jnp.concatenate([_dot(t_mat, _bd2(kbg.astype(BF16))), d["q"] * eg], axis=0).astype(BF16)
        d["k_dec"] = (d["k"] * jnp.exp(d["g_tot"] - d["g_i"])).astype(BF16)
        d["e_tot"] = jnp.exp(d["g_tot"])

    for d in inst:
        s = states[d["g"]]
        ws_qs = _dot(d["w_qg"], s)
        v_new = d["u"] - ws_qs[:CHUNK]
        o_ref[d["rs"], d["ls"]] = ws_qs[CHUNK:] + _dot(d["attn"], _bd2(v_new.astype(BF16)))
        states[d["g"]] = s * d["e_tot"] + jnp.where(bd_mask, _dot(d["k_dec"], v_new, TN), 0.0)

    for g in range(n_groups):
        s_scr[g] = states[g]

    @pl.when(sch_ref[3, n] == 1)
    def _():
        for g in range(n_groups):
            sfin_ref[0, :, g * 128:(g + 1) * 128] = states[g][:HEAD_DIM] + states[g][HEAD_DIM:]


def _gdn_scan(qkv, bl, s0, sched, reverse, direction):
    n = qkv.shape[0]
    n_seq = s0.shape[0]
    grid_spec = pltpu.PrefetchScalarGridSpec(
        num_scalar_prefetch=1,
        grid=(sched.shape[1],),
        in_specs=[pl.BlockSpec((TS, 3 * D_REC), lambda i, sch: (sch[0, i], 0)),
                  pl.BlockSpec((TS, 128), lambda i, sch: (sch[0, i], 0)),
                  pl.BlockSpec((1, HEAD_DIM, D_REC), lambda i, sch: (sch[1, i], 0, 0))],
        out_specs=[pl.BlockSpec((TS, D_REC), lambda i, sch: (sch[0, i], 0)),
                   pl.BlockSpec((1, HEAD_DIM, D_REC), lambda i, sch: (sch[1, i], 0, 0))],
        scratch_shapes=[pltpu.VMEM((N_HEADS_REC // 2, 128, 128), F32)],
    )
    return pl.pallas_call(
        functools.partial(_gdn_tile_kernel, reverse=reverse, direction=direction),
        out_shape=[jax.ShapeDtypeStruct((n, D_REC), F32), jax.ShapeDtypeStruct((n_seq, HEAD_DIM, D_REC), F32)],
        grid_spec=grid_spec,
        compiler_params=pltpu.CompilerParams(dimension_semantics=("arbitrary",), vmem_limit_bytes=VMEM_LIMIT),
        name="gdn_scan_bwd" if reverse else "gdn_scan_fwd",
    )(sched, qkv, bl, s0)


def _hgrn_tile_kernel(sch_ref, f_ref, i_ref, q_ref, lb_ref, s0_ref, o_ref, sfin_ref, s_scr, *, reverse):
    n = pl.program_id(0)

    @pl.when(sch_ref[2, n] == 1)
    def _():
        s_scr[...] = _block_diag(s0_ref[0])

    lb = lb_ref[...]
    tri = jnp.where(_tri(reverse), 1.0, 0.0).astype(BF16)
    row = _iota((CHUNK, D_REC), 0)
    col_tok = _iota((CHUNK, D_REC), 1) % HEAD_DIM
    r_in = row % SUB
    n_sub = CHUNK // SUB
    ones = _head_ones()
    bd_mask = _block_diag_mask()
    last = 0 if reverse else CHUNK - 1
    n_chunks = TS // CHUNK
    chunk_order = list(range(n_chunks - 1, -1, -1) if reverse else range(n_chunks))
    blocks = list(range(1, n_sub) if reverse else range(n_sub - 1))

    inst = []
    for c in chunk_order:
        rs = slice(c * CHUNK, (c + 1) * CHUNK)
        f = lb + (1.0 - lb) * _sigmoid(f_ref[rs, :])
        d = dict(rs=rs, kk=1.0 - f, q=q_ref[rs, :], v=i_ref[rs, :])
        d["b"] = _dot_sel(tri, jnp.log(jnp.maximum(f, TINY)))
        inst.append(d)

    for d in inst:
        b, q, kk = d["b"], d["q"], d["kk"]
        ends = []
        for p in range(n_sub):
            r = p * SUB if reverse else p * SUB + SUB - 1
            ends.append(jnp.broadcast_to(b[r:r + 1], (SUB, D_REC)))
        k_hat = kk * jnp.exp(jnp.concatenate(ends, axis=0) - b)
        lhs = []
        for p in blocks:
            r = p * SUB if reverse else p * SUB + SUB - 1
            after = (row < p * SUB) if reverse else (row >= (p + 1) * SUB)
            lhs.append(jnp.where(after, q * jnp.exp(jnp.where(after, b - b[r:r + 1], 0.0)), 0.0).astype(BF16))
        res = lax.dot_general(jnp.concatenate(lhs, axis=0), _block_diag(k_hat.astype(BF16)), NT,
                              preferred_element_type=F32)
        a_mat = None
        for m, p in enumerate(blocks):
            term = jnp.where(col_tok // SUB == p, res[m * CHUNK:(m + 1) * CHUNK], 0.0)
            a_mat = term if a_mat is None else a_mat + term
        d["a"] = a_mat

    for d in inst:
        b, q, kk = d["b"], d["q"], d["kk"]
        terms, oks = [], []
        for off in range(SUB):
            if off == 0:
                k_s, b_s, ok = kk, b, None
            elif reverse:
                k_s, b_s = pltpu.roll(kk, CHUNK - off, 0), pltpu.roll(b, CHUNK - off, 0)
                ok = r_in <= SUB - 1 - off
            else:
                k_s, b_s = pltpu.roll(kk, off, 0), pltpu.roll(b, off, 0)
                ok = r_in >= off
            if ok is None:
                e = q * k_s
            else:
                e = jnp.where(ok, q * k_s * jnp.exp(jnp.where(ok, b - b_s, 0.0)), 0.0)
            terms.append(e.astype(BF16))
            oks.append(ok)
        sums = jnp.dot(jnp.concatenate(terms, axis=0), ones, preferred_element_type=F32)
        a_mat = d["a"]
        for off in range(SUB):
            partner = (row + off) if reverse else (row - off)
            hit = col_tok == partner if oks[off] is None else jnp.logical_and(col_tok == partner, oks[off])
            a_mat = a_mat + jnp.where(hit, sums[off * CHUNK:(off + 1) * CHUNK], 0.0)
        d["a"] = a_mat

    for d in inst:
        b = d["b"]
        b_last = b[last:last + 1]
        d["av"] = _dot(d["a"], _block_diag(d["v"].astype(BF16)))
        d["kv"] = jnp.where(bd_mask, _dot(d["v"], d["kk"] * jnp.exp(b_last - b), TN), 0.0)
        d["qb"] = (d["q"] * jnp.exp(b)).astype(BF16)
        d["e_last"] = jnp.exp(b_last)

    s = s_scr[...]
    for d in inst:
        o_ref[d["rs"], :] = d["av"] + _dot(d["qb"], s, NT)
        s = s * d["e_last"] + d["kv"]
    s_scr[...] = s

    @pl.when(sch_ref[3, n] == 1)
    def _():
        sfin_ref[0] = _collapse_block_diag(s)


def _hgrn_scan(hg, lb_e, s0, sched, reverse):
    n = hg.shape[0]
    n_seq = s0.shape[0]
    col = lambda c: pl.BlockSpec((TS, D_REC), lambda i, sch: (sch[0, i], c))
    grid_spec = pltpu.PrefetchScalarGridSpec(
        num_scalar_prefetch=1,
        grid=(sched.shape[1],),
        in_specs=[col(1 if reverse else 0), col(2), col(3),
                  pl.BlockSpec((1, D_REC), lambda i, sch: (0, 0)),
                  pl.BlockSpec((1, HEAD_DIM, D_REC), lambda i, sch: (sch[1, i], 0, 0))],
        out_specs=[pl.BlockSpec((TS, D_REC), lambda i, sch: (sch[0, i], 0)),
                   pl.BlockSpec((1, HEAD_DIM, D_REC), lambda i, sch: (sch[1, i], 0, 0))],
        scratch_shapes=[pltpu.VMEM((D_REC, D_REC), F32)],
    )
    return pl.pallas_call(
        functools.partial(_hgrn_tile_kernel, reverse=reverse),
        out_shape=[jax.ShapeDtypeStruct((n, D_REC), F32), jax.ShapeDtypeStruct((n_seq, HEAD_DIM, D_REC), F32)],
        grid_spec=grid_spec,
        compiler_params=pltpu.CompilerParams(dimension_semantics=("arbitrary",), vmem_limit_bytes=VMEM_LIMIT),
        name="hgrn_scan_bwd" if reverse else "hgrn_scan_fwd",
    )(sched, hg, hg, hg, lb_e, s0)


def _gated_readout(o, gate, w_e, ones):
    ms = _dot_sel(ones, o * o, sel_is_lhs=False) * (1.0 / HEAD_DIM)
    return o * lax.rsqrt(ms + NORM_EPS) * w_e * _silu(gate)


def _out_proj_kernel(x_ref, mod_ref, oa_ref, bf_ref, bb_ref, cf_ref, cb_ref, gb_ref, gc_ref, nwb_ref, nwc_ref,
                     w_ref, lng_ref, lnb_ref, y_ref, *, alpha):
    m = mod_ref[0]
    ones = _head_ones()
    o_b = _gated_readout(bf_ref[...] + bb_ref[...], gb_ref[...], nwb_ref[...], ones)
    o_c = _gated_readout(cf_ref[...] + cb_ref[...], gc_ref[...], nwc_ref[...], ones)
    mix = jnp.dot(oa_ref[...].astype(BF16), w_ref[0:D_A], preferred_element_type=F32)
    mix = mix + jnp.dot(o_b.astype(BF16), w_ref[D_A:D_A + D_REC], preferred_element_type=F32)
    mix = mix + jnp.dot(o_c.astype(BF16), w_ref[D_A + D_REC:], preferred_element_type=F32)
    y_ref[...] = _layer_norm(alpha * x_ref[...] + m[2:3] * mix, lng_ref[...], lnb_ref[...])


def _out_proj(x, mod_l, o_a, ob_f, ob_b, oc_f, oc_b, g_b, hg, nw_b, nw_c, w_out, ln_g, ln_b, mod_index, alpha):
    n = x.shape[0]
    tile = lambda w: pl.BlockSpec((TM, w), lambda i: (i, 0))
    row = lambda w: pl.BlockSpec((1, w), lambda i: (0, 0))
    return pl.pallas_call(
        functools.partial(_out_proj_kernel, alpha=alpha),
        out_shape=jax.ShapeDtypeStruct((n, D_MODEL), F32),
        grid=(n // TM,),
        in_specs=[tile(D_MODEL),
                  pl.BlockSpec((1, 6, D_MODEL), lambda i: (mod_index(i), 0, 0)),
                  tile(D_A), tile(D_REC), tile(D_REC), tile(D_REC), tile(D_REC), tile(D_REC),
                  pl.BlockSpec((TM, D_REC), lambda i: (i, 4)),
                  row(D_REC), row(D_REC),
                  pl.BlockSpec((D_MODEL, D_MODEL), lambda i: (0, 0)),
                  row(D_MODEL), row(D_MODEL)],
        out_specs=tile(D_MODEL),
        compiler_params=pltpu.CompilerParams(dimension_semantics=("arbitrary",), vmem_limit_bytes=VMEM_LIMIT),
        name="out_proj_ln",
    )(x, mod_l, o_a, ob_f, ob_b, oc_f, oc_b, g_b, hg, nw_b, nw_c, w_out, ln_g, ln_b)


def _ffn_kernel(x_ref, mod_ref, w1_ref, w3_ref, w2_ref, lng_ref, lnb_ref, y_ref, *, alpha):
    m = mod_ref[0]
    x = x_ref[...]
    h = (x * (1.0 + m[4:5]) + m[3:4]).astype(BF16)
    acc = jnp.zeros((x.shape[0], D_MODEL), F32)
    for c in range(D_FF // FF_CHUNK):
        sl = slice(c * FF_CHUNK, (c + 1) * FF_CHUNK)
        a = jnp.dot(h, w1_ref[:, sl], preferred_element_type=F32)
        g = jnp.dot(h, w3_ref[:, sl], preferred_element_type=F32)
        acc = acc + jnp.dot((_silu(a) * g).astype(BF16), w2_ref[sl, :], preferred_element_type=F32)
    y_ref[...] = _layer_norm(alpha * x + m[5:6] * acc, lng_ref[...], lnb_ref[...])


def _ffn(x, mod_l, w1, w3, w2, ln_g, ln_b, mod_index_ffn, alpha):
    n = x.shape[0]
    whole = lambda a: pl.BlockSpec(a.shape, lambda i: (0, 0), pipeline_mode=pl.Buffered(1))
    row = pl.BlockSpec((1, D_MODEL), lambda i: (0, 0))
    return pl.pallas_call(
        functools.partial(_ffn_kernel, alpha=alpha),
        out_shape=jax.ShapeDtypeStruct((n, D_MODEL), F32),
        grid=(n // TM_FFN,),
        in_specs=[pl.BlockSpec((TM_FFN, D_MODEL), lambda i: (i, 0)),
                  pl.BlockSpec((1, 6, D_MODEL), lambda i: (mod_index_ffn(i), 0, 0)),
                  whole(w1), whole(w3), whole(w2), row, row],
        out_specs=pl.BlockSpec((TM_FFN, D_MODEL), lambda i: (i, 0)),
        compiler_params=pltpu.CompilerParams(dimension_semantics=("arbitrary",), vmem_limit_bytes=VMEM_LIMIT),
        name="ffn_ln",
    )(x, mod_l, w1, w3, w2, ln_g, ln_b)


def _moe_kernel(x_ref, mod_ref, wr_ref, br_ref, w1_ref, w3_ref, w2_ref, lng_ref, lnb_ref, y_ref,
                h_scr, comb_scr, acc_scr, *, alpha):
    e = pl.program_id(1)
    m = mod_ref[0]

    @pl.when(e == 0)
    def _():
        h = x_ref[...] * (1.0 + m[4:5]) + m[3:4]
        h_scr[...] = h.astype(BF16)
        logits = jnp.dot(h, wr_ref[...], preferred_element_type=F32, precision=lax.Precision.HIGHEST) + br_ref[...]
        lane = _iota(logits.shape, 1)
        m1 = jnp.max(logits, -1, keepdims=True)
        i1 = jnp.min(jnp.where(logits == m1, lane, 128), -1, keepdims=True)
        rest = jnp.where(lane == i1, NEG_BIG, logits)
        m2 = jnp.max(rest, -1, keepdims=True)
        i2 = jnp.min(jnp.where(rest == m2, lane, 128), -1, keepdims=True)
        e2 = jnp.exp(m2 - m1)
        g1 = 1.0 / (1.0 + e2)
        comb_scr[...] = jnp.where(lane == i1, g1, 0.0) + jnp.where(lane == i2, e2 * g1, 0.0)
        acc_scr[...] = jnp.zeros_like(acc_scr)

    h = h_scr[...]
    a = jnp.dot(h, w1_ref[0], preferred_element_type=F32)
    g = jnp.dot(h, w3_ref[0], preferred_element_type=F32)
    y = jnp.dot((_silu(a) * g).astype(BF16), w2_ref[0], preferred_element_type=F32)
    comb = comb_scr[...]
    gate = jnp.sum(jnp.where(_iota(comb.shape, 1) == e, comb, 0.0), -1, keepdims=True)
    acc_scr[...] += gate * y

    @pl.when(e == N_EXPERTS - 1)
    def _():
        y_ref[...] = _layer_norm(alpha * x_ref[...] + m[5:6] * acc_scr[...], lng_ref[...], lnb_ref[...])


def _moe(x, mod_l, w_r, b_r, w1, w3, w2, ln_g, ln_b, mod_index_moe, alpha):
    n = x.shape[0]
    row = pl.BlockSpec((1, D_MODEL), lambda i, e: (0, 0))
    expert = lambda a: pl.BlockSpec((1,) + a.shape[1:], lambda i, e: (e, 0, 0))
    return pl.pallas_call(
        functools.partial(_moe_kernel, alpha=alpha),
        out_shape=jax.ShapeDtypeStruct((n, D_MODEL), F32),
        grid=(n // TM_MOE, N_EXPERTS),
        in_specs=[pl.BlockSpec((TM_MOE, D_MODEL), lambda i, e: (i, 0)),
                  pl.BlockSpec((1, 6, D_MODEL), lambda i, e: (mod_index_moe(i), 0, 0)),
                  pl.BlockSpec((D_MODEL, 128), lambda i, e: (0, 0)),
                  pl.BlockSpec((1, 128), lambda i, e: (0, 0)),
                  expert(w1), expert(w3), expert(w2), row, row],
        out_specs=pl.BlockSpec((TM_MOE, D_MODEL), lambda i, e: (i, 0)),
        scratch_shapes=[pltpu.VMEM((TM_MOE, D_MODEL), BF16), pltpu.VMEM((TM_MOE, 128), F32),
                        pltpu.VMEM((TM_MOE, D_MODEL), F32)],
        compiler_params=pltpu.CompilerParams(dimension_semantics=("arbitrary", "arbitrary"),
                                             vmem_limit_bytes=VMEM_LIMIT),
        name="moe_ln",
    )(x, mod_l, w_r, b_r, w1, w3, w2, ln_g, ln_b)


def _rope_tables(lat_len):
    t = jnp.arange(lat_len)
    pos = jnp.stack([t // GRID_W, t % GRID_W], axis=1).astype(F32)
    half, quarter = HEAD_DIM // 2, HEAD_DIM // 4
    inv_freq = ROPE_THETA ** (-jnp.arange(quarter, dtype=F32) * 2.0 / half)
    lane = np.arange(HEAD_DIM)
    ang = pos[:, lane // half] * inv_freq[lane % quarter][None, :]
    sign = jnp.asarray(np.where((lane % half) < quarter, -1.0, 1.0), F32)
    cos = jnp.tile(jnp.cos(ang), (1, 128 // HEAD_DIM))
    sin = jnp.tile(jnp.sin(ang) * sign, (1, 128 // HEAD_DIM))
    return cos, sin


def kernel(x_prompt, x_sample, cache_k, cache_v, state_gdn, state_hgrn, c, c_ctx, w_mod, b_mod, w_in, conv_w, attn_sink, gdn_a_log, gdn_dt_bias, gdn_norm_w, hgrn_lb, hgrn_norm_w, w_out, ln_g, ln_b, ffn_w1, ffn_w3, ffn_w2, moe_router, moe_router_b, moe_w1, moe_w3, moe_w2):
    depth = w_in.shape[0]
    n_ctx, ctx_len, _ = x_prompt.shape
    lat_batch, lat_len, _ = x_sample.shape
    n_ctx_rows = n_ctx * ctx_len
    alpha = (2.0 * depth) ** 0.25

    x = jnp.concatenate([x_prompt.reshape(-1, D_MODEL), x_sample.reshape(-1, D_MODEL)], axis=0)

    def mod_index_for(tile):
        ctx_tiles, per_seq = n_ctx_rows // tile, lat_len // tile
        return lambda i: jnp.where(i < ctx_tiles, 0, 1 + (i - ctx_tiles) // per_seq)

    cvecs = jnp.concatenate([c_ctx[None], c], axis=0)
    mods = _modulation(cvecs, w_mod, b_mod).reshape(depth, cvecs.shape[0], 6, D_MODEL)

    cos_tab, sin_tab = _rope_tables(lat_len)
    sm = jax.nn.softmax(hgrn_lb.astype(F32), axis=0)
    lower = jnp.cumsum(sm, axis=0) - sm[0:1]

    w_in_r = jnp.concatenate([w_in[:, :, :1792], w_in[:, :, 1808:], w_in[:, :, 1792:1808],
                              jnp.zeros((depth, D_MODEL, C_END - 3088), F32)], axis=-1).astype(BF16)
    w_out_b = w_out.astype(BF16)

    zeros_state = jnp.zeros((n_ctx, HEAD_DIM, D_REC), F32)
    sched = [_scan_schedule(n_ctx, ctx_len, lat_batch, lat_len, reverse) for reverse in (False, True)]
    new_k, new_v, new_sg, new_sh = [], [], [], []

    for l in range(depth):
        qa, ka, va, qkvb, g_b, hg, ba = _in_proj(x, mods[l], w_in_r[l], cos_tab, sin_tab, mod_index_for(TM),
                                                 n_ctx_rows // TM, lat_len // TM)
        new_k.append(ka[:n_ctx_rows].reshape(n_ctx, ctx_len, A_KV_HEADS, HEAD_DIM))
        new_v.append(va[:n_ctx_rows].reshape(n_ctx, ctx_len, A_KV_HEADS, HEAD_DIM))

        o_a = _attention(qa, ka, va, cache_k[:, l].reshape(lat_batch, -1, D_KV), cache_v[:, l].reshape(lat_batch, -1, D_KV),
                         attn_sink[l], n_ctx, ctx_len, lat_batch, lat_len)

        sg0 = state_gdn[:, l].transpose(0, 1, 3, 2, 4).reshape(lat_batch, 2, HEAD_DIM, D_REC)
        sh0 = state_hgrn[:, l].transpose(0, 1, 4, 2, 3).reshape(lat_batch, 2, HEAD_DIM, D_REC)
        ob, oc, sg_fin, sh_fin = [], [], [], []
        alog_row = jnp.zeros((1, 128), F32).at[0, 8:16].set(gdn_a_log[l].reshape(-1))
        dtb_row = jnp.zeros((1, 128), F32).at[0, 8:16].set(gdn_dt_bias[l].reshape(-1))
        qkv, bl = _gdn_prep(qkvb, ba, conv_w[l], alog_row, dtb_row, n_ctx_rows // TM, ctx_len // TM, lat_len // TM)
        for d, reverse in enumerate((False, True)):
            o_d, s_fin = _gdn_scan(qkv, bl, jnp.concatenate([zeros_state, sg0[:, d]], axis=0), sched[d], reverse, d)
            ob.append(o_d)
            sg_fin.append(s_fin[:n_ctx].reshape(n_ctx, HEAD_DIM, N_HEADS_REC, HEAD_DIM).transpose(0, 2, 1, 3))

            o_d, s_fin = _hgrn_scan(hg, lower[l, d][None], jnp.concatenate([zeros_state, sh0[:, d]], axis=0),
                                    sched[d], reverse)
            oc.append(o_d)
            sh_fin.append(s_fin[:n_ctx].reshape(n_ctx, HEAD_DIM, N_HEADS_REC, HEAD_DIM).transpose(0, 2, 3, 1))
        new_sg.append(jnp.stack(sg_fin, axis=1))
        new_sh.append(jnp.stack(sh_fin, axis=1))

        nw_b = jnp.tile(gdn_norm_w[l], N_HEADS_REC)[None]
        nw_c = jnp.tile(hgrn_norm_w[l], N_HEADS_REC)[None]
        x = _out_proj(x, mods[l], o_a, ob[0], ob[1], oc[0], oc[1], g_b, hg, nw_b, nw_c, w_out_b[l],
                      ln_g[l, 0][None], ln_b[l, 0][None], mod_index_for(TM), alpha)

        i = l // 2
        if l % 2 == 0:
            x = _ffn(x, mods[l], ffn_w1[i].astype(BF16), ffn_w3[i].astype(BF16), ffn_w2[i].astype(BF16),
                     ln_g[l, 1][None], ln_b[l, 1][None], mod_index_for(TM_FFN), alpha)
        else:
            w_r = jnp.zeros((D_MODEL, 128), F32).at[:, :N_EXPERTS].set(moe_router[i])
            b_r = jnp.full((1, 128), NEG_BIG, F32).at[0, :N_EXPERTS].set(moe_router_b[i])
            x = _moe(x, mods[l], w_r, b_r, moe_w1[i].astype(BF16), moe_w3[i].astype(BF16), moe_w2[i].astype(BF16),
                     ln_g[l, 1][None], ln_b[l, 1][None], mod_index_for(TM_MOE), alpha)

    y_prompt = x[:n_ctx_rows].reshape(x_prompt.shape)
    y_sample = x[n_ctx_rows:].reshape(x_sample.shape)
    return (y_prompt, y_sample, jnp.stack(new_k, axis=1), jnp.stack(new_v, axis=1),
            jnp.stack(new_sg, axis=1), jnp.stack(new_sh, axis=1))
```

```python
import functools

import numpy as np
import jax
import jax.numpy as jnp
from jax import lax
from jax.experimental import pallas as pl
from jax.experimental.pallas import tpu as pltpu

F32 = jnp.float32
BF16 = jnp.bfloat16

D_MODEL = 1024
HEAD_DIM = 64
A_HEADS = 8
A_KV_HEADS = 2
GRID_W = 64
ATTN_BLOCK = 128
N_HEADS_REC = 4
D_REC = N_HEADS_REC * HEAD_DIM
CONV_W = 5
CHUNK = 64
N_LEVELS = 6
D_A = A_HEADS * HEAD_DIM
D_KV = A_KV_HEADS * HEAD_DIM
D_FF = 2816
N_EXPERTS = 8
EXPERT_FF = 1024
ROPE_THETA = 10000.0
NORM_EPS = 1e-5
TINY = 1e-30
NEG_BIG = -1e30

C_QA, C_KA, C_VA, C_QKVB, C_GB, C_HG, C_BA, C_END = 0, 512, 640, 768, 1536, 1792, 3072, 3200
D_HG = 1280
HALO = 8

TM = 512
TS = 256
TM_FFN = 512
FF_CHUNK = 256
TM_MOE = 1024
VMEM_LIMIT = 56 * 1024 * 1024

NN = (((1,), (0,)), ((), ()))
NT = (((1,), (1,)), ((), ()))
TN = (((0,), (0,)), ((), ()))


def _dot(a, b, dims=NN):
    return lax.dot_general(a.astype(BF16), b.astype(BF16), dims, preferred_element_type=F32)


def _split2(x):
    hi = x.astype(BF16)
    lo = (x - hi.astype(F32)).astype(BF16)
    return hi, lo


def _split3(x):
    hi = x.astype(BF16)
    r = x - hi.astype(F32)
    mid = r.astype(BF16)
    lo = (r - mid.astype(F32)).astype(BF16)
    return hi, mid, lo


def _dot_sel(sel, x, dims=NN, sel_is_lhs=True):
    out = None
    for part in _split3(x):
        a, b = (sel, part) if sel_is_lhs else (part, sel)
        t = lax.dot_general(a, b, dims, preferred_element_type=F32)
        out = t if out is None else out + t
    return out


def _sigmoid(x):
    return 1.0 / (1.0 + jnp.exp(-x))


def _silu(x):
    return x * _sigmoid(x)


def _layer_norm(x, g, b):
    mu = jnp.mean(x, -1, keepdims=True)
    xc = x - mu
    var = jnp.mean(xc * xc, -1, keepdims=True)
    return xc * lax.rsqrt(var + NORM_EPS) * g + b


def _iota(shape, dim):
    return lax.broadcasted_iota(jnp.int32, shape, dim)


def _head_ones():
    return jnp.where(_iota((D_REC, D_REC), 0) // HEAD_DIM == _iota((D_REC, D_REC), 1) // HEAD_DIM, 1.0, 0.0).astype(BF16)


def _block_diag(x):
    head = _iota(x.shape, 1) // HEAD_DIM
    return jnp.concatenate([jnp.where(head == h, x, jnp.zeros_like(x)) for h in range(N_HEADS_REC)], axis=0)


def _block_diag_mask():
    return _iota((D_REC, D_REC), 0) // HEAD_DIM == _iota((D_REC, D_REC), 1) // HEAD_DIM


def _collapse_block_diag(s):
    out = s[0:HEAD_DIM]
    for h in range(1, N_HEADS_REC):
        out = out + s[h * HEAD_DIM:(h + 1) * HEAD_DIM]
    return out


def _mod_kernel(ct_ref, w_ref, b_ref, o_ref):
    s = _silu(ct_ref[...])
    w = w_ref[0]
    rows = [jnp.sum(s[:, r:r + 1] * w, axis=0, keepdims=True) for r in range(o_ref.shape[1])]
    o_ref[0] = jnp.concatenate(rows, axis=0) + b_ref[0]


def _modulation(cvecs, w_mod, b_mod):
    depth, d, n6 = w_mod.shape
    r = cvecs.shape[0]
    tn = 1024
    ct = jnp.zeros((d, 8), F32).at[:, :r].set(cvecs.T)
    return pl.pallas_call(
        _mod_kernel,
        out_shape=jax.ShapeDtypeStruct((depth, r, n6), F32),
        grid=(depth, n6 // tn),
        in_specs=[pl.BlockSpec((d, 8), lambda l, j: (0, 0)),
                  pl.BlockSpec((1, d, tn), lambda l, j: (l, 0, j)),
                  pl.BlockSpec((1, 1, tn), lambda l, j: (l, 0, j))],
        out_specs=pl.BlockSpec((1, r, tn), lambda l, j: (l, 0, j)),
        compiler_params=pltpu.CompilerParams(dimension_semantics=("arbitrary", "arbitrary"),
                                             vmem_limit_bytes=VMEM_LIMIT),
        name="modulation",
    )(ct, w_mod, b_mod.reshape(depth, 1, n6))


def _rope(x, cos, sin):
    outs = []
    lane = _iota(cos.shape, 1)
    first = (lane % 32) < 16
    for m in range(x.shape[1] // 128):
        xs = x[:, m * 128:(m + 1) * 128]
        swapped = jnp.where(first, pltpu.roll(xs, 128 - 16, 1), pltpu.roll(xs, 16, 1))
        outs.append(xs * cos + swapped * sin)
    return outs[0] if len(outs) == 1 else jnp.concatenate(outs, axis=1)


def _token_specs(x_parts, n_ctx_tiles):
    if len(x_parts) == 1:
        return [pl.BlockSpec((TM, D_MODEL), lambda i: (i, 0))]
    return [pl.BlockSpec((TM, D_MODEL), lambda i: (jnp.minimum(i, n_ctx_tiles - 1), 0)),
            pl.BlockSpec((TM, D_MODEL), lambda i: (jnp.maximum(i - n_ctx_tiles, 0), 0))]


def _read_tokens(x_refs, n_ctx_tiles):
    if len(x_refs) == 1:
        return x_refs[0][...]
    return jnp.where(pl.program_id(0) < n_ctx_tiles, x_refs[0][...], x_refs[1][...])


def _in_proj_kernel(*refs, n_ctx_tiles, n_x):
    x_refs = refs[:n_x]
    mod_ref, w_ref, cos_ref, sin_ref, qa_ref, ka_ref, va_ref, qkvb_ref, gb_ref, hg_ref, ba_ref = refs[n_x:]
    i = pl.program_id(0)
    m = mod_ref[0]
    h = (_read_tokens(x_refs, n_ctx_tiles) * (1.0 + m[1:2]) + m[0:1]).astype(BF16)

    def proj(lo, hi):
        return jnp.dot(h, w_ref[:, lo:hi], preferred_element_type=F32)

    qa = proj(C_QA, C_KA)
    ka = proj(C_KA, C_VA)

    @pl.when(i < n_ctx_tiles)
    def _():
        qa_ref[...] = qa
        ka_ref[...] = ka

    @pl.when(i >= n_ctx_tiles)
    def _():
        cos, sin = cos_ref[...], sin_ref[...]
        qa_ref[...] = _rope(qa, cos, sin)
        ka_ref[...] = _rope(ka, cos, sin)

    va_ref[...] = proj(C_VA, C_QKVB)
    qkvb_ref[...] = proj(C_QKVB, C_GB)
    gb_ref[...] = proj(C_GB, C_HG)
    hg_ref[...] = proj(C_HG, C_BA)
    ba_ref[...] = proj(C_BA, C_END)


def _in_proj(x_parts, mod_l, w_in_r, cos_tab, sin_tab, mod_index, n_ctx_tiles, lat_tiles_per_seq):
    n = sum(p.shape[0] for p in x_parts)

    def tab_index(i):
        return (jnp.where(i < n_ctx_tiles, 0, (i - n_ctx_tiles) % lat_tiles_per_seq), 0)

    def out(width):
        return jax.ShapeDtypeStruct((n, width), F32), pl.BlockSpec((TM, width), lambda i: (i, 0))

    outs = [out(D_A), out(D_KV), out(D_KV), out(3 * D_REC), out(D_REC), out(D_HG), out(128)]
    return pl.pallas_call(
        functools.partial(_in_proj_kernel, n_ctx_tiles=n_ctx_tiles, n_x=len(x_parts)),
        out_shape=[o[0] for o in outs],
        grid=(n // TM,),
        in_specs=_token_specs(x_parts, n_ctx_tiles) + [
            pl.BlockSpec((1, 6, D_MODEL), lambda i: (mod_index(i), 0, 0)),
            pl.BlockSpec((D_MODEL, C_END), lambda i: (0, 0)),
            pl.BlockSpec((TM, 128), tab_index),
            pl.BlockSpec((TM, 128), tab_index)],
        out_specs=[o[1] for o in outs],
        compiler_params=pltpu.CompilerParams(dimension_semantics=("arbitrary",), vmem_limit_bytes=VMEM_LIMIT),
        name="in_proj",
    )(*x_parts, mod_l, w_in_r, cos_tab, sin_tab)


def _attn_heads(q_ref, k, v, bias, sink_ref, o_ref):
    k_same, k_swap = k.astype(BF16), pltpu.roll(k, HEAD_DIM, 1).astype(BF16)
    v_same, v_swap = v.astype(BF16), pltpu.roll(v, HEAD_DIM, 1).astype(BF16)
    nq = q_ref.shape[0]
    low_half = _iota((nq, 128), 1) < HEAD_DIM
    rep = A_HEADS // A_KV_HEADS
    for p in range(A_HEADS // 2):
        q2 = q_ref[:, p * 128:(p + 1) * 128] * (HEAD_DIM ** -0.5)
        halves = []
        for e in range(2):
            hq = 2 * p + e
            g = hq // rep
            qm = jnp.where(low_half if e == 0 else jnp.logical_not(low_half), q2, 0.0)
            s = _dot(qm, k_same if e == g else k_swap, NT)
            if bias is not None:
                s = s + bias
            snk = sink_ref[hq]
            mx = jnp.maximum(jnp.max(s, -1, keepdims=True), snk)
            ex = jnp.exp(s - mx)
            den = jnp.sum(ex, -1, keepdims=True) + jnp.exp(snk - mx)
            halves.append(_dot(ex, v_same if e == g else v_swap) / den)
        o_ref[:, p * 128:(p + 1) * 128] = jnp.where(low_half, halves[0], halves[1])


def _attn_kernel(q_ref, kx_ref, vx_ref, kp_ref, kc_ref, kn_ref, vp_ref, vc_ref, vn_ref, ck_ref, cv_ref, sink_ref,
                 o_ref, *, ctx_steps, n_blocks):
    i = pl.program_id(0)

    @pl.when(i < ctx_steps)
    def _():
        _attn_heads(q_ref, kx_ref[...], vx_ref[...], None, sink_ref, o_ref)

    @pl.when(i >= ctx_steps)
    def _():
        n = (i - ctx_steps) % n_blocks
        k = jnp.concatenate([kp_ref[...], kc_ref[...], kn_ref[...], ck_ref[0]], axis=0)
        v = jnp.concatenate([vp_ref[...], vc_ref[...], vn_ref[...], cv_ref[0]], axis=0)
        nk = k.shape[0]
        qi = _iota((ATTN_BLOCK, nk), 0)
        kj = _iota((ATTN_BLOCK, nk), 1)
        seg = kj // ATTN_BLOCK
        jj = kj % ATTN_BLOCK
        bias_prev = jnp.where(jj >= qi, 0.0, NEG_BIG) + jnp.where(n > 0, 0.0, NEG_BIG)
        bias_next = jnp.where(jj <= qi, 0.0, NEG_BIG) + jnp.where(n < n_blocks - 1, 0.0, NEG_BIG)
        bias = jnp.where(seg == 0, bias_prev, jnp.where(seg == 2, bias_next, 0.0))
        _attn_heads(q_ref, k, v, bias, sink_ref, o_ref)


def _attention(qa, ka, va, ck, cv, sink, n_ctx, ctx_len, lat_batch, lat_len):
    n = qa.shape[0]
    per_seq = ctx_len // ATTN_BLOCK
    ctx_steps = n_ctx * per_seq
    nb = lat_len // ATTN_BLOCK
    past = ck.shape[1]

    def lat(i):
        return jnp.maximum(i - ctx_steps, 0)

    def cur(i):
        return (ctx_steps + lat(i), 0)

    def prev(i):
        return (ctx_steps + (lat(i) // nb) * nb + jnp.maximum(lat(i) % nb - 1, 0), 0)

    def nxt(i):
        return (ctx_steps + (lat(i) // nb) * nb + jnp.minimum(lat(i) % nb + 1, nb - 1), 0)

    def own_seq(i):
        return (jnp.minimum(i // per_seq, n_ctx - 1), 0)

    kv = lambda f: pl.BlockSpec((ATTN_BLOCK, D_KV), f)
    cache = pl.BlockSpec((1, past, D_KV), lambda i: (lat(i) // nb, 0, 0))
    return pl.pallas_call(
        functools.partial(_attn_kernel, ctx_steps=ctx_steps, n_blocks=nb),
        out_shape=jax.ShapeDtypeStruct((n, D_A), F32),
        grid=(ctx_steps + lat_batch * nb,),
        in_specs=[pl.BlockSpec((ATTN_BLOCK, D_A), lambda i: (i, 0)),
                  pl.BlockSpec((ctx_len, D_KV), own_seq), pl.BlockSpec((ctx_len, D_KV), own_seq),
                  kv(prev), kv(cur), kv(nxt), kv(prev), kv(cur), kv(nxt), cache, cache,
                  pl.BlockSpec(memory_space=pltpu.SMEM)],
        out_specs=pl.BlockSpec((ATTN_BLOCK, D_A), lambda i: (i, 0)),
        compiler_params=pltpu.CompilerParams(dimension_semantics=("arbitrary",), vmem_limit_bytes=VMEM_LIMIT),
        name="attention",
    )(qa, ka, va, ka, ka, ka, va, va, va, ck, cv, sink)


def _tri(reverse, strict=False):
    i, j = _iota((CHUNK, CHUNK), 0), _iota((CHUNK, CHUNK), 1)
    if strict:
        return (i < j) if reverse else (i > j)
    return (i <= j) if reverse else (i >= j)


def _gdn_prep_kernel(cur_ref, prev_ref, next_ref, ba_ref, convw_ref, alog_ref, dtb_ref, qkv_ref, bl_ref,
                     *, ctx_tiles, ctx_tiles_per_seq, lat_tiles_per_seq):
    i = pl.program_id(0)
    in_ctx = i < ctx_tiles
    t_ctx = i % ctx_tiles_per_seq
    t_lat = (i - ctx_tiles) % lat_tiles_per_seq
    is_first = jnp.where(in_ctx, t_ctx == 0, t_lat == 0)
    is_last = jnp.where(in_ctx, t_ctx == ctx_tiles_per_seq - 1, t_lat == lat_tiles_per_seq - 1)
    has_prev = jnp.where(is_first, 0.0, 1.0)
    has_next = jnp.where(is_last, 0.0, 1.0)
    xh = jnp.concatenate([prev_ref[...] * has_prev, cur_ref[...], next_ref[...] * has_next], axis=0)
    cw = convw_ref[...]
    rows = cur_ref.shape[0]
    y = None
    for t in range(CONV_W):
        lo = HALO - CONV_W // 2 + t
        term = xh[lo:lo + rows] * cw[t:t + 1]
        y = term if y is None else y + term
    y = _silu(y)
    q, k, v = y[:, :D_REC], y[:, D_REC:2 * D_REC], y[:, 2 * D_REC:]
    ones = _head_ones()
    qq_hi, qq_lo = _split2(q * q)
    kk_hi, kk_lo = _split2(k * k)
    sums = jnp.dot(jnp.concatenate([qq_hi, qq_lo, kk_hi, kk_lo], axis=0), ones, preferred_element_type=F32)
    q_ss = sums[0:rows] + sums[rows:2 * rows]
    k_ss = sums[2 * rows:3 * rows] + sums[3 * rows:]
    qkv_ref[:, :D_REC] = q * lax.rsqrt(q_ss + 1e-6) * (HEAD_DIM ** -0.5)
    qkv_ref[:, D_REC:2 * D_REC] = k * lax.rsqrt(k_ss + 1e-6)
    qkv_ref[:, 2 * D_REC:] = v

    ba = ba_ref[...]
    za = ba + dtb_ref[...]
    softplus = jnp.maximum(za, 0.0) + jnp.log(1.0 + jnp.exp(-jnp.abs(za)))
    la = -jnp.exp(alog_ref[...]) * softplus
    bl_ref[...] = jnp.where(_iota(ba.shape, 1) < 2 * N_HEADS_REC, _sigmoid(ba), la)


def _gdn_prep(qkvb, ba, conv_w, alog_row, dtb_row, ctx_tiles, ctx_tiles_per_seq, lat_tiles_per_seq):
    n = qkvb.shape[0]
    per = TS // HALO
    const = lambda shape: pl.BlockSpec(shape, lambda i: (0,) * len(shape))
    return pl.pallas_call(
        functools.partial(_gdn_prep_kernel, ctx_tiles=ctx_tiles, ctx_tiles_per_seq=ctx_tiles_per_seq,
                          lat_tiles_per_seq=lat_tiles_per_seq),
        out_shape=[jax.ShapeDtypeStruct((n, 3 * D_REC), F32), jax.ShapeDtypeStruct((n, 128), F32)],
        grid=(n // TS,),
        in_specs=[pl.BlockSpec((TS, 3 * D_REC), lambda i: (i, 0)),
                  pl.BlockSpec((HALO, 3 * D_REC), lambda i: (jnp.maximum(i * per - 1, 0), 0)),
                  pl.BlockSpec((HALO, 3 * D_REC), lambda i: (jnp.minimum((i + 1) * per, n // HALO - 1), 0)),
                  pl.BlockSpec((TS, 128), lambda i: (i, 0)),
                  const((CONV_W, 3 * D_REC)), const((1, 128)), const((1, 128))],
        out_specs=[pl.BlockSpec((TS, 3 * D_REC), lambda i: (i, 0)), pl.BlockSpec((TS, 128), lambda i: (i, 0))],
        compiler_params=pltpu.CompilerParams(dimension_semantics=("arbitrary",), vmem_limit_bytes=VMEM_LIMIT),
        name="gdn_prep",
    )(qkvb, qkvb, qkvb, ba, conv_w, alog_row, dtb_row)


def _scan_schedule(n_ctx, ctx_len, lat_batch, lat_len, reverse):
    tiles_ctx, tiles_lat = ctx_len // TS, lat_len // TS
    seqs = [[s * tiles_ctx + t for t in range(tiles_ctx)] for s in range(n_ctx)]
    seqs += [[n_ctx * tiles_ctx + b * tiles_lat + t for t in range(tiles_lat)] for b in range(lat_batch)]
    cols = []
    for s, tiles in enumerate(seqs):
        order = tiles[::-1] if reverse else tiles
        for n, t in enumerate(order):
            cols.append((t, s, int(n == 0), int(n == len(order) - 1)))
    return jnp.asarray(np.array(cols, np.int32).T)


def _bd2(x):
    low = _iota(x.shape, 1) < HEAD_DIM
    zero = jnp.zeros_like(x)
    return jnp.concatenate([jnp.where(low, x, zero), jnp.where(low, zero, x)], axis=0)


def _mm3_stack(lhs_list, rhs):
    r_hi, r_lo = _split2(rhs)
    w_hi, w_lo = _bd2(r_hi), _bd2(r_lo)
    parts = [_split2(a) for a in lhs_list]
    his = [p[0] for p in parts]
    los = [p[1] for p in parts]
    n = len(lhs_list) * CHUNK
    big = jnp.dot(jnp.concatenate(his + los, axis=0), w_hi, preferred_element_type=F32)
    small = jnp.dot(his[0] if len(his) == 1 else jnp.concatenate(his, axis=0), w_lo, preferred_element_type=F32)
    out = big[:n] + big[n:] + small
    return [out[m * CHUNK:(m + 1) * CHUNK] for m in range(len(lhs_list))]


def _pick2(a, b):
    shape = (CHUNK, 128)
    return jnp.where(_iota(shape, 1) < HEAD_DIM, jnp.broadcast_to(a, shape), jnp.broadcast_to(b, shape))


def _gdn_tile_kernel(sch_ref, qkv_ref, bl_ref, s0_ref, o_ref, sfin_ref, s_scr, *, reverse, direction):
    n = pl.program_id(0)
    n_groups = N_HEADS_REC // 2

    @pl.when(sch_ref[2, n] == 1)
    def _():
        for g in range(n_groups):
            s_scr[g] = _bd2(s0_ref[0, :, g * 128:(g + 1) * 128])

    row, lane = _iota((CHUNK, 128), 0), _iota((CHUNK, 128), 1) % HEAD_DIM
    if reverse:
        incl, strict = row <= lane, row < lane
    else:
        incl, strict = row >= lane, row > lane
    eye = jnp.where(row == lane, 1.0, 0.0)
    tri = jnp.where(_tri(reverse), 1.0, 0.0).astype(BF16)
    tri_t2 = jnp.where(_iota((CHUNK, 128), 0) >= lane, 1.0, 0.0) if reverse else \
        jnp.where(_iota((CHUNK, 128), 0) <= lane, 1.0, 0.0)
    tri_t2 = tri_t2.astype(BF16)
    bd_mask = (_iota((128, 128), 0) // HEAD_DIM) == (_iota((128, 128), 1) // HEAD_DIM)
    last = 0 if reverse else CHUNK - 1
    n_chunks = TS // CHUNK
    states = [s_scr[g] for g in range(n_groups)]

    chunk_order = list(range(n_chunks - 1, -1, -1) if reverse else range(n_chunks))
    cb, ca = direction * N_HEADS_REC, 2 * N_HEADS_REC + direction * N_HEADS_REC
    inst = []
    for c in chunk_order:
        rs = slice(c * CHUNK, (c + 1) * CHUNK)
        bl = bl_ref[rs, :]
        g16, gt2 = None, None
        for part in _split3(bl):
            a = jnp.dot(tri, part, preferred_element_type=F32)
            b = lax.dot_general(part, tri_t2, TN, preferred_element_type=F32)
            g16 = a if g16 is None else g16 + a
            gt2 = b if gt2 is None else gt2 + b
        for g in range(n_groups):
            h0, h1 = 2 * g, 2 * g + 1
            ls = slice(g * 128, (g + 1) * 128)
            d = dict(c=c, g=g, rs=rs, ls=ls)
            d["q"] = qkv_ref[rs, ls]
            d["k"] = qkv_ref[rs, D_REC + g * 128:D_REC + (g + 1) * 128]
            d["v"] = qkv_ref[rs, 2 * D_REC + g * 128:2 * D_REC + (g + 1) * 128]
            d["beta"] = _pick2(bl[:, cb + h0:cb + h0 + 1], bl[:, cb + h1:cb + h1 + 1])
            d["g_i"] = _pick2(g16[:, ca + h0:ca + h0 + 1], g16[:, ca + h1:ca + h1 + 1])
            g_j = _pick2(gt2[ca + h0:ca + h0 + 1, :], gt2[ca + h1:ca + h1 + 1, :])
            d["g_tot"] = d["g_i"][last:last + 1]
            d["decay"] = jnp.where(incl, jnp.exp(jnp.where(incl, d["g_i"] - g_j, 0.0)), 0.0)
            inst.append(d)

    for d in inst:
        gram = _dot(jnp.concatenate([d["k"], d["q"]], axis=0), _bd2(d["k"].astype(BF16)), NT)
        d["low"] = jnp.where(strict, d["beta"] * gram[:CHUNK] * d["decay"], 0.0)
        d["attn"] = gram[CHUNK:] * d["decay"]

    for d in inst:
        d["t"] = eye - d["low"]
        (d["p"],) = _mm3_stack([d["low"]], d["low"])
    for _ in range(4):
        for d in inst:
            p_next, tp = _mm3_stack([d["p"], d["t"]], d["p"])
            d["t"] = d["t"] + tp
            d["p"] = p_next
    for d in inst:
        t_mat = d["t"] + _mm3_stack([d["t"]], d["p"])[0]
        eg = jnp.exp(d["g_i"])
        vb = d["v"] * d["beta"]
        kbg = d["k"] * d["beta"] * eg
        d["u"] = _dot(t_mat, _bd2(vb.astype(BF16)))
        d["w_qg"] = jnp.concatenate([_dot(t_mat, _bd2(kbg.astype(BF16))), d["q"] * eg], axis=0).astype(BF16)
        d["k_dec"] = (d["k"] * jnp.exp(d["g_tot"] - d["g_i"])).astype(BF16)
        d["e_tot"] = jnp.exp(d["g_tot"])

    for d in inst:
        s = states[d["g"]]
        ws_qs = _dot(d["w_qg"], s)
        v_new = d["u"] - ws_qs[:CHUNK]
        o_ref[d["rs"], d["ls"]] = ws_qs[CHUNK:] + _dot(d["attn"], _bd2(v_new.astype(BF16)))
        states[d["g"]] = s * d["e_tot"] + jnp.where(bd_mask, _dot(d["k_dec"], v_new, TN), 0.0)

    for g in range(n_groups):
        s_scr[g] = states[g]

    @pl.when(sch_ref[3, n] == 1)
    def _():
        for g in range(n_groups):
            sfin_ref[0, :, g * 128:(g + 1) * 128] = states[g][:HEAD_DIM] + states[g][HEAD_DIM:]


def _gdn_pipe_kernel(sch_ref, qkv_ref, bl_ref, s0_ref, o_ref, sfin_ref, s_scr, *sets, reverse, direction, n_steps):
    n = pl.program_id(0)
    n_groups = N_HEADS_REC // 2
    n_chunks = TS // CHUNK
    n_inst = n_chunks * n_groups
    prev = jnp.maximum(n - 1, 0)
    chain_live = n >= 1
    half = len(sets) // 2

    @pl.when(n == 0)
    def _():
        for ref in sets:
            ref[...] = jnp.zeros(ref.shape, ref.dtype)
        s_scr[...] = jnp.zeros(s_scr.shape, F32)

    @pl.when(jnp.logical_and(chain_live, sch_ref[2, prev] == 1))
    def _():
        for g in range(n_groups):
            s_scr[g] = _bd2(s0_ref[0, :, g * 128:(g + 1) * 128])

    row, lane = _iota((CHUNK, 128), 0), _iota((CHUNK, 128), 1) % HEAD_DIM
    if reverse:
        incl, strict = row <= lane, row < lane
    else:
        incl, strict = row >= lane, row > lane
    eye = jnp.where(row == lane, 1.0, 0.0)
    tri = jnp.where(_tri(reverse), 1.0, 0.0).astype(BF16)
    tri_t2 = jnp.where((row >= lane) if reverse else (row <= lane), 1.0, 0.0).astype(BF16)
    bd_mask = (_iota((128, 128), 0) // HEAD_DIM) == (_iota((128, 128), 1) // HEAD_DIM)
    last = 0 if reverse else CHUNK - 1
    chunk_order = list(range(n_chunks - 1, -1, -1) if reverse else range(n_chunks))
    cb, ca = direction * N_HEADS_REC, 2 * N_HEADS_REC + direction * N_HEADS_REC

    def step(wr, rd):
        u_w, wq_w, at_w, kd_w, et_w = wr
        u_r, wq_r, at_r, kd_r, et_r = rd
        states = [s_scr[g] for g in range(n_groups)]
        inst = []

        def chain(m):
            c, g = chunk_order[m // n_groups], m % n_groups
            rs, ls = slice(c * CHUNK, (c + 1) * CHUNK), slice(g * 128, (g + 1) * 128)
            s = states[g]
            ws_qs = _dot(wq_r[m], s)
            v_new = u_r[m] - ws_qs[:CHUNK]
            o_ref[rs, ls] = ws_qs[CHUNK:] + _dot(at_r[m], _bd2(v_new.astype(BF16)))
            states[g] = s * et_r[m][0:1] + jnp.where(bd_mask, _dot(kd_r[m], v_new, TN), 0.0)

        for c in chunk_order:
            rs = slice(c * CHUNK, (c + 1) * CHUNK)
            bl = bl_ref[rs, :]
            g16, gt2 = None, None
            for part in _split3(bl):
                a = jnp.dot(tri, part, preferred_element_type=F32)
                b = lax.dot_general(part, tri_t2, TN, preferred_element_type=F32)
                g16 = a if g16 is None else g16 + a
                gt2 = b if gt2 is None else gt2 + b
            for g in range(n_groups):
                h0, h1 = 2 * g, 2 * g + 1
                ls = slice(g * 128, (g + 1) * 128)
                d = dict(q=qkv_ref[rs, ls], k=qkv_ref[rs, D_REC + g * 128:D_REC + (g + 1) * 128],
                         v=qkv_ref[rs, 2 * D_REC + g * 128:2 * D_REC + (g + 1) * 128])
                d["beta"] = _pick2(bl[:, cb + h0:cb + h0 + 1], bl[:, cb + h1:cb + h1 + 1])
                d["g_i"] = _pick2(g16[:, ca + h0:ca + h0 + 1], g16[:, ca + h1:ca + h1 + 1])
                g_j = _pick2(gt2[ca + h0:ca + h0 + 1, :], gt2[ca + h1:ca + h1 + 1, :])
                d["g_tot"] = d["g_i"][last:last + 1]
                d["decay"] = jnp.where(incl, jnp.exp(jnp.where(incl, d["g_i"] - g_j, 0.0)), 0.0)
                inst.append(d)
        chain(0)
        chain(1)

        for d in inst:
            gram = _dot(jnp.concatenate([d["k"], d["q"]], axis=0), _bd2(d["k"].astype(BF16)), NT)
            d["low"] = jnp.where(strict, d["beta"] * gram[:CHUNK] * d["decay"], 0.0)
            d["attn"] = gram[CHUNK:] * d["decay"]

        for d in inst:
            d["t"] = eye - d["low"]
            (d["p"],) = _mm3_stack([d["low"]], d["low"])
        chain(2)
        chain(3)
        for it in range(4):
            for d in inst:
                p_next, tp = _mm3_stack([d["p"], d["t"]], d["p"])
                d["t"] = d["t"] + tp
                d["p"] = p_next
            if it < (n_inst - 4) // 2:
                chain(4 + 2 * it)
                chain(5 + 2 * it)
        for m, d in enumerate(inst):
            t_mat = d["t"] + _mm3_stack([d["t"]], d["p"])[0]
            eg = jnp.exp(d["g_i"])
            vb = d["v"] * d["beta"]
            kbg = d["k"] * d["beta"] * eg
            u_w[m] = _dot(t_mat, _bd2(vb.astype(BF16)))
            wq_w[m] = jnp.concatenate([_dot(t_mat, _bd2(kbg.astype(BF16))), d["q"] * eg], axis=0).astype(BF16)
            at_w[m] = d["attn"]
            kd_w[m] = (d["k"] * jnp.exp(d["g_tot"] - d["g_i"])).astype(BF16)
            et_w[m] = jnp.broadcast_to(jnp.exp(d["g_tot"]), (8, 128))

        for g in range(n_groups):
            s_scr[g] = states[g]

    @pl.when(n % 2 == 0)
    def _():
        step(sets[:half], sets[half:])

    @pl.when(n % 2 == 1)
    def _():
        step(sets[half:], sets[:half])

    @pl.when(jnp.logical_and(chain_live, sch_ref[3, prev] == 1))
    def _():
        for g in range(n_groups):
            sfin_ref[0, :, g * 128:(g + 1) * 128] = s_scr[g, :HEAD_DIM, :] + s_scr[g, HEAD_DIM:, :]


def _mm3_heads(lhs_list, rhs):
    r_hi, r_lo = _split2(rhs)
    w_hi, w_lo = _block_diag(r_hi), _block_diag(r_lo)
    parts = [_split2(a) for a in lhs_list]
    his = [p[0] for p in parts]
    los = [p[1] for p in parts]
    n = len(lhs_list) * CHUNK
    big = jnp.dot(jnp.concatenate(his + los, axis=0), w_hi, preferred_element_type=F32)
    small = jnp.dot(his[0] if len(his) == 1 else jnp.concatenate(his, axis=0), w_lo, preferred_element_type=F32)
    out = big[:n] + big[n:] + small
    return [out[m * CHUNK:(m + 1) * CHUNK] for m in range(len(lhs_list))]


def _pick_heads(vals):
    shape = (CHUNK, D_REC)
    head = _iota(shape, 1) // HEAD_DIM
    out = jnp.broadcast_to(vals[-1], shape)
    for h in range(N_HEADS_REC - 2, -1, -1):
        out = jnp.where(head == h, jnp.broadcast_to(vals[h], shape), out)
    return out


def _gdn_scan_kernel(sch_ref, qkv_ref, bl_ref, s0_ref, o_ref, sfin_ref, s_scr, *, reverse, direction):
    n = pl.program_id(0)

    @pl.when(sch_ref[2, n] == 1)
    def _():
        s_scr[...] = _block_diag(s0_ref[0])

    row, lane = _iota((CHUNK, D_REC), 0), _iota((CHUNK, D_REC), 1) % HEAD_DIM
    if reverse:
        incl, strict = row <= lane, row < lane
    else:
        incl, strict = row >= lane, row > lane
    eye = jnp.where(row == lane, 1.0, 0.0)
    tri = jnp.where(_tri(reverse), 1.0, 0.0).astype(BF16)
    tri_t = jnp.where((row >= lane) if reverse else (row <= lane), 1.0, 0.0).astype(BF16)
    bd_mask = _block_diag_mask()
    last = 0 if reverse else CHUNK - 1
    n_chunks = TS // CHUNK
    cb, ca = direction * N_HEADS_REC, 2 * N_HEADS_REC + direction * N_HEADS_REC

    inst = []
    for c in (range(n_chunks - 1, -1, -1) if reverse else range(n_chunks)):
        rs = slice(c * CHUNK, (c + 1) * CHUNK)
        bl = bl_ref[rs, :]
        g16, gt = None, None
        for part in _split3(bl):
            a = jnp.dot(tri, part, preferred_element_type=F32)
            b = lax.dot_general(part, tri_t, TN, preferred_element_type=F32)
            g16 = a if g16 is None else g16 + a
            gt = b if gt is None else gt + b
        d = dict(rs=rs, q=qkv_ref[rs, :D_REC], k=qkv_ref[rs, D_REC:2 * D_REC], v=qkv_ref[rs, 2 * D_REC:])
        d["beta"] = _pick_heads([bl[:, cb + h:cb + h + 1] for h in range(N_HEADS_REC)])
        d["g_i"] = _pick_heads([g16[:, ca + h:ca + h + 1] for h in range(N_HEADS_REC)])
        g_j = _pick_heads([gt[ca + h:ca + h + 1, :] for h in range(N_HEADS_REC)])
        d["g_tot"] = d["g_i"][last:last + 1]
        d["decay"] = jnp.where(incl, jnp.exp(jnp.where(incl, d["g_i"] - g_j, 0.0)), 0.0)
        inst.append(d)

    for d in inst:
        gram = _dot(jnp.concatenate([d["k"], d["q"]], axis=0), _block_diag(d["k"].astype(BF16)), NT)
        d["low"] = jnp.where(strict, d["beta"] * gram[:CHUNK] * d["decay"], 0.0)
        d["attn"] = gram[CHUNK:] * d["decay"]

    for d in inst:
        d["t"] = eye - d["low"]
        (d["p"],) = _mm3_heads([d["low"]], d["low"])
    for _ in range(4):
        for d in inst:
            p_next, tp = _mm3_heads([d["p"], d["t"]], d["p"])
            d["t"] = d["t"] + tp
            d["p"] = p_next
    for d in inst:
        t_mat = d["t"] + _mm3_heads([d["t"]], d["p"])[0]
        eg = jnp.exp(d["g_i"])
        vb = d["v"] * d["beta"]
        kbg = d["k"] * d["beta"] * eg
        d["u"] = _dot(t_mat, _block_diag(vb.astype(BF16)))
        d["w_qg"] = jnp.concatenate([_dot(t_mat, _block_diag(kbg.astype(BF16))), d["q"] * eg], axis=0).astype(BF16)
        d["k_dec"] = (d["k"] * jnp.exp(d["g_tot"] - d["g_i"])).astype(BF16)
        d["e_tot"] = jnp.exp(d["g_tot"])

    s = s_scr[...]
    for d in inst:
        ws_qs = _dot(d["w_qg"], s)
        v_new = d["u"] - ws_qs[:CHUNK]
        o_ref[d["rs"], :] = ws_qs[CHUNK:] + _dot(d["attn"], _block_diag(v_new.astype(BF16)))
        s = s * d["e_tot"] + jnp.where(bd_mask, _dot(d["k_dec"], v_new, TN), 0.0)
    s_scr[...] = s

    @pl.when(sch_ref[3, n] == 1)
    def _():
        sfin_ref[0] = _collapse_block_diag(s)


def _gdn_pipe4_kernel(sch_ref, qkv_ref, bl_ref, s0_ref, o_ref, sfin_ref, s_scr, *sets, reverse, direction):
    n = pl.program_id(0)
    n_chunks = TS // CHUNK
    prev = jnp.maximum(n - 1, 0)
    chain_live = n >= 1
    half = len(sets) // 2

    @pl.when(n == 0)
    def _():
        for ref in sets:
            ref[...] = jnp.zeros(ref.shape, ref.dtype)
        s_scr[...] = jnp.zeros(s_scr.shape, F32)

    @pl.when(jnp.logical_and(chain_live, sch_ref[2, prev] == 1))
    def _():
        s_scr[...] = _block_diag(s0_ref[0])

    row, lane = _iota((CHUNK, D_REC), 0), _iota((CHUNK, D_REC), 1) % HEAD_DIM
    if reverse:
        incl, strict = row <= lane, row < lane
    else:
        incl, strict = row >= lane, row > lane
    eye = jnp.where(row == lane, 1.0, 0.0)
    tri = jnp.where(_tri(reverse), 1.0, 0.0).astype(BF16)
    t_before_j = (row >= lane) if reverse else (row <= lane)
    ones8 = jnp.ones((8, CHUNK), BF16)
    bd_mask = _block_diag_mask()
    last = 0 if reverse else CHUNK - 1
    chunk_order = list(range(n_chunks - 1, -1, -1) if reverse else range(n_chunks))
    cb, ca = direction * N_HEADS_REC, 2 * N_HEADS_REC + direction * N_HEADS_REC

    def step(wr, rd):
        u_w, wq_w, at_w, kd_w, et_w = wr
        u_r, wq_r, at_r, kd_r, et_r = rd
        state = [s_scr[...]]
        inst = []

        def chain(m):
            c = chunk_order[m]
            rs = slice(c * CHUNK, (c + 1) * CHUNK)
            s = state[0]
            ws_qs = _dot(wq_r[m], s)
            v_new = u_r[m] - ws_qs[:CHUNK]
            o_ref[rs, :] = ws_qs[CHUNK:] + _dot(at_r[m], _block_diag(v_new.astype(BF16)))
            state[0] = s * et_r[m][0:1] + jnp.where(bd_mask, _dot(kd_r[m], v_new, TN), 0.0)

        for c in chunk_order:
            rs = slice(c * CHUNK, (c + 1) * CHUNK)
            bl = bl_ref[rs, :]
            d = dict(q=qkv_ref[rs, :D_REC], k=qkv_ref[rs, D_REC:2 * D_REC], v=qkv_ref[rs, 2 * D_REC:])
            d["beta"] = _pick_heads([bl[:, cb + h:cb + h + 1] for h in range(N_HEADS_REC)])
            la = _pick_heads([bl[:, ca + h:ca + h + 1] for h in range(N_HEADS_REC)])
            g_i, g_j = None, None
            for p_i, p_j in zip(_split3(la), _split3(jnp.where(t_before_j, la, 0.0))):
                a = jnp.dot(tri, p_i, preferred_element_type=F32)
                b = jnp.dot(ones8, p_j, preferred_element_type=F32)
                g_i = a if g_i is None else g_i + a
                g_j = b if g_j is None else g_j + b
            d["g_i"] = g_i
            d["g_tot"] = g_i[last:last + 1]
            d["decay"] = jnp.where(incl, jnp.exp(jnp.where(incl, g_i - g_j[0:1], 0.0)), 0.0)
            inst.append(d)
        chain(0)

        for d in inst:
            gram = _dot(jnp.concatenate([d["k"], d["q"]], axis=0), _block_diag(d["k"].astype(BF16)), NT)
            d["low"] = jnp.where(strict, d["beta"] * gram[:CHUNK] * d["decay"], 0.0)
            d["attn"] = gram[CHUNK:] * d["decay"]

        for d in inst:
            d["t"] = eye - d["low"]
            (d["p"],) = _mm3_heads([d["low"]], d["low"])
        for it in range(4):
            for d in inst:
                p_next, tp = _mm3_heads([d["p"], d["t"]], d["p"])
                d["t"] = d["t"] + tp
                d["p"] = p_next
            if 1 + it < n_chunks:
                chain(1 + it)
        for m, d in enumerate(inst):
            t_mat = d["t"] + _mm3_heads([d["t"]], d["p"])[0]
            eg = jnp.exp(d["g_i"])
            vb = d["v"] * d["beta"]
            kbg = d["k"] * d["beta"] * eg
            u_w[m] = _dot(t_mat, _block_diag(vb.astype(BF16)))
            wq_w[m] = jnp.concatenate([_dot(t_mat, _block_diag(kbg.astype(BF16))), d["q"] * eg], axis=0).astype(BF16)
            at_w[m] = d["attn"]
            kd_w[m] = (d["k"] * jnp.exp(d["g_tot"] - d["g_i"])).astype(BF16)
            et_w[m] = jnp.broadcast_to(jnp.exp(d["g_tot"]), (8, D_REC))

        s_scr[...] = state[0]

    @pl.when(n % 2 == 0)
    def _():
        step(sets[:half], sets[half:])

    @pl.when(n % 2 == 1)
    def _():
        step(sets[half:], sets[:half])

    @pl.when(jnp.logical_and(chain_live, sch_ref[3, prev] == 1))
    def _():
        sfin_ref[0] = _collapse_block_diag(s_scr[...])


def _gdn_bidir_kernel(sch_ref, qkvf_ref, blf_ref, qkvb_ref, blb_ref, s0f_ref, s0b_ref,
                      of_ref, sff_ref, ob_ref, sfb_ref, s_scr, *sets):
    n = pl.program_id(0)
    n_chunks = TS // CHUNK
    prev = jnp.maximum(n - 1, 0)
    chain_live = n >= 1
    half = len(sets) // 2
    lanes = ((False, qkvf_ref, blf_ref, s0f_ref, of_ref, sff_ref), (True, qkvb_ref, blb_ref, s0b_ref, ob_ref, sfb_ref))

    @pl.when(n == 0)
    def _():
        for ref in sets:
            ref[...] = jnp.zeros(ref.shape, ref.dtype)
        s_scr[...] = jnp.zeros(s_scr.shape, F32)

    for ln, lane_refs in enumerate(lanes):
        @pl.when(jnp.logical_and(chain_live, sch_ref[4 * ln + 2, prev] == 1))
        def _(ln=ln, s0_ref=lane_refs[3]):
            s_scr[ln] = _block_diag(s0_ref[0])

    row, lane = _iota((CHUNK, D_REC), 0), _iota((CHUNK, D_REC), 1) % HEAD_DIM
    eye = jnp.where(row == lane, 1.0, 0.0)
    ones8 = jnp.ones((8, CHUNK), BF16)
    bd_mask = _block_diag_mask()

    def step(wr, rd):
        u_w, wq_w, at_w, kd_w, et_w = wr
        u_r, wq_r, at_r, kd_r, et_r = rd
        state = [s_scr[0], s_scr[1]]
        inst = []

        def chain(m):
            for ln, (reverse, _, _, _, o_ref, _) in enumerate(lanes):
                c = (n_chunks - 1 - m) if reverse else m
                rs = slice(c * CHUNK, (c + 1) * CHUNK)
                k = ln * n_chunks + m
                s = state[ln]
                ws_qs = _dot(wq_r[k], s)
                v_new = u_r[k] - ws_qs[:CHUNK]
                o_ref[rs, :] = ws_qs[CHUNK:] + _dot(at_r[k], _block_diag(v_new.astype(BF16)))
                state[ln] = s * et_r[k][0:1] + jnp.where(bd_mask, _dot(kd_r[k], v_new, TN), 0.0)

        for ln, (reverse, qkv_ref, bl_ref, _, _, _) in enumerate(lanes):
            incl = (row <= lane) if reverse else (row >= lane)
            tri = jnp.where(_tri(reverse), 1.0, 0.0).astype(BF16)
            t_before_j = (row >= lane) if reverse else (row <= lane)
            last = 0 if reverse else CHUNK - 1
            cb, ca = ln * N_HEADS_REC, 2 * N_HEADS_REC + ln * N_HEADS_REC
            for m in range(n_chunks):
                c = (n_chunks - 1 - m) if reverse else m
                rs = slice(c * CHUNK, (c + 1) * CHUNK)
                bl = bl_ref[rs, :]
                d = dict(q=qkv_ref[rs, :D_REC], k=qkv_ref[rs, D_REC:2 * D_REC], v=qkv_ref[rs, 2 * D_REC:],
                         strict=(row < lane) if reverse else (row > lane), slot=ln * n_chunks + m)
                d["beta"] = _pick_heads([bl[:, cb + h:cb + h + 1] for h in range(N_HEADS_REC)])
                la = _pick_heads([bl[:, ca + h:ca + h + 1] for h in range(N_HEADS_REC)])
                g_i, g_j = None, None
                for p_i, p_j in zip(_split3(la), _split3(jnp.where(t_before_j, la, 0.0))):
                    a = jnp.dot(tri, p_i, preferred_element_type=F32)
                    b = jnp.dot(ones8, p_j, preferred_element_type=F32)
                    g_i = a if g_i is None else g_i + a
                    g_j = b if g_j is None else g_j + b
                d["g_i"] = g_i
                d["g_tot"] = g_i[last:last + 1]
                d["decay"] = jnp.where(incl, jnp.exp(jnp.where(incl, g_i - g_j[0:1], 0.0)), 0.0)
                inst.append(d)
        chain(0)

        for d in inst:
            gram = _dot(jnp.concatenate([d["k"], d["q"]], axis=0), _block_diag(d["k"].astype(BF16)), NT)
            d["low"] = jnp.where(d["strict"], d["beta"] * gram[:CHUNK] * d["decay"], 0.0)
            d["attn"] = gram[CHUNK:] * d["decay"]

        for d in inst:
            d["t"] = eye - d["low"]
            (d["p"],) = _mm3_heads([d["low"]], d["low"])
        for it in range(4):
            for d in inst:
                p_next, tp = _mm3_heads([d["p"], d["t"]], d["p"])
                d["t"] = d["t"] + tp
                d["p"] = p_next
            if 1 + it < n_chunks:
                chain(1 + it)
        for d in inst:
            k = d["slot"]
            t_mat = d["t"] + _mm3_heads([d["t"]], d["p"])[0]
            eg = jnp.exp(d["g_i"])
            vb = d["v"] * d["beta"]
            kbg = d["k"] * d["beta"] * eg
            u_w[k] = _dot(t_mat, _block_diag(vb.astype(BF16)))
            wq_w[k] = jnp.concatenate([_dot(t_mat, _block_diag(kbg.astype(BF16))), d["q"] * eg], axis=0).astype(BF16)
            at_w[k] = d["attn"]
            kd_w[k] = (d["k"] * jnp.exp(d["g_tot"] - d["g_i"])).astype(BF16)
            et_w[k] = jnp.broadcast_to(jnp.exp(d["g_tot"]), (8, D_REC))

        s_scr[0] = state[0]
        s_scr[1] = state[1]

    @pl.when(n % 2 == 0)
    def _():
        step(sets[:half], sets[half:])

    @pl.when(n % 2 == 1)
    def _():
        step(sets[half:], sets[:half])

    for ln, lane_refs in enumerate(lanes):
        @pl.when(jnp.logical_and(chain_live, sch_ref[4 * ln + 3, prev] == 1))
        def _(ln=ln, sfin_ref=lane_refs[5]):
            sfin_ref[0] = _collapse_block_diag(s_scr[ln])


def _gdn_scan_bidir(qkv, bl, s0_f, s0_b, sched):
    n = qkv.shape[0]
    n_seq = s0_f.shape[0]
    n_steps = sched.shape[1]
    n_inst = 2 * (TS // CHUNK)
    ahead = lambda i: jnp.minimum(i, n_steps - 1)
    behind = lambda i: jnp.maximum(i - 1, 0)
    scratch_set = [pltpu.VMEM((n_inst, CHUNK, D_REC), F32), pltpu.VMEM((n_inst, 2 * CHUNK, D_REC), BF16),
                   pltpu.VMEM((n_inst, CHUNK, D_REC), F32), pltpu.VMEM((n_inst, CHUNK, D_REC), BF16),
                   pltpu.VMEM((n_inst, 8, D_REC), F32)]

    def tile_in(width, ln):
        return pl.BlockSpec((TS, width), lambda i, sch: (sch[4 * ln, ahead(i)], 0))

    def tile_out(ln):
        return pl.BlockSpec((TS, D_REC), lambda i, sch: (sch[4 * ln, behind(i)], 0))

    def state_spec(ln):
        return pl.BlockSpec((1, HEAD_DIM, D_REC), lambda i, sch: (sch[4 * ln + 1, behind(i)], 0, 0))

    grid_spec = pltpu.PrefetchScalarGridSpec(
        num_scalar_prefetch=1,
        grid=(n_steps + 1,),
        in_specs=[tile_in(3 * D_REC, 0), tile_in(128, 0), tile_in(3 * D_REC, 1), tile_in(128, 1),
                  state_spec(0), state_spec(1)],
        out_specs=[tile_out(0), state_spec(0), tile_out(1), state_spec(1)],
        scratch_shapes=[pltpu.VMEM((2, D_REC, D_REC), F32)] + scratch_set + scratch_set,
    )
    o_sds = jax.ShapeDtypeStruct((n, D_REC), F32)
    s_sds = jax.ShapeDtypeStruct((n_seq, HEAD_DIM, D_REC), F32)
    return pl.pallas_call(
        _gdn_bidir_kernel,
        out_shape=[o_sds, s_sds, o_sds, s_sds],
        grid_spec=grid_spec,
        compiler_params=pltpu.CompilerParams(dimension_semantics=("arbitrary",), vmem_limit_bytes=VMEM_LIMIT),
        name="gdn_scan",
    )(sched, qkv, bl, qkv, bl, s0_f, s0_b)


def _gdn_scan(qkv, bl, s0, sched, reverse, direction):
    n = qkv.shape[0]
    n_seq = s0.shape[0]
    n_steps = sched.shape[1]
    n_inst = TS // CHUNK
    ahead = lambda i: jnp.minimum(i, n_steps - 1)
    behind = lambda i: jnp.maximum(i - 1, 0)
    scratch_set = [pltpu.VMEM((n_inst, CHUNK, D_REC), F32), pltpu.VMEM((n_inst, 2 * CHUNK, D_REC), BF16),
                   pltpu.VMEM((n_inst, CHUNK, D_REC), F32), pltpu.VMEM((n_inst, CHUNK, D_REC), BF16),
                   pltpu.VMEM((n_inst, 8, D_REC), F32)]
    grid_spec = pltpu.PrefetchScalarGridSpec(
        num_scalar_prefetch=1,
        grid=(n_steps + 1,),
        in_specs=[pl.BlockSpec((TS, 3 * D_REC), lambda i, sch: (sch[0, ahead(i)], 0)),
                  pl.BlockSpec((TS, 128), lambda i, sch: (sch[0, ahead(i)], 0)),
                  pl.BlockSpec((1, HEAD_DIM, D_REC), lambda i, sch: (sch[1, behind(i)], 0, 0))],
        out_specs=[pl.BlockSpec((TS, D_REC), lambda i, sch: (sch[0, behind(i)], 0)),
                   pl.BlockSpec((1, HEAD_DIM, D_REC), lambda i, sch: (sch[1, behind(i)], 0, 0))],
        scratch_shapes=[pltpu.VMEM((D_REC, D_REC), F32)] + scratch_set + scratch_set,
    )
    return pl.pallas_call(
        functools.partial(_gdn_pipe4_kernel, reverse=reverse, direction=direction),
        out_shape=[jax.ShapeDtypeStruct((n, D_REC), F32), jax.ShapeDtypeStruct((n_seq, HEAD_DIM, D_REC), F32)],
        grid_spec=grid_spec,
        compiler_params=pltpu.CompilerParams(dimension_semantics=("arbitrary",), vmem_limit_bytes=VMEM_LIMIT),
        name="gdn_scan_bwd" if reverse else "gdn_scan_fwd",
    )(sched, qkv, bl, s0)


def _pair_dots(lhs, w, dims):
    outs = []
    for p in range(N_HEADS_REC // 2):
        sl = slice(p * 128, (p + 1) * 128)
        outs.append(lax.dot_general(lhs[:, sl], _bd2(w[:, sl]), dims, preferred_element_type=F32))
    return jnp.concatenate(outs, axis=1)


def _hgrn_tile_kernel(sch_ref, f_ref, i_ref, q_ref, lb_ref, sel_ref, s0_ref, o_ref, sfin_ref, s_scr, *, reverse):
    n = pl.program_id(0)

    n_pairs = N_HEADS_REC // 2

    @pl.when(sch_ref[2, n] == 1)
    def _():
        for p in range(n_pairs):
            s_scr[p] = _bd2(s0_ref[0, :, p * 128:(p + 1) * 128])

    lb = lb_ref[...]
    tri = jnp.where(_tri(reverse), 1.0, 0.0).astype(BF16)
    row = _iota((CHUNK, D_REC), 0)
    col_tok = _iota((CHUNK, D_REC), 1) % HEAD_DIM
    ones = _head_ones()
    last = 0 if reverse else CHUNK - 1
    n_chunks = TS // CHUNK
    chunk_order = list(range(n_chunks - 1, -1, -1) if reverse else range(n_chunks))

    inst = []
    for c in chunk_order:
        rs = slice(c * CHUNK, (c + 1) * CHUNK)
        f = lb + (1.0 - lb) * _sigmoid(f_ref[rs, :])
        d = dict(rs=rs, kk=1.0 - f, q=q_ref[rs, :], v=i_ref[rs, :])
        d["b"] = _dot_sel(tri, jnp.log(jnp.maximum(f, TINY)))
        inst.append(d)

    sel = sel_ref[...]
    for d in inst:
        parts = _split3(d["b"])
        d["b_mid"] = sum(jnp.dot(sel, part, preferred_element_type=F32) for part in parts)

    for lvl in range(N_LEVELS):
        s_half = (CHUNK // 2) >> lvl
        in_block = row % (2 * s_half)
        later = (in_block < s_half) if reverse else (in_block >= s_half)
        same_block = (row // (2 * s_half)) == (col_tok // (2 * s_half))
        for d in inst:
            e = jnp.exp(-jnp.abs(d["b"] - d["b_mid"][lvl * CHUNK:(lvl + 1) * CHUNK]))
            q_hat = jnp.where(later, d["q"] * e, 0.0).astype(BF16)
            k_hat = jnp.where(later, 0.0, d["kk"] * e).astype(BF16)
            res = _pair_dots(q_hat, k_hat, NT)
            term = jnp.where(same_block, res, 0.0)
            d["a"] = term if lvl == 0 else d["a"] + term

    ones2 = ones[:128, :128]
    for d in inst:
        qk = (d["q"] * d["kk"]).astype(BF16)
        sums = jnp.concatenate([jnp.dot(qk[:, p * 128:(p + 1) * 128], ones2, preferred_element_type=F32)
                                for p in range(n_pairs)], axis=1)
        d["a"] = d["a"] + jnp.where(col_tok == row, sums, 0.0)

    bd2_mask = (_iota((128, 128), 0) // HEAD_DIM) == (_iota((128, 128), 1) // HEAD_DIM)
    for d in inst:
        b = d["b"]
        b_last = b[last:last + 1]
        d["av"] = _pair_dots(d["a"].astype(BF16), d["v"].astype(BF16), NN)
        v16, kd16 = d["v"].astype(BF16), (d["kk"] * jnp.exp(b_last - b)).astype(BF16)
        d["kv"] = [jnp.where(bd2_mask, lax.dot_general(v16[:, p * 128:(p + 1) * 128], kd16[:, p * 128:(p + 1) * 128],
                                                       TN, preferred_element_type=F32), 0.0) for p in range(n_pairs)]
        d["qb"] = (d["q"] * jnp.exp(b)).astype(BF16)
        d["e_last"] = jnp.exp(b_last)

    states = [s_scr[p] for p in range(n_pairs)]
    for d in inst:
        qs = [lax.dot_general(d["qb"][:, p * 128:(p + 1) * 128], states[p].astype(BF16), NT,
                              preferred_element_type=F32) for p in range(n_pairs)]
        o_ref[d["rs"], :] = d["av"] + jnp.concatenate(qs, axis=1)
        for p in range(n_pairs):
            states[p] = states[p] * d["e_last"][:, p * 128:(p + 1) * 128] + d["kv"][p]
    for p in range(n_pairs):
        s_scr[p] = states[p]

    @pl.when(sch_ref[3, n] == 1)
    def _():
        for p in range(n_pairs):
            sfin_ref[0, :, p * 128:(p + 1) * 128] = states[p][:HEAD_DIM] + states[p][HEAD_DIM:]


def _hgrn_scan_kernel(sch_ref, f_ref, i_ref, q_ref, lb_ref, sel_ref, s0_ref, o_ref, sfin_ref, s_scr, *, reverse):
    n = pl.program_id(0)

    @pl.when(sch_ref[2, n] == 1)
    def _():
        s_scr[...] = _block_diag(s0_ref[0])

    lb = lb_ref[...]
    sel = sel_ref[...]
    row = _iota((CHUNK, D_REC), 0)
    col_tok = _iota((CHUNK, D_REC), 1) % HEAD_DIM
    ones = _head_ones()
    bd_mask = _block_diag_mask()
    last = 0 if reverse else CHUNK - 1
    n_chunks = TS // CHUNK
    chunk_order = list(range(n_chunks - 1, -1, -1) if reverse else range(n_chunks))

    inst = []
    for c in chunk_order:
        rs = slice(c * CHUNK, (c + 1) * CHUNK)
        f = lb + (1.0 - lb) * _sigmoid(f_ref[rs, :])
        d = dict(rs=rs, kk=1.0 - f, q=q_ref[rs, :], v=i_ref[rs, :])
        cums = _dot_sel(sel, jnp.log(jnp.maximum(f, TINY)))
        d["b"] = cums[:CHUNK]
        d["b_mid_small"] = [cums[CHUNK:2 * CHUNK], cums[2 * CHUNK:]]
        inst.append(d)

    for lvl in range(N_LEVELS):
        s_half = (CHUNK // 2) >> lvl
        in_block = row % (2 * s_half)
        later = (in_block < s_half) if reverse else (in_block >= s_half)
        same_block = (row // (2 * s_half)) == (col_tok // (2 * s_half))
        for d in inst:
            b = d["b"]
            if s_half >= 4:
                pieces = []
                for blk in range(CHUNK // (2 * s_half)):
                    r = blk * 2 * s_half + (s_half if reverse else s_half - 1)
                    pieces.append(jnp.broadcast_to(b[r:r + 1], (2 * s_half, D_REC)))
                b_mid = pieces[0] if len(pieces) == 1 else jnp.concatenate(pieces, axis=0)
            else:
                b_mid = d["b_mid_small"][lvl - (N_LEVELS - 2)]
            e = jnp.exp(-jnp.abs(b - b_mid))
            q_hat = jnp.where(later, d["q"] * e, 0.0)
            k_hat = jnp.where(later, 0.0, d["kk"] * e)
            res = _dot(q_hat, _block_diag(k_hat.astype(BF16)), NT)
            term = jnp.where(same_block, res, 0.0)
            d["a"] = term if lvl == 0 else d["a"] + term

    for d in inst:
        sums = _dot(d["q"] * d["kk"], ones)
        d["a"] = d["a"] + jnp.where(col_tok == row, sums, 0.0)

    for d in inst:
        b = d["b"]
        b_last = b[last:last + 1]
        d["av"] = _dot(d["a"], _block_diag(d["v"].astype(BF16)))
        d["kv"] = jnp.where(bd_mask, _dot(d["v"], d["kk"] * jnp.exp(b_last - b), TN), 0.0)
        d["qb"] = (d["q"] * jnp.exp(b)).astype(BF16)
        d["e_last"] = jnp.exp(b_last)

    s = s_scr[...]
    for d in inst:
        o_ref[d["rs"], :] = d["av"] + _dot(d["qb"], s, NT)
        s = s * d["e_last"] + d["kv"]
    s_scr[...] = s

    @pl.when(sch_ref[3, n] == 1)
    def _():
        sfin_ref[0] = _collapse_block_diag(s)


def _hgrn_scan(hg, lb_e, s0, sched, reverse):
    n = hg.shape[0]
    n_seq = s0.shape[0]
    col = lambda c: pl.BlockSpec((TS, D_REC), lambda i, sch: (sch[0, i], c))
    i = np.arange(CHUNK)
    tri = (i[None, :] >= i[:, None]) if reverse else (i[None, :] <= i[:, None])
    sel = [tri]
    for s_half in (2, 1):
        mid = (i // (2 * s_half)) * (2 * s_half) + (s_half if reverse else s_half - 1)
        sel.append(tri[mid])
    sel = jnp.asarray(np.concatenate(sel, axis=0).astype(np.float32), BF16)
    grid_spec = pltpu.PrefetchScalarGridSpec(
        num_scalar_prefetch=1,
        grid=(sched.shape[1],),
        in_specs=[col(1 if reverse else 0), col(2), col(3),
                  pl.BlockSpec((1, D_REC), lambda i, sch: (0, 0)),
                  pl.BlockSpec((3 * CHUNK, CHUNK), lambda i, sch: (0, 0)),
                  pl.BlockSpec((1, HEAD_DIM, D_REC), lambda i, sch: (sch[1, i], 0, 0))],
        out_specs=[pl.BlockSpec((TS, D_REC), lambda i, sch: (sch[0, i], 0)),
                   pl.BlockSpec((1, HEAD_DIM, D_REC), lambda i, sch: (sch[1, i], 0, 0))],
        scratch_shapes=[pltpu.VMEM((D_REC, D_REC), F32)],
    )
    return pl.pallas_call(
        functools.partial(_hgrn_scan_kernel, reverse=reverse),
        out_shape=[jax.ShapeDtypeStruct((n, D_REC), F32), jax.ShapeDtypeStruct((n_seq, HEAD_DIM, D_REC), F32)],
        grid_spec=grid_spec,
        compiler_params=pltpu.CompilerParams(dimension_semantics=("arbitrary",), vmem_limit_bytes=VMEM_LIMIT),
        name="hgrn_scan_bwd" if reverse else "hgrn_scan_fwd",
    )(sched, hg, hg, hg, lb_e, sel, s0)


def _gated_readout(o, gate, w_e, ones):
    ms = _dot_sel(ones, o * o, sel_is_lhs=False) * (1.0 / HEAD_DIM)
    return o * lax.rsqrt(ms + NORM_EPS) * w_e * _silu(gate)


def _out_proj_kernel(*refs, alpha, n_ctx_tiles, n_x):
    x_refs = refs[:n_x]
    (mod_ref, oa_ref, bf_ref, bb_ref, cf_ref, cb_ref, gb_ref, gc_ref, nwb_ref, nwc_ref,
     w_ref, lng_ref, lnb_ref, y_ref) = refs[n_x:]
    x = _read_tokens(x_refs, n_ctx_tiles)
    m = mod_ref[0]
    ones = _head_ones()
    o_b = _gated_readout(bf_ref[...] + bb_ref[...], gb_ref[...], nwb_ref[...], ones)
    o_c = _gated_readout(cf_ref[...] + cb_ref[...], gc_ref[...], nwc_ref[...], ones)
    mix = jnp.dot(oa_ref[...].astype(BF16), w_ref[0:D_A], preferred_element_type=F32)
    mix = mix + jnp.dot(o_b.astype(BF16), w_ref[D_A:D_A + D_REC], preferred_element_type=F32)
    mix = mix + jnp.dot(o_c.astype(BF16), w_ref[D_A + D_REC:], preferred_element_type=F32)
    y_ref[...] = _layer_norm(alpha * x + m[2:3] * mix, lng_ref[...], lnb_ref[...])


def _out_proj(x_parts, mod_l, o_a, ob_f, ob_b, oc_f, oc_b, g_b, hg, nw_b, nw_c, w_out, ln_g, ln_b, mod_index, alpha,
              n_ctx_tiles):
    n = sum(p.shape[0] for p in x_parts)
    tile = lambda w: pl.BlockSpec((TM, w), lambda i: (i, 0))
    row = lambda w: pl.BlockSpec((1, w), lambda i: (0, 0))
    return pl.pallas_call(
        functools.partial(_out_proj_kernel, alpha=alpha, n_ctx_tiles=n_ctx_tiles, n_x=len(x_parts)),
        out_shape=jax.ShapeDtypeStruct((n, D_MODEL), F32),
        grid=(n // TM,),
        in_specs=_token_specs(x_parts, n_ctx_tiles) + [
                  pl.BlockSpec((1, 6, D_MODEL), lambda i: (mod_index(i), 0, 0)),
                  tile(D_A), tile(D_REC), tile(D_REC), tile(D_REC), tile(D_REC), tile(D_REC),
                  pl.BlockSpec((TM, D_REC), lambda i: (i, 4)),
                  row(D_REC), row(D_REC),
                  pl.BlockSpec((D_MODEL, D_MODEL), lambda i: (0, 0)),
                  row(D_MODEL), row(D_MODEL)],
        out_specs=tile(D_MODEL),
        compiler_params=pltpu.CompilerParams(dimension_semantics=("arbitrary",), vmem_limit_bytes=VMEM_LIMIT),
        name="out_proj_ln",
    )(*x_parts, mod_l, o_a, ob_f, ob_b, oc_f, oc_b, g_b, hg, nw_b, nw_c, w_out, ln_g, ln_b)


def _ffn_kernel(x_ref, mod_ref, w1_ref, w3_ref, w2_ref, lng_ref, lnb_ref, y_ref, *, alpha):
    m = mod_ref[0]
    x = x_ref[...]
    h = (x * (1.0 + m[4:5]) + m[3:4]).astype(BF16)
    acc = jnp.zeros((x.shape[0], D_MODEL), F32)
    for c in range(D_FF // FF_CHUNK):
        sl = slice(c * FF_CHUNK, (c + 1) * FF_CHUNK)
        a = jnp.dot(h, w1_ref[:, sl], preferred_element_type=F32)
        g = jnp.dot(h, w3_ref[:, sl], preferred_element_type=F32)
        acc = acc + jnp.dot((_silu(a) * g).astype(BF16), w2_ref[sl, :], preferred_element_type=F32)
    y_ref[...] = _layer_norm(alpha * x + m[5:6] * acc, lng_ref[...], lnb_ref[...])


def _ffn(x, mod_l, w1, w3, w2, ln_g, ln_b, mod_index_ffn, alpha):
    n = x.shape[0]
    whole = lambda a: pl.BlockSpec(a.shape, lambda i: (0, 0), pipeline_mode=pl.Buffered(1))
    row = pl.BlockSpec((1, D_MODEL), lambda i: (0, 0))
    return pl.pallas_call(
        functools.partial(_ffn_kernel, alpha=alpha),
        out_shape=jax.ShapeDtypeStruct((n, D_MODEL), F32),
        grid=(n // TM_FFN,),
        in_specs=[pl.BlockSpec((TM_FFN, D_MODEL), lambda i: (i, 0)),
                  pl.BlockSpec((1, 6, D_MODEL), lambda i: (mod_index_ffn(i), 0, 0)),
                  whole(w1), whole(w3), whole(w2), row, row],
        out_specs=pl.BlockSpec((TM_FFN, D_MODEL), lambda i: (i, 0)),
        compiler_params=pltpu.CompilerParams(dimension_semantics=("arbitrary",), vmem_limit_bytes=VMEM_LIMIT),
        name="ffn_ln",
    )(x, mod_l, w1, w3, w2, ln_g, ln_b)


def _moe_kernel(x_ref, mod_ref, wr_ref, br_ref, w1_ref, w3_ref, w2_ref, lng_ref, lnb_ref, y_ref,
                h_scr, comb_scr, acc_scr, *, alpha):
    e = pl.program_id(1)
    m = mod_ref[0]

    @pl.when(e == 0)
    def _():
        h = x_ref[...] * (1.0 + m[4:5]) + m[3:4]
        h_scr[...] = h.astype(BF16)
        logits = jnp.dot(h, wr_ref[...], preferred_element_type=F32, precision=lax.Precision.HIGHEST) + br_ref[...]
        lane = _iota(logits.shape, 1)
        m1 = jnp.max(logits, -1, keepdims=True)
        i1 = jnp.min(jnp.where(logits == m1, lane, 128), -1, keepdims=True)
        rest = jnp.where(lane == i1, NEG_BIG, logits)
        m2 = jnp.max(rest, -1, keepdims=True)
        i2 = jnp.min(jnp.where(rest == m2, lane, 128), -1, keepdims=True)
        e2 = jnp.exp(m2 - m1)
        g1 = 1.0 / (1.0 + e2)
        comb_scr[...] = jnp.where(lane == i1, g1, 0.0) + jnp.where(lane == i2, e2 * g1, 0.0)
        acc_scr[...] = jnp.zeros_like(acc_scr)

    h = h_scr[...]
    a = jnp.dot(h, w1_ref[0], preferred_element_type=F32)
    g = jnp.dot(h, w3_ref[0], preferred_element_type=F32)
    y = jnp.dot((_silu(a) * g).astype(BF16), w2_ref[0], preferred_element_type=F32)
    comb = comb_scr[...]
    gate = jnp.sum(jnp.where(_iota(comb.shape, 1) == e, comb, 0.0), -1, keepdims=True)
    acc_scr[...] += gate * y

    @pl.when(e == N_EXPERTS - 1)
    def _():
        y_ref[...] = _layer_norm(alpha * x_ref[...] + m[5:6] * acc_scr[...], lng_ref[...], lnb_ref[...])


def _moe(x, mod_l, w_r, b_r, w1, w3, w2, ln_g, ln_b, mod_index_moe, alpha):
    n = x.shape[0]
    row = pl.BlockSpec((1, D_MODEL), lambda i, e: (0, 0))
    expert = lambda a: pl.BlockSpec((1,) + a.shape[1:], lambda i, e: (e, 0, 0))
    return pl.pallas_call(
        functools.partial(_moe_kernel, alpha=alpha),
        out_shape=jax.ShapeDtypeStruct((n, D_MODEL), F32),
        grid=(n // TM_MOE, N_EXPERTS),
        in_specs=[pl.BlockSpec((TM_MOE, D_MODEL), lambda i, e: (i, 0)),
                  pl.BlockSpec((1, 6, D_MODEL), lambda i, e: (mod_index_moe(i), 0, 0)),
                  pl.BlockSpec((D_MODEL, 128), lambda i, e: (0, 0)),
                  pl.BlockSpec((1, 128), lambda i, e: (0, 0)),
                  expert(w1), expert(w3), expert(w2), row, row],
        out_specs=pl.BlockSpec((TM_MOE, D_MODEL), lambda i, e: (i, 0)),
        scratch_shapes=[pltpu.VMEM((TM_MOE, D_MODEL), BF16), pltpu.VMEM((TM_MOE, 128), F32),
                        pltpu.VMEM((TM_MOE, D_MODEL), F32)],
        compiler_params=pltpu.CompilerParams(dimension_semantics=("arbitrary", "arbitrary"),
                                             vmem_limit_bytes=VMEM_LIMIT),
        name="moe_ln",
    )(x, mod_l, w_r, b_r, w1, w3, w2, ln_g, ln_b)


def _rope_tables(lat_len):
    t = jnp.arange(lat_len)
    pos = jnp.stack([t // GRID_W, t % GRID_W], axis=1).astype(F32)
    half, quarter = HEAD_DIM // 2, HEAD_DIM // 4
    inv_freq = ROPE_THETA ** (-jnp.arange(quarter, dtype=F32) * 2.0 / half)
    lane = np.arange(HEAD_DIM)
    ang = pos[:, lane // half] * inv_freq[lane % quarter][None, :]
    sign = jnp.asarray(np.where((lane % half) < quarter, -1.0, 1.0), F32)
    cos = jnp.tile(jnp.cos(ang), (1, 128 // HEAD_DIM))
    sin = jnp.tile(jnp.sin(ang) * sign, (1, 128 // HEAD_DIM))
    return cos, sin


def kernel(x_prompt, x_sample, cache_k, cache_v, state_gdn, state_hgrn, c, c_ctx, w_mod, b_mod, w_in, conv_w, attn_sink, gdn_a_log, gdn_dt_bias, gdn_norm_w, hgrn_lb, hgrn_norm_w, w_out, ln_g, ln_b, ffn_w1, ffn_w3, ffn_w2, moe_router, moe_router_b, moe_w1, moe_w3, moe_w2):
    depth = w_in.shape[0]
    n_ctx, ctx_len, _ = x_prompt.shape
    lat_batch, lat_len, _ = x_sample.shape
    n_ctx_rows = n_ctx * ctx_len
    alpha = (2.0 * depth) ** 0.25

    x_parts = [x_prompt.reshape(-1, D_MODEL), x_sample.reshape(-1, D_MODEL)]

    def mod_index_for(tile):
        ctx_tiles, per_seq = n_ctx_rows // tile, lat_len // tile
        return lambda i: jnp.where(i < ctx_tiles, 0, 1 + (i - ctx_tiles) // per_seq)

    cvecs = jnp.concatenate([c_ctx[None], c], axis=0)
    mods = _modulation(cvecs, w_mod, b_mod).reshape(depth, cvecs.shape[0], 6, D_MODEL)

    cos_tab, sin_tab = _rope_tables(lat_len)
    sm = jax.nn.softmax(hgrn_lb.astype(F32), axis=0)
    lower = jnp.cumsum(sm, axis=0) - sm[0:1]

    w_in_r = jnp.concatenate([w_in[:, :, :1792], w_in[:, :, 1808:], w_in[:, :, 1792:1808],
                              jnp.zeros((depth, D_MODEL, C_END - 3088), F32)], axis=-1).astype(BF16)
    w_out_b = w_out.astype(BF16)

    zeros_state = jnp.zeros((n_ctx, HEAD_DIM, D_REC), F32)
    sched = [_scan_schedule(n_ctx, ctx_len, lat_batch, lat_len, reverse) for reverse in (False, True)]
    new_k, new_v, new_sg, new_sh = [], [], [], []

    for l in range(depth):
        qa, ka, va, qkvb, g_b, hg, ba = _in_proj(x_parts, mods[l], w_in_r[l], cos_tab, sin_tab, mod_index_for(TM),
                                                 n_ctx_rows // TM, lat_len // TM)
        new_k.append(ka[:n_ctx_rows].reshape(n_ctx, ctx_len, A_KV_HEADS, HEAD_DIM))
        new_v.append(va[:n_ctx_rows].reshape(n_ctx, ctx_len, A_KV_HEADS, HEAD_DIM))

        o_a = _attention(qa, ka, va, cache_k[:, l].reshape(lat_batch, -1, D_KV), cache_v[:, l].reshape(lat_batch, -1, D_KV),
                         attn_sink[l], n_ctx, ctx_len, lat_batch, lat_len)

        sg0 = state_gdn[:, l].transpose(0, 1, 3, 2, 4).reshape(lat_batch, 2, HEAD_DIM, D_REC)
        sh0 = state_hgrn[:, l].transpose(0, 1, 4, 2, 3).reshape(lat_batch, 2, HEAD_DIM, D_REC)
        ob, oc, sg_fin, sh_fin = [], [], [], []
        alog_row = jnp.zeros((1, 128), F32).at[0, 8:16].set(gdn_a_log[l].reshape(-1))
        dtb_row = jnp.zeros((1, 128), F32).at[0, 8:16].set(gdn_dt_bias[l].reshape(-1))
        qkv, bl = _gdn_prep(qkvb, ba, conv_w[l], alog_row, dtb_row, n_ctx_rows // TS, ctx_len // TS, lat_len // TS)
        o_f, s_f, o_b, s_b = _gdn_scan_bidir(qkv, bl, jnp.concatenate([zeros_state, sg0[:, 0]], axis=0),
                                             jnp.concatenate([zeros_state, sg0[:, 1]], axis=0),
                                             jnp.concatenate(sched, axis=0))
        ob = [o_f, o_b]
        sg_fin = [s[:n_ctx].reshape(n_ctx, HEAD_DIM, N_HEADS_REC, HEAD_DIM).transpose(0, 2, 1, 3) for s in (s_f, s_b)]
        for d, reverse in enumerate((False, True)):
            o_d, s_fin = _hgrn_scan(hg, lower[l, d][None], jnp.concatenate([zeros_state, sh0[:, d]], axis=0),
                                    sched[d], reverse)
            oc.append(o_d)
            sh_fin.append(s_fin[:n_ctx].reshape(n_ctx, HEAD_DIM, N_HEADS_REC, HEAD_DIM).transpose(0, 2, 3, 1))
        new_sg.append(jnp.stack(sg_fin, axis=1))
        new_sh.append(jnp.stack(sh_fin, axis=1))

        nw_b = jnp.tile(gdn_norm_w[l], N_HEADS_REC)[None]
        nw_c = jnp.tile(hgrn_norm_w[l], N_HEADS_REC)[None]
        x = _out_proj(x_parts, mods[l], o_a, ob[0], ob[1], oc[0], oc[1], g_b, hg, nw_b, nw_c, w_out_b[l],
                      ln_g[l, 0][None], ln_b[l, 0][None], mod_index_for(TM), alpha, n_ctx_rows // TM)

        i = l // 2
        if l % 2 == 0:
            x = _ffn(x, mods[l], ffn_w1[i].astype(BF16), ffn_w3[i].astype(BF16), ffn_w2[i].astype(BF16),
                     ln_g[l, 1][None], ln_b[l, 1][None], mod_index_for(TM_FFN), alpha)
        else:
            w_r = jnp.zeros((D_MODEL, 128), F32).at[:, :N_EXPERTS].set(moe_router[i])
            b_r = jnp.full((1, 128), NEG_BIG, F32).at[0, :N_EXPERTS].set(moe_router_b[i])
            x = _moe(x, mods[l], w_r, b_r, moe_w1[i].astype(BF16), moe_w3[i].astype(BF16), moe_w2[i].astype(BF16),
                     ln_g[l, 1][None], ln_b[l, 1][None], mod_index_for(TM_MOE), alpha)
        x_parts = [x]

    y_prompt = x[:n_ctx_rows].reshape(x_prompt.shape)
    y_sample = x[n_ctx_rows:].reshape(x_sample.shape)
    return (y_prompt, y_sample, jnp.stack(new_k, axis=1), jnp.stack(new_v, axis=1),
            jnp.stack(new_sg, axis=1), jnp.stack(new_sh, axis=1))
```

```python
import functools

import numpy as np
import jax
import jax.numpy as jnp
from jax import lax
from jax.experimental import pallas as pl
from jax.experimental.pallas import tpu as pltpu

F32 = jnp.float32
BF16 = jnp.bfloat16

D_MODEL = 1024
HEAD_DIM = 64
A_HEADS = 8
A_KV_HEADS = 2
GRID_W = 64
ATTN_BLOCK = 128
N_HEADS_REC = 4
D_REC = N_HEADS_REC * HEAD_DIM
CONV_W = 5
CHUNK = 64
N_LEVELS = 6
D_A = A_HEADS * HEAD_DIM
D_KV = A_KV_HEADS * HEAD_DIM
D_FF = 2816
N_EXPERTS = 8
EXPERT_FF = 1024
ROPE_THETA = 10000.0
NORM_EPS = 1e-5
TINY = 1e-30
NEG_BIG = -1e30

C_QA, C_KA, C_VA, C_QKVB, C_GB, C_HG, C_BA, C_END = 0, 512, 640, 768, 1536, 1792, 3072, 3200
D_HG = 1280
HALO = 8

TM = 512
TS = 256
TM_FFN = 512
FF_CHUNK = 256
TM_MOE = 1024
MOE_BLOCK = 128
VMEM_LIMIT = 56 * 1024 * 1024

NN = (((1,), (0,)), ((), ()))
NT = (((1,), (1,)), ((), ()))
TN = (((0,), (0,)), ((), ()))


def _dot(a, b, dims=NN):
    return lax.dot_general(a.astype(BF16), b.astype(BF16), dims, preferred_element_type=F32)


def _split2(x):
    hi = x.astype(BF16)
    lo = (x - hi.astype(F32)).astype(BF16)
    return hi, lo


def _split3(x):
    hi = x.astype(BF16)
    r = x - hi.astype(F32)
    mid = r.astype(BF16)
    lo = (r - mid.astype(F32)).astype(BF16)
    return hi, mid, lo


def _dot_sel(sel, x, dims=NN, sel_is_lhs=True):
    out = None
    for part in _split3(x):
        a, b = (sel, part) if sel_is_lhs else (part, sel)
        t = lax.dot_general(a, b, dims, preferred_element_type=F32)
        out = t if out is None else out + t
    return out


def _sigmoid(x):
    return 1.0 / (1.0 + jnp.exp(-x))


def _silu(x):
    return x * _sigmoid(x)


def _layer_norm(x, g, b):
    mu = jnp.mean(x, -1, keepdims=True)
    xc = x - mu
    var = jnp.mean(xc * xc, -1, keepdims=True)
    return xc * lax.rsqrt(var + NORM_EPS) * g + b


def _iota(shape, dim):
    return lax.broadcasted_iota(jnp.int32, shape, dim)


def _head_ones():
    return jnp.where(_iota((D_REC, D_REC), 0) // HEAD_DIM == _iota((D_REC, D_REC), 1) // HEAD_DIM, 1.0, 0.0).astype(BF16)


def _block_diag(x):
    head = _iota(x.shape, 1) // HEAD_DIM
    return jnp.concatenate([jnp.where(head == h, x, jnp.zeros_like(x)) for h in range(N_HEADS_REC)], axis=0)


def _block_diag_mask():
    return _iota((D_REC, D_REC), 0) // HEAD_DIM == _iota((D_REC, D_REC), 1) // HEAD_DIM


def _collapse_block_diag(s):
    out = s[0:HEAD_DIM]
    for h in range(1, N_HEADS_REC):
        out = out + s[h * HEAD_DIM:(h + 1) * HEAD_DIM]
    return out


def _mod_kernel(ct_ref, w_ref, b_ref, o_ref):
    s = _silu(ct_ref[...])
    w = w_ref[0]
    rows = [jnp.sum(s[:, r:r + 1] * w, axis=0, keepdims=True) for r in range(o_ref.shape[1])]
    o_ref[0] = jnp.concatenate(rows, axis=0) + b_ref[0]


def _modulation(cvecs, w_mod, b_mod):
    depth, d, n6 = w_mod.shape
    r = cvecs.shape[0]
    tn = 1024
    ct = jnp.zeros((d, 8), F32).at[:, :r].set(cvecs.T)
    return pl.pallas_call(
        _mod_kernel,
        out_shape=jax.ShapeDtypeStruct((depth, r, n6), F32),
        grid=(depth, n6 // tn),
        in_specs=[pl.BlockSpec((d, 8), lambda l, j: (0, 0)),
                  pl.BlockSpec((1, d, tn), lambda l, j: (l, 0, j)),
                  pl.BlockSpec((1, 1, tn), lambda l, j: (l, 0, j))],
        out_specs=pl.BlockSpec((1, r, tn), lambda l, j: (l, 0, j)),
        compiler_params=pltpu.CompilerParams(dimension_semantics=("arbitrary", "arbitrary"),
                                             vmem_limit_bytes=VMEM_LIMIT),
        name="modulation",
    )(ct, w_mod, b_mod.reshape(depth, 1, n6))


def _rope(x, cos, sin):
    outs = []
    lane = _iota(cos.shape, 1)
    first = (lane % 32) < 16
    for m in range(x.shape[1] // 128):
        xs = x[:, m * 128:(m + 1) * 128]
        swapped = jnp.where(first, pltpu.roll(xs, 128 - 16, 1), pltpu.roll(xs, 16, 1))
        outs.append(xs * cos + swapped * sin)
    return outs[0] if len(outs) == 1 else jnp.concatenate(outs, axis=1)


def _token_specs(x_parts, n_ctx_tiles):
    if len(x_parts) == 1:
        return [pl.BlockSpec((TM, D_MODEL), lambda i: (i, 0))]
    return [pl.BlockSpec((TM, D_MODEL), lambda i: (jnp.minimum(i, n_ctx_tiles - 1), 0)),
            pl.BlockSpec((TM, D_MODEL), lambda i: (jnp.maximum(i - n_ctx_tiles, 0), 0))]


def _read_tokens(x_refs, n_ctx_tiles):
    if len(x_refs) == 1:
        return x_refs[0][...]
    return jnp.where(pl.program_id(0) < n_ctx_tiles, x_refs[0][...], x_refs[1][...])


def _in_proj_kernel(*refs, n_ctx_tiles, n_x):
    x_refs = refs[:n_x]
    mod_ref, w_ref, cos_ref, sin_ref, qa_ref, ka_ref, va_ref, qkvb_ref, gb_ref, hg_ref, ba_ref = refs[n_x:]
    i = pl.program_id(0)
    m = mod_ref[0]
    h = (_read_tokens(x_refs, n_ctx_tiles) * (1.0 + m[1:2]) + m[0:1]).astype(BF16)

    def proj(lo, hi):
        return jnp.dot(h, w_ref[:, lo:hi], preferred_element_type=F32)

    qa = proj(C_QA, C_KA)
    ka = proj(C_KA, C_VA)

    @pl.when(i < n_ctx_tiles)
    def _():
        qa_ref[...] = qa
        ka_ref[...] = ka

    @pl.when(i >= n_ctx_tiles)
    def _():
        cos, sin = cos_ref[...], sin_ref[...]
        qa_ref[...] = _rope(qa, cos, sin)
        ka_ref[...] = _rope(ka, cos, sin)

    va_ref[...] = proj(C_VA, C_QKVB)
    qkvb_ref[...] = proj(C_QKVB, C_GB)
    gb_ref[...] = proj(C_GB, C_HG)
    hg_ref[...] = proj(C_HG, C_BA)
    ba_ref[...] = proj(C_BA, C_END)


def _in_proj(x_parts, mod_l, w_in_r, cos_tab, sin_tab, mod_index, n_ctx_tiles, lat_tiles_per_seq):
    n = sum(p.shape[0] for p in x_parts)

    def tab_index(i):
        return (jnp.where(i < n_ctx_tiles, 0, (i - n_ctx_tiles) % lat_tiles_per_seq), 0)

    def out(width):
        return jax.ShapeDtypeStruct((n, width), F32), pl.BlockSpec((TM, width), lambda i: (i, 0))

    outs = [out(D_A), out(D_KV), out(D_KV), out(3 * D_REC), out(D_REC), out(D_HG), out(128)]
    return pl.pallas_call(
        functools.partial(_in_proj_kernel, n_ctx_tiles=n_ctx_tiles, n_x=len(x_parts)),
        out_shape=[o[0] for o in outs],
        grid=(n // TM,),
        in_specs=_token_specs(x_parts, n_ctx_tiles) + [
            pl.BlockSpec((1, 6, D_MODEL), lambda i: (mod_index(i), 0, 0)),
            pl.BlockSpec((D_MODEL, C_END), lambda i: (0, 0)),
            pl.BlockSpec((TM, 128), tab_index),
            pl.BlockSpec((TM, 128), tab_index)],
        out_specs=[o[1] for o in outs],
        compiler_params=pltpu.CompilerParams(dimension_semantics=("arbitrary",), vmem_limit_bytes=VMEM_LIMIT),
        name="in_proj",
    )(*x_parts, mod_l, w_in_r, cos_tab, sin_tab)


def _attn_heads(q_ref, k, v, bias, sink_ref, o_ref):
    k_same, k_swap = k.astype(BF16), pltpu.roll(k, HEAD_DIM, 1).astype(BF16)
    v_same, v_swap = v.astype(BF16), pltpu.roll(v, HEAD_DIM, 1).astype(BF16)
    nq = q_ref.shape[0]
    low_half = _iota((nq, 128), 1) < HEAD_DIM
    rep = A_HEADS // A_KV_HEADS
    for g in range(A_KV_HEADS):
        heads = range(g * rep, (g + 1) * rep)
        scores, exps, dens, outs = {}, {}, {}, {}
        for hq in heads:
            p, e = hq // 2, hq % 2
            q2 = q_ref[:, p * 128:(p + 1) * 128] * (HEAD_DIM ** -0.5)
            qm = jnp.where(low_half if e == 0 else jnp.logical_not(low_half), q2, 0.0)
            s = _dot(qm, k_same if e == g else k_swap, NT)
            scores[hq] = s if bias is None else s + bias
        for hq in heads:
            snk = sink_ref[hq]
            mx = jnp.maximum(jnp.max(scores[hq], -1, keepdims=True), snk)
            exps[hq] = jnp.exp(scores[hq] - mx)
            dens[hq] = jnp.sum(exps[hq], -1, keepdims=True) + jnp.exp(snk - mx)
        for hq in heads:
            outs[hq] = _dot(exps[hq], v_same if hq % 2 == g else v_swap) / dens[hq]
        for p in range(g * rep // 2, (g + 1) * rep // 2):
            o_ref[:, p * 128:(p + 1) * 128] = jnp.where(low_half, outs[2 * p], outs[2 * p + 1])


def _attn_kernel(q_ref, kx_ref, vx_ref, kp_ref, kc_ref, kn_ref, vp_ref, vc_ref, vn_ref, ck_ref, cv_ref, sink_ref,
                 o_ref, *, ctx_steps, n_blocks):
    i = pl.program_id(0)

    @pl.when(i < ctx_steps)
    def _():
        _attn_heads(q_ref, kx_ref[...], vx_ref[...], None, sink_ref, o_ref)

    @pl.when(i >= ctx_steps)
    def _():
        n = (i - ctx_steps) % n_blocks
        k = jnp.concatenate([kp_ref[...], kc_ref[...], kn_ref[...], ck_ref[0]], axis=0)
        v = jnp.concatenate([vp_ref[...], vc_ref[...], vn_ref[...], cv_ref[0]], axis=0)
        nk = k.shape[0]
        qi = _iota((ATTN_BLOCK, nk), 0)
        kj = _iota((ATTN_BLOCK, nk), 1)
        seg = kj // ATTN_BLOCK
        jj = kj % ATTN_BLOCK
        bias_prev = jnp.where(jj >= qi, 0.0, NEG_BIG) + jnp.where(n > 0, 0.0, NEG_BIG)
        bias_next = jnp.where(jj <= qi, 0.0, NEG_BIG) + jnp.where(n < n_blocks - 1, 0.0, NEG_BIG)
        bias = jnp.where(seg == 0, bias_prev, jnp.where(seg == 2, bias_next, 0.0))
        _attn_heads(q_ref, k, v, bias, sink_ref, o_ref)


def _attention(qa, ka, va, ck, cv, sink, n_ctx, ctx_len, lat_batch, lat_len):
    n = qa.shape[0]
    per_seq = ctx_len // ATTN_BLOCK
    ctx_steps = n_ctx * per_seq
    nb = lat_len // ATTN_BLOCK
    past = ck.shape[1]

    def lat(i):
        return jnp.maximum(i - ctx_steps, 0)

    def cur(i):
        return (ctx_steps + lat(i), 0)

    def prev(i):
        return (ctx_steps + (lat(i) // nb) * nb + jnp.maximum(lat(i) % nb - 1, 0), 0)

    def nxt(i):
        return (ctx_steps + (lat(i) // nb) * nb + jnp.minimum(lat(i) % nb + 1, nb - 1), 0)

    def own_seq(i):
        return (jnp.minimum(i // per_seq, n_ctx - 1), 0)

    kv = lambda f: pl.BlockSpec((ATTN_BLOCK, D_KV), f)
    cache = pl.BlockSpec((1, past, D_KV), lambda i: (lat(i) // nb, 0, 0))
    return pl.pallas_call(
        functools.partial(_attn_kernel, ctx_steps=ctx_steps, n_blocks=nb),
        out_shape=jax.ShapeDtypeStruct((n, D_A), F32),
        grid=(ctx_steps + lat_batch * nb,),
        in_specs=[pl.BlockSpec((ATTN_BLOCK, D_A), lambda i: (i, 0)),
                  pl.BlockSpec((ctx_len, D_KV), own_seq), pl.BlockSpec((ctx_len, D_KV), own_seq),
                  kv(prev), kv(cur), kv(nxt), kv(prev), kv(cur), kv(nxt), cache, cache,
                  pl.BlockSpec(memory_space=pltpu.SMEM)],
        out_specs=pl.BlockSpec((ATTN_BLOCK, D_A), lambda i: (i, 0)),
        compiler_params=pltpu.CompilerParams(dimension_semantics=("arbitrary",), vmem_limit_bytes=VMEM_LIMIT),
        name="attention",
    )(qa, ka, va, ka, ka, ka, va, va, va, ck, cv, sink)


def _tri(reverse, strict=False):
    i, j = _iota((CHUNK, CHUNK), 0), _iota((CHUNK, CHUNK), 1)
    if strict:
        return (i < j) if reverse else (i > j)
    return (i <= j) if reverse else (i >= j)


def _gdn_prep_kernel(cur_ref, prev_ref, next_ref, ba_ref, convw_ref, alog_ref, dtb_ref, qkv_ref, bl_ref,
                     *, ctx_tiles, ctx_tiles_per_seq, lat_tiles_per_seq):
    i = pl.program_id(0)
    in_ctx = i < ctx_tiles
    t_ctx = i % ctx_tiles_per_seq
    t_lat = (i - ctx_tiles) % lat_tiles_per_seq
    is_first = jnp.where(in_ctx, t_ctx == 0, t_lat == 0)
    is_last = jnp.where(in_ctx, t_ctx == ctx_tiles_per_seq - 1, t_lat == lat_tiles_per_seq - 1)
    has_prev = jnp.where(is_first, 0.0, 1.0)
    has_next = jnp.where(is_last, 0.0, 1.0)
    xh = jnp.concatenate([prev_ref[...] * has_prev, cur_ref[...], next_ref[...] * has_next], axis=0)
    cw = convw_ref[...]
    rows = cur_ref.shape[0]
    y = None
    for t in range(CONV_W):
        lo = HALO - CONV_W // 2 + t
        term = xh[lo:lo + rows] * cw[t:t + 1]
        y = term if y is None else y + term
    y = _silu(y)
    q, k, v = y[:, :D_REC], y[:, D_REC:2 * D_REC], y[:, 2 * D_REC:]
    ones = _head_ones()
    qq_hi, qq_lo = _split2(q * q)
    kk_hi, kk_lo = _split2(k * k)
    sums = jnp.dot(jnp.concatenate([qq_hi, qq_lo, kk_hi, kk_lo], axis=0), ones, preferred_element_type=F32)
    q_ss = sums[0:rows] + sums[rows:2 * rows]
    k_ss = sums[2 * rows:3 * rows] + sums[3 * rows:]
    qkv_ref[:, :D_REC] = q * lax.rsqrt(q_ss + 1e-6) * (HEAD_DIM ** -0.5)
    qkv_ref[:, D_REC:2 * D_REC] = k * lax.rsqrt(k_ss + 1e-6)
    qkv_ref[:, 2 * D_REC:] = v

    ba = ba_ref[...]
    za = ba + dtb_ref[...]
    softplus = jnp.maximum(za, 0.0) + jnp.log(1.0 + jnp.exp(-jnp.abs(za)))
    la = -jnp.exp(alog_ref[...]) * softplus
    bl_ref[...] = jnp.where(_iota(ba.shape, 1) < 2 * N_HEADS_REC, _sigmoid(ba), la)


def _gdn_prep(qkvb, ba, conv_w, alog_row, dtb_row, ctx_tiles, ctx_tiles_per_seq, lat_tiles_per_seq):
    n = qkvb.shape[0]
    per = TS // HALO
    const = lambda shape: pl.BlockSpec(shape, lambda i: (0,) * len(shape))
    return pl.pallas_call(
        functools.partial(_gdn_prep_kernel, ctx_tiles=ctx_tiles, ctx_tiles_per_seq=ctx_tiles_per_seq,
                          lat_tiles_per_seq=lat_tiles_per_seq),
        out_shape=[jax.ShapeDtypeStruct((n, 3 * D_REC), F32), jax.ShapeDtypeStruct((n, 128), F32)],
        grid=(n // TS,),
        in_specs=[pl.BlockSpec((TS, 3 * D_REC), lambda i: (i, 0)),
                  pl.BlockSpec((HALO, 3 * D_REC), lambda i: (jnp.maximum(i * per - 1, 0), 0)),
                  pl.BlockSpec((HALO, 3 * D_REC), lambda i: (jnp.minimum((i + 1) * per, n // HALO - 1), 0)),
                  pl.BlockSpec((TS, 128), lambda i: (i, 0)),
                  const((CONV_W, 3 * D_REC)), const((1, 128)), const((1, 128))],
        out_specs=[pl.BlockSpec((TS, 3 * D_REC), lambda i: (i, 0)), pl.BlockSpec((TS, 128), lambda i: (i, 0))],
        compiler_params=pltpu.CompilerParams(dimension_semantics=("arbitrary",), vmem_limit_bytes=VMEM_LIMIT),
        name="gdn_prep",
    )(qkvb, qkvb, qkvb, ba, conv_w, alog_row, dtb_row)


def _scan_schedule(n_ctx, ctx_len, lat_batch, lat_len, reverse):
    tiles_ctx, tiles_lat = ctx_len // TS, lat_len // TS
    seqs = [[s * tiles_ctx + t for t in range(tiles_ctx)] for s in range(n_ctx)]
    seqs += [[n_ctx * tiles_ctx + b * tiles_lat + t for t in range(tiles_lat)] for b in range(lat_batch)]
    cols = []
    for s, tiles in enumerate(seqs):
        order = tiles[::-1] if reverse else tiles
        for n, t in enumerate(order):
            cols.append((t, s, int(n == 0), int(n == len(order) - 1)))
    return jnp.asarray(np.array(cols, np.int32).T)


def _bd2(x):
    low = _iota(x.shape, 1) < HEAD_DIM
    zero = jnp.zeros_like(x)
    return jnp.concatenate([jnp.where(low, x, zero), jnp.where(low, zero, x)], axis=0)


def _mm3_stack(lhs_list, rhs):
    r_hi, r_lo = _split2(rhs)
    w_hi, w_lo = _bd2(r_hi), _bd2(r_lo)
    parts = [_split2(a) for a in lhs_list]
    his = [p[0] for p in parts]
    los = [p[1] for p in parts]
    n = len(lhs_list) * CHUNK
    big = jnp.dot(jnp.concatenate(his + los, axis=0), w_hi, preferred_element_type=F32)
    small = jnp.dot(his[0] if len(his) == 1 else jnp.concatenate(his, axis=0), w_lo, preferred_element_type=F32)
    out = big[:n] + big[n:] + small
    return [out[m * CHUNK:(m + 1) * CHUNK] for m in range(len(lhs_list))]


def _pick2(a, b):
    shape = (CHUNK, 128)
    return jnp.where(_iota(shape, 1) < HEAD_DIM, jnp.broadcast_to(a, shape), jnp.broadcast_to(b, shape))


def _gdn_tile_kernel(sch_ref, qkv_ref, bl_ref, s0_ref, o_ref, sfin_ref, s_scr, *, reverse, direction):
    n = pl.program_id(0)
    n_groups = N_HEADS_REC // 2

    @pl.when(sch_ref[2, n] == 1)
    def _():
        for g in range(n_groups):
            s_scr[g] = _bd2(s0_ref[0, :, g * 128:(g + 1) * 128])

    row, lane = _iota((CHUNK, 128), 0), _iota((CHUNK, 128), 1) % HEAD_DIM
    if reverse:
        incl, strict = row <= lane, row < lane
    else:
        incl, strict = row >= lane, row > lane
    eye = jnp.where(row == lane, 1.0, 0.0)
    tri = jnp.where(_tri(reverse), 1.0, 0.0).astype(BF16)
    tri_t2 = jnp.where(_iota((CHUNK, 128), 0) >= lane, 1.0, 0.0) if reverse else \
        jnp.where(_iota((CHUNK, 128), 0) <= lane, 1.0, 0.0)
    tri_t2 = tri_t2.astype(BF16)
    bd_mask = (_iota((128, 128), 0) // HEAD_DIM) == (_iota((128, 128), 1) // HEAD_DIM)
    last = 0 if reverse else CHUNK - 1
    n_chunks = TS // CHUNK
    states = [s_scr[g] for g in range(n_groups)]

    chunk_order = list(range(n_chunks - 1, -1, -1) if reverse else range(n_chunks))
    cb, ca = direction * N_HEADS_REC, 2 * N_HEADS_REC + direction * N_HEADS_REC
    inst = []
    for c in chunk_order:
        rs = slice(c * CHUNK, (c + 1) * CHUNK)
        bl = bl_ref[rs, :]
        g16, gt2 = None, None
        for part in _split3(bl):
            a = jnp.dot(tri, part, preferred_element_type=F32)
            b = lax.dot_general(part, tri_t2, TN, preferred_element_type=F32)
            g16 = a if g16 is None else g16 + a
            gt2 = b if gt2 is None else gt2 + b
        for g in range(n_groups):
            h0, h1 = 2 * g, 2 * g + 1
            ls = slice(g * 128, (g + 1) * 128)
            d = dict(c=c, g=g, rs=rs, ls=ls)
            d["q"] = qkv_ref[rs, ls]
            d["k"] = qkv_ref[rs, D_REC + g * 128:D_REC + (g + 1) * 128]
            d["v"] = qkv_ref[rs, 2 * D_REC + g * 128:2 * D_REC + (g + 1) * 128]
            d["beta"] = _pick2(bl[:, cb + h0:cb + h0 + 1], bl[:, cb + h1:cb + h1 + 1])
            d["g_i"] = _pick2(g16[:, ca + h0:ca + h0 + 1], g16[:, ca + h1:ca + h1 + 1])
            g_j = _pick2(gt2[ca + h0:ca + h0 + 1, :], gt2[ca + h1:ca + h1 + 1, :])
            d["g_tot"] = d["g_i"][last:last + 1]
            d["decay"] = jnp.where(incl, jnp.exp(jnp.where(incl, d["g_i"] - g_j, 0.0)), 0.0)
            inst.append(d)

    for d in inst:
        gram = _dot(jnp.concatenate([d["k"], d["q"]], axis=0), _bd2(d["k"].astype(BF16)), NT)
        d["low"] = jnp.where(strict, d["beta"] * gram[:CHUNK] * d["decay"], 0.0)
        d["attn"] = gram[CHUNK:] * d["decay"]

    for d in inst:
        d["t"] = eye - d["low"]
        (d["p"],) = _mm3_stack([d["low"]], d["low"])
    for _ in range(4):
        for d in inst:
            p_next, tp = _mm3_stack([d["p"], d["t"]], d["p"])
            d["t"] = d["t"] + tp
            d["p"] = p_next
    for d in inst:
        t_mat = d["t"] + _mm3_stack([d["t"]], d["p"])[0]
        eg = jnp.exp(d["g_i"])
        vb = d["v"] * d["beta"]
        kbg = d["k"] * d["beta"] * eg
        d["u"] = _dot(t_mat, _bd2(vb.astype(BF16)))
        d["w_qg"] = jnp.concatenate([_dot(t_mat, _bd2(kbg.astype(BF16))), d["q"] * eg], axis=0).astype(BF16)
        d["k_dec"] = (d["k"] * jnp.exp(d["g_tot"] - d["g_i"])).astype(BF16)
        d["e_tot"] = jnp.exp(d["g_tot"])

    for d in inst:
        s = states[d["g"]]
        ws_qs = _dot(d["w_qg"], s)
        v_new = d["u"] - ws_qs[:CHUNK]
        o_ref[d["rs"], d["ls"]] = ws_qs[CHUNK:] + _dot(d["attn"], _bd2(v_new.astype(BF16)))
        states[d["g"]] = s * d["e_tot"] + jnp.where(bd_mask, _dot(d["k_dec"], v_new, TN), 0.0)

    for g in range(n_groups):
        s_scr[g] = states[g]

    @pl.when(sch_ref[3, n] == 1)
    def _():
        for g in range(n_groups):
            sfin_ref[0, :, g * 128:(g + 1) * 128] = states[g][:HEAD_DIM] + states[g][HEAD_DIM:]


def _gdn_pipe_kernel(sch_ref, qkv_ref, bl_ref, s0_ref, o_ref, sfin_ref, s_scr, *sets, reverse, direction, n_steps):
    n = pl.program_id(0)
    n_groups = N_HEADS_REC // 2
    n_chunks = TS // CHUNK
    n_inst = n_chunks * n_groups
    prev = jnp.maximum(n - 1, 0)
    chain_live = n >= 1
    half = len(sets) // 2

    @pl.when(n == 0)
    def _():
        for ref in sets:
            ref[...] = jnp.zeros(ref.shape, ref.dtype)
        s_scr[...] = jnp.zeros(s_scr.shape, F32)

    @pl.when(jnp.logical_and(chain_live, sch_ref[2, prev] == 1))
    def _():
        for g in range(n_groups):
            s_scr[g] = _bd2(s0_ref[0, :, g * 128:(g + 1) * 128])

    row, lane = _iota((CHUNK, 128), 0), _iota((CHUNK, 128), 1) % HEAD_DIM
    if reverse:
        incl, strict = row <= lane, row < lane
    else:
        incl, strict = row >= lane, row > lane
    eye = jnp.where(row == lane, 1.0, 0.0)
    tri = jnp.where(_tri(reverse), 1.0, 0.0).astype(BF16)
    tri_t2 = jnp.where((row >= lane) if reverse else (row <= lane), 1.0, 0.0).astype(BF16)
    bd_mask = (_iota((128, 128), 0) // HEAD_DIM) == (_iota((128, 128), 1) // HEAD_DIM)
    last = 0 if reverse else CHUNK - 1
    chunk_order = list(range(n_chunks - 1, -1, -1) if reverse else range(n_chunks))
    cb, ca = direction * N_HEADS_REC, 2 * N_HEADS_REC + direction * N_HEADS_REC

    def step(wr, rd):
        u_w, wq_w, at_w, kd_w, et_w = wr
        u_r, wq_r, at_r, kd_r, et_r = rd
        states = [s_scr[g] for g in range(n_groups)]
        inst = []

        def chain(m):
            c, g = chunk_order[m // n_groups], m % n_groups
            rs, ls = slice(c * CHUNK, (c + 1) * CHUNK), slice(g * 128, (g + 1) * 128)
            s = states[g]
            ws_qs = _dot(wq_r[m], s)
            v_new = u_r[m] - ws_qs[:CHUNK]
            o_ref[rs, ls] = ws_qs[CHUNK:] + _dot(at_r[m], _bd2(v_new.astype(BF16)))
            states[g] = s * et_r[m][0:1] + jnp.where(bd_mask, _dot(kd_r[m], v_new, TN), 0.0)

        for c in chunk_order:
            rs = slice(c * CHUNK, (c + 1) * CHUNK)
            bl = bl_ref[rs, :]
            g16, gt2 = None, None
            for part in _split3(bl):
                a = jnp.dot(tri, part, preferred_element_type=F32)
                b = lax.dot_general(part, tri_t2, TN, preferred_element_type=F32)
                g16 = a if g16 is None else g16 + a
                gt2 = b if gt2 is None else gt2 + b
            for g in range(n_groups):
                h0, h1 = 2 * g, 2 * g + 1
                ls = slice(g * 128, (g + 1) * 128)
                d = dict(q=qkv_ref[rs, ls], k=qkv_ref[rs, D_REC + g * 128:D_REC + (g + 1) * 128],
                         v=qkv_ref[rs, 2 * D_REC + g * 128:2 * D_REC + (g + 1) * 128])
                d["beta"] = _pick2(bl[:, cb + h0:cb + h0 + 1], bl[:, cb + h1:cb + h1 + 1])
                d["g_i"] = _pick2(g16[:, ca + h0:ca + h0 + 1], g16[:, ca + h1:ca + h1 + 1])
                g_j = _pick2(gt2[ca + h0:ca + h0 + 1, :], gt2[ca + h1:ca + h1 + 1, :])
                d["g_tot"] = d["g_i"][last:last + 1]
                d["decay"] = jnp.where(incl, jnp.exp(jnp.where(incl, d["g_i"] - g_j, 0.0)), 0.0)
                inst.append(d)
        chain(0)
        chain(1)

        for d in inst:
            gram = _dot(jnp.concatenate([d["k"], d["q"]], axis=0), _bd2(d["k"].astype(BF16)), NT)
            d["low"] = jnp.where(strict, d["beta"] * gram[:CHUNK] * d["decay"], 0.0)
            d["attn"] = gram[CHUNK:] * d["decay"]

        for d in inst:
            d["t"] = eye - d["low"]
            (d["p"],) = _mm3_stack([d["low"]], d["low"])
        chain(2)
        chain(3)
        for it in range(4):
            for d in inst:
                p_next, tp = _mm3_stack([d["p"], d["t"]], d["p"])
                d["t"] = d["t"] + tp
                d["p"] = p_next
            if it < (n_inst - 4) // 2:
                chain(4 + 2 * it)
                chain(5 + 2 * it)
        for m, d in enumerate(inst):
            t_mat = d["t"] + _mm3_stack([d["t"]], d["p"])[0]
            eg = jnp.exp(d["g_i"])
            vb = d["v"] * d["beta"]
            kbg = d["k"] * d["beta"] * eg
            u_w[m] = _dot(t_mat, _bd2(vb.astype(BF16)))
            wq_w[m] = jnp.concatenate([_dot(t_mat, _bd2(kbg.astype(BF16))), d["q"] * eg], axis=0).astype(BF16)
            at_w[m] = d["attn"]
            kd_w[m] = (d["k"] * jnp.exp(d["g_tot"] - d["g_i"])).astype(BF16)
            et_w[m] = jnp.broadcast_to(jnp.exp(d["g_tot"]), (8, 128))

        for g in range(n_groups):
            s_scr[g] = states[g]

    @pl.when(n % 2 == 0)
    def _():
        step(sets[:half], sets[half:])

    @pl.when(n % 2 == 1)
    def _():
        step(sets[half:], sets[:half])

    @pl.when(jnp.logical_and(chain_live, sch_ref[3, prev] == 1))
    def _():
        for g in range(n_groups):
            sfin_ref[0, :, g * 128:(g + 1) * 128] = s_scr[g, :HEAD_DIM, :] + s_scr[g, HEAD_DIM:, :]


def _mm3_heads(lhs_list, rhs):
    r_hi, r_lo = _split2(rhs)
    w_hi, w_lo = _block_diag(r_hi), _block_diag(r_lo)
    parts = [_split2(a) for a in lhs_list]
    his = [p[0] for p in parts]
    los = [p[1] for p in parts]
    n = len(lhs_list) * CHUNK
    big = jnp.dot(jnp.concatenate(his + los, axis=0), w_hi, preferred_element_type=F32)
    small = jnp.dot(his[0] if len(his) == 1 else jnp.concatenate(his, axis=0), w_lo, preferred_element_type=F32)
    out = big[:n] + big[n:] + small
    return [out[m * CHUNK:(m + 1) * CHUNK] for m in range(len(lhs_list))]


def _pick_heads(vals):
    shape = (CHUNK, D_REC)
    head = _iota(shape, 1) // HEAD_DIM
    out = jnp.broadcast_to(vals[-1], shape)
    for h in range(N_HEADS_REC - 2, -1, -1):
        out = jnp.where(head == h, jnp.broadcast_to(vals[h], shape), out)
    return out


def _gdn_scan_kernel(sch_ref, qkv_ref, bl_ref, s0_ref, o_ref, sfin_ref, s_scr, *, reverse, direction):
    n = pl.program_id(0)

    @pl.when(sch_ref[2, n] == 1)
    def _():
        s_scr[...] = _block_diag(s0_ref[0])

    row, lane = _iota((CHUNK, D_REC), 0), _iota((CHUNK, D_REC), 1) % HEAD_DIM
    if reverse:
        incl, strict = row <= lane, row < lane
    else:
        incl, strict = row >= lane, row > lane
    eye = jnp.where(row == lane, 1.0, 0.0)
    tri = jnp.where(_tri(reverse), 1.0, 0.0).astype(BF16)
    tri_t = jnp.where((row >= lane) if reverse else (row <= lane), 1.0, 0.0).astype(BF16)
    bd_mask = _block_diag_mask()
    last = 0 if reverse else CHUNK - 1
    n_chunks = TS // CHUNK
    cb, ca = direction * N_HEADS_REC, 2 * N_HEADS_REC + direction * N_HEADS_REC

    inst = []
    for c in (range(n_chunks - 1, -1, -1) if reverse else range(n_chunks)):
        rs = slice(c * CHUNK, (c + 1) * CHUNK)
        bl = bl_ref[rs, :]
        g16, gt = None, None
        for part in _split3(bl):
            a = jnp.dot(tri, part, preferred_element_type=F32)
            b = lax.dot_general(part, tri_t, TN, preferred_element_type=F32)
            g16 = a if g16 is None else g16 + a
            gt = b if gt is None else gt + b
        d = dict(rs=rs, q=qkv_ref[rs, :D_REC], k=qkv_ref[rs, D_REC:2 * D_REC], v=qkv_ref[rs, 2 * D_REC:])
        d["beta"] = _pick_heads([bl[:, cb + h:cb + h + 1] for h in range(N_HEADS_REC)])
        d["g_i"] = _pick_heads([g16[:, ca + h:ca + h + 1] for h in range(N_HEADS_REC)])
        g_j = _pick_heads([gt[ca + h:ca + h + 1, :] for h in range(N_HEADS_REC)])
        d["g_tot"] = d["g_i"][last:last + 1]
        d["decay"] = jnp.where(incl, jnp.exp(jnp.where(incl, d["g_i"] - g_j, 0.0)), 0.0)
        inst.append(d)

    for d in inst:
        gram = _dot(jnp.concatenate([d["k"], d["q"]], axis=0), _block_diag(d["k"].astype(BF16)), NT)
        d["low"] = jnp.where(strict, d["beta"] * gram[:CHUNK] * d["decay"], 0.0)
        d["attn"] = gram[CHUNK:] * d["decay"]

    for d in inst:
        d["t"] = eye - d["low"]
        (d["p"],) = _mm3_heads([d["low"]], d["low"])
    for _ in range(4):
        for d in inst:
            p_next, tp = _mm3_heads([d["p"], d["t"]], d["p"])
            d["t"] = d["t"] + tp
            d["p"] = p_next
    for d in inst:
        t_mat = d["t"] + _mm3_heads([d["t"]], d["p"])[0]
        eg = jnp.exp(d["g_i"])
        vb = d["v"] * d["beta"]
        kbg = d["k"] * d["beta"] * eg
        d["u"] = _dot(t_mat, _block_diag(vb.astype(BF16)))
        d["w_qg"] = jnp.concatenate([_dot(t_mat, _block_diag(kbg.astype(BF16))), d["q"] * eg], axis=0).astype(BF16)
        d["k_dec"] = (d["k"] * jnp.exp(d["g_tot"] - d["g_i"])).astype(BF16)
        d["e_tot"] = jnp.exp(d["g_tot"])

    s = s_scr[...]
    for d in inst:
        ws_qs = _dot(d["w_qg"], s)
        v_new = d["u"] - ws_qs[:CHUNK]
        o_ref[d["rs"], :] = ws_qs[CHUNK:] + _dot(d["attn"], _block_diag(v_new.astype(BF16)))
        s = s * d["e_tot"] + jnp.where(bd_mask, _dot(d["k_dec"], v_new, TN), 0.0)
    s_scr[...] = s

    @pl.when(sch_ref[3, n] == 1)
    def _():
        sfin_ref[0] = _collapse_block_diag(s)


def _gdn_pipe4_kernel(sch_ref, qkv_ref, bl_ref, s0_ref, o_ref, sfin_ref, s_scr, *sets, reverse, direction):
    n = pl.program_id(0)
    n_chunks = TS // CHUNK
    prev = jnp.maximum(n - 1, 0)
    chain_live = n >= 1
    half = len(sets) // 2

    @pl.when(n == 0)
    def _():
        for ref in sets:
            ref[...] = jnp.zeros(ref.shape, ref.dtype)
        s_scr[...] = jnp.zeros(s_scr.shape, F32)

    @pl.when(jnp.logical_and(chain_live, sch_ref[2, prev] == 1))
    def _():
        s_scr[...] = _block_diag(s0_ref[0])

    row, lane = _iota((CHUNK, D_REC), 0), _iota((CHUNK, D_REC), 1) % HEAD_DIM
    if reverse:
        incl, strict = row <= lane, row < lane
    else:
        incl, strict = row >= lane, row > lane
    eye = jnp.where(row == lane, 1.0, 0.0)
    tri = jnp.where(_tri(reverse), 1.0, 0.0).astype(BF16)
    t_before_j = (row >= lane) if reverse else (row <= lane)
    ones8 = jnp.ones((8, CHUNK), BF16)
    bd_mask = _block_diag_mask()
    last = 0 if reverse else CHUNK - 1
    chunk_order = list(range(n_chunks - 1, -1, -1) if reverse else range(n_chunks))
    cb, ca = direction * N_HEADS_REC, 2 * N_HEADS_REC + direction * N_HEADS_REC

    def step(wr, rd):
        u_w, wq_w, at_w, kd_w, et_w = wr
        u_r, wq_r, at_r, kd_r, et_r = rd
        state = [s_scr[...]]
        inst = []

        def chain(m):
            c = chunk_order[m]
            rs = slice(c * CHUNK, (c + 1) * CHUNK)
            s = state[0]
            ws_qs = _dot(wq_r[m], s)
            v_new = u_r[m] - ws_qs[:CHUNK]
            o_ref[rs, :] = ws_qs[CHUNK:] + _dot(at_r[m], _block_diag(v_new.astype(BF16)))
            state[0] = s * et_r[m][0:1] + jnp.where(bd_mask, _dot(kd_r[m], v_new, TN), 0.0)

        for c in chunk_order:
            rs = slice(c * CHUNK, (c + 1) * CHUNK)
            bl = bl_ref[rs, :]
            d = dict(q=qkv_ref[rs, :D_REC], k=qkv_ref[rs, D_REC:2 * D_REC], v=qkv_ref[rs, 2 * D_REC:])
            d["beta"] = _pick_heads([bl[:, cb + h:cb + h + 1] for h in range(N_HEADS_REC)])
            la = _pick_heads([bl[:, ca + h:ca + h + 1] for h in range(N_HEADS_REC)])
            g_i, g_j = None, None
            for p_i, p_j in zip(_split3(la), _split3(jnp.where(t_before_j, la, 0.0))):
                a = jnp.dot(tri, p_i, preferred_element_type=F32)
                b = jnp.dot(ones8, p_j, preferred_element_type=F32)
                g_i = a if g_i is None else g_i + a
                g_j = b if g_j is None else g_j + b
            d["g_i"] = g_i
            d["g_tot"] = g_i[last:last + 1]
            d["decay"] = jnp.where(incl, jnp.exp(jnp.where(incl, g_i - g_j[0:1], 0.0)), 0.0)
            inst.append(d)
        chain(0)

        for d in inst:
            gram = _dot(jnp.concatenate([d["k"], d["q"]], axis=0), _block_diag(d["k"].astype(BF16)), NT)
            d["low"] = jnp.where(strict, d["beta"] * gram[:CHUNK] * d["decay"], 0.0)
            d["attn"] = gram[CHUNK:] * d["decay"]

        for d in inst:
            d["t"] = eye - d["low"]
            (d["p"],) = _mm3_heads([d["low"]], d["low"])
        for it in range(4):
            for d in inst:
                p_next, tp = _mm3_heads([d["p"], d["t"]], d["p"])
                d["t"] = d["t"] + tp
                d["p"] = p_next
            if 1 + it < n_chunks:
                chain(1 + it)
        for m, d in enumerate(inst):
            t_mat = d["t"] + _mm3_heads([d["t"]], d["p"])[0]
            eg = jnp.exp(d["g_i"])
            vb = d["v"] * d["beta"]
            kbg = d["k"] * d["beta"] * eg
            u_w[m] = _dot(t_mat, _block_diag(vb.astype(BF16)))
            wq_w[m] = jnp.concatenate([_dot(t_mat, _block_diag(kbg.astype(BF16))), d["q"] * eg], axis=0).astype(BF16)
            at_w[m] = d["attn"]
            kd_w[m] = (d["k"] * jnp.exp(d["g_tot"] - d["g_i"])).astype(BF16)
            et_w[m] = jnp.broadcast_to(jnp.exp(d["g_tot"]), (8, D_REC))

        s_scr[...] = state[0]

    @pl.when(n % 2 == 0)
    def _():
        step(sets[:half], sets[half:])

    @pl.when(n % 2 == 1)
    def _():
        step(sets[half:], sets[:half])

    @pl.when(jnp.logical_and(chain_live, sch_ref[3, prev] == 1))
    def _():
        sfin_ref[0] = _collapse_block_diag(s_scr[...])


def _gdn_bidir_kernel(sch_ref, qkvf_ref, blf_ref, qkvb_ref, blb_ref, s0f_ref, s0b_ref,
                      of_ref, sff_ref, ob_ref, sfb_ref, s_scr, *sets):
    n = pl.program_id(0)
    n_chunks = TS // CHUNK
    prev = jnp.maximum(n - 1, 0)
    chain_live = n >= 1
    half = len(sets) // 2
    lanes = ((False, qkvf_ref, blf_ref, s0f_ref, of_ref, sff_ref), (True, qkvb_ref, blb_ref, s0b_ref, ob_ref, sfb_ref))

    @pl.when(n == 0)
    def _():
        for ref in sets:
            ref[...] = jnp.zeros(ref.shape, ref.dtype)
        s_scr[...] = jnp.zeros(s_scr.shape, F32)

    for ln, lane_refs in enumerate(lanes):
        @pl.when(jnp.logical_and(chain_live, sch_ref[4 * ln + 2, prev] == 1))
        def _(ln=ln, s0_ref=lane_refs[3]):
            s_scr[ln] = _block_diag(s0_ref[0])

    row, lane = _iota((CHUNK, D_REC), 0), _iota((CHUNK, D_REC), 1) % HEAD_DIM
    eye = jnp.where(row == lane, 1.0, 0.0)
    ones8 = jnp.ones((8, CHUNK), BF16)
    bd_mask = _block_diag_mask()

    def step(wr, rd):
        u_w, wq_w, at_w, kd_w, et_w = wr
        u_r, wq_r, at_r, kd_r, et_r = rd
        state = [s_scr[0], s_scr[1]]
        inst = []

        def chain(m):
            for ln, (reverse, _, _, _, o_ref, _) in enumerate(lanes):
                c = (n_chunks - 1 - m) if reverse else m
                rs = slice(c * CHUNK, (c + 1) * CHUNK)
                k = ln * n_chunks + m
                s = state[ln]
                ws_qs = _dot(wq_r[k], s)
                v_new = u_r[k] - ws_qs[:CHUNK]
                o_ref[rs, :] = ws_qs[CHUNK:] + _dot(at_r[k], _block_diag(v_new.astype(BF16)))
                state[ln] = s * et_r[k][0:1] + jnp.where(bd_mask, _dot(kd_r[k], v_new, TN), 0.0)

        for ln, (reverse, qkv_ref, bl_ref, _, _, _) in enumerate(lanes):
            incl = (row <= lane) if reverse else (row >= lane)
            tri = jnp.where(_tri(reverse), 1.0, 0.0).astype(BF16)
            t_before_j = (row >= lane) if reverse else (row <= lane)
            last = 0 if reverse else CHUNK - 1
            cb, ca = ln * N_HEADS_REC, 2 * N_HEADS_REC + ln * N_HEADS_REC
            for m in range(n_chunks):
                c = (n_chunks - 1 - m) if reverse else m
                rs = slice(c * CHUNK, (c + 1) * CHUNK)
                bl = bl_ref[rs, :]
                d = dict(q=qkv_ref[rs, :D_REC], k=qkv_ref[rs, D_REC:2 * D_REC], v=qkv_ref[rs, 2 * D_REC:],
                         strict=(row < lane) if reverse else (row > lane), slot=ln * n_chunks + m)
                d["beta"] = _pick_heads([bl[:, cb + h:cb + h + 1] for h in range(N_HEADS_REC)])
                la = _pick_heads([bl[:, ca + h:ca + h + 1] for h in range(N_HEADS_REC)])
                g_i, g_j = None, None
                for p_i, p_j in zip(_split3(la), _split3(jnp.where(t_before_j, la, 0.0))):
                    a = jnp.dot(tri, p_i, preferred_element_type=F32)
                    b = jnp.dot(ones8, p_j, preferred_element_type=F32)
                    g_i = a if g_i is None else g_i + a
                    g_j = b if g_j is None else g_j + b
                d["g_i"] = g_i
                d["g_tot"] = g_i[last:last + 1]
                d["decay"] = jnp.where(incl, jnp.exp(jnp.where(incl, g_i - g_j[0:1], 0.0)), 0.0)
                inst.append(d)
        chain(0)

        for d in inst:
            gram = _dot(jnp.concatenate([d["k"], d["q"]], axis=0), _block_diag(d["k"].astype(BF16)), NT)
            d["low"] = jnp.where(d["strict"], d["beta"] * gram[:CHUNK] * d["decay"], 0.0)
            d["attn"] = gram[CHUNK:] * d["decay"]

        for d in inst:
            d["t"] = eye - d["low"]
            (d["p"],) = _mm3_heads([d["low"]], d["low"])
        for it in range(4):
            for d in inst:
                p_next, tp = _mm3_heads([d["p"], d["t"]], d["p"])
                d["t"] = d["t"] + tp
                d["p"] = p_next
            if 1 + it < n_chunks:
                chain(1 + it)
        for d in inst:
            k = d["slot"]
            t_mat = d["t"] + _mm3_heads([d["t"]], d["p"])[0]
            eg = jnp.exp(d["g_i"])
            vb = d["v"] * d["beta"]
            kbg = d["k"] * d["beta"] * eg
            u_w[k] = _dot(t_mat, _block_diag(vb.astype(BF16)))
            wq_w[k] = jnp.concatenate([_dot(t_mat, _block_diag(kbg.astype(BF16))), d["q"] * eg], axis=0).astype(BF16)
            at_w[k] = d["attn"]
            kd_w[k] = (d["k"] * jnp.exp(d["g_tot"] - d["g_i"])).astype(BF16)
            et_w[k] = jnp.broadcast_to(jnp.exp(d["g_tot"]), (8, D_REC))

        s_scr[0] = state[0]
        s_scr[1] = state[1]

    @pl.when(n % 2 == 0)
    def _():
        step(sets[:half], sets[half:])

    @pl.when(n % 2 == 1)
    def _():
        step(sets[half:], sets[:half])

    for ln, lane_refs in enumerate(lanes):
        @pl.when(jnp.logical_and(chain_live, sch_ref[4 * ln + 3, prev] == 1))
        def _(ln=ln, sfin_ref=lane_refs[5]):
            sfin_ref[0] = _collapse_block_diag(s_scr[ln])


def _gdn_scan_bidir(qkv, bl, s0_f, s0_b, sched):
    n = qkv.shape[0]
    n_seq = s0_f.shape[0]
    n_steps = sched.shape[1]
    n_inst = 2 * (TS // CHUNK)
    ahead = lambda i: jnp.minimum(i, n_steps - 1)
    behind = lambda i: jnp.maximum(i - 1, 0)
    scratch_set = [pltpu.VMEM((n_inst, CHUNK, D_REC), F32), pltpu.VMEM((n_inst, 2 * CHUNK, D_REC), BF16),
                   pltpu.VMEM((n_inst, CHUNK, D_REC), F32), pltpu.VMEM((n_inst, CHUNK, D_REC), BF16),
                   pltpu.VMEM((n_inst, 8, D_REC), F32)]

    def tile_in(width, ln):
        return pl.BlockSpec((TS, width), lambda i, sch: (sch[4 * ln, ahead(i)], 0))

    def tile_out(ln):
        return pl.BlockSpec((TS, D_REC), lambda i, sch: (sch[4 * ln, behind(i)], 0))

    def state_spec(ln):
        return pl.BlockSpec((1, HEAD_DIM, D_REC), lambda i, sch: (sch[4 * ln + 1, behind(i)], 0, 0))

    grid_spec = pltpu.PrefetchScalarGridSpec(
        num_scalar_prefetch=1,
        grid=(n_steps + 1,),
        in_specs=[tile_in(3 * D_REC, 0), tile_in(128, 0), tile_in(3 * D_REC, 1), tile_in(128, 1),
                  state_spec(0), state_spec(1)],
        out_specs=[tile_out(0), state_spec(0), tile_out(1), state_spec(1)],
        scratch_shapes=[pltpu.VMEM((2, D_REC, D_REC), F32)] + scratch_set + scratch_set,
    )
    o_sds = jax.ShapeDtypeStruct((n, D_REC), F32)
    s_sds = jax.ShapeDtypeStruct((n_seq, HEAD_DIM, D_REC), F32)
    return pl.pallas_call(
        _gdn_bidir_kernel,
        out_shape=[o_sds, s_sds, o_sds, s_sds],
        grid_spec=grid_spec,
        compiler_params=pltpu.CompilerParams(dimension_semantics=("arbitrary",), vmem_limit_bytes=VMEM_LIMIT),
        name="gdn_scan",
    )(sched, qkv, bl, qkv, bl, s0_f, s0_b)


def _gdn_scan(qkv, bl, s0, sched, reverse, direction):
    n = qkv.shape[0]
    n_seq = s0.shape[0]
    n_steps = sched.shape[1]
    n_inst = TS // CHUNK
    ahead = lambda i: jnp.minimum(i, n_steps - 1)
    behind = lambda i: jnp.maximum(i - 1, 0)
    scratch_set = [pltpu.VMEM((n_inst, CHUNK, D_REC), F32), pltpu.VMEM((n_inst, 2 * CHUNK, D_REC), BF16),
                   pltpu.VMEM((n_inst, CHUNK, D_REC), F32), pltpu.VMEM((n_inst, CHUNK, D_REC), BF16),
                   pltpu.VMEM((n_inst, 8, D_REC), F32)]
    grid_spec = pltpu.PrefetchScalarGridSpec(
        num_scalar_prefetch=1,
        grid=(n_steps + 1,),
        in_specs=[pl.BlockSpec((TS, 3 * D_REC), lambda i, sch: (sch[0, ahead(i)], 0)),
                  pl.BlockSpec((TS, 128), lambda i, sch: (sch[0, ahead(i)], 0)),
                  pl.BlockSpec((1, HEAD_DIM, D_REC), lambda i, sch: (sch[1, behind(i)], 0, 0))],
        out_specs=[pl.BlockSpec((TS, D_REC), lambda i, sch: (sch[0, behind(i)], 0)),
                   pl.BlockSpec((1, HEAD_DIM, D_REC), lambda i, sch: (sch[1, behind(i)], 0, 0))],
        scratch_shapes=[pltpu.VMEM((D_REC, D_REC), F32)] + scratch_set + scratch_set,
    )
    return pl.pallas_call(
        functools.partial(_gdn_pipe4_kernel, reverse=reverse, direction=direction),
        out_shape=[jax.ShapeDtypeStruct((n, D_REC), F32), jax.ShapeDtypeStruct((n_seq, HEAD_DIM, D_REC), F32)],
        grid_spec=grid_spec,
        compiler_params=pltpu.CompilerParams(dimension_semantics=("arbitrary",), vmem_limit_bytes=VMEM_LIMIT),
        name="gdn_scan_bwd" if reverse else "gdn_scan_fwd",
    )(sched, qkv, bl, s0)


def _pair_dots(lhs, w, dims):
    outs = []
    for p in range(N_HEADS_REC // 2):
        sl = slice(p * 128, (p + 1) * 128)
        outs.append(lax.dot_general(lhs[:, sl], _bd2(w[:, sl]), dims, preferred_element_type=F32))
    return jnp.concatenate(outs, axis=1)


def _hgrn_tile_kernel(sch_ref, f_ref, i_ref, q_ref, lb_ref, sel_ref, s0_ref, o_ref, sfin_ref, s_scr, *, reverse):
    n = pl.program_id(0)

    n_pairs = N_HEADS_REC // 2

    @pl.when(sch_ref[2, n] == 1)
    def _():
        for p in range(n_pairs):
            s_scr[p] = _bd2(s0_ref[0, :, p * 128:(p + 1) * 128])

    lb = lb_ref[...]
    tri = jnp.where(_tri(reverse), 1.0, 0.0).astype(BF16)
    row = _iota((CHUNK, D_REC), 0)
    col_tok = _iota((CHUNK, D_REC), 1) % HEAD_DIM
    ones = _head_ones()
    last = 0 if reverse else CHUNK - 1
    n_chunks = TS // CHUNK
    chunk_order = list(range(n_chunks - 1, -1, -1) if reverse else range(n_chunks))

    inst = []
    for c in chunk_order:
        rs = slice(c * CHUNK, (c + 1) * CHUNK)
        f = lb + (1.0 - lb) * _sigmoid(f_ref[rs, :])
        d = dict(rs=rs, kk=1.0 - f, q=q_ref[rs, :], v=i_ref[rs, :])
        d["b"] = _dot_sel(tri, jnp.log(jnp.maximum(f, TINY)))
        inst.append(d)

    sel = sel_ref[...]
    for d in inst:
        parts = _split3(d["b"])
        d["b_mid"] = sum(jnp.dot(sel, part, preferred_element_type=F32) for part in parts)

    for lvl in range(N_LEVELS):
        s_half = (CHUNK // 2) >> lvl
        in_block = row % (2 * s_half)
        later = (in_block < s_half) if reverse else (in_block >= s_half)
        same_block = (row // (2 * s_half)) == (col_tok // (2 * s_half))
        for d in inst:
            e = jnp.exp(-jnp.abs(d["b"] - d["b_mid"][lvl * CHUNK:(lvl + 1) * CHUNK]))
            q_hat = jnp.where(later, d["q"] * e, 0.0).astype(BF16)
            k_hat = jnp.where(later, 0.0, d["kk"] * e).astype(BF16)
            res = _pair_dots(q_hat, k_hat, NT)
            term = jnp.where(same_block, res, 0.0)
            d["a"] = term if lvl == 0 else d["a"] + term

    ones2 = ones[:128, :128]
    for d in inst:
        qk = (d["q"] * d["kk"]).astype(BF16)
        sums = jnp.concatenate([jnp.dot(qk[:, p * 128:(p + 1) * 128], ones2, preferred_element_type=F32)
                                for p in range(n_pairs)], axis=1)
        d["a"] = d["a"] + jnp.where(col_tok == row, sums, 0.0)

    bd2_mask = (_iota((128, 128), 0) // HEAD_DIM) == (_iota((128, 128), 1) // HEAD_DIM)
    for d in inst:
        b = d["b"]
        b_last = b[last:last + 1]
        d["av"] = _pair_dots(d["a"].astype(BF16), d["v"].astype(BF16), NN)
        v16, kd16 = d["v"].astype(BF16), (d["kk"] * jnp.exp(b_last - b)).astype(BF16)
        d["kv"] = [jnp.where(bd2_mask, lax.dot_general(v16[:, p * 128:(p + 1) * 128], kd16[:, p * 128:(p + 1) * 128],
                                                       TN, preferred_element_type=F32), 0.0) for p in range(n_pairs)]
        d["qb"] = (d["q"] * jnp.exp(b)).astype(BF16)
        d["e_last"] = jnp.exp(b_last)

    states = [s_scr[p] for p in range(n_pairs)]
    for d in inst:
        qs = [lax.dot_general(d["qb"][:, p * 128:(p + 1) * 128], states[p].astype(BF16), NT,
                              preferred_element_type=F32) for p in range(n_pairs)]
        o_ref[d["rs"], :] = d["av"] + jnp.concatenate(qs, axis=1)
        for p in range(n_pairs):
            states[p] = states[p] * d["e_last"][:, p * 128:(p + 1) * 128] + d["kv"][p]
    for p in range(n_pairs):
        s_scr[p] = states[p]

    @pl.when(sch_ref[3, n] == 1)
    def _():
        for p in range(n_pairs):
            sfin_ref[0, :, p * 128:(p + 1) * 128] = states[p][:HEAD_DIM] + states[p][HEAD_DIM:]


def _hgrn_scan_kernel(sch_ref, f_ref, i_ref, q_ref, lb_ref, sel_ref, s0_ref, o_ref, sfin_ref, s_scr, *, reverse):
    n = pl.program_id(0)

    @pl.when(sch_ref[2, n] == 1)
    def _():
        s_scr[...] = _block_diag(s0_ref[0])

    lb = lb_ref[...]
    sel = sel_ref[...]
    row = _iota((CHUNK, D_REC), 0)
    col_tok = _iota((CHUNK, D_REC), 1) % HEAD_DIM
    ones = _head_ones()
    bd_mask = _block_diag_mask()
    last = 0 if reverse else CHUNK - 1
    n_chunks = TS // CHUNK
    chunk_order = list(range(n_chunks - 1, -1, -1) if reverse else range(n_chunks))

    inst = []
    for c in chunk_order:
        rs = slice(c * CHUNK, (c + 1) * CHUNK)
        f = lb + (1.0 - lb) * _sigmoid(f_ref[rs, :])
        d = dict(rs=rs, kk=1.0 - f, q=q_ref[rs, :], v=i_ref[rs, :])
        cums = _dot_sel(sel, jnp.log(jnp.maximum(f, TINY)))
        d["b"] = cums[:CHUNK]
        d["b_mid_small"] = [cums[CHUNK:2 * CHUNK], cums[2 * CHUNK:]]
        inst.append(d)

    for lvl in range(N_LEVELS):
        s_half = (CHUNK // 2) >> lvl
        in_block = row % (2 * s_half)
        later = (in_block < s_half) if reverse else (in_block >= s_half)
        same_block = (row // (2 * s_half)) == (col_tok // (2 * s_half))
        for d in inst:
            b = d["b"]
            if s_half >= 4:
                pieces = []
                for blk in range(CHUNK // (2 * s_half)):
                    r = blk * 2 * s_half + (s_half if reverse else s_half - 1)
                    pieces.append(jnp.broadcast_to(b[r:r + 1], (2 * s_half, D_REC)))
                b_mid = pieces[0] if len(pieces) == 1 else jnp.concatenate(pieces, axis=0)
            else:
                b_mid = d["b_mid_small"][lvl - (N_LEVELS - 2)]
            e = jnp.exp(-jnp.abs(b - b_mid))
            q_hat = jnp.where(later, d["q"] * e, 0.0)
            k_hat = jnp.where(later, 0.0, d["kk"] * e)
            res = _dot(q_hat, _block_diag(k_hat.astype(BF16)), NT)
            term = jnp.where(same_block, res, 0.0)
            d["a"] = term if lvl == 0 else d["a"] + term

    for d in inst:
        sums = _dot(d["q"] * d["kk"], ones)
        d["a"] = d["a"] + jnp.where(col_tok == row, sums, 0.0)

    for d in inst:
        b = d["b"]
        b_last = b[last:last + 1]
        d["av"] = _dot(d["a"], _block_diag(d["v"].astype(BF16)))
        d["kv"] = jnp.where(bd_mask, _dot(d["v"], d["kk"] * jnp.exp(b_last - b), TN), 0.0)
        d["qb"] = (d["q"] * jnp.exp(b)).astype(BF16)
        d["e_last"] = jnp.exp(b_last)

    s = s_scr[...]
    for d in inst:
        o_ref[d["rs"], :] = d["av"] + _dot(d["qb"], s, NT)
        s = s * d["e_last"] + d["kv"]
    s_scr[...] = s

    @pl.when(sch_ref[3, n] == 1)
    def _():
        sfin_ref[0] = _collapse_block_diag(s)


def _hgrn_scan(hg, lb_e, s0, sched, reverse):
    n = hg.shape[0]
    n_seq = s0.shape[0]
    col = lambda c: pl.BlockSpec((TS, D_REC), lambda i, sch: (sch[0, i], c))
    i = np.arange(CHUNK)
    tri = (i[None, :] >= i[:, None]) if reverse else (i[None, :] <= i[:, None])
    sel = [tri]
    for s_half in (2, 1):
        mid = (i // (2 * s_half)) * (2 * s_half) + (s_half if reverse else s_half - 1)
        sel.append(tri[mid])
    sel = jnp.asarray(np.concatenate(sel, axis=0).astype(np.float32), BF16)
    grid_spec = pltpu.PrefetchScalarGridSpec(
        num_scalar_prefetch=1,
        grid=(sched.shape[1],),
        in_specs=[col(1 if reverse else 0), col(2), col(3),
                  pl.BlockSpec((1, D_REC), lambda i, sch: (0, 0)),
                  pl.BlockSpec((3 * CHUNK, CHUNK), lambda i, sch: (0, 0)),
                  pl.BlockSpec((1, HEAD_DIM, D_REC), lambda i, sch: (sch[1, i], 0, 0))],
        out_specs=[pl.BlockSpec((TS, D_REC), lambda i, sch: (sch[0, i], 0)),
                   pl.BlockSpec((1, HEAD_DIM, D_REC), lambda i, sch: (sch[1, i], 0, 0))],
        scratch_shapes=[pltpu.VMEM((D_REC, D_REC), F32)],
    )
    return pl.pallas_call(
        functools.partial(_hgrn_scan_kernel, reverse=reverse),
        out_shape=[jax.ShapeDtypeStruct((n, D_REC), F32), jax.ShapeDtypeStruct((n_seq, HEAD_DIM, D_REC), F32)],
        grid_spec=grid_spec,
        compiler_params=pltpu.CompilerParams(dimension_semantics=("arbitrary",), vmem_limit_bytes=VMEM_LIMIT),
        name="hgrn_scan_bwd" if reverse else "hgrn_scan_fwd",
    )(sched, hg, hg, hg, lb_e, sel, s0)


def _gated_readout(o, gate, w_e, ones):
    ms = _dot_sel(ones, o * o, sel_is_lhs=False) * (1.0 / HEAD_DIM)
    return o * lax.rsqrt(ms + NORM_EPS) * w_e * _silu(gate)


def _out_proj_kernel(*refs, alpha, n_ctx_tiles, n_x):
    x_refs = refs[:n_x]
    (mod_ref, oa_ref, bf_ref, bb_ref, cf_ref, cb_ref, gb_ref, gc_ref, nwb_ref, nwc_ref,
     w_ref, lng_ref, lnb_ref, y_ref) = refs[n_x:]
    x = _read_tokens(x_refs, n_ctx_tiles)
    m = mod_ref[0]
    ones = _head_ones()
    o_b = _gated_readout(bf_ref[...] + bb_ref[...], gb_ref[...], nwb_ref[...], ones)
    o_c = _gated_readout(cf_ref[...] + cb_ref[...], gc_ref[...], nwc_ref[...], ones)
    mix = jnp.dot(oa_ref[...].astype(BF16), w_ref[0:D_A], preferred_element_type=F32)
    mix = mix + jnp.dot(o_b.astype(BF16), w_ref[D_A:D_A + D_REC], preferred_element_type=F32)
    mix = mix + jnp.dot(o_c.astype(BF16), w_ref[D_A + D_REC:], preferred_element_type=F32)
    y_ref[...] = _layer_norm(alpha * x + m[2:3] * mix, lng_ref[...], lnb_ref[...])


def _out_proj(x_parts, mod_l, o_a, ob_f, ob_b, oc_f, oc_b, g_b, hg, nw_b, nw_c, w_out, ln_g, ln_b, mod_index, alpha,
              n_ctx_tiles):
    n = sum(p.shape[0] for p in x_parts)
    tile = lambda w: pl.BlockSpec((TM, w), lambda i: (i, 0))
    row = lambda w: pl.BlockSpec((1, w), lambda i: (0, 0))
    return pl.pallas_call(
        functools.partial(_out_proj_kernel, alpha=alpha, n_ctx_tiles=n_ctx_tiles, n_x=len(x_parts)),
        out_shape=jax.ShapeDtypeStruct((n, D_MODEL), F32),
        grid=(n // TM,),
        in_specs=_token_specs(x_parts, n_ctx_tiles) + [
                  pl.BlockSpec((1, 6, D_MODEL), lambda i: (mod_index(i), 0, 0)),
                  tile(D_A), tile(D_REC), tile(D_REC), tile(D_REC), tile(D_REC), tile(D_REC),
                  pl.BlockSpec((TM, D_REC), lambda i: (i, 4)),
                  row(D_REC), row(D_REC),
                  pl.BlockSpec((D_MODEL, D_MODEL), lambda i: (0, 0)),
                  row(D_MODEL), row(D_MODEL)],
        out_specs=tile(D_MODEL),
        compiler_params=pltpu.CompilerParams(dimension_semantics=("arbitrary",), vmem_limit_bytes=VMEM_LIMIT),
        name="out_proj_ln",
    )(*x_parts, mod_l, o_a, ob_f, ob_b, oc_f, oc_b, g_b, hg, nw_b, nw_c, w_out, ln_g, ln_b)


def _ffn_kernel(x_ref, mod_ref, w1_ref, w3_ref, w2_ref, lng_ref, lnb_ref, y_ref, *, alpha):
    m = mod_ref[0]
    x = x_ref[...]
    h = (x * (1.0 + m[4:5]) + m[3:4]).astype(BF16)
    acc = jnp.zeros((x.shape[0], D_MODEL), F32)
    for c in range(D_FF // FF_CHUNK):
        sl = slice(c * FF_CHUNK, (c + 1) * FF_CHUNK)
        a = jnp.dot(h, w1_ref[:, sl], preferred_element_type=F32)
        g = jnp.dot(h, w3_ref[:, sl], preferred_element_type=F32)
        acc = acc + jnp.dot((_silu(a) * g).astype(BF16), w2_ref[sl, :], preferred_element_type=F32)
    y_ref[...] = _layer_norm(alpha * x + m[5:6] * acc, lng_ref[...], lnb_ref[...])


def _ffn(x, mod_l, w1, w3, w2, ln_g, ln_b, mod_index_ffn, alpha):
    n = x.shape[0]
    whole = lambda a: pl.BlockSpec(a.shape, lambda i: (0, 0), pipeline_mode=pl.Buffered(1))
    row = pl.BlockSpec((1, D_MODEL), lambda i: (0, 0))
    return pl.pallas_call(
        functools.partial(_ffn_kernel, alpha=alpha),
        out_shape=jax.ShapeDtypeStruct((n, D_MODEL), F32),
        grid=(n // TM_FFN,),
        in_specs=[pl.BlockSpec((TM_FFN, D_MODEL), lambda i: (i, 0)),
                  pl.BlockSpec((1, 6, D_MODEL), lambda i: (mod_index_ffn(i), 0, 0)),
                  whole(w1), whole(w3), whole(w2), row, row],
        out_specs=pl.BlockSpec((TM_FFN, D_MODEL), lambda i: (i, 0)),
        compiler_params=pltpu.CompilerParams(dimension_semantics=("arbitrary",), vmem_limit_bytes=VMEM_LIMIT),
        name="ffn_ln",
    )(x, mod_l, w1, w3, w2, ln_g, ln_b)


def _moe_kernel(x_ref, mod_ref, wr_ref, br_ref, w1_ref, w3_ref, w2_ref, lng_ref, lnb_ref, y_ref,
                h_scr, comb_scr, acc_scr, *, alpha):
    e = pl.program_id(1)
    m = mod_ref[0]

    @pl.when(e == 0)
    def _():
        h = x_ref[...] * (1.0 + m[4:5]) + m[3:4]
        h_scr[...] = h.astype(BF16)
        logits = jnp.dot(h, wr_ref[...], preferred_element_type=F32, precision=lax.Precision.HIGHEST) + br_ref[...]
        lane = _iota(logits.shape, 1)
        m1 = jnp.max(logits, -1, keepdims=True)
        i1 = jnp.min(jnp.where(logits == m1, lane, 128), -1, keepdims=True)
        rest = jnp.where(lane == i1, NEG_BIG, logits)
        m2 = jnp.max(rest, -1, keepdims=True)
        i2 = jnp.min(jnp.where(rest == m2, lane, 128), -1, keepdims=True)
        e2 = jnp.exp(m2 - m1)
        g1 = 1.0 / (1.0 + e2)
        comb_scr[...] = jnp.where(lane == i1, g1, 0.0) + jnp.where(lane == i2, e2 * g1, 0.0)
        acc_scr[...] = jnp.zeros_like(acc_scr)

    h = h_scr[...]
    a = jnp.dot(h, w1_ref[0], preferred_element_type=F32)
    g = jnp.dot(h, w3_ref[0], preferred_element_type=F32)
    y = jnp.dot((_silu(a) * g).astype(BF16), w2_ref[0], preferred_element_type=F32)
    comb = comb_scr[...]
    gate = jnp.sum(jnp.where(_iota(comb.shape, 1) == e, comb, 0.0), -1, keepdims=True)
    acc_scr[...] += gate * y

    @pl.when(e == N_EXPERTS - 1)
    def _():
        y_ref[...] = _layer_norm(alpha * x_ref[...] + m[5:6] * acc_scr[...], lng_ref[...], lnb_ref[...])


def _moe_routed_kernel(x_ref, mod_ref, wr_ref, br_ref, w1_ref, w3_ref, w2_ref, lng_ref, lnb_ref, y_ref,
                       h_scr, mem_scr, gate_scr, rank_scr, acc_scr, earlier_scr, *, alpha):
    e = pl.program_id(1)
    m = mod_ref[0]
    t_tile = x_ref.shape[0]

    @pl.when(jnp.logical_and(pl.program_id(0) == 0, e == 0))
    def _():
        earlier_scr[...] = jnp.where(_iota((t_tile, t_tile), 0) < _iota((t_tile, t_tile), 1), 1.0, 0.0).astype(BF16)

    @pl.when(e == 0)
    def _():
        h = x_ref[...] * (1.0 + m[4:5]) + m[3:4]
        h_hi, h_lo = _split2(h)
        h_scr[...] = h_hi
        w_hi, w_lo = _split2(wr_ref[...])
        both = lax.dot_general(jnp.concatenate([w_hi, w_lo], axis=0), h_hi, NT, preferred_element_type=F32)
        lt = (both[:N_EXPERTS] + both[N_EXPERTS:] + lax.dot_general(w_hi, h_lo, NT, preferred_element_type=F32)
              + br_ref[...])
        eidx = _iota(lt.shape, 0)
        m1 = jnp.max(lt, 0, keepdims=True)
        i1 = jnp.min(jnp.where(lt == m1, eidx, N_EXPERTS), 0, keepdims=True)
        rest = jnp.where(eidx == i1, NEG_BIG, lt)
        m2 = jnp.max(rest, 0, keepdims=True)
        i2 = jnp.min(jnp.where(rest == m2, eidx, N_EXPERTS), 0, keepdims=True)
        e2 = jnp.exp(m2 - m1)
        g1 = 1.0 / (1.0 + e2)
        mem = jnp.where(eidx == i1, 1.0, jnp.where(eidx == i2, 1.0, 0.0))
        mem_scr[...] = mem
        gate_scr[...] = jnp.where(eidx == i1, g1, jnp.where(eidx == i2, e2 * g1, 0.0))
        rank_scr[...] = jnp.dot(mem.astype(BF16), earlier_scr[...], preferred_element_type=F32)
        acc_scr[...] = jnp.zeros_like(acc_scr)

    mem_row = mem_scr[pl.ds(e, 1), :]
    gate_row = gate_scr[pl.ds(e, 1), :]
    rank_row = rank_scr[pl.ds(e, 1), :]
    count = jnp.sum(mem_row).astype(jnp.int32)
    slot0 = _iota((MOE_BLOCK, t_tile), 0).astype(F32)

    def block(j, carry):
        slot = slot0 + (j * MOE_BLOCK).astype(F32)
        hit = jnp.where(rank_row == slot, mem_row, 0.0)
        sel = hit.astype(BF16)
        hb = jnp.dot(sel, h_scr[...], preferred_element_type=F32).astype(BF16)
        a = jnp.dot(hb, w1_ref[0], preferred_element_type=F32)
        g = jnp.dot(hb, w3_ref[0], preferred_element_type=F32)
        yb = jnp.dot((_silu(a) * g).astype(BF16), w2_ref[0], preferred_element_type=F32)
        gate_slot = jnp.sum(hit * gate_row, -1, keepdims=True)
        acc_scr[...] += lax.dot_general(sel, (yb * gate_slot).astype(BF16), TN, preferred_element_type=F32)
        return carry

    lax.fori_loop(0, (count + MOE_BLOCK - 1) // MOE_BLOCK, block, 0)

    @pl.when(e == N_EXPERTS - 1)
    def _():
        y_ref[...] = _layer_norm(alpha * x_ref[...] + m[5:6] * acc_scr[...], lng_ref[...], lnb_ref[...])


def _moe(x, mod_l, w_r, b_r, w1, w3, w2, ln_g, ln_b, mod_index_moe, alpha):
    n = x.shape[0]
    row = pl.BlockSpec((1, D_MODEL), lambda i, e: (0, 0))
    expert = lambda a: pl.BlockSpec((1,) + a.shape[1:], lambda i, e: (e, 0, 0))
    routing = pltpu.VMEM((N_EXPERTS, TM_MOE), F32)
    return pl.pallas_call(
        functools.partial(_moe_routed_kernel, alpha=alpha),
        out_shape=jax.ShapeDtypeStruct((n, D_MODEL), F32),
        grid=(n // TM_MOE, N_EXPERTS),
        in_specs=[pl.BlockSpec((TM_MOE, D_MODEL), lambda i, e: (i, 0)),
                  pl.BlockSpec((1, 6, D_MODEL), lambda i, e: (mod_index_moe(i), 0, 0)),
                  pl.BlockSpec((N_EXPERTS, D_MODEL), lambda i, e: (0, 0)),
                  pl.BlockSpec((N_EXPERTS, 1), lambda i, e: (0, 0)),
                  expert(w1), expert(w3), expert(w2), row, row],
        out_specs=pl.BlockSpec((TM_MOE, D_MODEL), lambda i, e: (i, 0)),
        scratch_shapes=[pltpu.VMEM((TM_MOE, D_MODEL), BF16), routing, routing, routing,
                        pltpu.VMEM((TM_MOE, D_MODEL), F32), pltpu.VMEM((TM_MOE, TM_MOE), BF16)],
        compiler_params=pltpu.CompilerParams(dimension_semantics=("arbitrary", "arbitrary"),
                                             vmem_limit_bytes=VMEM_LIMIT),
        name="moe_ln",
    )(x, mod_l, w_r, b_r, w1, w3, w2, ln_g, ln_b)


def _moe_dense(x, mod_l, w_r, b_r, w1, w3, w2, ln_g, ln_b, mod_index_moe, alpha):
    n = x.shape[0]
    row = pl.BlockSpec((1, D_MODEL), lambda i, e: (0, 0))
    expert = lambda a: pl.BlockSpec((1,) + a.shape[1:], lambda i, e: (e, 0, 0))
    return pl.pallas_call(
        functools.partial(_moe_kernel, alpha=alpha),
        out_shape=jax.ShapeDtypeStruct((n, D_MODEL), F32),
        grid=(n // TM_MOE, N_EXPERTS),
        in_specs=[pl.BlockSpec((TM_MOE, D_MODEL), lambda i, e: (i, 0)),
                  pl.BlockSpec((1, 6, D_MODEL), lambda i, e: (mod_index_moe(i), 0, 0)),
                  pl.BlockSpec((D_MODEL, 128), lambda i, e: (0, 0)),
                  pl.BlockSpec((1, 128), lambda i, e: (0, 0)),
                  expert(w1), expert(w3), expert(w2), row, row],
        out_specs=pl.BlockSpec((TM_MOE, D_MODEL), lambda i, e: (i, 0)),
        scratch_shapes=[pltpu.VMEM((TM_MOE, D_MODEL), BF16), pltpu.VMEM((TM_MOE, 128), F32),
                        pltpu.VMEM((TM_MOE, D_MODEL), F32)],
        compiler_params=pltpu.CompilerParams(dimension_semantics=("arbitrary", "arbitrary"),
                                             vmem_limit_bytes=VMEM_LIMIT),
        name="moe_ln",
    )(x, mod_l, w_r, b_r, w1, w3, w2, ln_g, ln_b)


def _rope_tables(lat_len):
    t = jnp.arange(lat_len)
    pos = jnp.stack([t // GRID_W, t % GRID_W], axis=1).astype(F32)
    half, quarter = HEAD_DIM // 2, HEAD_DIM // 4
    inv_freq = ROPE_THETA ** (-jnp.arange(quarter, dtype=F32) * 2.0 / half)
    lane = np.arange(HEAD_DIM)
    ang = pos[:, lane // half] * inv_freq[lane % quarter][None, :]
    sign = jnp.asarray(np.where((lane % half) < quarter, -1.0, 1.0), F32)
    cos = jnp.tile(jnp.cos(ang), (1, 128 // HEAD_DIM))
    sin = jnp.tile(jnp.sin(ang) * sign, (1, 128 // HEAD_DIM))
    return cos, sin


def kernel(x_prompt, x_sample, cache_k, cache_v, state_gdn, state_hgrn, c, c_ctx, w_mod, b_mod, w_in, conv_w, attn_sink, gdn_a_log, gdn_dt_bias, gdn_norm_w, hgrn_lb, hgrn_norm_w, w_out, ln_g, ln_b, ffn_w1, ffn_w3, ffn_w2, moe_router, moe_router_b, moe_w1, moe_w3, moe_w2):
    depth = w_in.shape[0]
    n_ctx, ctx_len, _ = x_prompt.shape
    lat_batch, lat_len, _ = x_sample.shape
    n_ctx_rows = n_ctx * ctx_len
    alpha = (2.0 * depth) ** 0.25

    x_parts = [x_prompt.reshape(-1, D_MODEL), x_sample.reshape(-1, D_MODEL)]

    def mod_index_for(tile):
        ctx_tiles, per_seq = n_ctx_rows // tile, lat_len // tile
        return lambda i: jnp.where(i < ctx_tiles, 0, 1 + (i - ctx_tiles) // per_seq)

    cvecs = jnp.concatenate([c_ctx[None], c], axis=0)
    mods = _modulation(cvecs, w_mod, b_mod).reshape(depth, cvecs.shape[0], 6, D_MODEL)

    cos_tab, sin_tab = _rope_tables(lat_len)
    sm = jax.nn.softmax(hgrn_lb.astype(F32), axis=0)
    lower = jnp.cumsum(sm, axis=0) - sm[0:1]

    w_in_r = jnp.concatenate([w_in[:, :, :1792], w_in[:, :, 1808:], w_in[:, :, 1792:1808],
                              jnp.zeros((depth, D_MODEL, C_END - 3088), F32)], axis=-1).astype(BF16)
    w_out_b = w_out.astype(BF16)

    zeros_state = jnp.zeros((n_ctx, HEAD_DIM, D_REC), F32)
    sched = [_scan_schedule(n_ctx, ctx_len, lat_batch, lat_len, reverse) for reverse in (False, True)]
    new_k, new_v, new_sg, new_sh = [], [], [], []

    for l in range(depth):
        qa, ka, va, qkvb, g_b, hg, ba = _in_proj(x_parts, mods[l], w_in_r[l], cos_tab, sin_tab, mod_index_for(TM),
                                                 n_ctx_rows // TM, lat_len // TM)
        new_k.append(ka[:n_ctx_rows].reshape(n_ctx, ctx_len, A_KV_HEADS, HEAD_DIM))
        new_v.append(va[:n_ctx_rows].reshape(n_ctx, ctx_len, A_KV_HEADS, HEAD_DIM))

        o_a = _attention(qa, ka, va, cache_k[:, l].reshape(lat_batch, -1, D_KV), cache_v[:, l].reshape(lat_batch, -1, D_KV),
                         attn_sink[l], n_ctx, ctx_len, lat_batch, lat_len)

        sg0 = state_gdn[:, l].transpose(0, 1, 3, 2, 4).reshape(lat_batch, 2, HEAD_DIM, D_REC)
        sh0 = state_hgrn[:, l].transpose(0, 1, 4, 2, 3).reshape(lat_batch, 2, HEAD_DIM, D_REC)
        ob, oc, sg_fin, sh_fin = [], [], [], []
        alog_row = jnp.zeros((1, 128), F32).at[0, 8:16].set(gdn_a_log[l].reshape(-1))
        dtb_row = jnp.zeros((1, 128), F32).at[0, 8:16].set(gdn_dt_bias[l].reshape(-1))
        qkv, bl = _gdn_prep(qkvb, ba, conv_w[l], alog_row, dtb_row, n_ctx_rows // TS, ctx_len // TS, lat_len // TS)
        o_f, s_f, o_b, s_b = _gdn_scan_bidir(qkv, bl, jnp.concatenate([zeros_state, sg0[:, 0]], axis=0),
                                             jnp.concatenate([zeros_state, sg0[:, 1]], axis=0),
                                             jnp.concatenate(sched, axis=0))
        ob = [o_f, o_b]
        sg_fin = [s[:n_ctx].reshape(n_ctx, HEAD_DIM, N_HEADS_REC, HEAD_DIM).transpose(0, 2, 1, 3) for s in (s_f, s_b)]
        for d, reverse in enumerate((False, True)):
            o_d, s_fin = _hgrn_scan(hg, lower[l, d][None], jnp.concatenate([zeros_state, sh0[:, d]], axis=0),
                                    sched[d], reverse)
            oc.append(o_d)
            sh_fin.append(s_fin[:n_ctx].reshape(n_ctx, HEAD_DIM, N_HEADS_REC, HEAD_DIM).transpose(0, 2, 3, 1))
        new_sg.append(jnp.stack(sg_fin, axis=1))
        new_sh.append(jnp.stack(sh_fin, axis=1))

        nw_b = jnp.tile(gdn_norm_w[l], N_HEADS_REC)[None]
        nw_c = jnp.tile(hgrn_norm_w[l], N_HEADS_REC)[None]
        x = _out_proj(x_parts, mods[l], o_a, ob[0], ob[1], oc[0], oc[1], g_b, hg, nw_b, nw_c, w_out_b[l],
                      ln_g[l, 0][None], ln_b[l, 0][None], mod_index_for(TM), alpha, n_ctx_rows // TM)

        i = l // 2
        if l % 2 == 0:
            x = _ffn(x, mods[l], ffn_w1[i].astype(BF16), ffn_w3[i].astype(BF16), ffn_w2[i].astype(BF16),
                     ln_g[l, 1][None], ln_b[l, 1][None], mod_index_for(TM_FFN), alpha)
        else:
            x = _moe(x, mods[l], moe_router[i].T, moe_router_b[i][:, None], moe_w1[i].astype(BF16), moe_w3[i].astype(BF16), moe_w2[i].astype(BF16),
                     ln_g[l, 1][None], ln_b[l, 1][None], mod_index_for(TM_MOE), alpha)
        x_parts = [x]

    y_prompt = x[:n_ctx_rows].reshape(x_prompt.shape)
    y_sample = x[n_ctx_rows:].reshape(x_sample.shape)
    return (y_prompt, y_sample, jnp.stack(new_k, axis=1), jnp.stack(new_v, axis=1),
            jnp.stack(new_sg, axis=1), jnp.stack(new_sh, axis=1))
```

```python
import functools

import numpy as np
import jax
import jax.numpy as jnp
from jax import lax
from jax.experimental import pallas as pl
from jax.experimental.pallas import tpu as pltpu

F32 = jnp.float32
BF16 = jnp.bfloat16

D_MODEL = 1024
HEAD_DIM = 64
A_HEADS = 8
A_KV_HEADS = 2
GRID_W = 64
ATTN_BLOCK = 128
N_HEADS_REC = 4
D_REC = N_HEADS_REC * HEAD_DIM
CONV_W = 5
CHUNK = 64
N_LEVELS = 6
D_A = A_HEADS * HEAD_DIM
D_KV = A_KV_HEADS * HEAD_DIM
D_FF = 2816
N_EXPERTS = 8
EXPERT_FF = 1024
ROPE_THETA = 10000.0
NORM_EPS = 1e-5
TINY = 1e-30
NEG_BIG = -1e30

C_QA, C_KA, C_VA, C_QKVB, C_GB, C_HG, C_BA, C_END = 0, 512, 640, 768, 1536, 1792, 3072, 3200
D_HG = 1280
HALO = 8

TM = 512
TS = 256
TM_FFN = 512
FF_CHUNK = 256
TM_MOE = 1024
MOE_BLOCK = 128
VMEM_LIMIT = 56 * 1024 * 1024

NN = (((1,), (0,)), ((), ()))
NT = (((1,), (1,)), ((), ()))
TN = (((0,), (0,)), ((), ()))


def _dot(a, b, dims=NN):
    return lax.dot_general(a.astype(BF16), b.astype(BF16), dims, preferred_element_type=F32)


def _split2(x):
    hi = x.astype(BF16)
    lo = (x - hi.astype(F32)).astype(BF16)
    return hi, lo


def _split3(x):
    hi = x.astype(BF16)
    r = x - hi.astype(F32)
    mid = r.astype(BF16)
    lo = (r - mid.astype(F32)).astype(BF16)
    return hi, mid, lo


def _dot_sel(sel, x, dims=NN, sel_is_lhs=True):
    out = None
    for part in _split3(x):
        a, b = (sel, part) if sel_is_lhs else (part, sel)
        t = lax.dot_general(a, b, dims, preferred_element_type=F32)
        out = t if out is None else out + t
    return out


def _sigmoid(x):
    return 1.0 / (1.0 + jnp.exp(-x))


def _silu(x):
    return x * _sigmoid(x)


def _layer_norm(x, g, b):
    mu = jnp.mean(x, -1, keepdims=True)
    xc = x - mu
    var = jnp.mean(xc * xc, -1, keepdims=True)
    return xc * lax.rsqrt(var + NORM_EPS) * g + b


def _iota(shape, dim):
    return lax.broadcasted_iota(jnp.int32, shape, dim)


def _head_ones():
    return jnp.where(_iota((D_REC, D_REC), 0) // HEAD_DIM == _iota((D_REC, D_REC), 1) // HEAD_DIM, 1.0, 0.0).astype(BF16)


def _block_diag(x):
    head = _iota(x.shape, 1) // HEAD_DIM
    return jnp.concatenate([jnp.where(head == h, x, jnp.zeros_like(x)) for h in range(N_HEADS_REC)], axis=0)


def _block_diag_mask():
    return _iota((D_REC, D_REC), 0) // HEAD_DIM == _iota((D_REC, D_REC), 1) // HEAD_DIM


def _collapse_block_diag(s):
    out = s[0:HEAD_DIM]
    for h in range(1, N_HEADS_REC):
        out = out + s[h * HEAD_DIM:(h + 1) * HEAD_DIM]
    return out


def _mod_kernel(ct_ref, w_ref, b_ref, o_ref):
    s = _silu(ct_ref[...])
    w = w_ref[0]
    rows = [jnp.sum(s[:, r:r + 1] * w, axis=0, keepdims=True) for r in range(o_ref.shape[1])]
    o_ref[0] = jnp.concatenate(rows, axis=0) + b_ref[0]


def _modulation(cvecs, w_mod, b_mod):
    depth, d, n6 = w_mod.shape
    r = cvecs.shape[0]
    tn = 1024
    ct = jnp.zeros((d, 8), F32).at[:, :r].set(cvecs.T)
    return pl.pallas_call(
        _mod_kernel,
        out_shape=jax.ShapeDtypeStruct((depth, r, n6), F32),
        grid=(depth, n6 // tn),
        in_specs=[pl.BlockSpec((d, 8), lambda l, j: (0, 0)),
                  pl.BlockSpec((1, d, tn), lambda l, j: (l, 0, j)),
                  pl.BlockSpec((1, 1, tn), lambda l, j: (l, 0, j))],
        out_specs=pl.BlockSpec((1, r, tn), lambda l, j: (l, 0, j)),
        compiler_params=pltpu.CompilerParams(dimension_semantics=("arbitrary", "arbitrary"),
                                             vmem_limit_bytes=VMEM_LIMIT),
        name="modulation",
    )(ct, w_mod, b_mod.reshape(depth, 1, n6))


def _rope(x, cos, sin):
    outs = []
    lane = _iota(cos.shape, 1)
    first = (lane % 32) < 16
    for m in range(x.shape[1] // 128):
        xs = x[:, m * 128:(m + 1) * 128]
        swapped = jnp.where(first, pltpu.roll(xs, 128 - 16, 1), pltpu.roll(xs, 16, 1))
        outs.append(xs * cos + swapped * sin)
    return outs[0] if len(outs) == 1 else jnp.concatenate(outs, axis=1)


def _token_specs(x_parts, n_ctx_tiles):
    if len(x_parts) == 1:
        return [pl.BlockSpec((TM, D_MODEL), lambda i: (i, 0))]
    return [pl.BlockSpec((TM, D_MODEL), lambda i: (jnp.minimum(i, n_ctx_tiles - 1), 0)),
            pl.BlockSpec((TM, D_MODEL), lambda i: (jnp.maximum(i - n_ctx_tiles, 0), 0))]


def _read_tokens(x_refs, n_ctx_tiles):
    if len(x_refs) == 1:
        return x_refs[0][...]
    return jnp.where(pl.program_id(0) < n_ctx_tiles, x_refs[0][...], x_refs[1][...])


def _in_proj_kernel(*refs, n_ctx_tiles, n_x):
    x_refs = refs[:n_x]
    mod_ref, w_ref, cos_ref, sin_ref, qa_ref, ka_ref, va_ref, qkvb_ref, gb_ref, hg_ref, ba_ref = refs[n_x:]
    i = pl.program_id(0)
    m = mod_ref[0]
    h = (_read_tokens(x_refs, n_ctx_tiles) * (1.0 + m[1:2]) + m[0:1]).astype(BF16)

    def proj(lo, hi):
        return jnp.dot(h, w_ref[:, lo:hi], preferred_element_type=F32)

    qa_ref[...] = proj(C_QA, C_KA)
    ka_ref[...] = proj(C_KA, C_VA)
    va_ref[...] = proj(C_VA, C_QKVB)
    qkvb_ref[...] = proj(C_QKVB, C_GB)
    gb_ref[...] = proj(C_GB, C_HG)
    hg_ref[...] = proj(C_HG, C_BA)
    ba_ref[...] = proj(C_BA, C_END)

    @pl.when(i >= n_ctx_tiles)
    def _():
        cos, sin = cos_ref[...], sin_ref[...]
        qa_ref[...] = _rope(qa_ref[...], cos, sin)
        ka_ref[...] = _rope(ka_ref[...], cos, sin)


def _in_proj(x_parts, mod_l, w_in_r, cos_tab, sin_tab, mod_index, n_ctx_tiles, lat_tiles_per_seq):
    n = sum(p.shape[0] for p in x_parts)

    def tab_index(i):
        return (jnp.where(i < n_ctx_tiles, 0, (i - n_ctx_tiles) % lat_tiles_per_seq), 0)

    def out(width):
        return jax.ShapeDtypeStruct((n, width), F32), pl.BlockSpec((TM, width), lambda i: (i, 0))

    outs = [out(D_A), out(D_KV), out(D_KV), out(3 * D_REC), out(D_REC), out(D_HG), out(128)]
    return pl.pallas_call(
        functools.partial(_in_proj_kernel, n_ctx_tiles=n_ctx_tiles, n_x=len(x_parts)),
        out_shape=[o[0] for o in outs],
        grid=(n // TM,),
        in_specs=_token_specs(x_parts, n_ctx_tiles) + [
            pl.BlockSpec((1, 6, D_MODEL), lambda i: (mod_index(i), 0, 0)),
            pl.BlockSpec((D_MODEL, C_END), lambda i: (0, 0)),
            pl.BlockSpec((TM, 128), tab_index),
            pl.BlockSpec((TM, 128), tab_index)],
        out_specs=[o[1] for o in outs],
        compiler_params=pltpu.CompilerParams(dimension_semantics=("arbitrary",), vmem_limit_bytes=VMEM_LIMIT),
        name="in_proj",
    )(*x_parts, mod_l, w_in_r, cos_tab, sin_tab)


def _attn_heads(q_ref, k, v, bias, sink_ref, o_ref):
    k_same, k_swap = k.astype(BF16), pltpu.roll(k, HEAD_DIM, 1).astype(BF16)
    v_same, v_swap = v.astype(BF16), pltpu.roll(v, HEAD_DIM, 1).astype(BF16)
    nq = q_ref.shape[0]
    low_half = _iota((nq, 128), 1) < HEAD_DIM
    rep = A_HEADS // A_KV_HEADS
    for g in range(A_KV_HEADS):
        heads = range(g * rep, (g + 1) * rep)
        scores, exps, dens, outs = {}, {}, {}, {}
        for hq in heads:
            p, e = hq // 2, hq % 2
            q2 = q_ref[:, p * 128:(p + 1) * 128] * (HEAD_DIM ** -0.5)
            qm = jnp.where(low_half if e == 0 else jnp.logical_not(low_half), q2, 0.0)
            s = _dot(qm, k_same if e == g else k_swap, NT)
            scores[hq] = s if bias is None else s + bias
        for hq in heads:
            snk = sink_ref[hq]
            mx = jnp.maximum(jnp.max(scores[hq], -1, keepdims=True), snk)
            exps[hq] = jnp.exp(scores[hq] - mx)
            dens[hq] = jnp.sum(exps[hq], -1, keepdims=True) + jnp.exp(snk - mx)
        for hq in heads:
            outs[hq] = _dot(exps[hq], v_same if hq % 2 == g else v_swap) / dens[hq]
        for p in range(g * rep // 2, (g + 1) * rep // 2):
            o_ref[:, p * 128:(p + 1) * 128] = jnp.where(low_half, outs[2 * p], outs[2 * p + 1])


def _attn_kernel(q_ref, kx_ref, vx_ref, kp_ref, kc_ref, kn_ref, vp_ref, vc_ref, vn_ref, ck_ref, cv_ref, sink_ref,
                 o_ref, *, ctx_steps, n_blocks):
    i = pl.program_id(0)

    @pl.when(i < ctx_steps)
    def _():
        _attn_heads(q_ref, kx_ref[...], vx_ref[...], None, sink_ref, o_ref)

    @pl.when(i >= ctx_steps)
    def _():
        n = (i - ctx_steps) % n_blocks
        k = jnp.concatenate([kp_ref[...], kc_ref[...], kn_ref[...], ck_ref[0]], axis=0)
        v = jnp.concatenate([vp_ref[...], vc_ref[...], vn_ref[...], cv_ref[0]], axis=0)
        nk = k.shape[0]
        qi = _iota((ATTN_BLOCK, nk), 0)
        kj = _iota((ATTN_BLOCK, nk), 1)
        seg = kj // ATTN_BLOCK
        jj = kj % ATTN_BLOCK
        bias_prev = jnp.where(jj >= qi, 0.0, NEG_BIG) + jnp.where(n > 0, 0.0, NEG_BIG)
        bias_next = jnp.where(jj <= qi, 0.0, NEG_BIG) + jnp.where(n < n_blocks - 1, 0.0, NEG_BIG)
        bias = jnp.where(seg == 0, bias_prev, jnp.where(seg == 2, bias_next, 0.0))
        _attn_heads(q_ref, k, v, bias, sink_ref, o_ref)


def _attention(qa, ka, va, ck, cv, sink, n_ctx, ctx_len, lat_batch, lat_len):
    n = qa.shape[0]
    per_seq = ctx_len // ATTN_BLOCK
    ctx_steps = n_ctx * per_seq
    nb = lat_len // ATTN_BLOCK
    past = ck.shape[1]

    def lat(i):
        return jnp.maximum(i - ctx_steps, 0)

    def cur(i):
        return (ctx_steps + lat(i), 0)

    def prev(i):
        return (ctx_steps + (lat(i) // nb) * nb + jnp.maximum(lat(i) % nb - 1, 0), 0)

    def nxt(i):
        return (ctx_steps + (lat(i) // nb) * nb + jnp.minimum(lat(i) % nb + 1, nb - 1), 0)

    def own_seq(i):
        return (jnp.minimum(i // per_seq, n_ctx - 1), 0)

    kv = lambda f: pl.BlockSpec((ATTN_BLOCK, D_KV), f)
    cache = pl.BlockSpec((1, past, D_KV), lambda i: (lat(i) // nb, 0, 0))
    return pl.pallas_call(
        functools.partial(_attn_kernel, ctx_steps=ctx_steps, n_blocks=nb),
        out_shape=jax.ShapeDtypeStruct((n, D_A), F32),
        grid=(ctx_steps + lat_batch * nb,),
        in_specs=[pl.BlockSpec((ATTN_BLOCK, D_A), lambda i: (i, 0)),
                  pl.BlockSpec((ctx_len, D_KV), own_seq), pl.BlockSpec((ctx_len, D_KV), own_seq),
                  kv(prev), kv(cur), kv(nxt), kv(prev), kv(cur), kv(nxt), cache, cache,
                  pl.BlockSpec(memory_space=pltpu.SMEM)],
        out_specs=pl.BlockSpec((ATTN_BLOCK, D_A), lambda i: (i, 0)),
        compiler_params=pltpu.CompilerParams(dimension_semantics=("arbitrary",), vmem_limit_bytes=VMEM_LIMIT),
        name="attention",
    )(qa, ka, va, ka, ka, ka, va, va, va, ck, cv, sink)


def _tri(reverse, strict=False):
    i, j = _iota((CHUNK, CHUNK), 0), _iota((CHUNK, CHUNK), 1)
    if strict:
        return (i < j) if reverse else (i > j)
    return (i <= j) if reverse else (i >= j)


def _gdn_prep_kernel(cur_ref, prev_ref, next_ref, ba_ref, convw_ref, alog_ref, dtb_ref, qkv_ref, bl_ref,
                     *, ctx_tiles, ctx_tiles_per_seq, lat_tiles_per_seq):
    i = pl.program_id(0)
    in_ctx = i < ctx_tiles
    t_ctx = i % ctx_tiles_per_seq
    t_lat = (i - ctx_tiles) % lat_tiles_per_seq
    is_first = jnp.where(in_ctx, t_ctx == 0, t_lat == 0)
    is_last = jnp.where(in_ctx, t_ctx == ctx_tiles_per_seq - 1, t_lat == lat_tiles_per_seq - 1)
    has_prev = jnp.where(is_first, 0.0, 1.0)
    has_next = jnp.where(is_last, 0.0, 1.0)
    xh = jnp.concatenate([prev_ref[...] * has_prev, cur_ref[...], next_ref[...] * has_next], axis=0)
    cw = convw_ref[...]
    rows = cur_ref.shape[0]
    y = None
    for t in range(CONV_W):
        lo = HALO - CONV_W // 2 + t
        term = xh[lo:lo + rows] * cw[t:t + 1]
        y = term if y is None else y + term
    y = _silu(y)
    q, k, v = y[:, :D_REC], y[:, D_REC:2 * D_REC], y[:, 2 * D_REC:]
    ones = _head_ones()
    qq_hi, qq_lo = _split2(q * q)
    kk_hi, kk_lo = _split2(k * k)
    sums = jnp.dot(jnp.concatenate([qq_hi, qq_lo, kk_hi, kk_lo], axis=0), ones, preferred_element_type=F32)
    q_ss = sums[0:rows] + sums[rows:2 * rows]
    k_ss = sums[2 * rows:3 * rows] + sums[3 * rows:]
    qkv_ref[:, :D_REC] = q * lax.rsqrt(q_ss + 1e-6) * (HEAD_DIM ** -0.5)
    qkv_ref[:, D_REC:2 * D_REC] = k * lax.rsqrt(k_ss + 1e-6)
    qkv_ref[:, 2 * D_REC:] = v

    ba = ba_ref[...]
    za = ba + dtb_ref[...]
    softplus = jnp.maximum(za, 0.0) + jnp.log(1.0 + jnp.exp(-jnp.abs(za)))
    la = -jnp.exp(alog_ref[...]) * softplus
    bl_ref[...] = jnp.where(_iota(ba.shape, 1) < 2 * N_HEADS_REC, _sigmoid(ba), la)


def _gdn_prep(qkvb, ba, conv_w, alog_row, dtb_row, ctx_tiles, ctx_tiles_per_seq, lat_tiles_per_seq):
    n = qkvb.shape[0]
    per = TS // HALO
    const = lambda shape: pl.BlockSpec(shape, lambda i: (0,) * len(shape))
    return pl.pallas_call(
        functools.partial(_gdn_prep_kernel, ctx_tiles=ctx_tiles, ctx_tiles_per_seq=ctx_tiles_per_seq,
                          lat_tiles_per_seq=lat_tiles_per_seq),
        out_shape=[jax.ShapeDtypeStruct((n, 3 * D_REC), F32), jax.ShapeDtypeStruct((n, 128), F32)],
        grid=(n // TS,),
        in_specs=[pl.BlockSpec((TS, 3 * D_REC), lambda i: (i, 0)),
                  pl.BlockSpec((HALO, 3 * D_REC), lambda i: (jnp.maximum(i * per - 1, 0), 0)),
                  pl.BlockSpec((HALO, 3 * D_REC), lambda i: (jnp.minimum((i + 1) * per, n // HALO - 1), 0)),
                  pl.BlockSpec((TS, 128), lambda i: (i, 0)),
                  const((CONV_W, 3 * D_REC)), const((1, 128)), const((1, 128))],
        out_specs=[pl.BlockSpec((TS, 3 * D_REC), lambda i: (i, 0)), pl.BlockSpec((TS, 128), lambda i: (i, 0))],
        compiler_params=pltpu.CompilerParams(dimension_semantics=("arbitrary",), vmem_limit_bytes=VMEM_LIMIT),
        name="gdn_prep",
    )(qkvb, qkvb, qkvb, ba, conv_w, alog_row, dtb_row)


def _scan_schedule(n_ctx, ctx_len, lat_batch, lat_len, reverse):
    tiles_ctx, tiles_lat = ctx_len // TS, lat_len // TS
    seqs = [[s * tiles_ctx + t for t in range(tiles_ctx)] for s in range(n_ctx)]
    seqs += [[n_ctx * tiles_ctx + b * tiles_lat + t for t in range(tiles_lat)] for b in range(lat_batch)]
    cols = []
    for s, tiles in enumerate(seqs):
        order = tiles[::-1] if reverse else tiles
        for n, t in enumerate(order):
            cols.append((t, s, int(n == 0), int(n == len(order) - 1)))
    return jnp.asarray(np.array(cols, np.int32).T)


def _bd2(x):
    low = _iota(x.shape, 1) < HEAD_DIM
    zero = jnp.zeros_like(x)
    return jnp.concatenate([jnp.where(low, x, zero), jnp.where(low, zero, x)], axis=0)


def _mm3_stack(lhs_list, rhs):
    r_hi, r_lo = _split2(rhs)
    w_hi, w_lo = _bd2(r_hi), _bd2(r_lo)
    parts = [_split2(a) for a in lhs_list]
    his = [p[0] for p in parts]
    los = [p[1] for p in parts]
    n = len(lhs_list) * CHUNK
    big = jnp.dot(jnp.concatenate(his + los, axis=0), w_hi, preferred_element_type=F32)
    small = jnp.dot(his[0] if len(his) == 1 else jnp.concatenate(his, axis=0), w_lo, preferred_element_type=F32)
    out = big[:n] + big[n:] + small
    return [out[m * CHUNK:(m + 1) * CHUNK] for m in range(len(lhs_list))]


def _pick2(a, b):
    shape = (CHUNK, 128)
    return jnp.where(_iota(shape, 1) < HEAD_DIM, jnp.broadcast_to(a, shape), jnp.broadcast_to(b, shape))


def _gdn_tile_kernel(sch_ref, qkv_ref, bl_ref, s0_ref, o_ref, sfin_ref, s_scr, *, reverse, direction):
    n = pl.program_id(0)
    n_groups = N_HEADS_REC // 2

    @pl.when(sch_ref[2, n] == 1)
    def _():
        for g in range(n_groups):
            s_scr[g] = _bd2(s0_ref[0, :, g * 128:(g + 1) * 128])

    row, lane = _iota((CHUNK, 128), 0), _iota((CHUNK, 128), 1) % HEAD_DIM
    if reverse:
        incl, strict = row <= lane, row < lane
    else:
        incl, strict = row >= lane, row > lane
    eye = jnp.where(row == lane, 1.0, 0.0)
    tri = jnp.where(_tri(reverse), 1.0, 0.0).astype(BF16)
    tri_t2 = jnp.where(_iota((CHUNK, 128), 0) >= lane, 1.0, 0.0) if reverse else \
        jnp.where(_iota((CHUNK, 128), 0) <= lane, 1.0, 0.0)
    tri_t2 = tri_t2.astype(BF16)
    bd_mask = (_iota((128, 128), 0) // HEAD_DIM) == (_iota((128, 128), 1) // HEAD_DIM)
    last = 0 if reverse else CHUNK - 1
    n_chunks = TS // CHUNK
    states = [s_scr[g] for g in range(n_groups)]

    chunk_order = list(range(n_chunks - 1, -1, -1) if reverse else range(n_chunks))
    cb, ca = direction * N_HEADS_REC, 2 * N_HEADS_REC + direction * N_HEADS_REC
    inst = []
    for c in chunk_order:
        rs = slice(c * CHUNK, (c + 1) * CHUNK)
        bl = bl_ref[rs, :]
        g16, gt2 = None, None
        for part in _split3(bl):
            a = jnp.dot(tri, part, preferred_element_type=F32)
            b = lax.dot_general(part, tri_t2, TN, preferred_element_type=F32)
            g16 = a if g16 is None else g16 + a
            gt2 = b if gt2 is None else gt2 + b
        for g in range(n_groups):
            h0, h1 = 2 * g, 2 * g + 1
            ls = slice(g * 128, (g + 1) * 128)
            d = dict(c=c, g=g, rs=rs, ls=ls)
            d["q"] = qkv_ref[rs, ls]
            d["k"] = qkv_ref[rs, D_REC + g * 128:D_REC + (g + 1) * 128]
            d["v"] = qkv_ref[rs, 2 * D_REC + g * 128:2 * D_REC + (g + 1) * 128]
            d["beta"] = _pick2(bl[:, cb + h0:cb + h0 + 1], bl[:, cb + h1:cb + h1 + 1])
            d["g_i"] = _pick2(g16[:, ca + h0:ca + h0 + 1], g16[:, ca + h1:ca + h1 + 1])
            g_j = _pick2(gt2[ca + h0:ca + h0 + 1, :], gt2[ca + h1:ca + h1 + 1, :])
            d["g_tot"] = d["g_i"][last:last + 1]
            d["decay"] = jnp.where(incl, jnp.exp(jnp.where(incl, d["g_i"] - g_j, 0.0)), 0.0)
            inst.append(d)

    for d in inst:
        gram = _dot(jnp.concatenate([d["k"], d["q"]], axis=0), _bd2(d["k"].astype(BF16)), NT)
        d["low"] = jnp.where(strict, d["beta"] * gram[:CHUNK] * d["decay"], 0.0)
        d["attn"] = gram[CHUNK:] * d["decay"]

    for d in inst:
        d["t"] = eye - d["low"]
        (d["p"],) = _mm3_stack([d["low"]], d["low"])
    for _ in range(4):
        for d in inst:
            p_next, tp = _mm3_stack([d["p"], d["t"]], d["p"])
            d["t"] = d["t"] + tp
            d["p"] = p_next
    for d in inst:
        t_mat = d["t"] + _mm3_stack([d["t"]], d["p"])[0]
        eg = jnp.exp(d["g_i"])
        vb = d["v"] * d["beta"]
        kbg = d["k"] * d["beta"] * eg
        d["u"] = _dot(t_mat, _bd2(vb.astype(BF16)))
        d["w_qg"] = jnp.concatenate([_dot(t_mat, _bd2(kbg.astype(BF16))), d["q"] * eg], axis=0).astype(BF16)
        d["k_dec"] = (d["k"] * jnp.exp(d["g_tot"] - d["g_i"])).astype(BF16)
        d["e_tot"] = jnp.exp(d["g_tot"])

    for d in inst:
        s = states[d["g"]]
        ws_qs = _dot(d["w_qg"], s)
        v_new = d["u"] - ws_qs[:CHUNK]
        o_ref[d["rs"], d["ls"]] = ws_qs[CHUNK:] + _dot(d["attn"], _bd2(v_new.astype(BF16)))
        states[d["g"]] = s * d["e_tot"] + jnp.where(bd_mask, _dot(d["k_dec"], v_new, TN), 0.0)

    for g in range(n_groups):
        s_scr[g] = states[g]

    @pl.when(sch_ref[3, n] == 1)
    def _():
        for g in range(n_groups):
            sfin_ref[0, :, g * 128:(g + 1) * 128] = states[g][:HEAD_DIM] + states[g][HEAD_DIM:]


def _gdn_pipe_kernel(sch_ref, qkv_ref, bl_ref, s0_ref, o_ref, sfin_ref, s_scr, *sets, reverse, direction, n_steps):
    n = pl.program_id(0)
    n_groups = N_HEADS_REC // 2
    n_chunks = TS // CHUNK
    n_inst = n_chunks * n_groups
    prev = jnp.maximum(n - 1, 0)
    chain_live = n >= 1
    half = len(sets) // 2

    @pl.when(n == 0)
    def _():
        for ref in sets:
            ref[...] = jnp.zeros(ref.shape, ref.dtype)
        s_scr[...] = jnp.zeros(s_scr.shape, F32)

    @pl.when(jnp.logical_and(chain_live, sch_ref[2, prev] == 1))
    def _():
        for g in range(n_groups):
            s_scr[g] = _bd2(s0_ref[0, :, g * 128:(g + 1) * 128])

    row, lane = _iota((CHUNK, 128), 0), _iota((CHUNK, 128), 1) % HEAD_DIM
    if reverse:
        incl, strict = row <= lane, row < lane
    else:
        incl, strict = row >= lane, row > lane
    eye = jnp.where(row == lane, 1.0, 0.0)
    tri = jnp.where(_tri(reverse), 1.0, 0.0).astype(BF16)
    tri_t2 = jnp.where((row >= lane) if reverse else (row <= lane), 1.0, 0.0).astype(BF16)
    bd_mask = (_iota((128, 128), 0) // HEAD_DIM) == (_iota((128, 128), 1) // HEAD_DIM)
    last = 0 if reverse else CHUNK - 1
    chunk_order = list(range(n_chunks - 1, -1, -1) if reverse else range(n_chunks))
    cb, ca = direction * N_HEADS_REC, 2 * N_HEADS_REC + direction * N_HEADS_REC

    def step(wr, rd):
        u_w, wq_w, at_w, kd_w, et_w = wr
        u_r, wq_r, at_r, kd_r, et_r = rd
        states = [s_scr[g] for g in range(n_groups)]
        inst = []

        def chain(m):
            c, g = chunk_order[m // n_groups], m % n_groups
            rs, ls = slice(c * CHUNK, (c + 1) * CHUNK), slice(g * 128, (g + 1) * 128)
            s = states[g]
            ws_qs = _dot(wq_r[m], s)
            v_new = u_r[m] - ws_qs[:CHUNK]
            o_ref[rs, ls] = ws_qs[CHUNK:] + _dot(at_r[m], _bd2(v_new.astype(BF16)))
            states[g] = s * et_r[m][0:1] + jnp.where(bd_mask, _dot(kd_r[m], v_new, TN), 0.0)

        for c in chunk_order:
            rs = slice(c * CHUNK, (c + 1) * CHUNK)
            bl = bl_ref[rs, :]
            g16, gt2 = None, None
            for part in _split3(bl):
                a = jnp.dot(tri, part, preferred_element_type=F32)
                b = lax.dot_general(part, tri_t2, TN, preferred_element_type=F32)
                g16 = a if g16 is None else g16 + a
                gt2 = b if gt2 is None else gt2 + b
            for g in range(n_groups):
                h0, h1 = 2 * g, 2 * g + 1
                ls = slice(g * 128, (g + 1) * 128)
                d = dict(q=qkv_ref[rs, ls], k=qkv_ref[rs, D_REC + g * 128:D_REC + (g + 1) * 128],
                         v=qkv_ref[rs, 2 * D_REC + g * 128:2 * D_REC + (g + 1) * 128])
                d["beta"] = _pick2(bl[:, cb + h0:cb + h0 + 1], bl[:, cb + h1:cb + h1 + 1])
                d["g_i"] = _pick2(g16[:, ca + h0:ca + h0 + 1], g16[:, ca + h1:ca + h1 + 1])
                g_j = _pick2(gt2[ca + h0:ca + h0 + 1, :], gt2[ca + h1:ca + h1 + 1, :])
                d["g_tot"] = d["g_i"][last:last + 1]
                d["decay"] = jnp.where(incl, jnp.exp(jnp.where(incl, d["g_i"] - g_j, 0.0)), 0.0)
                inst.append(d)
        chain(0)
        chain(1)

        for d in inst:
            gram = _dot(jnp.concatenate([d["k"], d["q"]], axis=0), _bd2(d["k"].astype(BF16)), NT)
            d["low"] = jnp.where(strict, d["beta"] * gram[:CHUNK] * d["decay"], 0.0)
            d["attn"] = gram[CHUNK:] * d["decay"]

        for d in inst:
            d["t"] = eye - d["low"]
            (d["p"],) = _mm3_stack([d["low"]], d["low"])
        chain(2)
        chain(3)
        for it in range(4):
            for d in inst:
                p_next, tp = _mm3_stack([d["p"], d["t"]], d["p"])
                d["t"] = d["t"] + tp
                d["p"] = p_next
            if it < (n_inst - 4) // 2:
                chain(4 + 2 * it)
                chain(5 + 2 * it)
        for m, d in enumerate(inst):
            t_mat = d["t"] + _mm3_stack([d["t"]], d["p"])[0]
            eg = jnp.exp(d["g_i"])
            vb = d["v"] * d["beta"]
            kbg = d["k"] * d["beta"] * eg
            u_w[m] = _dot(t_mat, _bd2(vb.astype(BF16)))
            wq_w[m] = jnp.concatenate([_dot(t_mat, _bd2(kbg.astype(BF16))), d["q"] * eg], axis=0).astype(BF16)
            at_w[m] = d["attn"]
            kd_w[m] = (d["k"] * jnp.exp(d["g_tot"] - d["g_i"])).astype(BF16)
            et_w[m] = jnp.broadcast_to(jnp.exp(d["g_tot"]), (8, 128))

        for g in range(n_groups):
            s_scr[g] = states[g]

    @pl.when(n % 2 == 0)
    def _():
        step(sets[:half], sets[half:])

    @pl.when(n % 2 == 1)
    def _():
        step(sets[half:], sets[:half])

    @pl.when(jnp.logical_and(chain_live, sch_ref[3, prev] == 1))
    def _():
        for g in range(n_groups):
            sfin_ref[0, :, g * 128:(g + 1) * 128] = s_scr[g, :HEAD_DIM, :] + s_scr[g, HEAD_DIM:, :]


def _mm3_heads(lhs_list, rhs):
    r_hi, r_lo = _split2(rhs)
    w_hi, w_lo = _block_diag(r_hi), _block_diag(r_lo)
    parts = [_split2(a) for a in lhs_list]
    his = [p[0] for p in parts]
    los = [p[1] for p in parts]
    n = len(lhs_list) * CHUNK
    big = jnp.dot(jnp.concatenate(his + los, axis=0), w_hi, preferred_element_type=F32)
    small = jnp.dot(his[0] if len(his) == 1 else jnp.concatenate(his, axis=0), w_lo, preferred_element_type=F32)
    out = big[:n] + big[n:] + small
    return [out[m * CHUNK:(m + 1) * CHUNK] for m in range(len(lhs_list))]


def _pick_heads(vals):
    shape = (CHUNK, D_REC)
    head = _iota(shape, 1) // HEAD_DIM
    out = jnp.broadcast_to(vals[-1], shape)
    for h in range(N_HEADS_REC - 2, -1, -1):
        out = jnp.where(head == h, jnp.broadcast_to(vals[h], shape), out)
    return out


def _gdn_scan_kernel(sch_ref, qkv_ref, bl_ref, s0_ref, o_ref, sfin_ref, s_scr, *, reverse, direction):
    n = pl.program_id(0)

    @pl.when(sch_ref[2, n] == 1)
    def _():
        s_scr[...] = _block_diag(s0_ref[0])

    row, lane = _iota((CHUNK, D_REC), 0), _iota((CHUNK, D_REC), 1) % HEAD_DIM
    if reverse:
        incl, strict = row <= lane, row < lane
    else:
        incl, strict = row >= lane, row > lane
    eye = jnp.where(row == lane, 1.0, 0.0)
    tri = jnp.where(_tri(reverse), 1.0, 0.0).astype(BF16)
    tri_t = jnp.where((row >= lane) if reverse else (row <= lane), 1.0, 0.0).astype(BF16)
    bd_mask = _block_diag_mask()
    last = 0 if reverse else CHUNK - 1
    n_chunks = TS // CHUNK
    cb, ca = direction * N_HEADS_REC, 2 * N_HEADS_REC + direction * N_HEADS_REC

    inst = []
    for c in (range(n_chunks - 1, -1, -1) if reverse else range(n_chunks)):
        rs = slice(c * CHUNK, (c + 1) * CHUNK)
        bl = bl_ref[rs, :]
        g16, gt = None, None
        for part in _split3(bl):
            a = jnp.dot(tri, part, preferred_element_type=F32)
            b = lax.dot_general(part, tri_t, TN, preferred_element_type=F32)
            g16 = a if g16 is None else g16 + a
            gt = b if gt is None else gt + b
        d = dict(rs=rs, q=qkv_ref[rs, :D_REC], k=qkv_ref[rs, D_REC:2 * D_REC], v=qkv_ref[rs, 2 * D_REC:])
        d["beta"] = _pick_heads([bl[:, cb + h:cb + h + 1] for h in range(N_HEADS_REC)])
        d["g_i"] = _pick_heads([g16[:, ca + h:ca + h + 1] for h in range(N_HEADS_REC)])
        g_j = _pick_heads([gt[ca + h:ca + h + 1, :] for h in range(N_HEADS_REC)])
        d["g_tot"] = d["g_i"][last:last + 1]
        d["decay"] = jnp.where(incl, jnp.exp(jnp.where(incl, d["g_i"] - g_j, 0.0)), 0.0)
        inst.append(d)

    for d in inst:
        gram = _dot(jnp.concatenate([d["k"], d["q"]], axis=0), _block_diag(d["k"].astype(BF16)), NT)
        d["low"] = jnp.where(strict, d["beta"] * gram[:CHUNK] * d["decay"], 0.0)
        d["attn"] = gram[CHUNK:] * d["decay"]

    for d in inst:
        d["t"] = eye - d["low"]
        (d["p"],) = _mm3_heads([d["low"]], d["low"])
    for _ in range(4):
        for d in inst:
            p_next, tp = _mm3_heads([d["p"], d["t"]], d["p"])
            d["t"] = d["t"] + tp
            d["p"] = p_next
    for d in inst:
        t_mat = d["t"] + _mm3_heads([d["t"]], d["p"])[0]
        eg = jnp.exp(d["g_i"])
        vb = d["v"] * d["beta"]
        kbg = d["k"] * d["beta"] * eg
        d["u"] = _dot(t_mat, _block_diag(vb.astype(BF16)))
        d["w_qg"] = jnp.concatenate([_dot(t_mat, _block_diag(kbg.astype(BF16))), d["q"] * eg], axis=0).astype(BF16)
        d["k_dec"] = (d["k"] * jnp.exp(d["g_tot"] - d["g_i"])).astype(BF16)
        d["e_tot"] = jnp.exp(d["g_tot"])

    s = s_scr[...]
    for d in inst:
        ws_qs = _dot(d["w_qg"], s)
        v_new = d["u"] - ws_qs[:CHUNK]
        o_ref[d["rs"], :] = ws_qs[CHUNK:] + _dot(d["attn"], _block_diag(v_new.astype(BF16)))
        s = s * d["e_tot"] + jnp.where(bd_mask, _dot(d["k_dec"], v_new, TN), 0.0)
    s_scr[...] = s

    @pl.when(sch_ref[3, n] == 1)
    def _():
        sfin_ref[0] = _collapse_block_diag(s)


def _gdn_pipe4_kernel(sch_ref, qkv_ref, bl_ref, s0_ref, o_ref, sfin_ref, s_scr, *sets, reverse, direction):
    n = pl.program_id(0)
    n_chunks = TS // CHUNK
    prev = jnp.maximum(n - 1, 0)
    chain_live = n >= 1
    half = len(sets) // 2

    @pl.when(n == 0)
    def _():
        for ref in sets:
            ref[...] = jnp.zeros(ref.shape, ref.dtype)
        s_scr[...] = jnp.zeros(s_scr.shape, F32)

    @pl.when(jnp.logical_and(chain_live, sch_ref[2, prev] == 1))
    def _():
        s_scr[...] = _block_diag(s0_ref[0])

    row, lane = _iota((CHUNK, D_REC), 0), _iota((CHUNK, D_REC), 1) % HEAD_DIM
    if reverse:
        incl, strict = row <= lane, row < lane
    else:
        incl, strict = row >= lane, row > lane
    eye = jnp.where(row == lane, 1.0, 0.0)
    tri = jnp.where(_tri(reverse), 1.0, 0.0).astype(BF16)
    t_before_j = (row >= lane) if reverse else (row <= lane)
    ones8 = jnp.ones((8, CHUNK), BF16)
    bd_mask = _block_diag_mask()
    last = 0 if reverse else CHUNK - 1
    chunk_order = list(range(n_chunks - 1, -1, -1) if reverse else range(n_chunks))
    cb, ca = direction * N_HEADS_REC, 2 * N_HEADS_REC + direction * N_HEADS_REC

    def step(wr, rd):
        u_w, wq_w, at_w, kd_w, et_w = wr
        u_r, wq_r, at_r, kd_r, et_r = rd
        state = [s_scr[...]]
        inst = []

        def chain(m):
            c = chunk_order[m]
            rs = slice(c * CHUNK, (c + 1) * CHUNK)
            s = state[0]
            ws_qs = _dot(wq_r[m], s)
            v_new = u_r[m] - ws_qs[:CHUNK]
            o_ref[rs, :] = ws_qs[CHUNK:] + _dot(at_r[m], _block_diag(v_new.astype(BF16)))
            state[0] = s * et_r[m][0:1] + jnp.where(bd_mask, _dot(kd_r[m], v_new, TN), 0.0)

        for c in chunk_order:
            rs = slice(c * CHUNK, (c + 1) * CHUNK)
            bl = bl_ref[rs, :]
            d = dict(q=qkv_ref[rs, :D_REC], k=qkv_ref[rs, D_REC:2 * D_REC], v=qkv_ref[rs, 2 * D_REC:])
            d["beta"] = _pick_heads([bl[:, cb + h:cb + h + 1] for h in range(N_HEADS_REC)])
            la = _pick_heads([bl[:, ca + h:ca + h + 1] for h in range(N_HEADS_REC)])
            g_i, g_j = None, None
            for p_i, p_j in zip(_split3(la), _split3(jnp.where(t_before_j, la, 0.0))):
                a = jnp.dot(tri, p_i, preferred_element_type=F32)
                b = jnp.dot(ones8, p_j, preferred_element_type=F32)
                g_i = a if g_i is None else g_i + a
                g_j = b if g_j is None else g_j + b
            d["g_i"] = g_i
            d["g_tot"] = g_i[last:last + 1]
            d["decay"] = jnp.where(incl, jnp.exp(jnp.where(incl, g_i - g_j[0:1], 0.0)), 0.0)
            inst.append(d)
        chain(0)

        for d in inst:
            gram = _dot(jnp.concatenate([d["k"], d["q"]], axis=0), _block_diag(d["k"].astype(BF16)), NT)
            d["low"] = jnp.where(strict, d["beta"] * gram[:CHUNK] * d["decay"], 0.0)
            d["attn"] = gram[CHUNK:] * d["decay"]

        for d in inst:
            d["t"] = eye - d["low"]
            (d["p"],) = _mm3_heads([d["low"]], d["low"])
        for it in range(4):
            for d in inst:
                p_next, tp = _mm3_heads([d["p"], d["t"]], d["p"])
                d["t"] = d["t"] + tp
                d["p"] = p_next
            if 1 + it < n_chunks:
                chain(1 + it)
        for m, d in enumerate(inst):
            t_mat = d["t"] + _mm3_heads([d["t"]], d["p"])[0]
            eg = jnp.exp(d["g_i"])
            vb = d["v"] * d["beta"]
            kbg = d["k"] * d["beta"] * eg
            u_w[m] = _dot(t_mat, _block_diag(vb.astype(BF16)))
            wq_w[m] = jnp.concatenate([_dot(t_mat, _block_diag(kbg.astype(BF16))), d["q"] * eg], axis=0).astype(BF16)
            at_w[m] = d["attn"]
            kd_w[m] = (d["k"] * jnp.exp(d["g_tot"] - d["g_i"])).astype(BF16)
            et_w[m] = jnp.broadcast_to(jnp.exp(d["g_tot"]), (8, D_REC))

        s_scr[...] = state[0]

    @pl.when(n % 2 == 0)
    def _():
        step(sets[:half], sets[half:])

    @pl.when(n % 2 == 1)
    def _():
        step(sets[half:], sets[:half])

    @pl.when(jnp.logical_and(chain_live, sch_ref[3, prev] == 1))
    def _():
        sfin_ref[0] = _collapse_block_diag(s_scr[...])


def _gdn_bidir_kernel(sch_ref, qkvf_ref, blf_ref, qkvb_ref, blb_ref, s0f_ref, s0b_ref,
                      of_ref, sff_ref, ob_ref, sfb_ref, s_scr, *sets):
    n = pl.program_id(0)
    n_chunks = TS // CHUNK
    prev = jnp.maximum(n - 1, 0)
    chain_live = n >= 1
    half = len(sets) // 2
    lanes = ((False, qkvf_ref, blf_ref, s0f_ref, of_ref, sff_ref), (True, qkvb_ref, blb_ref, s0b_ref, ob_ref, sfb_ref))

    @pl.when(n == 0)
    def _():
        for ref in sets:
            ref[...] = jnp.zeros(ref.shape, ref.dtype)
        s_scr[...] = jnp.zeros(s_scr.shape, F32)

    for ln, lane_refs in enumerate(lanes):
        @pl.when(jnp.logical_and(chain_live, sch_ref[4 * ln + 2, prev] == 1))
        def _(ln=ln, s0_ref=lane_refs[3]):
            s_scr[ln] = _block_diag(s0_ref[0])

    row, lane = _iota((CHUNK, D_REC), 0), _iota((CHUNK, D_REC), 1) % HEAD_DIM
    eye = jnp.where(row == lane, 1.0, 0.0)
    ones8 = jnp.ones((8, CHUNK), BF16)
    bd_mask = _block_diag_mask()

    def step(wr, rd):
        u_w, wq_w, at_w, kd_w, et_w = wr
        u_r, wq_r, at_r, kd_r, et_r = rd
        state = [s_scr[0], s_scr[1]]
        inst = []

        def chain(m):
            for ln, (reverse, _, _, _, o_ref, _) in enumerate(lanes):
                c = (n_chunks - 1 - m) if reverse else m
                rs = slice(c * CHUNK, (c + 1) * CHUNK)
                k = ln * n_chunks + m
                s = state[ln]
                ws_qs = _dot(wq_r[k], s)
                v_new = u_r[k] - ws_qs[:CHUNK]
                o_ref[rs, :] = ws_qs[CHUNK:] + _dot(at_r[k], _block_diag(v_new.astype(BF16)))
                state[ln] = s * et_r[k][0:1] + jnp.where(bd_mask, _dot(kd_r[k], v_new, TN), 0.0)

        for ln, (reverse, qkv_ref, bl_ref, _, _, _) in enumerate(lanes):
            incl = (row <= lane) if reverse else (row >= lane)
            tri = jnp.where(_tri(reverse), 1.0, 0.0).astype(BF16)
            t_before_j = (row >= lane) if reverse else (row <= lane)
            last = 0 if reverse else CHUNK - 1
            cb, ca = ln * N_HEADS_REC, 2 * N_HEADS_REC + ln * N_HEADS_REC
            for m in range(n_chunks):
                c = (n_chunks - 1 - m) if reverse else m
                rs = slice(c * CHUNK, (c + 1) * CHUNK)
                bl = bl_ref[rs, :]
                d = dict(q=qkv_ref[rs, :D_REC], k=qkv_ref[rs, D_REC:2 * D_REC], v=qkv_ref[rs, 2 * D_REC:],
                         strict=(row < lane) if reverse else (row > lane), slot=ln * n_chunks + m)
                d["beta"] = _pick_heads([bl[:, cb + h:cb + h + 1] for h in range(N_HEADS_REC)])
                la = _pick_heads([bl[:, ca + h:ca + h + 1] for h in range(N_HEADS_REC)])
                g_i, g_j = None, None
                for p_i, p_j in zip(_split3(la), _split3(jnp.where(t_before_j, la, 0.0))):
                    a = jnp.dot(tri, p_i, preferred_element_type=F32)
                    b = jnp.dot(ones8, p_j, preferred_element_type=F32)
                    g_i = a if g_i is None else g_i + a
                    g_j = b if g_j is None else g_j + b
                d["g_i"] = g_i
                d["g_tot"] = g_i[last:last + 1]
                d["decay"] = jnp.where(incl, jnp.exp(jnp.where(incl, g_i - g_j[0:1], 0.0)), 0.0)
                inst.append(d)
        chain(0)

        for d in inst:
            gram = _dot(jnp.concatenate([d["k"], d["q"]], axis=0), _block_diag(d["k"].astype(BF16)), NT)
            d["low"] = jnp.where(d["strict"], d["beta"] * gram[:CHUNK] * d["decay"], 0.0)
            d["attn"] = gram[CHUNK:] * d["decay"]

        for d in inst:
            d["t"] = eye - d["low"]
            (d["p"],) = _mm3_heads([d["low"]], d["low"])
        for it in range(4):
            for d in inst:
                p_next, tp = _mm3_heads([d["p"], d["t"]], d["p"])
                d["t"] = d["t"] + tp
                d["p"] = p_next
            if 1 + it < n_chunks:
                chain(1 + it)
        for d in inst:
            k = d["slot"]
            t_mat = d["t"] + _mm3_heads([d["t"]], d["p"])[0]
            eg = jnp.exp(d["g_i"])
            vb = d["v"] * d["beta"]
            kbg = d["k"] * d["beta"] * eg
            u_w[k] = _dot(t_mat, _block_diag(vb.astype(BF16)))
            wq_w[k] = jnp.concatenate([_dot(t_mat, _block_diag(kbg.astype(BF16))), d["q"] * eg], axis=0).astype(BF16)
            at_w[k] = d["attn"]
            kd_w[k] = (d["k"] * jnp.exp(d["g_tot"] - d["g_i"])).astype(BF16)
            et_w[k] = jnp.broadcast_to(jnp.exp(d["g_tot"]), (8, D_REC))

        s_scr[0] = state[0]
        s_scr[1] = state[1]

    @pl.when(n % 2 == 0)
    def _():
        step(sets[:half], sets[half:])

    @pl.when(n % 2 == 1)
    def _():
        step(sets[half:], sets[:half])

    for ln, lane_refs in enumerate(lanes):
        @pl.when(jnp.logical_and(chain_live, sch_ref[4 * ln + 3, prev] == 1))
        def _(ln=ln, sfin_ref=lane_refs[5]):
            sfin_ref[0] = _collapse_block_diag(s_scr[ln])


def _gdn_scan_bidir(qkv, bl, s0_f, s0_b, sched):
    n = qkv.shape[0]
    n_seq = s0_f.shape[0]
    n_steps = sched.shape[1]
    n_inst = 2 * (TS // CHUNK)
    ahead = lambda i: jnp.minimum(i, n_steps - 1)
    behind = lambda i: jnp.maximum(i - 1, 0)
    scratch_set = [pltpu.VMEM((n_inst, CHUNK, D_REC), F32), pltpu.VMEM((n_inst, 2 * CHUNK, D_REC), BF16),
                   pltpu.VMEM((n_inst, CHUNK, D_REC), F32), pltpu.VMEM((n_inst, CHUNK, D_REC), BF16),
                   pltpu.VMEM((n_inst, 8, D_REC), F32)]

    def tile_in(width, ln):
        return pl.BlockSpec((TS, width), lambda i, sch: (sch[4 * ln, ahead(i)], 0))

    def tile_out(ln):
        return pl.BlockSpec((TS, D_REC), lambda i, sch: (sch[4 * ln, behind(i)], 0))

    def state_spec(ln):
        return pl.BlockSpec((1, HEAD_DIM, D_REC), lambda i, sch: (sch[4 * ln + 1, behind(i)], 0, 0))

    grid_spec = pltpu.PrefetchScalarGridSpec(
        num_scalar_prefetch=1,
        grid=(n_steps + 1,),
        in_specs=[tile_in(3 * D_REC, 0), tile_in(128, 0), tile_in(3 * D_REC, 1), tile_in(128, 1),
                  state_spec(0), state_spec(1)],
        out_specs=[tile_out(0), state_spec(0), tile_out(1), state_spec(1)],
        scratch_shapes=[pltpu.VMEM((2, D_REC, D_REC), F32)] + scratch_set + scratch_set,
    )
    o_sds = jax.ShapeDtypeStruct((n, D_REC), F32)
    s_sds = jax.ShapeDtypeStruct((n_seq, HEAD_DIM, D_REC), F32)
    return pl.pallas_call(
        _gdn_bidir_kernel,
        out_shape=[o_sds, s_sds, o_sds, s_sds],
        grid_spec=grid_spec,
        compiler_params=pltpu.CompilerParams(dimension_semantics=("arbitrary",), vmem_limit_bytes=VMEM_LIMIT),
        name="gdn_scan",
    )(sched, qkv, bl, qkv, bl, s0_f, s0_b)


def _gdn_scan(qkv, bl, s0, sched, reverse, direction):
    n = qkv.shape[0]
    n_seq = s0.shape[0]
    n_steps = sched.shape[1]
    n_inst = TS // CHUNK
    ahead = lambda i: jnp.minimum(i, n_steps - 1)
    behind = lambda i: jnp.maximum(i - 1, 0)
    scratch_set = [pltpu.VMEM((n_inst, CHUNK, D_REC), F32), pltpu.VMEM((n_inst, 2 * CHUNK, D_REC), BF16),
                   pltpu.VMEM((n_inst, CHUNK, D_REC), F32), pltpu.VMEM((n_inst, CHUNK, D_REC), BF16),
                   pltpu.VMEM((n_inst, 8, D_REC), F32)]
    grid_spec = pltpu.PrefetchScalarGridSpec(
        num_scalar_prefetch=1,
        grid=(n_steps + 1,),
        in_specs=[pl.BlockSpec((TS, 3 * D_REC), lambda i, sch: (sch[0, ahead(i)], 0)),
                  pl.BlockSpec((TS, 128), lambda i, sch: (sch[0, ahead(i)], 0)),
                  pl.BlockSpec((1, HEAD_DIM, D_REC), lambda i, sch: (sch[1, behind(i)], 0, 0))],
        out_specs=[pl.BlockSpec((TS, D_REC), lambda i, sch: (sch[0, behind(i)], 0)),
                   pl.BlockSpec((1, HEAD_DIM, D_REC), lambda i, sch: (sch[1, behind(i)], 0, 0))],
        scratch_shapes=[pltpu.VMEM((D_REC, D_REC), F32)] + scratch_set + scratch_set,
    )
    return pl.pallas_call(
        functools.partial(_gdn_pipe4_kernel, reverse=reverse, direction=direction),
        out_shape=[jax.ShapeDtypeStruct((n, D_REC), F32), jax.ShapeDtypeStruct((n_seq, HEAD_DIM, D_REC), F32)],
        grid_spec=grid_spec,
        compiler_params=pltpu.CompilerParams(dimension_semantics=("arbitrary",), vmem_limit_bytes=VMEM_LIMIT),
        name="gdn_scan_bwd" if reverse else "gdn_scan_fwd",
    )(sched, qkv, bl, s0)


def _pair_dots(lhs, w, dims):
    outs = []
    for p in range(N_HEADS_REC // 2):
        sl = slice(p * 128, (p + 1) * 128)
        outs.append(lax.dot_general(lhs[:, sl], _bd2(w[:, sl]), dims, preferred_element_type=F32))
    return jnp.concatenate(outs, axis=1)


def _hgrn_tile_kernel(sch_ref, f_ref, i_ref, q_ref, lb_ref, sel_ref, s0_ref, o_ref, sfin_ref, s_scr, *, reverse):
    n = pl.program_id(0)

    n_pairs = N_HEADS_REC // 2

    @pl.when(sch_ref[2, n] == 1)
    def _():
        for p in range(n_pairs):
            s_scr[p] = _bd2(s0_ref[0, :, p * 128:(p + 1) * 128])

    lb = lb_ref[...]
    tri = jnp.where(_tri(reverse), 1.0, 0.0).astype(BF16)
    row = _iota((CHUNK, D_REC), 0)
    col_tok = _iota((CHUNK, D_REC), 1) % HEAD_DIM
    ones = _head_ones()
    last = 0 if reverse else CHUNK - 1
    n_chunks = TS // CHUNK
    chunk_order = list(range(n_chunks - 1, -1, -1) if reverse else range(n_chunks))

    inst = []
    for c in chunk_order:
        rs = slice(c * CHUNK, (c + 1) * CHUNK)
        f = lb + (1.0 - lb) * _sigmoid(f_ref[rs, :])
        d = dict(rs=rs, kk=1.0 - f, q=q_ref[rs, :], v=i_ref[rs, :])
        d["b"] = _dot_sel(tri, jnp.log(jnp.maximum(f, TINY)))
        inst.append(d)

    sel = sel_ref[...]
    for d in inst:
        parts = _split3(d["b"])
        d["b_mid"] = sum(jnp.dot(sel, part, preferred_element_type=F32) for part in parts)

    for lvl in range(N_LEVELS):
        s_half = (CHUNK // 2) >> lvl
        in_block = row % (2 * s_half)
        later = (in_block < s_half) if reverse else (in_block >= s_half)
        same_block = (row // (2 * s_half)) == (col_tok // (2 * s_half))
        for d in inst:
            e = jnp.exp(-jnp.abs(d["b"] - d["b_mid"][lvl * CHUNK:(lvl + 1) * CHUNK]))
            q_hat = jnp.where(later, d["q"] * e, 0.0).astype(BF16)
            k_hat = jnp.where(later, 0.0, d["kk"] * e).astype(BF16)
            res = _pair_dots(q_hat, k_hat, NT)
            term = jnp.where(same_block, res, 0.0)
            d["a"] = term if lvl == 0 else d["a"] + term

    ones2 = ones[:128, :128]
    for d in inst:
        qk = (d["q"] * d["kk"]).astype(BF16)
        sums = jnp.concatenate([jnp.dot(qk[:, p * 128:(p + 1) * 128], ones2, preferred_element_type=F32)
                                for p in range(n_pairs)], axis=1)
        d["a"] = d["a"] + jnp.where(col_tok == row, sums, 0.0)

    bd2_mask = (_iota((128, 128), 0) // HEAD_DIM) == (_iota((128, 128), 1) // HEAD_DIM)
    for d in inst:
        b = d["b"]
        b_last = b[last:last + 1]
        d["av"] = _pair_dots(d["a"].astype(BF16), d["v"].astype(BF16), NN)
        v16, kd16 = d["v"].astype(BF16), (d["kk"] * jnp.exp(b_last - b)).astype(BF16)
        d["kv"] = [jnp.where(bd2_mask, lax.dot_general(v16[:, p * 128:(p + 1) * 128], kd16[:, p * 128:(p + 1) * 128],
                                                       TN, preferred_element_type=F32), 0.0) for p in range(n_pairs)]
        d["qb"] = (d["q"] * jnp.exp(b)).astype(BF16)
        d["e_last"] = jnp.exp(b_last)

    states = [s_scr[p] for p in range(n_pairs)]
    for d in inst:
        qs = [lax.dot_general(d["qb"][:, p * 128:(p + 1) * 128], states[p].astype(BF16), NT,
                              preferred_element_type=F32) for p in range(n_pairs)]
        o_ref[d["rs"], :] = d["av"] + jnp.concatenate(qs, axis=1)
        for p in range(n_pairs):
            states[p] = states[p] * d["e_last"][:, p * 128:(p + 1) * 128] + d["kv"][p]
    for p in range(n_pairs):
        s_scr[p] = states[p]

    @pl.when(sch_ref[3, n] == 1)
    def _():
        for p in range(n_pairs):
            sfin_ref[0, :, p * 128:(p + 1) * 128] = states[p][:HEAD_DIM] + states[p][HEAD_DIM:]


def _hgrn_scan_kernel(sch_ref, f_ref, i_ref, q_ref, lb_ref, sel_ref, s0_ref, o_ref, sfin_ref, s_scr, *, reverse):
    n = pl.program_id(0)

    @pl.when(sch_ref[2, n] == 1)
    def _():
        s_scr[...] = _block_diag(s0_ref[0])

    lb = lb_ref[...]
    sel = sel_ref[...]
    row = _iota((CHUNK, D_REC), 0)
    col_tok = _iota((CHUNK, D_REC), 1) % HEAD_DIM
    ones = _head_ones()
    bd_mask = _block_diag_mask()
    last = 0 if reverse else CHUNK - 1
    n_chunks = TS // CHUNK
    chunk_order = list(range(n_chunks - 1, -1, -1) if reverse else range(n_chunks))

    inst = []
    for c in chunk_order:
        rs = slice(c * CHUNK, (c + 1) * CHUNK)
        f = lb + (1.0 - lb) * _sigmoid(f_ref[rs, :])
        d = dict(rs=rs, kk=1.0 - f, q=q_ref[rs, :], v=i_ref[rs, :])
        cums = _dot_sel(sel, jnp.log(jnp.maximum(f, TINY)))
        d["b"] = cums[:CHUNK]
        d["b_mid_small"] = [cums[CHUNK:2 * CHUNK], cums[2 * CHUNK:]]
        inst.append(d)

    for lvl in range(N_LEVELS):
        s_half = (CHUNK // 2) >> lvl
        in_block = row % (2 * s_half)
        later = (in_block < s_half) if reverse else (in_block >= s_half)
        same_block = (row // (2 * s_half)) == (col_tok // (2 * s_half))
        for d in inst:
            b = d["b"]
            if s_half >= 4:
                pieces = []
                for blk in range(CHUNK // (2 * s_half)):
                    r = blk * 2 * s_half + (s_half if reverse else s_half - 1)
                    pieces.append(jnp.broadcast_to(b[r:r + 1], (2 * s_half, D_REC)))
                b_mid = pieces[0] if len(pieces) == 1 else jnp.concatenate(pieces, axis=0)
            else:
                b_mid = d["b_mid_small"][lvl - (N_LEVELS - 2)]
            e = jnp.exp(-jnp.abs(b - b_mid))
            q_hat = jnp.where(later, d["q"] * e, 0.0)
            k_hat = jnp.where(later, 0.0, d["kk"] * e)
            res = _dot(q_hat, _block_diag(k_hat.astype(BF16)), NT)
            term = jnp.where(same_block, res, 0.0)
            d["a"] = term if lvl == 0 else d["a"] + term

    for d in inst:
        sums = _dot(d["q"] * d["kk"], ones)
        d["a"] = d["a"] + jnp.where(col_tok == row, sums, 0.0)

    for d in inst:
        b = d["b"]
        b_last = b[last:last + 1]
        d["av"] = _dot(d["a"], _block_diag(d["v"].astype(BF16)))
        d["kv"] = jnp.where(bd_mask, _dot(d["v"], d["kk"] * jnp.exp(b_last - b), TN), 0.0)
        d["qb"] = (d["q"] * jnp.exp(b)).astype(BF16)
        d["e_last"] = jnp.exp(b_last)

    s = s_scr[...]
    for d in inst:
        o_ref[d["rs"], :] = d["av"] + _dot(d["qb"], s, NT)
        s = s * d["e_last"] + d["kv"]
    s_scr[...] = s

    @pl.when(sch_ref[3, n] == 1)
    def _():
        sfin_ref[0] = _collapse_block_diag(s)


def _hgrn_scan(hg, lb_e, s0, sched, reverse):
    n = hg.shape[0]
    n_seq = s0.shape[0]
    col = lambda c: pl.BlockSpec((TS, D_REC), lambda i, sch: (sch[0, i], c))
    i = np.arange(CHUNK)
    tri = (i[None, :] >= i[:, None]) if reverse else (i[None, :] <= i[:, None])
    sel = [tri]
    for s_half in (2, 1):
        mid = (i // (2 * s_half)) * (2 * s_half) + (s_half if reverse else s_half - 1)
        sel.append(tri[mid])
    sel = jnp.asarray(np.concatenate(sel, axis=0).astype(np.float32), BF16)
    grid_spec = pltpu.PrefetchScalarGridSpec(
        num_scalar_prefetch=1,
        grid=(sched.shape[1],),
        in_specs=[col(1 if reverse else 0), col(2), col(3),
                  pl.BlockSpec((1, D_REC), lambda i, sch: (0, 0)),
                  pl.BlockSpec((3 * CHUNK, CHUNK), lambda i, sch: (0, 0)),
                  pl.BlockSpec((1, HEAD_DIM, D_REC), lambda i, sch: (sch[1, i], 0, 0))],
        out_specs=[pl.BlockSpec((TS, D_REC), lambda i, sch: (sch[0, i], 0)),
                   pl.BlockSpec((1, HEAD_DIM, D_REC), lambda i, sch: (sch[1, i], 0, 0))],
        scratch_shapes=[pltpu.VMEM((D_REC, D_REC), F32)],
    )
    return pl.pallas_call(
        functools.partial(_hgrn_scan_kernel, reverse=reverse),
        out_shape=[jax.ShapeDtypeStruct((n, D_REC), F32), jax.ShapeDtypeStruct((n_seq, HEAD_DIM, D_REC), F32)],
        grid_spec=grid_spec,
        compiler_params=pltpu.CompilerParams(dimension_semantics=("arbitrary",), vmem_limit_bytes=VMEM_LIMIT),
        name="hgrn_scan_bwd" if reverse else "hgrn_scan_fwd",
    )(sched, hg, hg, hg, lb_e, sel, s0)


def _hgrn_bidir_kernel(sch_ref, ff_ref, if_ref, qf_ref, fb_ref, ib_ref, qb_ref, lb_ref, sel_ref, s0f_ref, s0b_ref,
                       of_ref, sff_ref, ob_ref, sfb_ref, s_scr):
    n = pl.program_id(0)
    n_chunks = TS // CHUNK
    lanes = ((False, ff_ref, if_ref, qf_ref, s0f_ref, of_ref, sff_ref), (True, fb_ref, ib_ref, qb_ref, s0b_ref, ob_ref, sfb_ref))

    for ln, lane_refs in enumerate(lanes):
        @pl.when(sch_ref[4 * ln + 2, n] == 1)
        def _(ln=ln, s0_ref=lane_refs[4]):
            s_scr[ln] = _block_diag(s0_ref[0])

    row = _iota((CHUNK, D_REC), 0)
    col_tok = _iota((CHUNK, D_REC), 1) % HEAD_DIM
    ones = _head_ones()
    bd_mask = _block_diag_mask()

    inst = []
    for ln, (reverse, f_ref, i_ref, q_ref, _, o_ref, _) in enumerate(lanes):
        lb = lb_ref[ln:ln + 1, :]
        sel = sel_ref[ln]
        for m in range(n_chunks):
            c = (n_chunks - 1 - m) if reverse else m
            rs = slice(c * CHUNK, (c + 1) * CHUNK)
            f = lb + (1.0 - lb) * _sigmoid(f_ref[rs, :])
            d = dict(rs=rs, kk=1.0 - f, q=q_ref[rs, :], v=i_ref[rs, :], reverse=reverse, ln=ln, o_ref=o_ref)
            cums = _dot_sel(sel, jnp.log(jnp.maximum(f, TINY)))
            d["b"] = cums[:CHUNK]
            d["b_mid_small"] = [cums[CHUNK:2 * CHUNK], cums[2 * CHUNK:]]
            inst.append(d)

    for lvl in range(N_LEVELS):
        s_half = (CHUNK // 2) >> lvl
        in_block = row % (2 * s_half)
        same_block = (row // (2 * s_half)) == (col_tok // (2 * s_half))
        for d in inst:
            b, reverse = d["b"], d["reverse"]
            later = (in_block < s_half) if reverse else (in_block >= s_half)
            if s_half >= 4:
                pieces = []
                for blk in range(CHUNK // (2 * s_half)):
                    r = blk * 2 * s_half + (s_half if reverse else s_half - 1)
                    pieces.append(jnp.broadcast_to(b[r:r + 1], (2 * s_half, D_REC)))
                b_mid = pieces[0] if len(pieces) == 1 else jnp.concatenate(pieces, axis=0)
            else:
                b_mid = d["b_mid_small"][lvl - (N_LEVELS - 2)]
            e = jnp.exp(-jnp.abs(b - b_mid))
            q_hat = jnp.where(later, d["q"] * e, 0.0)
            k_hat = jnp.where(later, 0.0, d["kk"] * e)
            res = _dot(q_hat, _block_diag(k_hat.astype(BF16)), NT)
            term = jnp.where(same_block, res, 0.0)
            d["a"] = term if lvl == 0 else d["a"] + term

    for d in inst:
        sums = _dot(d["q"] * d["kk"], ones)
        d["a"] = d["a"] + jnp.where(col_tok == row, sums, 0.0)

    for d in inst:
        b = d["b"]
        last = 0 if d["reverse"] else CHUNK - 1
        b_last = b[last:last + 1]
        d["av"] = _dot(d["a"], _block_diag(d["v"].astype(BF16)))
        d["kv"] = jnp.where(bd_mask, _dot(d["v"], d["kk"] * jnp.exp(b_last - b), TN), 0.0)
        d["qb"] = (d["q"] * jnp.exp(b)).astype(BF16)
        d["e_last"] = jnp.exp(b_last)

    state = [s_scr[0], s_scr[1]]
    for d in inst:
        s = state[d["ln"]]
        d["o_ref"][d["rs"], :] = d["av"] + _dot(d["qb"], s, NT)
        state[d["ln"]] = s * d["e_last"] + d["kv"]
    s_scr[0] = state[0]
    s_scr[1] = state[1]

    for ln, lane_refs in enumerate(lanes):
        @pl.when(sch_ref[4 * ln + 3, n] == 1)
        def _(ln=ln, sfin_ref=lane_refs[6]):
            sfin_ref[0] = _collapse_block_diag(s_scr[ln])


def _hgrn_scan_bidir(hg, lb_e, s0_f, s0_b, sched):
    n = hg.shape[0]
    n_seq = s0_f.shape[0]
    i = np.arange(CHUNK)
    sels = []
    for reverse in (False, True):
        tri = (i[None, :] >= i[:, None]) if reverse else (i[None, :] <= i[:, None])
        rows = [tri]
        for s_half in (2, 1):
            mid = (i // (2 * s_half)) * (2 * s_half) + (s_half if reverse else s_half - 1)
            rows.append(tri[mid])
        sels.append(np.concatenate(rows, axis=0))
    sel = jnp.asarray(np.stack(sels).astype(np.float32), BF16)

    def col(c, ln):
        return pl.BlockSpec((TS, D_REC), lambda i, sch: (sch[4 * ln, i], c))

    def state_spec(ln):
        return pl.BlockSpec((1, HEAD_DIM, D_REC), lambda i, sch: (sch[4 * ln + 1, i], 0, 0))

    def tile_out(ln):
        return pl.BlockSpec((TS, D_REC), lambda i, sch: (sch[4 * ln, i], 0))

    grid_spec = pltpu.PrefetchScalarGridSpec(
        num_scalar_prefetch=1,
        grid=(sched.shape[1],),
        in_specs=[col(0, 0), col(2, 0), col(3, 0), col(1, 1), col(2, 1), col(3, 1),
                  pl.BlockSpec((2, D_REC), lambda i, sch: (0, 0)),
                  pl.BlockSpec((2, 3 * CHUNK, CHUNK), lambda i, sch: (0, 0, 0)),
                  state_spec(0), state_spec(1)],
        out_specs=[tile_out(0), state_spec(0), tile_out(1), state_spec(1)],
        scratch_shapes=[pltpu.VMEM((2, D_REC, D_REC), F32)],
    )
    o_sds = jax.ShapeDtypeStruct((n, D_REC), F32)
    s_sds = jax.ShapeDtypeStruct((n_seq, HEAD_DIM, D_REC), F32)
    return pl.pallas_call(
        _hgrn_bidir_kernel,
        out_shape=[o_sds, s_sds, o_sds, s_sds],
        grid_spec=grid_spec,
        compiler_params=pltpu.CompilerParams(dimension_semantics=("arbitrary",), vmem_limit_bytes=VMEM_LIMIT),
        name="hgrn_scan",
    )(sched, hg, hg, hg, hg, hg, hg, lb_e, sel, s0_f, s0_b)


def _gated_readout(o, gate, w_e, ones):
    ms = _dot_sel(ones, o * o, sel_is_lhs=False) * (1.0 / HEAD_DIM)
    return o * lax.rsqrt(ms + NORM_EPS) * w_e * _silu(gate)


def _out_proj_kernel(*refs, alpha, n_ctx_tiles, n_x):
    x_refs = refs[:n_x]
    (mod_ref, oa_ref, bf_ref, bb_ref, cf_ref, cb_ref, gb_ref, gc_ref, nwb_ref, nwc_ref,
     w_ref, lng_ref, lnb_ref, y_ref) = refs[n_x:]
    x = _read_tokens(x_refs, n_ctx_tiles)
    m = mod_ref[0]
    ones = _head_ones()
    o_b = _gated_readout(bf_ref[...] + bb_ref[...], gb_ref[...], nwb_ref[...], ones)
    o_c = _gated_readout(cf_ref[...] + cb_ref[...], gc_ref[...], nwc_ref[...], ones)
    mix = jnp.dot(oa_ref[...].astype(BF16), w_ref[0:D_A], preferred_element_type=F32)
    mix = mix + jnp.dot(o_b.astype(BF16), w_ref[D_A:D_A + D_REC], preferred_element_type=F32)
    mix = mix + jnp.dot(o_c.astype(BF16), w_ref[D_A + D_REC:], preferred_element_type=F32)
    y_ref[...] = _layer_norm(alpha * x + m[2:3] * mix, lng_ref[...], lnb_ref[...])


def _out_proj(x_parts, mod_l, o_a, ob_f, ob_b, oc_f, oc_b, g_b, hg, nw_b, nw_c, w_out, ln_g, ln_b, mod_index, alpha,
              n_ctx_tiles):
    n = sum(p.shape[0] for p in x_parts)
    tile = lambda w: pl.BlockSpec((TM, w), lambda i: (i, 0))
    row = lambda w: pl.BlockSpec((1, w), lambda i: (0, 0))
    return pl.pallas_call(
        functools.partial(_out_proj_kernel, alpha=alpha, n_ctx_tiles=n_ctx_tiles, n_x=len(x_parts)),
        out_shape=jax.ShapeDtypeStruct((n, D_MODEL), F32),
        grid=(n // TM,),
        in_specs=_token_specs(x_parts, n_ctx_tiles) + [
                  pl.BlockSpec((1, 6, D_MODEL), lambda i: (mod_index(i), 0, 0)),
                  tile(D_A), tile(D_REC), tile(D_REC), tile(D_REC), tile(D_REC), tile(D_REC),
                  pl.BlockSpec((TM, D_REC), lambda i: (i, 4)),
                  row(D_REC), row(D_REC),
                  pl.BlockSpec((D_MODEL, D_MODEL), lambda i: (0, 0)),
                  row(D_MODEL), row(D_MODEL)],
        out_specs=tile(D_MODEL),
        compiler_params=pltpu.CompilerParams(dimension_semantics=("arbitrary",), vmem_limit_bytes=VMEM_LIMIT),
        name="out_proj_ln",
    )(*x_parts, mod_l, o_a, ob_f, ob_b, oc_f, oc_b, g_b, hg, nw_b, nw_c, w_out, ln_g, ln_b)


def _ffn_kernel(x_ref, mod_ref, w1_ref, w3_ref, w2_ref, lng_ref, lnb_ref, y_ref, *, alpha):
    m = mod_ref[0]
    x = x_ref[...]
    h = (x * (1.0 + m[4:5]) + m[3:4]).astype(BF16)
    acc = jnp.zeros((x.shape[0], D_MODEL), F32)
    for c in range(D_FF // FF_CHUNK):
        sl = slice(c * FF_CHUNK, (c + 1) * FF_CHUNK)
        a = jnp.dot(h, w1_ref[:, sl], preferred_element_type=F32)
        g = jnp.dot(h, w3_ref[:, sl], preferred_element_type=F32)
        acc = acc + jnp.dot((_silu(a) * g).astype(BF16), w2_ref[sl, :], preferred_element_type=F32)
    y_ref[...] = _layer_norm(alpha * x + m[5:6] * acc, lng_ref[...], lnb_ref[...])


def _ffn(x, mod_l, w1, w3, w2, ln_g, ln_b, mod_index_ffn, alpha):
    n = x.shape[0]
    whole = lambda a: pl.BlockSpec(a.shape, lambda i: (0, 0), pipeline_mode=pl.Buffered(1))
    row = pl.BlockSpec((1, D_MODEL), lambda i: (0, 0))
    return pl.pallas_call(
        functools.partial(_ffn_kernel, alpha=alpha),
        out_shape=jax.ShapeDtypeStruct((n, D_MODEL), F32),
        grid=(n // TM_FFN,),
        in_specs=[pl.BlockSpec((TM_FFN, D_MODEL), lambda i: (i, 0)),
                  pl.BlockSpec((1, 6, D_MODEL), lambda i: (mod_index_ffn(i), 0, 0)),
                  whole(w1), whole(w3), whole(w2), row, row],
        out_specs=pl.BlockSpec((TM_FFN, D_MODEL), lambda i: (i, 0)),
        compiler_params=pltpu.CompilerParams(dimension_semantics=("arbitrary",), vmem_limit_bytes=VMEM_LIMIT),
        name="ffn_ln",
    )(x, mod_l, w1, w3, w2, ln_g, ln_b)


def _moe_kernel(x_ref, mod_ref, wr_ref, br_ref, w1_ref, w3_ref, w2_ref, lng_ref, lnb_ref, y_ref,
                h_scr, comb_scr, acc_scr, *, alpha):
    e = pl.program_id(1)
    m = mod_ref[0]

    @pl.when(e == 0)
    def _():
        h = x_ref[...] * (1.0 + m[4:5]) + m[3:4]
        h_scr[...] = h.astype(BF16)
        logits = jnp.dot(h, wr_ref[...], preferred_element_type=F32, precision=lax.Precision.HIGHEST) + br_ref[...]
        lane = _iota(logits.shape, 1)
        m1 = jnp.max(logits, -1, keepdims=True)
        i1 = jnp.min(jnp.where(logits == m1, lane, 128), -1, keepdims=True)
        rest = jnp.where(lane == i1, NEG_BIG, logits)
        m2 = jnp.max(rest, -1, keepdims=True)
        i2 = jnp.min(jnp.where(rest == m2, lane, 128), -1, keepdims=True)
        e2 = jnp.exp(m2 - m1)
        g1 = 1.0 / (1.0 + e2)
        comb_scr[...] = jnp.where(lane == i1, g1, 0.0) + jnp.where(lane == i2, e2 * g1, 0.0)
        acc_scr[...] = jnp.zeros_like(acc_scr)

    h = h_scr[...]
    a = jnp.dot(h, w1_ref[0], preferred_element_type=F32)
    g = jnp.dot(h, w3_ref[0], preferred_element_type=F32)
    y = jnp.dot((_silu(a) * g).astype(BF16), w2_ref[0], preferred_element_type=F32)
    comb = comb_scr[...]
    gate = jnp.sum(jnp.where(_iota(comb.shape, 1) == e, comb, 0.0), -1, keepdims=True)
    acc_scr[...] += gate * y

    @pl.when(e == N_EXPERTS - 1)
    def _():
        y_ref[...] = _layer_norm(alpha * x_ref[...] + m[5:6] * acc_scr[...], lng_ref[...], lnb_ref[...])


def _moe_routed_kernel(x_ref, mod_ref, wr_ref, br_ref, w1_ref, w3_ref, w2_ref, lng_ref, lnb_ref, *rest,
                       alpha, split_ctx_tiles):
    if split_ctx_tiles is None:
        y_ref, h_scr, mem_scr, gate_scr, rank_scr, acc_scr, earlier_scr = rest
    else:
        y_ref, y2_ref, h_scr, mem_scr, gate_scr, rank_scr, acc_scr, earlier_scr = rest
    _moe_routed_body(x_ref, mod_ref, wr_ref, br_ref, w1_ref, w3_ref, w2_ref, lng_ref, lnb_ref, y_ref,
                     None if split_ctx_tiles is None else y2_ref, h_scr, mem_scr, gate_scr, rank_scr, acc_scr,
                     earlier_scr, alpha=alpha, split_ctx_tiles=split_ctx_tiles)


def _moe_routed_body(x_ref, mod_ref, wr_ref, br_ref, w1_ref, w3_ref, w2_ref, lng_ref, lnb_ref, y_ref, y2_ref,
                     h_scr, mem_scr, gate_scr, rank_scr, acc_scr, earlier_scr, *, alpha, split_ctx_tiles):
    e = pl.program_id(1)
    m = mod_ref[0]
    t_tile = x_ref.shape[0]

    @pl.when(jnp.logical_and(pl.program_id(0) == 0, e == 0))
    def _():
        earlier_scr[...] = jnp.where(_iota((t_tile, t_tile), 0) < _iota((t_tile, t_tile), 1), 1.0, 0.0).astype(BF16)

    @pl.when(e == 0)
    def _():
        h = x_ref[...] * (1.0 + m[4:5]) + m[3:4]
        h_hi, h_lo = _split2(h)
        h_scr[...] = h_hi
        w_hi, w_lo = _split2(wr_ref[...])
        both = lax.dot_general(jnp.concatenate([w_hi, w_lo], axis=0), h_hi, NT, preferred_element_type=F32)
        lt = (both[:N_EXPERTS] + both[N_EXPERTS:] + lax.dot_general(w_hi, h_lo, NT, preferred_element_type=F32)
              + br_ref[...])
        eidx = _iota(lt.shape, 0)
        m1 = jnp.max(lt, 0, keepdims=True)
        i1 = jnp.min(jnp.where(lt == m1, eidx, N_EXPERTS), 0, keepdims=True)
        rest = jnp.where(eidx == i1, NEG_BIG, lt)
        m2 = jnp.max(rest, 0, keepdims=True)
        i2 = jnp.min(jnp.where(rest == m2, eidx, N_EXPERTS), 0, keepdims=True)
        e2 = jnp.exp(m2 - m1)
        g1 = 1.0 / (1.0 + e2)
        mem = jnp.where(eidx == i1, 1.0, jnp.where(eidx == i2, 1.0, 0.0))
        mem_scr[...] = mem
        gate_scr[...] = jnp.where(eidx == i1, g1, jnp.where(eidx == i2, e2 * g1, 0.0))
        rank_scr[...] = jnp.dot(mem.astype(BF16), earlier_scr[...], preferred_element_type=F32)
        acc_scr[...] = jnp.zeros_like(acc_scr)

    mem_row = mem_scr[pl.ds(e, 1), :]
    gate_row = gate_scr[pl.ds(e, 1), :]
    rank_row = rank_scr[pl.ds(e, 1), :]
    count = jnp.sum(mem_row).astype(jnp.int32)
    slot0 = _iota((MOE_BLOCK, t_tile), 0).astype(F32)

    def block(j, carry):
        slot = slot0 + (j * MOE_BLOCK).astype(F32)
        hit = jnp.where(rank_row == slot, mem_row, 0.0)
        sel = hit.astype(BF16)
        hb = jnp.dot(sel, h_scr[...], preferred_element_type=F32).astype(BF16)
        a = jnp.dot(hb, w1_ref[0], preferred_element_type=F32)
        g = jnp.dot(hb, w3_ref[0], preferred_element_type=F32)
        yb = jnp.dot((_silu(a) * g).astype(BF16), w2_ref[0], preferred_element_type=F32)
        gate_slot = jnp.sum(hit * gate_row, -1, keepdims=True)
        acc_scr[...] += lax.dot_general(sel, (yb * gate_slot).astype(BF16), TN, preferred_element_type=F32)
        return carry

    lax.fori_loop(0, (count + MOE_BLOCK - 1) // MOE_BLOCK, block, 0)

    @pl.when(e == N_EXPERTS - 1)
    def _():
        y = _layer_norm(alpha * x_ref[...] + m[5:6] * acc_scr[...], lng_ref[...], lnb_ref[...])
        if split_ctx_tiles is None:
            y_ref[...] = y
        else:
            @pl.when(pl.program_id(0) < split_ctx_tiles)
            def _():
                y_ref[...] = y

            @pl.when(pl.program_id(0) >= split_ctx_tiles)
            def _():
                y2_ref[...] = y


def _moe(x, mod_l, w_r, b_r, w1, w3, w2, ln_g, ln_b, mod_index_moe, alpha, split_ctx_tiles=None):
    n = x.shape[0]
    row = pl.BlockSpec((1, D_MODEL), lambda i, e: (0, 0))
    expert = lambda a: pl.BlockSpec((1,) + a.shape[1:], lambda i, e: (e, 0, 0))
    routing = pltpu.VMEM((N_EXPERTS, TM_MOE), F32)
    if split_ctx_tiles is None:
        out_shape = jax.ShapeDtypeStruct((n, D_MODEL), F32)
        out_specs = pl.BlockSpec((TM_MOE, D_MODEL), lambda i, e: (i, 0))
    else:
        n_ctx_rows = split_ctx_tiles * TM_MOE
        out_shape = [jax.ShapeDtypeStruct((n_ctx_rows, D_MODEL), F32), jax.ShapeDtypeStruct((n - n_ctx_rows, D_MODEL), F32)]
        out_specs = [pl.BlockSpec((TM_MOE, D_MODEL), lambda i, e: (jnp.minimum(i, split_ctx_tiles - 1), 0)),
                     pl.BlockSpec((TM_MOE, D_MODEL), lambda i, e: (jnp.maximum(i - split_ctx_tiles, 0), 0))]
    return pl.pallas_call(
        functools.partial(_moe_routed_kernel, alpha=alpha, split_ctx_tiles=split_ctx_tiles),
        out_shape=out_shape,
        grid=(n // TM_MOE, N_EXPERTS),
        in_specs=[pl.BlockSpec((TM_MOE, D_MODEL), lambda i, e: (i, 0)),
                  pl.BlockSpec((1, 6, D_MODEL), lambda i, e: (mod_index_moe(i), 0, 0)),
                  pl.BlockSpec((N_EXPERTS, D_MODEL), lambda i, e: (0, 0)),
                  pl.BlockSpec((N_EXPERTS, 1), lambda i, e: (0, 0)),
                  expert(w1), expert(w3), expert(w2), row, row],
        out_specs=out_specs,
        scratch_shapes=[pltpu.VMEM((TM_MOE, D_MODEL), BF16), routing, routing, routing,
                        pltpu.VMEM((TM_MOE, D_MODEL), F32), pltpu.VMEM((TM_MOE, TM_MOE), BF16)],
        compiler_params=pltpu.CompilerParams(dimension_semantics=("arbitrary", "arbitrary"),
                                             vmem_limit_bytes=VMEM_LIMIT),
        name="moe_ln",
    )(x, mod_l, w_r, b_r, w1, w3, w2, ln_g, ln_b)


def _moe_dense(x, mod_l, w_r, b_r, w1, w3, w2, ln_g, ln_b, mod_index_moe, alpha):
    n = x.shape[0]
    row = pl.BlockSpec((1, D_MODEL), lambda i, e: (0, 0))
    expert = lambda a: pl.BlockSpec((1,) + a.shape[1:], lambda i, e: (e, 0, 0))
    return pl.pallas_call(
        functools.partial(_moe_kernel, alpha=alpha),
        out_shape=jax.ShapeDtypeStruct((n, D_MODEL), F32),
        grid=(n // TM_MOE, N_EXPERTS),
        in_specs=[pl.BlockSpec((TM_MOE, D_MODEL), lambda i, e: (i, 0)),
                  pl.BlockSpec((1, 6, D_MODEL), lambda i, e: (mod_index_moe(i), 0, 0)),
                  pl.BlockSpec((D_MODEL, 128), lambda i, e: (0, 0)),
                  pl.BlockSpec((1, 128), lambda i, e: (0, 0)),
                  expert(w1), expert(w3), expert(w2), row, row],
        out_specs=pl.BlockSpec((TM_MOE, D_MODEL), lambda i, e: (i, 0)),
        scratch_shapes=[pltpu.VMEM((TM_MOE, D_MODEL), BF16), pltpu.VMEM((TM_MOE, 128), F32),
                        pltpu.VMEM((TM_MOE, D_MODEL), F32)],
        compiler_params=pltpu.CompilerParams(dimension_semantics=("arbitrary", "arbitrary"),
                                             vmem_limit_bytes=VMEM_LIMIT),
        name="moe_ln",
    )(x, mod_l, w_r, b_r, w1, w3, w2, ln_g, ln_b)


def _rope_tables(lat_len):
    t = jnp.arange(lat_len)
    pos = jnp.stack([t // GRID_W, t % GRID_W], axis=1).astype(F32)
    half, quarter = HEAD_DIM // 2, HEAD_DIM // 4
    inv_freq = ROPE_THETA ** (-jnp.arange(quarter, dtype=F32) * 2.0 / half)
    lane = np.arange(HEAD_DIM)
    ang = pos[:, lane // half] * inv_freq[lane % quarter][None, :]
    sign = jnp.asarray(np.where((lane % half) < quarter, -1.0, 1.0), F32)
    cos = jnp.tile(jnp.cos(ang), (1, 128 // HEAD_DIM))
    sin = jnp.tile(jnp.sin(ang) * sign, (1, 128 // HEAD_DIM))
    return cos, sin


def kernel(x_prompt, x_sample, cache_k, cache_v, state_gdn, state_hgrn, c, c_ctx, w_mod, b_mod, w_in, conv_w, attn_sink, gdn_a_log, gdn_dt_bias, gdn_norm_w, hgrn_lb, hgrn_norm_w, w_out, ln_g, ln_b, ffn_w1, ffn_w3, ffn_w2, moe_router, moe_router_b, moe_w1, moe_w3, moe_w2):
    depth = w_in.shape[0]
    n_ctx, ctx_len, _ = x_prompt.shape
    lat_batch, lat_len, _ = x_sample.shape
    n_ctx_rows = n_ctx * ctx_len
    alpha = (2.0 * depth) ** 0.25

    x_parts = [x_prompt.reshape(-1, D_MODEL), x_sample.reshape(-1, D_MODEL)]

    def mod_index_for(tile):
        ctx_tiles, per_seq = n_ctx_rows // tile, lat_len // tile
        return lambda i: jnp.where(i < ctx_tiles, 0, 1 + (i - ctx_tiles) // per_seq)

    cvecs = jnp.concatenate([c_ctx[None], c], axis=0)
    mods = _modulation(cvecs, w_mod, b_mod).reshape(depth, cvecs.shape[0], 6, D_MODEL)

    cos_tab, sin_tab = _rope_tables(lat_len)
    sm = jax.nn.softmax(hgrn_lb.astype(F32), axis=0)
    lower = jnp.cumsum(sm, axis=0) - sm[0:1]

    w_in_r = jnp.concatenate([w_in[:, :, :1792], w_in[:, :, 1808:], w_in[:, :, 1792:1808],
                              jnp.zeros((depth, D_MODEL, C_END - 3088), F32)], axis=-1).astype(BF16)
    w_out_b = w_out.astype(BF16)

    zeros_state = jnp.zeros((n_ctx, HEAD_DIM, D_REC), F32)
    sched = [_scan_schedule(n_ctx, ctx_len, lat_batch, lat_len, reverse) for reverse in (False, True)]
    new_k, new_v, new_sg, new_sh = [], [], [], []

    for l in range(depth):
        qa, ka, va, qkvb, g_b, hg, ba = _in_proj(x_parts, mods[l], w_in_r[l], cos_tab, sin_tab, mod_index_for(TM),
                                                 n_ctx_rows // TM, lat_len // TM)
        new_k.append(ka[:n_ctx_rows].reshape(n_ctx, ctx_len, A_KV_HEADS, HEAD_DIM))
        new_v.append(va[:n_ctx_rows].reshape(n_ctx, ctx_len, A_KV_HEADS, HEAD_DIM))

        o_a = _attention(qa, ka, va, cache_k[:, l].reshape(lat_batch, -1, D_KV), cache_v[:, l].reshape(lat_batch, -1, D_KV),
                         attn_sink[l], n_ctx, ctx_len, lat_batch, lat_len)

        sg0 = state_gdn[:, l].transpose(0, 1, 3, 2, 4).reshape(lat_batch, 2, HEAD_DIM, D_REC)
        sh0 = state_hgrn[:, l].transpose(0, 1, 4, 2, 3).reshape(lat_batch, 2, HEAD_DIM, D_REC)
        ob, oc, sg_fin, sh_fin = [], [], [], []
        alog_row = jnp.zeros((1, 128), F32).at[0, 8:16].set(gdn_a_log[l].reshape(-1))
        dtb_row = jnp.zeros((1, 128), F32).at[0, 8:16].set(gdn_dt_bias[l].reshape(-1))
        qkv, bl = _gdn_prep(qkvb, ba, conv_w[l], alog_row, dtb_row, n_ctx_rows // TS, ctx_len // TS, lat_len // TS)
        o_f, s_f, o_b, s_b = _gdn_scan_bidir(qkv, bl, jnp.concatenate([zeros_state, sg0[:, 0]], axis=0),
                                             jnp.concatenate([zeros_state, sg0[:, 1]], axis=0),
                                             jnp.concatenate(sched, axis=0))
        ob = [o_f, o_b]
        sg_fin = [s[:n_ctx].reshape(n_ctx, HEAD_DIM, N_HEADS_REC, HEAD_DIM).transpose(0, 2, 1, 3) for s in (s_f, s_b)]
        o_f, s_f, o_b, s_b = _hgrn_scan_bidir(hg, lower[l], jnp.concatenate([zeros_state, sh0[:, 0]], axis=0),
                                              jnp.concatenate([zeros_state, sh0[:, 1]], axis=0),
                                              jnp.concatenate(sched, axis=0))
        oc = [o_f, o_b]
        sh_fin = [s[:n_ctx].reshape(n_ctx, HEAD_DIM, N_HEADS_REC, HEAD_DIM).transpose(0, 2, 3, 1) for s in (s_f, s_b)]
        new_sg.append(jnp.stack(sg_fin, axis=1))
        new_sh.append(jnp.stack(sh_fin, axis=1))

        nw_b = jnp.tile(gdn_norm_w[l], N_HEADS_REC)[None]
        nw_c = jnp.tile(hgrn_norm_w[l], N_HEADS_REC)[None]
        x = _out_proj(x_parts, mods[l], o_a, ob[0], ob[1], oc[0], oc[1], g_b, hg, nw_b, nw_c, w_out_b[l],
                      ln_g[l, 0][None], ln_b[l, 0][None], mod_index_for(TM), alpha, n_ctx_rows // TM)

        i = l // 2
        if l % 2 == 0:
            x = _ffn(x, mods[l], ffn_w1[i].astype(BF16), ffn_w3[i].astype(BF16), ffn_w2[i].astype(BF16),
                     ln_g[l, 1][None], ln_b[l, 1][None], mod_index_for(TM_FFN), alpha)
        else:
            split = n_ctx_rows // TM_MOE if l == depth - 1 else None
            x = _moe(x, mods[l], moe_router[i].T, moe_router_b[i][:, None],
                     moe_w1[i].astype(BF16), moe_w3[i].astype(BF16), moe_w2[i].astype(BF16),
                     ln_g[l, 1][None], ln_b[l, 1][None], mod_index_for(TM_MOE), alpha, split)
        x_parts = [x]

    if isinstance(x, (list, tuple)):
        y_prompt, y_sample = x[0].reshape(x_prompt.shape), x[1].reshape(x_sample.shape)
    else:
        y_prompt = x[:n_ctx_rows].reshape(x_prompt.shape)
        y_sample = x[n_ctx_rows:].reshape(x_sample.shape)
    return (y_prompt, y_sample, jnp.stack(new_k, axis=1), jnp.stack(new_v, axis=1),
            jnp.stack(new_sg, axis=1), jnp.stack(new_sh, axis=1))
```

```python
import functools

import numpy as np
import jax
import jax.numpy as jnp
from jax import lax
from jax.experimental import pallas as pl
from jax.experimental.pallas import tpu as pltpu

F32 = jnp.float32
BF16 = jnp.bfloat16

D_MODEL = 1024
HEAD_DIM = 64
A_HEADS = 8
A_KV_HEADS = 2
GRID_W = 64
ATTN_BLOCK = 128
N_HEADS_REC = 4
D_REC = N_HEADS_REC * HEAD_DIM
CONV_W = 5
CHUNK = 64
N_LEVELS = 6
D_A = A_HEADS * HEAD_DIM
D_KV = A_KV_HEADS * HEAD_DIM
D_FF = 2816
N_EXPERTS = 8
EXPERT_FF = 1024
ROPE_THETA = 10000.0
NORM_EPS = 1e-5
TINY = 1e-30
NEG_BIG = -1e30

C_QA, C_KA, C_VA, C_QKVB, C_GB, C_HG = 0, 512, 640, 768, 1536, 1792
C_BA_END = C_HG + 16
D_HG = 1280
HALO = 8

TM = 512
TS = 256
TM_FFN = 512
FF_CHUNK = 256
TM_MOE = 1024
MOE_BLOCK = 128
VMEM_LIMIT = 56 * 1024 * 1024

NN = (((1,), (0,)), ((), ()))
NT = (((1,), (1,)), ((), ()))
TN = (((0,), (0,)), ((), ()))


def _dot(a, b, dims=NN):
    return lax.dot_general(a.astype(BF16), b.astype(BF16), dims, preferred_element_type=F32)


def _split2(x):
    hi = x.astype(BF16)
    lo = (x - hi.astype(F32)).astype(BF16)
    return hi, lo


def _split3(x):
    hi = x.astype(BF16)
    r = x - hi.astype(F32)
    mid = r.astype(BF16)
    lo = (r - mid.astype(F32)).astype(BF16)
    return hi, mid, lo


def _dot_sel(sel, x, dims=NN, sel_is_lhs=True):
    out = None
    for part in _split3(x):
        a, b = (sel, part) if sel_is_lhs else (part, sel)
        t = lax.dot_general(a, b, dims, preferred_element_type=F32)
        out = t if out is None else out + t
    return out


def _sigmoid(x):
    return 1.0 / (1.0 + jnp.exp(-x))


def _silu(x):
    return x * _sigmoid(x)


def _layer_norm(x, g, b):
    mu = jnp.mean(x, -1, keepdims=True)
    xc = x - mu
    var = jnp.mean(xc * xc, -1, keepdims=True)
    return xc * lax.rsqrt(var + NORM_EPS) * g + b


def _iota(shape, dim):
    return lax.broadcasted_iota(jnp.int32, shape, dim)


def _head_ones():
    return jnp.where(_iota((D_REC, D_REC), 0) // HEAD_DIM == _iota((D_REC, D_REC), 1) // HEAD_DIM, 1.0, 0.0).astype(BF16)


def _block_diag(x):
    head = _iota(x.shape, 1) // HEAD_DIM
    return jnp.concatenate([jnp.where(head == h, x, jnp.zeros_like(x)) for h in range(N_HEADS_REC)], axis=0)


def _block_diag_mask():
    return _iota((D_REC, D_REC), 0) // HEAD_DIM == _iota((D_REC, D_REC), 1) // HEAD_DIM


def _collapse_block_diag(s):
    out = s[0:HEAD_DIM]
    for h in range(1, N_HEADS_REC):
        out = out + s[h * HEAD_DIM:(h + 1) * HEAD_DIM]
    return out


def _mod_kernel(ct_ref, w_ref, b_ref, o_ref):
    s = _silu(ct_ref[...])
    w = w_ref[0]
    rows = [jnp.sum(s[:, r:r + 1] * w, axis=0, keepdims=True) for r in range(o_ref.shape[1])]
    o_ref[0] = jnp.concatenate(rows, axis=0) + b_ref[0]


def _modulation(cvecs, w_mod, b_mod):
    depth, d, n6 = w_mod.shape
    r = cvecs.shape[0]
    tn = 1024
    ct = jnp.zeros((d, 8), F32).at[:, :r].set(cvecs.T)
    return pl.pallas_call(
        _mod_kernel,
        out_shape=jax.ShapeDtypeStruct((depth, r, n6), F32),
        grid=(depth, n6 // tn),
        in_specs=[pl.BlockSpec((d, 8), lambda l, j: (0, 0)),
                  pl.BlockSpec((1, d, tn), lambda l, j: (l, 0, j)),
                  pl.BlockSpec((1, 1, tn), lambda l, j: (l, 0, j))],
        out_specs=pl.BlockSpec((1, r, tn), lambda l, j: (l, 0, j)),
        compiler_params=pltpu.CompilerParams(dimension_semantics=("arbitrary", "arbitrary"),
                                             vmem_limit_bytes=VMEM_LIMIT),
        name="modulation",
    )(ct, w_mod, b_mod.reshape(depth, 1, n6))


def _rope(x, cos, sin):
    outs = []
    lane = _iota(cos.shape, 1)
    first = (lane % 32) < 16
    for m in range(x.shape[1] // 128):
        xs = x[:, m * 128:(m + 1) * 128]
        swapped = jnp.where(first, pltpu.roll(xs, 128 - 16, 1), pltpu.roll(xs, 16, 1))
        outs.append(xs * cos + swapped * sin)
    return outs[0] if len(outs) == 1 else jnp.concatenate(outs, axis=1)


def _token_specs(x_parts, n_ctx_tiles):
    if len(x_parts) == 1:
        return [pl.BlockSpec((TM, D_MODEL), lambda i: (i, 0))]
    return [pl.BlockSpec((TM, D_MODEL), lambda i: (jnp.minimum(i, n_ctx_tiles - 1), 0)),
            pl.BlockSpec((TM, D_MODEL), lambda i: (jnp.maximum(i - n_ctx_tiles, 0), 0))]


def _read_tokens(x_refs, n_ctx_tiles):
    if len(x_refs) == 1:
        return x_refs[0][...]
    return jnp.where(pl.program_id(0) < n_ctx_tiles, x_refs[0][...], x_refs[1][...])


def _halo_specs(x_parts, n_ctx_tiles):
    per = TM // HALO
    specs = []
    for p, part in enumerate(x_parts):
        last_block = part.shape[0] // HALO - 1
        if len(x_parts) == 1:
            local = lambda i: i
        elif p == 0:
            local = lambda i: jnp.minimum(i, n_ctx_tiles - 1)
        else:
            local = lambda i: jnp.maximum(i - n_ctx_tiles, 0)
        specs.append(pl.BlockSpec((HALO, D_MODEL), lambda i, local=local: (jnp.maximum(local(i) * per - 1, 0), 0)))
        specs.append(pl.BlockSpec((HALO, D_MODEL),
                                  lambda i, local=local, last_block=last_block:
                                  (jnp.minimum((local(i) + 1) * per, last_block), 0)))
    return specs


def _in_proj_kernel(*refs, n_ctx_tiles, n_x, ctx_len, lat_len):
    x_refs, halo_refs = refs[:n_x], refs[n_x:3 * n_x]
    (mod_ref, w_ref, whg_ref, wba_ref, cos_ref, sin_ref, convw_ref, alog_ref, dtb_ref,
     qa_ref, ka_ref, va_ref, qkv_ref, gb_ref, hg_ref, bl_ref) = refs[3 * n_x:]
    i = pl.program_id(0)
    in_ctx = i < n_ctx_tiles
    m = mod_ref[0]
    h = (_read_tokens(x_refs, n_ctx_tiles) * (1.0 + m[1:2]) + m[0:1]).astype(BF16)

    def proj(lo, hi):
        return jnp.dot(h, w_ref[:, lo:hi], preferred_element_type=F32)

    if n_x == 1:
        x_halo = jnp.concatenate([halo_refs[0][...], halo_refs[1][...]], axis=0)
    else:
        x_halo = jnp.concatenate([jnp.where(in_ctx, halo_refs[0][...], halo_refs[2][...]),
                                  jnp.where(in_ctx, halo_refs[1][...], halo_refs[3][...])], axis=0)
    h_halo = (x_halo * (1.0 + m[1:2]) + m[0:1]).astype(BF16)
    p_halo = jnp.dot(h_halo, w_ref[:, C_QKVB:C_GB], preferred_element_type=F32)
    xh = jnp.concatenate([p_halo[:HALO], proj(C_QKVB, C_GB), p_halo[HALO:]], axis=0)

    r = _iota((TM, 1), 0)
    pos_ctx = (i * TM + r) % ctx_len
    pos_lat = ((i - n_ctx_tiles) * TM + r) % lat_len
    cw = convw_ref[...]
    y = None
    for t in range(CONV_W):
        lo = HALO - CONV_W // 2 + t
        term = xh[lo:lo + TM] * cw[t:t + 1]
        off = t - CONV_W // 2
        if off != 0:
            ok_ctx = jnp.logical_and(pos_ctx + off >= 0, pos_ctx + off < ctx_len)
            ok_lat = jnp.logical_and(pos_lat + off >= 0, pos_lat + off < lat_len)
            keep = jnp.where(in_ctx, jnp.where(ok_ctx, 1.0, 0.0), jnp.where(ok_lat, 1.0, 0.0))
            term = term * keep
        y = term if y is None else y + term
        if t == 1:
            qa_ref[...] = proj(C_QA, C_KA)
        if t == 3:
            hg_ref[:, :2 * D_REC] = jnp.dot(h, whg_ref[:, :2 * D_REC], preferred_element_type=F32)
    ka_ref[...] = proj(C_KA, C_VA)
    va_ref[...] = proj(C_VA, C_QKVB)
    y = _silu(y)
    hg_ref[:, 2 * D_REC:4 * D_REC] = jnp.dot(h, whg_ref[:, 2 * D_REC:4 * D_REC], preferred_element_type=F32)
    q, k, v = y[:, :D_REC], y[:, D_REC:2 * D_REC], y[:, 2 * D_REC:]
    ones = _head_ones()
    qq_hi, qq_lo = _split2(q * q)
    kk_hi, kk_lo = _split2(k * k)
    sums = jnp.dot(jnp.concatenate([qq_hi, qq_lo, kk_hi, kk_lo], axis=0), ones, preferred_element_type=F32)
    hg_ref[:, 4 * D_REC:] = jnp.dot(h, whg_ref[:, 4 * D_REC:], preferred_element_type=F32)
    gb_ref[...] = proj(C_GB, C_HG)
    ba = jnp.dot(h, wba_ref[...], preferred_element_type=F32)
    q_ss = sums[0:TM] + sums[TM:2 * TM]
    k_ss = sums[2 * TM:3 * TM] + sums[3 * TM:]
    qkv_ref[:, :D_REC] = q * lax.rsqrt(q_ss + 1e-6) * (HEAD_DIM ** -0.5)
    qkv_ref[:, D_REC:2 * D_REC] = k * lax.rsqrt(k_ss + 1e-6)
    qkv_ref[:, 2 * D_REC:] = v

    za = ba + dtb_ref[...]
    softplus = jnp.maximum(za, 0.0) + jnp.log(1.0 + jnp.exp(-jnp.abs(za)))
    la = -jnp.exp(alog_ref[...]) * softplus
    bl_ref[...] = jnp.where(_iota(ba.shape, 1) < 2 * N_HEADS_REC, _sigmoid(ba), la)

    @pl.when(i >= n_ctx_tiles)
    def _():
        cos, sin = cos_ref[...], sin_ref[...]
        qa_ref[...] = _rope(qa_ref[...], cos, sin)
        ka_ref[...] = _rope(ka_ref[...], cos, sin)


def _in_proj(x_parts, mod_l, w_in_l, cos_tab, sin_tab, conv_w, alog_row, dtb_row, mod_index, n_ctx_tiles,
             ctx_len, lat_len):
    n = sum(p.shape[0] for p in x_parts)
    lat_tiles_per_seq = lat_len // TM
    w_main = w_in_l[:, :C_HG].astype(BF16)
    w_hg = w_in_l[:, C_BA_END:].astype(BF16)
    w_ba = jnp.pad(w_in_l[:, C_HG:C_BA_END], ((0, 0), (0, 128 - (C_BA_END - C_HG)))).astype(BF16)
    whole = lambda a: pl.BlockSpec(a.shape, lambda i: (0,) * a.ndim)

    def tab_index(i):
        return (jnp.where(i < n_ctx_tiles, 0, (i - n_ctx_tiles) % lat_tiles_per_seq), 0)

    def out(width):
        return jax.ShapeDtypeStruct((n, width), F32), pl.BlockSpec((TM, width), lambda i: (i, 0))

    outs = [out(D_A), out(D_KV), out(D_KV), out(3 * D_REC), out(D_REC), out(D_HG), out(128)]
    return pl.pallas_call(
        functools.partial(_in_proj_kernel, n_ctx_tiles=n_ctx_tiles, n_x=len(x_parts), ctx_len=ctx_len,
                          lat_len=lat_len),
        out_shape=[o[0] for o in outs],
        grid=(n // TM,),
        in_specs=_token_specs(x_parts, n_ctx_tiles) + _halo_specs(x_parts, n_ctx_tiles) + [
            pl.BlockSpec((1, 6, D_MODEL), lambda i: (mod_index(i), 0, 0)),
            whole(w_main), whole(w_hg), whole(w_ba),
            pl.BlockSpec((TM, 128), tab_index),
            pl.BlockSpec((TM, 128), tab_index),
            whole(conv_w), whole(alog_row), whole(dtb_row)],
        out_specs=[o[1] for o in outs],
        compiler_params=pltpu.CompilerParams(dimension_semantics=("arbitrary",), vmem_limit_bytes=VMEM_LIMIT),
        name="in_proj",
    )(*x_parts, *[p for part in x_parts for p in (part, part)], mod_l, w_main, w_hg, w_ba, cos_tab, sin_tab,
      conv_w, alog_row, dtb_row)


def _attn_heads(q_ref, k, v, bias, sink_ref, o_ref):
    k_same, k_swap = k.astype(BF16), pltpu.roll(k, HEAD_DIM, 1).astype(BF16)
    v_same, v_swap = v.astype(BF16), pltpu.roll(v, HEAD_DIM, 1).astype(BF16)
    nq = q_ref.shape[0]
    low_half = _iota((nq, 128), 1) < HEAD_DIM
    rep = A_HEADS // A_KV_HEADS
    for g in range(A_KV_HEADS):
        heads = range(g * rep, (g + 1) * rep)
        scores, exps, dens, outs = {}, {}, {}, {}
        for hq in heads:
            p, e = hq // 2, hq % 2
            q2 = q_ref[:, p * 128:(p + 1) * 128] * (HEAD_DIM ** -0.5)
            qm = jnp.where(low_half if e == 0 else jnp.logical_not(low_half), q2, 0.0)
            s = _dot(qm, k_same if e == g else k_swap, NT)
            scores[hq] = s if bias is None else s + bias
        for hq in heads:
            snk = sink_ref[hq]
            mx = jnp.maximum(jnp.max(scores[hq], -1, keepdims=True), snk)
            exps[hq] = jnp.exp(scores[hq] - mx)
            dens[hq] = jnp.sum(exps[hq], -1, keepdims=True) + jnp.exp(snk - mx)
        for hq in heads:
            outs[hq] = _dot(exps[hq], v_same if hq % 2 == g else v_swap) / dens[hq]
        for p in range(g * rep // 2, (g + 1) * rep // 2):
            o_ref[:, p * 128:(p + 1) * 128] = jnp.where(low_half, outs[2 * p], outs[2 * p + 1])


def _attn_kernel(q_ref, kx_ref, vx_ref, kp_ref, kc_ref, kn_ref, vp_ref, vc_ref, vn_ref, ck_ref, cv_ref, sink_ref,
                 o_ref, *, ctx_steps, n_blocks):
    i = pl.program_id(0)

    @pl.when(i < ctx_steps)
    def _():
        _attn_heads(q_ref, kx_ref[...], vx_ref[...], None, sink_ref, o_ref)

    @pl.when(i >= ctx_steps)
    def _():
        n = (i - ctx_steps) % n_blocks
        k = jnp.concatenate([kp_ref[...], kc_ref[...], kn_ref[...], ck_ref[0]], axis=0)
        v = jnp.concatenate([vp_ref[...], vc_ref[...], vn_ref[...], cv_ref[0]], axis=0)
        nk = k.shape[0]
        qi = _iota((ATTN_BLOCK, nk), 0)
        kj = _iota((ATTN_BLOCK, nk), 1)
        seg = kj // ATTN_BLOCK
        jj = kj % ATTN_BLOCK
        bias_prev = jnp.where(jj >= qi, 0.0, NEG_BIG) + jnp.where(n > 0, 0.0, NEG_BIG)
        bias_next = jnp.where(jj <= qi, 0.0, NEG_BIG) + jnp.where(n < n_blocks - 1, 0.0, NEG_BIG)
        bias = jnp.where(seg == 0, bias_prev, jnp.where(seg == 2, bias_next, 0.0))
        _attn_heads(q_ref, k, v, bias, sink_ref, o_ref)


def _attention(qa, ka, va, ck, cv, sink, n_ctx, ctx_len, lat_batch, lat_len):
    n = qa.shape[0]
    per_seq = ctx_len // ATTN_BLOCK
    ctx_steps = n_ctx * per_seq
    nb = lat_len // ATTN_BLOCK
    past = ck.shape[1]

    def lat(i):
        return jnp.maximum(i - ctx_steps, 0)

    def cur(i):
        return (ctx_steps + lat(i), 0)

    def prev(i):
        return (ctx_steps + (lat(i) // nb) * nb + jnp.maximum(lat(i) % nb - 1, 0), 0)

    def nxt(i):
        return (ctx_steps + (lat(i) // nb) * nb + jnp.minimum(lat(i) % nb + 1, nb - 1), 0)

    def own_seq(i):
        return (jnp.minimum(i // per_seq, n_ctx - 1), 0)

    kv = lambda f: pl.BlockSpec((ATTN_BLOCK, D_KV), f)
    cache = pl.BlockSpec((1, past, D_KV), lambda i: (lat(i) // nb, 0, 0))
    return pl.pallas_call(
        functools.partial(_attn_kernel, ctx_steps=ctx_steps, n_blocks=nb),
        out_shape=jax.ShapeDtypeStruct((n, D_A), F32),
        grid=(ctx_steps + lat_batch * nb,),
        in_specs=[pl.BlockSpec((ATTN_BLOCK, D_A), lambda i: (i, 0)),
                  pl.BlockSpec((ctx_len, D_KV), own_seq), pl.BlockSpec((ctx_len, D_KV), own_seq),
                  kv(prev), kv(cur), kv(nxt), kv(prev), kv(cur), kv(nxt), cache, cache,
                  pl.BlockSpec(memory_space=pltpu.SMEM)],
        out_specs=pl.BlockSpec((ATTN_BLOCK, D_A), lambda i: (i, 0)),
        compiler_params=pltpu.CompilerParams(dimension_semantics=("arbitrary",), vmem_limit_bytes=VMEM_LIMIT),
        name="attention",
    )(qa, ka, va, ka, ka, ka, va, va, va, ck, cv, sink)


def _tri(reverse, strict=False):
    i, j = _iota((CHUNK, CHUNK), 0), _iota((CHUNK, CHUNK), 1)
    if strict:
        return (i < j) if reverse else (i > j)
    return (i <= j) if reverse else (i >= j)


def _gdn_prep_kernel(cur_ref, prev_ref, next_ref, ba_ref, convw_ref, alog_ref, dtb_ref, qkv_ref, bl_ref,
                     *, ctx_tiles, ctx_tiles_per_seq, lat_tiles_per_seq):
    i = pl.program_id(0)
    in_ctx = i < ctx_tiles
    t_ctx = i % ctx_tiles_per_seq
    t_lat = (i - ctx_tiles) % lat_tiles_per_seq
    is_first = jnp.where(in_ctx, t_ctx == 0, t_lat == 0)
    is_last = jnp.where(in_ctx, t_ctx == ctx_tiles_per_seq - 1, t_lat == lat_tiles_per_seq - 1)
    has_prev = jnp.where(is_first, 0.0, 1.0)
    has_next = jnp.where(is_last, 0.0, 1.0)
    xh = jnp.concatenate([prev_ref[...] * has_prev, cur_ref[...], next_ref[...] * has_next], axis=0)
    cw = convw_ref[...]
    rows = cur_ref.shape[0]
    y = None
    for t in range(CONV_W):
        lo = HALO - CONV_W // 2 + t
        term = xh[lo:lo + rows] * cw[t:t + 1]
        y = term if y is None else y + term
    y = _silu(y)
    q, k, v = y[:, :D_REC], y[:, D_REC:2 * D_REC], y[:, 2 * D_REC:]
    ones = _head_ones()
    qq_hi, qq_lo = _split2(q * q)
    kk_hi, kk_lo = _split2(k * k)
    sums = jnp.dot(jnp.concatenate([qq_hi, qq_lo, kk_hi, kk_lo], axis=0), ones, preferred_element_type=F32)
    q_ss = sums[0:rows] + sums[rows:2 * rows]
    k_ss = sums[2 * rows:3 * rows] + sums[3 * rows:]
    qkv_ref[:, :D_REC] = q * lax.rsqrt(q_ss + 1e-6) * (HEAD_DIM ** -0.5)
    qkv_ref[:, D_REC:2 * D_REC] = k * lax.rsqrt(k_ss + 1e-6)
    qkv_ref[:, 2 * D_REC:] = v

    ba = ba_ref[...]
    za = ba + dtb_ref[...]
    softplus = jnp.maximum(za, 0.0) + jnp.log(1.0 + jnp.exp(-jnp.abs(za)))
    la = -jnp.exp(alog_ref[...]) * softplus
    bl_ref[...] = jnp.where(_iota(ba.shape, 1) < 2 * N_HEADS_REC, _sigmoid(ba), la)


def _gdn_prep(qkvb, ba, conv_w, alog_row, dtb_row, ctx_tiles, ctx_tiles_per_seq, lat_tiles_per_seq):
    n = qkvb.shape[0]
    per = TS // HALO
    const = lambda shape: pl.BlockSpec(shape, lambda i: (0,) * len(shape))
    return pl.pallas_call(
        functools.partial(_gdn_prep_kernel, ctx_tiles=ctx_tiles, ctx_tiles_per_seq=ctx_tiles_per_seq,
                          lat_tiles_per_seq=lat_tiles_per_seq),
        out_shape=[jax.ShapeDtypeStruct((n, 3 * D_REC), F32), jax.ShapeDtypeStruct((n, 128), F32)],
        grid=(n // TS,),
        in_specs=[pl.BlockSpec((TS, 3 * D_REC), lambda i: (i, 0)),
                  pl.BlockSpec((HALO, 3 * D_REC), lambda i: (jnp.maximum(i * per - 1, 0), 0)),
                  pl.BlockSpec((HALO, 3 * D_REC), lambda i: (jnp.minimum((i + 1) * per, n // HALO - 1), 0)),
                  pl.BlockSpec((TS, 128), lambda i: (i, 0)),
                  const((CONV_W, 3 * D_REC)), const((1, 128)), const((1, 128))],
        out_specs=[pl.BlockSpec((TS, 3 * D_REC), lambda i: (i, 0)), pl.BlockSpec((TS, 128), lambda i: (i, 0))],
        compiler_params=pltpu.CompilerParams(dimension_semantics=("arbitrary",), vmem_limit_bytes=VMEM_LIMIT),
        name="gdn_prep",
    )(qkvb, qkvb, qkvb, ba, conv_w, alog_row, dtb_row)


def _scan_schedule(n_ctx, ctx_len, lat_batch, lat_len, reverse):
    tiles_ctx, tiles_lat = ctx_len // TS, lat_len // TS
    seqs = [[s * tiles_ctx + t for t in range(tiles_ctx)] for s in range(n_ctx)]
    seqs += [[n_ctx * tiles_ctx + b * tiles_lat + t for t in range(tiles_lat)] for b in range(lat_batch)]
    cols = []
    for s, tiles in enumerate(seqs):
        order = tiles[::-1] if reverse else tiles
        for n, t in enumerate(order):
            cols.append((t, s, int(n == 0), int(n == len(order) - 1)))
    return jnp.asarray(np.array(cols, np.int32).T)


def _gdn_pipe_kernel(sch_ref, qkv_ref, bl_ref, s0_ref, o_ref, sfin_ref, s_scr, *sets, reverse, direction, n_steps):
    n = pl.program_id(0)
    n_groups = N_HEADS_REC // 2
    n_chunks = TS // CHUNK
    n_inst = n_chunks * n_groups
    prev = jnp.maximum(n - 1, 0)
    chain_live = n >= 1
    half = len(sets) // 2

    @pl.when(n == 0)
    def _():
        for ref in sets:
            ref[...] = jnp.zeros(ref.shape, ref.dtype)
        s_scr[...] = jnp.zeros(s_scr.shape, F32)

    @pl.when(jnp.logical_and(chain_live, sch_ref[2, prev] == 1))
    def _():
        for g in range(n_groups):
            s_scr[g] = _bd2(s0_ref[0, :, g * 128:(g + 1) * 128])

    row, lane = _iota((CHUNK, 128), 0), _iota((CHUNK, 128), 1) % HEAD_DIM
    if reverse:
        incl, strict = row <= lane, row < lane
    else:
        incl, strict = row >= lane, row > lane
    eye = jnp.where(row == lane, 1.0, 0.0)
    tri = jnp.where(_tri(reverse), 1.0, 0.0).astype(BF16)
    tri_t2 = jnp.where((row >= lane) if reverse else (row <= lane), 1.0, 0.0).astype(BF16)
    bd_mask = (_iota((128, 128), 0) // HEAD_DIM) == (_iota((128, 128), 1) // HEAD_DIM)
    last = 0 if reverse else CHUNK - 1
    chunk_order = list(range(n_chunks - 1, -1, -1) if reverse else range(n_chunks))
    cb, ca = direction * N_HEADS_REC, 2 * N_HEADS_REC + direction * N_HEADS_REC

    def step(wr, rd):
        u_w, wq_w, at_w, kd_w, et_w = wr
        u_r, wq_r, at_r, kd_r, et_r = rd
        states = [s_scr[g] for g in range(n_groups)]
        inst = []

        def chain(m):
            c, g = chunk_order[m // n_groups], m % n_groups
            rs, ls = slice(c * CHUNK, (c + 1) * CHUNK), slice(g * 128, (g + 1) * 128)
            s = states[g]
            ws_qs = _dot(wq_r[m], s)
            v_new = u_r[m] - ws_qs[:CHUNK]
            o_ref[rs, ls] = ws_qs[CHUNK:] + _dot(at_r[m], _bd2(v_new.astype(BF16)))
            states[g] = s * et_r[m][0:1] + jnp.where(bd_mask, _dot(kd_r[m], v_new, TN), 0.0)

        for c in chunk_order:
            rs = slice(c * CHUNK, (c + 1) * CHUNK)
            bl = bl_ref[rs, :]
            g16, gt2 = None, None
            for part in _split3(bl):
                a = jnp.dot(tri, part, preferred_element_type=F32)
                b = lax.dot_general(part, tri_t2, TN, preferred_element_type=F32)
                g16 = a if g16 is None else g16 + a
                gt2 = b if gt2 is None else gt2 + b
            for g in range(n_groups):
                h0, h1 = 2 * g, 2 * g + 1
                ls = slice(g * 128, (g + 1) * 128)
                d = dict(q=qkv_ref[rs, ls], k=qkv_ref[rs, D_REC + g * 128:D_REC + (g + 1) * 128],
                         v=qkv_ref[rs, 2 * D_REC + g * 128:2 * D_REC + (g + 1) * 128])
                d["beta"] = _pick2(bl[:, cb + h0:cb + h0 + 1], bl[:, cb + h1:cb + h1 + 1])
                d["g_i"] = _pick2(g16[:, ca + h0:ca + h0 + 1], g16[:, ca + h1:ca + h1 + 1])
                g_j = _pick2(gt2[ca + h0:ca + h0 + 1, :], gt2[ca + h1:ca + h1 + 1, :])
                d["g_tot"] = d["g_i"][last:last + 1]
                d["decay"] = jnp.where(incl, jnp.exp(jnp.where(incl, d["g_i"] - g_j, 0.0)), 0.0)
                inst.append(d)
        chain(0)
        chain(1)

        for d in inst:
            gram = _dot(jnp.concatenate([d["k"], d["q"]], axis=0), _bd2(d["k"].astype(BF16)), NT)
            d["low"] = jnp.where(strict, d["beta"] * gram[:CHUNK] * d["decay"], 0.0)
            d["attn"] = gram[CHUNK:] * d["decay"]

        for d in inst:
            d["t"] = eye - d["low"]
            (d["p"],) = _mm3_stack([d["low"]], d["low"])
        chain(2)
        chain(3)
        for it in range(4):
            for d in inst:
                p_next, tp = _mm3_stack([d["p"], d["t"]], d["p"])
                d["t"] = d["t"] + tp
                d["p"] = p_next
            if it < (n_inst - 4) // 2:
                chain(4 + 2 * it)
                chain(5 + 2 * it)
        for m, d in enumerate(inst):
            t_mat = d["t"] + _mm3_stack([d["t"]], d["p"])[0]
            eg = jnp.exp(d["g_i"])
            vb = d["v"] * d["beta"]
            kbg = d["k"] * d["beta"] * eg
            u_w[m] = _dot(t_mat, _bd2(vb.astype(BF16)))
            wq_w[m] = jnp.concatenate([_dot(t_mat, _bd2(kbg.astype(BF16))), d["q"] * eg], axis=0).astype(BF16)
            at_w[m] = d["attn"]
            kd_w[m] = (d["k"] * jnp.exp(d["g_tot"] - d["g_i"])).astype(BF16)
            et_w[m] = jnp.broadcast_to(jnp.exp(d["g_tot"]), (8, 128))

        for g in range(n_groups):
            s_scr[g] = states[g]

    @pl.when(n % 2 == 0)
    def _():
        step(sets[:half], sets[half:])

    @pl.when(n % 2 == 1)
    def _():
        step(sets[half:], sets[:half])

    @pl.when(jnp.logical_and(chain_live, sch_ref[3, prev] == 1))
    def _():
        for g in range(n_groups):
            sfin_ref[0, :, g * 128:(g + 1) * 128] = s_scr[g, :HEAD_DIM, :] + s_scr[g, HEAD_DIM:, :]


def _mm3_heads(lhs_list, rhs):
    r_hi, r_lo = _split2(rhs)
    w_hi, w_lo = _block_diag(r_hi), _block_diag(r_lo)
    parts = [_split2(a) for a in lhs_list]
    his = [p[0] for p in parts]
    los = [p[1] for p in parts]
    n = len(lhs_list) * CHUNK
    big = jnp.dot(jnp.concatenate(his + los, axis=0), w_hi, preferred_element_type=F32)
    small = jnp.dot(his[0] if len(his) == 1 else jnp.concatenate(his, axis=0), w_lo, preferred_element_type=F32)
    out = big[:n] + big[n:] + small
    return [out[m * CHUNK:(m + 1) * CHUNK] for m in range(len(lhs_list))]


def _pick_heads(vals):
    shape = (CHUNK, D_REC)
    head = _iota(shape, 1) // HEAD_DIM
    out = jnp.broadcast_to(vals[-1], shape)
    for h in range(N_HEADS_REC - 2, -1, -1):
        out = jnp.where(head == h, jnp.broadcast_to(vals[h], shape), out)
    return out


def _gdn_scan_kernel(sch_ref, qkv_ref, bl_ref, s0_ref, o_ref, sfin_ref, s_scr, *, reverse, direction):
    n = pl.program_id(0)

    @pl.when(sch_ref[2, n] == 1)
    def _():
        s_scr[...] = _block_diag(s0_ref[0])

    row, lane = _iota((CHUNK, D_REC), 0), _iota((CHUNK, D_REC), 1) % HEAD_DIM
    if reverse:
        incl, strict = row <= lane, row < lane
    else:
        incl, strict = row >= lane, row > lane
    eye = jnp.where(row == lane, 1.0, 0.0)
    tri = jnp.where(_tri(reverse), 1.0, 0.0).astype(BF16)
    tri_t = jnp.where((row >= lane) if reverse else (row <= lane), 1.0, 0.0).astype(BF16)
    bd_mask = _block_diag_mask()
    last = 0 if reverse else CHUNK - 1
    n_chunks = TS // CHUNK
    cb, ca = direction * N_HEADS_REC, 2 * N_HEADS_REC + direction * N_HEADS_REC

    inst = []
    for c in (range(n_chunks - 1, -1, -1) if reverse else range(n_chunks)):
        rs = slice(c * CHUNK, (c + 1) * CHUNK)
        bl = bl_ref[rs, :]
        g16, gt = None, None
        for part in _split3(bl):
            a = jnp.dot(tri, part, preferred_element_type=F32)
            b = lax.dot_general(part, tri_t, TN, preferred_element_type=F32)
            g16 = a if g16 is None else g16 + a
            gt = b if gt is None else gt + b
        d = dict(rs=rs, q=qkv_ref[rs, :D_REC], k=qkv_ref[rs, D_REC:2 * D_REC], v=qkv_ref[rs, 2 * D_REC:])
        d["beta"] = _pick_heads([bl[:, cb + h:cb + h + 1] for h in range(N_HEADS_REC)])
        d["g_i"] = _pick_heads([g16[:, ca + h:ca + h + 1] for h in range(N_HEADS_REC)])
        g_j = _pick_heads([gt[ca + h:ca + h + 1, :] for h in range(N_HEADS_REC)])
        d["g_tot"] = d["g_i"][last:last + 1]
        d["decay"] = jnp.where(incl, jnp.exp(jnp.where(incl, d["g_i"] - g_j, 0.0)), 0.0)
        inst.append(d)

    for d in inst:
        gram = _dot(jnp.concatenate([d["k"], d["q"]], axis=0), _block_diag(d["k"].astype(BF16)), NT)
        d["low"] = jnp.where(strict, d["beta"] * gram[:CHUNK] * d["decay"], 0.0)
        d["attn"] = gram[CHUNK:] * d["decay"]

    for d in inst:
        d["t"] = eye - d["low"]
        (d["p"],) = _mm3_heads([d["low"]], d["low"])
    for _ in range(4):
        for d in inst:
            p_next, tp = _mm3_heads([d["p"], d["t"]], d["p"])
            d["t"] = d["t"] + tp
            d["p"] = p_next
    for d in inst:
        t_mat = d["t"] + _mm3_heads([d["t"]], d["p"])[0]
        eg = jnp.exp(d["g_i"])
        vb = d["v"] * d["beta"]
        kbg = d["k"] * d["beta"] * eg
        d["u"] = _dot(t_mat, _block_diag(vb.astype(BF16)))
        d["w_qg"] = jnp.concatenate([_dot(t_mat, _block_diag(kbg.astype(BF16))), d["q"] * eg], axis=0).astype(BF16)
        d["k_dec"] = (d["k"] * jnp.exp(d["g_tot"] - d["g_i"])).astype(BF16)
        d["e_tot"] = jnp.exp(d["g_tot"])

    s = s_scr[...]
    for d in inst:
        ws_qs = _dot(d["w_qg"], s)
        v_new = d["u"] - ws_qs[:CHUNK]
        o_ref[d["rs"], :] = ws_qs[CHUNK:] + _dot(d["attn"], _block_diag(v_new.astype(BF16)))
        s = s * d["e_tot"] + jnp.where(bd_mask, _dot(d["k_dec"], v_new, TN), 0.0)
    s_scr[...] = s

    @pl.when(sch_ref[3, n] == 1)
    def _():
        sfin_ref[0] = _collapse_block_diag(s)


def _gdn_pipe4_kernel(sch_ref, qkv_ref, bl_ref, s0_ref, o_ref, sfin_ref, s_scr, *sets, reverse, direction):
    n = pl.program_id(0)
    n_chunks = TS // CHUNK
    prev = jnp.maximum(n - 1, 0)
    chain_live = n >= 1
    half = len(sets) // 2

    @pl.when(n == 0)
    def _():
        for ref in sets:
            ref[...] = jnp.zeros(ref.shape, ref.dtype)
        s_scr[...] = jnp.zeros(s_scr.shape, F32)

    @pl.when(jnp.logical_and(chain_live, sch_ref[2, prev] == 1))
    def _():
        s_scr[...] = _block_diag(s0_ref[0])

    row, lane = _iota((CHUNK, D_REC), 0), _iota((CHUNK, D_REC), 1) % HEAD_DIM
    if reverse:
        incl, strict = row <= lane, row < lane
    else:
        incl, strict = row >= lane, row > lane
    eye = jnp.where(row == lane, 1.0, 0.0)
    tri = jnp.where(_tri(reverse), 1.0, 0.0).astype(BF16)
    t_before_j = (row >= lane) if reverse else (row <= lane)
    ones8 = jnp.ones((8, CHUNK), BF16)
    bd_mask = _block_diag_mask()
    last = 0 if reverse else CHUNK - 1
    chunk_order = list(range(n_chunks - 1, -1, -1) if reverse else range(n_chunks))
    cb, ca = direction * N_HEADS_REC, 2 * N_HEADS_REC + direction * N_HEADS_REC

    def step(wr, rd):
        u_w, wq_w, at_w, kd_w, et_w = wr
        u_r, wq_r, at_r, kd_r, et_r = rd
        state = [s_scr[...]]
        inst = []

        def chain(m):
            c = chunk_order[m]
            rs = slice(c * CHUNK, (c + 1) * CHUNK)
            s = state[0]
            ws_qs = _dot(wq_r[m], s)
            v_new = u_r[m] - ws_qs[:CHUNK]
            o_ref[rs, :] = ws_qs[CHUNK:] + _dot(at_r[m], _block_diag(v_new.astype(BF16)))
            state[0] = s * et_r[m][0:1] + jnp.where(bd_mask, _dot(kd_r[m], v_new, TN), 0.0)

        for c in chunk_order:
            rs = slice(c * CHUNK, (c + 1) * CHUNK)
            bl = bl_ref[rs, :]
            d = dict(q=qkv_ref[rs, :D_REC], k=qkv_ref[rs, D_REC:2 * D_REC], v=qkv_ref[rs, 2 * D_REC:])
            d["beta"] = _pick_heads([bl[:, cb + h:cb + h + 1] for h in range(N_HEADS_REC)])
            la = _pick_heads([bl[:, ca + h:ca + h + 1] for h in range(N_HEADS_REC)])
            g_i, g_j = None, None
            for p_i, p_j in zip(_split3(la), _split3(jnp.where(t_before_j, la, 0.0))):
                a = jnp.dot(tri, p_i, preferred_element_type=F32)
                b = jnp.dot(ones8, p_j, preferred_element_type=F32)
                g_i = a if g_i is None else g_i + a
                g_j = b if g_j is None else g_j + b
            d["g_i"] = g_i
            d["g_tot"] = g_i[last:last + 1]
            d["decay"] = jnp.where(incl, jnp.exp(jnp.where(incl, g_i - g_j[0:1], 0.0)), 0.0)
            inst.append(d)
        chain(0)

        for d in inst:
            gram = _dot(jnp.concatenate([d["k"], d["q"]], axis=0), _block_diag(d["k"].astype(BF16)), NT)
            d["low"] = jnp.where(strict, d["beta"] * gram[:CHUNK] * d["decay"], 0.0)
            d["attn"] = gram[CHUNK:] * d["decay"]

        for d in inst:
            d["t"] = eye - d["low"]
            (d["p"],) = _mm3_heads([d["low"]], d["low"])
        for it in range(4):
            for d in inst:
                p_next, tp = _mm3_heads([d["p"], d["t"]], d["p"])
                d["t"] = d["t"] + tp
                d["p"] = p_next
            if 1 + it < n_chunks:
                chain(1 + it)
        for m, d in enumerate(inst):
            t_mat = d["t"] + _mm3_heads([d["t"]], d["p"])[0]
            eg = jnp.exp(d["g_i"])
            vb = d["v"] * d["beta"]
            kbg = d["k"] * d["beta"] * eg
            u_w[m] = _dot(t_mat, _block_diag(vb.astype(BF16)))
            wq_w[m] = jnp.concatenate([_dot(t_mat, _block_diag(kbg.astype(BF16))), d["q"] * eg], axis=0).astype(BF16)
            at_w[m] = d["attn"]
            kd_w[m] = (d["k"] * jnp.exp(d["g_tot"] - d["g_i"])).astype(BF16)
            et_w[m] = jnp.broadcast_to(jnp.exp(d["g_tot"]), (8, D_REC))

        s_scr[...] = state[0]

    @pl.when(n % 2 == 0)
    def _():
        step(sets[:half], sets[half:])

    @pl.when(n % 2 == 1)
    def _():
        step(sets[half:], sets[:half])

    @pl.when(jnp.logical_and(chain_live, sch_ref[3, prev] == 1))
    def _():
        sfin_ref[0] = _collapse_block_diag(s_scr[...])


def _gdn_bidir_kernel(sch_ref, qkvf_ref, blf_ref, qkvb_ref, blb_ref, s0f_ref, s0b_ref,
                      of_ref, sff_ref, ob_ref, sfb_ref, s_scr, *sets):
    n = pl.program_id(0)
    n_chunks = TS // CHUNK
    prev = jnp.maximum(n - 1, 0)
    chain_live = n >= 1
    half = len(sets) // 2
    lanes = ((False, qkvf_ref, blf_ref, s0f_ref, of_ref, sff_ref), (True, qkvb_ref, blb_ref, s0b_ref, ob_ref, sfb_ref))

    @pl.when(n == 0)
    def _():
        for ref in sets:
            ref[...] = jnp.zeros(ref.shape, ref.dtype)
        s_scr[...] = jnp.zeros(s_scr.shape, F32)

    for ln, lane_refs in enumerate(lanes):
        @pl.when(jnp.logical_and(chain_live, sch_ref[4 * ln + 2, prev] == 1))
        def _(ln=ln, s0_ref=lane_refs[3]):
            s_scr[ln] = _block_diag(s0_ref[0])

    row, lane = _iota((CHUNK, D_REC), 0), _iota((CHUNK, D_REC), 1) % HEAD_DIM
    eye = jnp.where(row == lane, 1.0, 0.0)
    ones8 = jnp.ones((8, CHUNK), BF16)
    bd_mask = _block_diag_mask()

    def step(wr, rd):
        u_w, wq_w, at_w, kd_w, et_w = wr
        u_r, wq_r, at_r, kd_r, et_r = rd
        state = [s_scr[0], s_scr[1]]
        inst = []

        def chain(m):
            for ln, (reverse, _, _, _, o_ref, _) in enumerate(lanes):
                c = (n_chunks - 1 - m) if reverse else m
                rs = slice(c * CHUNK, (c + 1) * CHUNK)
                k = ln * n_chunks + m
                s = state[ln]
                ws_qs = _dot(wq_r[k], s)
                v_new = u_r[k] - ws_qs[:CHUNK]
                o_ref[rs, :] = ws_qs[CHUNK:] + _dot(at_r[k], _block_diag(v_new.astype(BF16)))
                state[ln] = s * et_r[k][0:1] + jnp.where(bd_mask, _dot(kd_r[k], v_new, TN), 0.0)

        for ln, (reverse, qkv_ref, bl_ref, _, _, _) in enumerate(lanes):
            incl = (row <= lane) if reverse else (row >= lane)
            tri = jnp.where(_tri(reverse), 1.0, 0.0).astype(BF16)
            t_before_j = (row >= lane) if reverse else (row <= lane)
            last = 0 if reverse else CHUNK - 1
            cb, ca = ln * N_HEADS_REC, 2 * N_HEADS_REC + ln * N_HEADS_REC
            for m in range(n_chunks):
                c = (n_chunks - 1 - m) if reverse else m
                rs = slice(c * CHUNK, (c + 1) * CHUNK)
                bl = bl_ref[rs, :]
                d = dict(q=qkv_ref[rs, :D_REC], k=qkv_ref[rs, D_REC:2 * D_REC], v=qkv_ref[rs, 2 * D_REC:],
                         strict=(row < lane) if reverse else (row > lane), slot=ln * n_chunks + m)
                d["beta"] = _pick_heads([bl[:, cb + h:cb + h + 1] for h in range(N_HEADS_REC)])
                la = _pick_heads([bl[:, ca + h:ca + h + 1] for h in range(N_HEADS_REC)])
                g_i, g_j = None, None
                for p_i, p_j in zip(_split3(la), _split3(jnp.where(t_before_j, la, 0.0))):
                    a = jnp.dot(tri, p_i, preferred_element_type=F32)
                    b = jnp.dot(ones8, p_j, preferred_element_type=F32)
                    g_i = a if g_i is None else g_i + a
                    g_j = b if g_j is None else g_j + b
                d["g_i"] = g_i
                d["g_tot"] = g_i[last:last + 1]
                d["decay"] = jnp.where(incl, jnp.exp(jnp.where(incl, g_i - g_j[0:1], 0.0)), 0.0)
                inst.append(d)
        chain(0)

        for d in inst:
            gram = _dot(jnp.concatenate([d["k"], d["q"]], axis=0), _block_diag(d["k"].astype(BF16)), NT)
            d["low"] = jnp.where(d["strict"], d["beta"] * gram[:CHUNK] * d["decay"], 0.0)
            d["attn"] = gram[CHUNK:] * d["decay"]

        for d in inst:
            d["t"] = eye - d["low"]
            (d["p"],) = _mm3_heads([d["low"]], d["low"])
        for it in range(4):
            for d in inst:
                p_next, tp = _mm3_heads([d["p"], d["t"]], d["p"])
                d["t"] = d["t"] + tp
                d["p"] = p_next
            if 1 + it < n_chunks:
                chain(1 + it)
        for d in inst:
            k = d["slot"]
            t_mat = d["t"] + _mm3_heads([d["t"]], d["p"])[0]
            eg = jnp.exp(d["g_i"])
            vb = d["v"] * d["beta"]
            kbg = d["k"] * d["beta"] * eg
            u_w[k] = _dot(t_mat, _block_diag(vb.astype(BF16)))
            wq_w[k] = jnp.concatenate([_dot(t_mat, _block_diag(kbg.astype(BF16))), d["q"] * eg], axis=0).astype(BF16)
            at_w[k] = d["attn"]
            kd_w[k] = (d["k"] * jnp.exp(d["g_tot"] - d["g_i"])).astype(BF16)
            et_w[k] = jnp.broadcast_to(jnp.exp(d["g_tot"]), (8, D_REC))

        s_scr[0] = state[0]
        s_scr[1] = state[1]

    @pl.when(n % 2 == 0)
    def _():
        step(sets[:half], sets[half:])

    @pl.when(n % 2 == 1)
    def _():
        step(sets[half:], sets[:half])

    for ln, lane_refs in enumerate(lanes):
        @pl.when(jnp.logical_and(chain_live, sch_ref[4 * ln + 3, prev] == 1))
        def _(ln=ln, sfin_ref=lane_refs[5]):
            sfin_ref[0] = _collapse_block_diag(s_scr[ln])


def _gdn_scan_bidir(qkv, bl, s0_f, s0_b, sched):
    n = qkv.shape[0]
    n_seq = s0_f.shape[0]
    n_steps = sched.shape[1]
    n_inst = 2 * (TS // CHUNK)
    ahead = lambda i: jnp.minimum(i, n_steps - 1)
    behind = lambda i: jnp.maximum(i - 1, 0)
    scratch_set = [pltpu.VMEM((n_inst, CHUNK, D_REC), F32), pltpu.VMEM((n_inst, 2 * CHUNK, D_REC), BF16),
                   pltpu.VMEM((n_inst, CHUNK, D_REC), F32), pltpu.VMEM((n_inst, CHUNK, D_REC), BF16),
                   pltpu.VMEM((n_inst, 8, D_REC), F32)]

    def tile_in(width, ln):
        return pl.BlockSpec((TS, width), lambda i, sch: (sch[4 * ln, ahead(i)], 0))

    def tile_out(ln):
        return pl.BlockSpec((TS, D_REC), lambda i, sch: (sch[4 * ln, behind(i)], 0))

    def state_spec(ln):
        return pl.BlockSpec((1, HEAD_DIM, D_REC), lambda i, sch: (sch[4 * ln + 1, behind(i)], 0, 0))

    grid_spec = pltpu.PrefetchScalarGridSpec(
        num_scalar_prefetch=1,
        grid=(n_steps + 1,),
        in_specs=[tile_in(3 * D_REC, 0), tile_in(128, 0), tile_in(3 * D_REC, 1), tile_in(128, 1),
                  state_spec(0), state_spec(1)],
        out_specs=[tile_out(0), state_spec(0), tile_out(1), state_spec(1)],
        scratch_shapes=[pltpu.VMEM((2, D_REC, D_REC), F32)] + scratch_set + scratch_set,
    )
    o_sds = jax.ShapeDtypeStruct((n, D_REC), F32)
    s_sds = jax.ShapeDtypeStruct((n_seq, HEAD_DIM, D_REC), F32)
    return pl.pallas_call(
        _gdn_bidir_kernel,
        out_shape=[o_sds, s_sds, o_sds, s_sds],
        grid_spec=grid_spec,
        compiler_params=pltpu.CompilerParams(dimension_semantics=("arbitrary",), vmem_limit_bytes=VMEM_LIMIT),
        name="gdn_scan",
    )(sched, qkv, bl, qkv, bl, s0_f, s0_b)


def _gdn_scan(qkv, bl, s0, sched, reverse, direction):
    n = qkv.shape[0]
    n_seq = s0.shape[0]
    n_steps = sched.shape[1]
    n_inst = TS // CHUNK
    ahead = lambda i: jnp.minimum(i, n_steps - 1)
    behind = lambda i: jnp.maximum(i - 1, 0)
    scratch_set = [pltpu.VMEM((n_inst, CHUNK, D_REC), F32), pltpu.VMEM((n_inst, 2 * CHUNK, D_REC), BF16),
                   pltpu.VMEM((n_inst, CHUNK, D_REC), F32), pltpu.VMEM((n_inst, CHUNK, D_REC), BF16),
                   pltpu.VMEM((n_inst, 8, D_REC), F32)]
    grid_spec = pltpu.PrefetchScalarGridSpec(
        num_scalar_prefetch=1,
        grid=(n_steps + 1,),
        in_specs=[pl.BlockSpec((TS, 3 * D_REC), lambda i, sch: (sch[0, ahead(i)], 0)),
                  pl.BlockSpec((TS, 128), lambda i, sch: (sch[0, ahead(i)], 0)),
                  pl.BlockSpec((1, HEAD_DIM, D_REC), lambda i, sch: (sch[1, behind(i)], 0, 0))],
        out_specs=[pl.BlockSpec((TS, D_REC), lambda i, sch: (sch[0, behind(i)], 0)),
                   pl.BlockSpec((1, HEAD_DIM, D_REC), lambda i, sch: (sch[1, behind(i)], 0, 0))],
        scratch_shapes=[pltpu.VMEM((D_REC, D_REC), F32)] + scratch_set + scratch_set,
    )
    return pl.pallas_call(
        functools.partial(_gdn_pipe4_kernel, reverse=reverse, direction=direction),
        out_shape=[jax.ShapeDtypeStruct((n, D_REC), F32), jax.ShapeDtypeStruct((n_seq, HEAD_DIM, D_REC), F32)],
        grid_spec=grid_spec,
        compiler_params=pltpu.CompilerParams(dimension_semantics=("arbitrary",), vmem_limit_bytes=VMEM_LIMIT),
        name="gdn_scan_bwd" if reverse else "gdn_scan_fwd",
    )(sched, qkv, bl, s0)


def _pair_dots(lhs, w, dims):
    outs = []
    for p in range(N_HEADS_REC // 2):
        sl = slice(p * 128, (p + 1) * 128)
        outs.append(lax.dot_general(lhs[:, sl], _bd2(w[:, sl]), dims, preferred_element_type=F32))
    return jnp.concatenate(outs, axis=1)


def _hgrn_tile_kernel(sch_ref, f_ref, i_ref, q_ref, lb_ref, sel_ref, s0_ref, o_ref, sfin_ref, s_scr, *, reverse):
    n = pl.program_id(0)

    n_pairs = N_HEADS_REC // 2

    @pl.when(sch_ref[2, n] == 1)
    def _():
        for p in range(n_pairs):
            s_scr[p] = _bd2(s0_ref[0, :, p * 128:(p + 1) * 128])

    lb = lb_ref[...]
    tri = jnp.where(_tri(reverse), 1.0, 0.0).astype(BF16)
    row = _iota((CHUNK, D_REC), 0)
    col_tok = _iota((CHUNK, D_REC), 1) % HEAD_DIM
    ones = _head_ones()
    last = 0 if reverse else CHUNK - 1
    n_chunks = TS // CHUNK
    chunk_order = list(range(n_chunks - 1, -1, -1) if reverse else range(n_chunks))

    inst = []
    for c in chunk_order:
        rs = slice(c * CHUNK, (c + 1) * CHUNK)
        f = lb + (1.0 - lb) * _sigmoid(f_ref[rs, :])
        d = dict(rs=rs, kk=1.0 - f, q=q_ref[rs, :], v=i_ref[rs, :])
        d["b"] = _dot_sel(tri, jnp.log(jnp.maximum(f, TINY)))
        inst.append(d)

    sel = sel_ref[...]
    for d in inst:
        parts = _split3(d["b"])
        d["b_mid"] = sum(jnp.dot(sel, part, preferred_element_type=F32) for part in parts)

    for lvl in range(N_LEVELS):
        s_half = (CHUNK // 2) >> lvl
        in_block = row % (2 * s_half)
        later = (in_block < s_half) if reverse else (in_block >= s_half)
        same_block = (row // (2 * s_half)) == (col_tok // (2 * s_half))
        for d in inst:
            e = jnp.exp(-jnp.abs(d["b"] - d["b_mid"][lvl * CHUNK:(lvl + 1) * CHUNK]))
            q_hat = jnp.where(later, d["q"] * e, 0.0).astype(BF16)
            k_hat = jnp.where(later, 0.0, d["kk"] * e).astype(BF16)
            res = _pair_dots(q_hat, k_hat, NT)
            term = jnp.where(same_block, res, 0.0)
            d["a"] = term if lvl == 0 else d["a"] + term

    ones2 = ones[:128, :128]
    for d in inst:
        qk = (d["q"] * d["kk"]).astype(BF16)
        sums = jnp.concatenate([jnp.dot(qk[:, p * 128:(p + 1) * 128], ones2, preferred_element_type=F32)
                                for p in range(n_pairs)], axis=1)
        d["a"] = d["a"] + jnp.where(col_tok == row, sums, 0.0)

    bd2_mask = (_iota((128, 128), 0) // HEAD_DIM) == (_iota((128, 128), 1) // HEAD_DIM)
    for d in inst:
        b = d["b"]
        b_last = b[last:last + 1]
        d["av"] = _pair_dots(d["a"].astype(BF16), d["v"].astype(BF16), NN)
        v16, kd16 = d["v"].astype(BF16), (d["kk"] * jnp.exp(b_last - b)).astype(BF16)
        d["kv"] = [jnp.where(bd2_mask, lax.dot_general(v16[:, p * 128:(p + 1) * 128], kd16[:, p * 128:(p + 1) * 128],
                                                       TN, preferred_element_type=F32), 0.0) for p in range(n_pairs)]
        d["qb"] = (d["q"] * jnp.exp(b)).astype(BF16)
        d["e_last"] = jnp.exp(b_last)

    states = [s_scr[p] for p in range(n_pairs)]
    for d in inst:
        qs = [lax.dot_general(d["qb"][:, p * 128:(p + 1) * 128], states[p].astype(BF16), NT,
                              preferred_element_type=F32) for p in range(n_pairs)]
        o_ref[d["rs"], :] = d["av"] + jnp.concatenate(qs, axis=1)
        for p in range(n_pairs):
            states[p] = states[p] * d["e_last"][:, p * 128:(p + 1) * 128] + d["kv"][p]
    for p in range(n_pairs):
        s_scr[p] = states[p]

    @pl.when(sch_ref[3, n] == 1)
    def _():
        for p in range(n_pairs):
            sfin_ref[0, :, p * 128:(p + 1) * 128] = states[p][:HEAD_DIM] + states[p][HEAD_DIM:]


def _hgrn_scan_kernel(sch_ref, f_ref, i_ref, q_ref, lb_ref, sel_ref, s0_ref, o_ref, sfin_ref, s_scr, *, reverse):
    n = pl.program_id(0)

    @pl.when(sch_ref[2, n] == 1)
    def _():
        s_scr[...] = _block_diag(s0_ref[0])

    lb = lb_ref[...]
    sel = sel_ref[...]
    row = _iota((CHUNK, D_REC), 0)
    col_tok = _iota((CHUNK, D_REC), 1) % HEAD_DIM
    ones = _head_ones()
    bd_mask = _block_diag_mask()
    last = 0 if reverse else CHUNK - 1
    n_chunks = TS // CHUNK
    chunk_order = list(range(n_chunks - 1, -1, -1) if reverse else range(n_chunks))

    inst = []
    for c in chunk_order:
        rs = slice(c * CHUNK, (c + 1) * CHUNK)
        f = lb + (1.0 - lb) * _sigmoid(f_ref[rs, :])
        d = dict(rs=rs, kk=1.0 - f, q=q_ref[rs, :], v=i_ref[rs, :])
        cums = _dot_sel(sel, jnp.log(jnp.maximum(f, TINY)))
        d["b"] = cums[:CHUNK]
        d["b_mid_small"] = [cums[CHUNK:2 * CHUNK], cums[2 * CHUNK:]]
        inst.append(d)

    for lvl in range(N_LEVELS):
        s_half = (CHUNK // 2) >> lvl
        in_block = row % (2 * s_half)
        later = (in_block < s_half) if reverse else (in_block >= s_half)
        same_block = (row // (2 * s_half)) == (col_tok // (2 * s_half))
        for d in inst:
            b = d["b"]
            if s_half >= 4:
                pieces = []
                for blk in range(CHUNK // (2 * s_half)):
                    r = blk * 2 * s_half + (s_half if reverse else s_half - 1)
                    pieces.append(jnp.broadcast_to(b[r:r + 1], (2 * s_half, D_REC)))
                b_mid = pieces[0] if len(pieces) == 1 else jnp.concatenate(pieces, axis=0)
            else:
                b_mid = d["b_mid_small"][lvl - (N_LEVELS - 2)]
            e = jnp.exp(-jnp.abs(b - b_mid))
            q_hat = jnp.where(later, d["q"] * e, 0.0)
            k_hat = jnp.where(later, 0.0, d["kk"] * e)
            res = _dot(q_hat, _block_diag(k_hat.astype(BF16)), NT)
            term = jnp.where(same_block, res, 0.0)
            d["a"] = term if lvl == 0 else d["a"] + term

    for d in inst:
        sums = _dot(d["q"] * d["kk"], ones)
        d["a"] = d["a"] + jnp.where(col_tok == row, sums, 0.0)

    for d in inst:
        b = d["b"]
        b_last = b[last:last + 1]
        d["av"] = _dot(d["a"], _block_diag(d["v"].astype(BF16)))
        d["kv"] = jnp.where(bd_mask, _dot(d["v"], d["kk"] * jnp.exp(b_last - b), TN), 0.0)
        d["qb"] = (d["q"] * jnp.exp(b)).astype(BF16)
        d["e_last"] = jnp.exp(b_last)

    s = s_scr[...]
    for d in inst:
        o_ref[d["rs"], :] = d["av"] + _dot(d["qb"], s, NT)
        s = s * d["e_last"] + d["kv"]
    s_scr[...] = s

    @pl.when(sch_ref[3, n] == 1)
    def _():
        sfin_ref[0] = _collapse_block_diag(s)


def _hgrn_scan(hg, lb_e, s0, sched, reverse):
    n = hg.shape[0]
    n_seq = s0.shape[0]
    col = lambda c: pl.BlockSpec((TS, D_REC), lambda i, sch: (sch[0, i], c))
    i = np.arange(CHUNK)
    tri = (i[None, :] >= i[:, None]) if reverse else (i[None, :] <= i[:, None])
    sel = [tri]
    for s_half in (2, 1):
        mid = (i // (2 * s_half)) * (2 * s_half) + (s_half if reverse else s_half - 1)
        sel.append(tri[mid])
    sel = jnp.asarray(np.concatenate(sel, axis=0).astype(np.float32), BF16)
    grid_spec = pltpu.PrefetchScalarGridSpec(
        num_scalar_prefetch=1,
        grid=(sched.shape[1],),
        in_specs=[col(1 if reverse else 0), col(2), col(3),
                  pl.BlockSpec((1, D_REC), lambda i, sch: (0, 0)),
                  pl.BlockSpec((3 * CHUNK, CHUNK), lambda i, sch: (0, 0)),
                  pl.BlockSpec((1, HEAD_DIM, D_REC), lambda i, sch: (sch[1, i], 0, 0))],
        out_specs=[pl.BlockSpec((TS, D_REC), lambda i, sch: (sch[0, i], 0)),
                   pl.BlockSpec((1, HEAD_DIM, D_REC), lambda i, sch: (sch[1, i], 0, 0))],
        scratch_shapes=[pltpu.VMEM((D_REC, D_REC), F32)],
    )
    return pl.pallas_call(
        functools.partial(_hgrn_scan_kernel, reverse=reverse),
        out_shape=[jax.ShapeDtypeStruct((n, D_REC), F32), jax.ShapeDtypeStruct((n_seq, HEAD_DIM, D_REC), F32)],
        grid_spec=grid_spec,
        compiler_params=pltpu.CompilerParams(dimension_semantics=("arbitrary",), vmem_limit_bytes=VMEM_LIMIT),
        name="hgrn_scan_bwd" if reverse else "hgrn_scan_fwd",
    )(sched, hg, hg, hg, lb_e, sel, s0)


def _hgrn_bidir_kernel(sch_ref, ff_ref, if_ref, qf_ref, fb_ref, ib_ref, qb_ref, lb_ref, sel_ref, s0f_ref, s0b_ref,
                       of_ref, sff_ref, ob_ref, sfb_ref, s_scr):
    n = pl.program_id(0)
    n_chunks = TS // CHUNK
    lanes = ((False, ff_ref, if_ref, qf_ref, s0f_ref, of_ref, sff_ref), (True, fb_ref, ib_ref, qb_ref, s0b_ref, ob_ref, sfb_ref))

    for ln, lane_refs in enumerate(lanes):
        @pl.when(sch_ref[4 * ln + 2, n] == 1)
        def _(ln=ln, s0_ref=lane_refs[4]):
            s_scr[ln] = _block_diag(s0_ref[0])

    row = _iota((CHUNK, D_REC), 0)
    col_tok = _iota((CHUNK, D_REC), 1) % HEAD_DIM
    ones = _head_ones()
    bd_mask = _block_diag_mask()

    inst = []
    for ln, (reverse, f_ref, i_ref, q_ref, _, o_ref, _) in enumerate(lanes):
        lb = lb_ref[ln:ln + 1, :]
        sel = sel_ref[ln]
        for m in range(n_chunks):
            c = (n_chunks - 1 - m) if reverse else m
            rs = slice(c * CHUNK, (c + 1) * CHUNK)
            f = lb + (1.0 - lb) * _sigmoid(f_ref[rs, :])
            d = dict(rs=rs, kk=1.0 - f, q=q_ref[rs, :], v=i_ref[rs, :], reverse=reverse, ln=ln, o_ref=o_ref)
            cums = _dot_sel(sel, jnp.log(jnp.maximum(f, TINY)))
            d["b"] = cums[:CHUNK]
            d["b_mid_small"] = [cums[CHUNK:2 * CHUNK], cums[2 * CHUNK:]]
            inst.append(d)

    for lvl in range(N_LEVELS):
        s_half = (CHUNK // 2) >> lvl
        in_block = row % (2 * s_half)
        same_block = (row // (2 * s_half)) == (col_tok // (2 * s_half))
        for d in inst:
            b, reverse = d["b"], d["reverse"]
            later = (in_block < s_half) if reverse else (in_block >= s_half)
            if s_half >= 4:
                pieces = []
                for blk in range(CHUNK // (2 * s_half)):
                    r = blk * 2 * s_half + (s_half if reverse else s_half - 1)
                    pieces.append(jnp.broadcast_to(b[r:r + 1], (2 * s_half, D_REC)))
                b_mid = pieces[0] if len(pieces) == 1 else jnp.concatenate(pieces, axis=0)
            else:
                b_mid = d["b_mid_small"][lvl - (N_LEVELS - 2)]
            e = jnp.exp(-jnp.abs(b - b_mid))
            q_hat = jnp.where(later, d["q"] * e, 0.0)
            k_hat = jnp.where(later, 0.0, d["kk"] * e)
            res = _dot(q_hat, _block_diag(k_hat.astype(BF16)), NT)
            term = jnp.where(same_block, res, 0.0)
            d["a"] = term if lvl == 0 else d["a"] + term

    for d in inst:
        sums = _dot(d["q"] * d["kk"], ones)
        d["a"] = d["a"] + jnp.where(col_tok == row, sums, 0.0)

    for d in inst:
        b = d["b"]
        last = 0 if d["reverse"] else CHUNK - 1
        b_last = b[last:last + 1]
        d["av"] = _dot(d["a"], _block_diag(d["v"].astype(BF16)))
        d["kv"] = jnp.where(bd_mask, _dot(d["v"], d["kk"] * jnp.exp(b_last - b), TN), 0.0)
        d["qb"] = (d["q"] * jnp.exp(b)).astype(BF16)
        d["e_last"] = jnp.exp(b_last)

    state = [s_scr[0], s_scr[1]]
    for d in inst:
        s = state[d["ln"]]
        d["o_ref"][d["rs"], :] = d["av"] + _dot(d["qb"], s, NT)
        state[d["ln"]] = s * d["e_last"] + d["kv"]
    s_scr[0] = state[0]
    s_scr[1] = state[1]

    for ln, lane_refs in enumerate(lanes):
        @pl.when(sch_ref[4 * ln + 3, n] == 1)
        def _(ln=ln, sfin_ref=lane_refs[6]):
            sfin_ref[0] = _collapse_block_diag(s_scr[ln])


def _hgrn_scan_bidir(hg, lb_e, s0_f, s0_b, sched):
    n = hg.shape[0]
    n_seq = s0_f.shape[0]
    i = np.arange(CHUNK)
    sels = []
    for reverse in (False, True):
        tri = (i[None, :] >= i[:, None]) if reverse else (i[None, :] <= i[:, None])
        rows = [tri]
        for s_half in (2, 1):
            mid = (i // (2 * s_half)) * (2 * s_half) + (s_half if reverse else s_half - 1)
            rows.append(tri[mid])
        sels.append(np.concatenate(rows, axis=0))
    sel = jnp.asarray(np.stack(sels).astype(np.float32), BF16)

    def col(c, ln):
        return pl.BlockSpec((TS, D_REC), lambda i, sch: (sch[4 * ln, i], c))

    def state_spec(ln):
        return pl.BlockSpec((1, HEAD_DIM, D_REC), lambda i, sch: (sch[4 * ln + 1, i], 0, 0))

    def tile_out(ln):
        return pl.BlockSpec((TS, D_REC), lambda i, sch: (sch[4 * ln, i], 0))

    grid_spec = pltpu.PrefetchScalarGridSpec(
        num_scalar_prefetch=1,
        grid=(sched.shape[1],),
        in_specs=[col(0, 0), col(2, 0), col(3, 0), col(1, 1), col(2, 1), col(3, 1),
                  pl.BlockSpec((2, D_REC), lambda i, sch: (0, 0)),
                  pl.BlockSpec((2, 3 * CHUNK, CHUNK), lambda i, sch: (0, 0, 0)),
                  state_spec(0), state_spec(1)],
        out_specs=[tile_out(0), state_spec(0), tile_out(1), state_spec(1)],
        scratch_shapes=[pltpu.VMEM((2, D_REC, D_REC), F32)],
    )
    o_sds = jax.ShapeDtypeStruct((n, D_REC), F32)
    s_sds = jax.ShapeDtypeStruct((n_seq, HEAD_DIM, D_REC), F32)
    return pl.pallas_call(
        _hgrn_bidir_kernel,
        out_shape=[o_sds, s_sds, o_sds, s_sds],
        grid_spec=grid_spec,
        compiler_params=pltpu.CompilerParams(dimension_semantics=("arbitrary",), vmem_limit_bytes=VMEM_LIMIT),
        name="hgrn_scan",
    )(sched, hg, hg, hg, hg, hg, hg, lb_e, sel, s0_f, s0_b)


def _gated_readout(o, gate, w_e, ones):
    ms = _dot_sel(ones, o * o, sel_is_lhs=False) * (1.0 / HEAD_DIM)
    return o * lax.rsqrt(ms + NORM_EPS) * w_e * _silu(gate)


def _out_proj_kernel(*refs, alpha, n_ctx_tiles, n_x):
    x_refs = refs[:n_x]
    (mod_ref, oa_ref, bf_ref, bb_ref, cf_ref, cb_ref, gb_ref, gc_ref, nwb_ref, nwc_ref,
     w_ref, lng_ref, lnb_ref, y_ref) = refs[n_x:]
    x = _read_tokens(x_refs, n_ctx_tiles)
    m = mod_ref[0]
    ones = _head_ones()
    o_b = _gated_readout(bf_ref[...] + bb_ref[...], gb_ref[...], nwb_ref[...], ones)
    o_c = _gated_readout(cf_ref[...] + cb_ref[...], gc_ref[...], nwc_ref[...], ones)
    mix = jnp.dot(oa_ref[...].astype(BF16), w_ref[0:D_A], preferred_element_type=F32)
    mix = mix + jnp.dot(o_b.astype(BF16), w_ref[D_A:D_A + D_REC], preferred_element_type=F32)
    mix = mix + jnp.dot(o_c.astype(BF16), w_ref[D_A + D_REC:], preferred_element_type=F32)
    y_ref[...] = _layer_norm(alpha * x + m[2:3] * mix, lng_ref[...], lnb_ref[...])


def _out_proj(x_parts, mod_l, o_a, ob_f, ob_b, oc_f, oc_b, g_b, hg, nw_b, nw_c, w_out, ln_g, ln_b, mod_index, alpha,
              n_ctx_tiles):
    n = sum(p.shape[0] for p in x_parts)
    tile = lambda w: pl.BlockSpec((TM, w), lambda i: (i, 0))
    row = lambda w: pl.BlockSpec((1, w), lambda i: (0, 0))
    return pl.pallas_call(
        functools.partial(_out_proj_kernel, alpha=alpha, n_ctx_tiles=n_ctx_tiles, n_x=len(x_parts)),
        out_shape=jax.ShapeDtypeStruct((n, D_MODEL), F32),
        grid=(n // TM,),
        in_specs=_token_specs(x_parts, n_ctx_tiles) + [
                  pl.BlockSpec((1, 6, D_MODEL), lambda i: (mod_index(i), 0, 0)),
                  tile(D_A), tile(D_REC), tile(D_REC), tile(D_REC), tile(D_REC), tile(D_REC),
                  pl.BlockSpec((TM, D_REC), lambda i: (i, 4)),
                  row(D_REC), row(D_REC),
                  pl.BlockSpec((D_MODEL, D_MODEL), lambda i: (0, 0)),
                  row(D_MODEL), row(D_MODEL)],
        out_specs=tile(D_MODEL),
        compiler_params=pltpu.CompilerParams(dimension_semantics=("arbitrary",), vmem_limit_bytes=VMEM_LIMIT),
        name="out_proj_ln",
    )(*x_parts, mod_l, o_a, ob_f, ob_b, oc_f, oc_b, g_b, hg, nw_b, nw_c, w_out, ln_g, ln_b)


def _ffn_kernel(x_ref, mod_ref, w1_ref, w3_ref, w2_ref, lng_ref, lnb_ref, y_ref, *, alpha):
    m = mod_ref[0]
    x = x_ref[...]
    h = (x * (1.0 + m[4:5]) + m[3:4]).astype(BF16)
    acc = jnp.zeros((x.shape[0], D_MODEL), F32)
    for c in range(D_FF // FF_CHUNK):
        sl = slice(c * FF_CHUNK, (c + 1) * FF_CHUNK)
        a = jnp.dot(h, w1_ref[:, sl], preferred_element_type=F32)
        g = jnp.dot(h, w3_ref[:, sl], preferred_element_type=F32)
        acc = acc + jnp.dot((_silu(a) * g).astype(BF16), w2_ref[sl, :], preferred_element_type=F32)
    y_ref[...] = _layer_norm(alpha * x + m[5:6] * acc, lng_ref[...], lnb_ref[...])


def _ffn(x, mod_l, w1, w3, w2, ln_g, ln_b, mod_index_ffn, alpha):
    n = x.shape[0]
    whole = lambda a: pl.BlockSpec(a.shape, lambda i: (0, 0), pipeline_mode=pl.Buffered(1))
    row = pl.BlockSpec((1, D_MODEL), lambda i: (0, 0))
    return pl.pallas_call(
        functools.partial(_ffn_kernel, alpha=alpha),
        out_shape=jax.ShapeDtypeStruct((n, D_MODEL), F32),
        grid=(n // TM_FFN,),
        in_specs=[pl.BlockSpec((TM_FFN, D_MODEL), lambda i: (i, 0)),
                  pl.BlockSpec((1, 6, D_MODEL), lambda i: (mod_index_ffn(i), 0, 0)),
                  whole(w1), whole(w3), whole(w2), row, row],
        out_specs=pl.BlockSpec((TM_FFN, D_MODEL), lambda i: (i, 0)),
        compiler_params=pltpu.CompilerParams(dimension_semantics=("arbitrary",), vmem_limit_bytes=VMEM_LIMIT),
        name="ffn_ln",
    )(x, mod_l, w1, w3, w2, ln_g, ln_b)


def _moe_kernel(x_ref, mod_ref, wr_ref, br_ref, w1_ref, w3_ref, w2_ref, lng_ref, lnb_ref, y_ref,
                h_scr, comb_scr, acc_scr, *, alpha):
    e = pl.program_id(1)
    m = mod_ref[0]

    @pl.when(e == 0)
    def _():
        h = x_ref[...] * (1.0 + m[4:5]) + m[3:4]
        h_scr[...] = h.astype(BF16)
        logits = jnp.dot(h, wr_ref[...], preferred_element_type=F32, precision=lax.Precision.HIGHEST) + br_ref[...]
        lane = _iota(logits.shape, 1)
        m1 = jnp.max(logits, -1, keepdims=True)
        i1 = jnp.min(jnp.where(logits == m1, lane, 128), -1, keepdims=True)
        rest = jnp.where(lane == i1, NEG_BIG, logits)
        m2 = jnp.max(rest, -1, keepdims=True)
        i2 = jnp.min(jnp.where(rest == m2, lane, 128), -1, keepdims=True)
        e2 = jnp.exp(m2 - m1)
        g1 = 1.0 / (1.0 + e2)
        comb_scr[...] = jnp.where(lane == i1, g1, 0.0) + jnp.where(lane == i2, e2 * g1, 0.0)
        acc_scr[...] = jnp.zeros_like(acc_scr)

    h = h_scr[...]
    a = jnp.dot(h, w1_ref[0], preferred_element_type=F32)
    g = jnp.dot(h, w3_ref[0], preferred_element_type=F32)
    y = jnp.dot((_silu(a) * g).astype(BF16), w2_ref[0], preferred_element_type=F32)
    comb = comb_scr[...]
    gate = jnp.sum(jnp.where(_iota(comb.shape, 1) == e, comb, 0.0), -1, keepdims=True)
    acc_scr[...] += gate * y

    @pl.when(e == N_EXPERTS - 1)
    def _():
        y_ref[...] = _layer_norm(alpha * x_ref[...] + m[5:6] * acc_scr[...], lng_ref[...], lnb_ref[...])


def _moe_routed_kernel(x_ref, mod_ref, wr_ref, br_ref, w1_ref, w3_ref, w2_ref, lng_ref, lnb_ref, *rest,
                       alpha, split_ctx_tiles):
    if split_ctx_tiles is None:
        y_ref, h_scr, mem_scr, gate_scr, rank_scr, acc_scr, earlier_scr = rest
    else:
        y_ref, y2_ref, h_scr, mem_scr, gate_scr, rank_scr, acc_scr, earlier_scr = rest
    _moe_routed_body(x_ref, mod_ref, wr_ref, br_ref, w1_ref, w3_ref, w2_ref, lng_ref, lnb_ref, y_ref,
                     None if split_ctx_tiles is None else y2_ref, h_scr, mem_scr, gate_scr, rank_scr, acc_scr,
                     earlier_scr, alpha=alpha, split_ctx_tiles=split_ctx_tiles)


def _moe_routed_body(x_ref, mod_ref, wr_ref, br_ref, w1_ref, w3_ref, w2_ref, lng_ref, lnb_ref, y_ref, y2_ref,
                     h_scr, mem_scr, gate_scr, rank_scr, acc_scr, earlier_scr, *, alpha, split_ctx_tiles):
    e = pl.program_id(1)
    m = mod_ref[0]
    t_tile = x_ref.shape[0]

    @pl.when(jnp.logical_and(pl.program_id(0) == 0, e == 0))
    def _():
        earlier_scr[...] = jnp.where(_iota((t_tile, t_tile), 0) < _iota((t_tile, t_tile), 1), 1.0, 0.0).astype(BF16)

    @pl.when(e == 0)
    def _():
        h = x_ref[...] * (1.0 + m[4:5]) + m[3:4]
        h_hi, h_lo = _split2(h)
        h_scr[...] = h_hi
        w_hi, w_lo = _split2(wr_ref[...])
        both = lax.dot_general(jnp.concatenate([w_hi, w_lo], axis=0), h_hi, NT, preferred_element_type=F32)
        lt = (both[:N_EXPERTS] + both[N_EXPERTS:] + lax.dot_general(w_hi, h_lo, NT, preferred_element_type=F32)
              + br_ref[...])
        eidx = _iota(lt.shape, 0)
        m1 = jnp.max(lt, 0, keepdims=True)
        i1 = jnp.min(jnp.where(lt == m1, eidx, N_EXPERTS), 0, keepdims=True)
        rest = jnp.where(eidx == i1, NEG_BIG, lt)
        m2 = jnp.max(rest, 0, keepdims=True)
        i2 = jnp.min(jnp.where(rest == m2, eidx, N_EXPERTS), 0, keepdims=True)
        e2 = jnp.exp(m2 - m1)
        g1 = 1.0 / (1.0 + e2)
        mem = jnp.where(eidx == i1, 1.0, jnp.where(eidx == i2, 1.0, 0.0))
        mem_scr[...] = mem
        gate_scr[...] = jnp.where(eidx == i1, g1, jnp.where(eidx == i2, e2 * g1, 0.0))
        rank_scr[...] = jnp.dot(mem.astype(BF16), earlier_scr[...], preferred_element_type=F32)
        acc_scr[...] = jnp.zeros_like(acc_scr)

    mem_row = mem_scr[pl.ds(e, 1), :]
    gate_row = gate_scr[pl.ds(e, 1), :]
    rank_row = rank_scr[pl.ds(e, 1), :]
    count = jnp.sum(mem_row).astype(jnp.int32)
    slot0 = _iota((MOE_BLOCK, t_tile), 0).astype(F32)

    def block(j, carry):
        slot = slot0 + (j * MOE_BLOCK).astype(F32)
        hit = jnp.where(rank_row == slot, mem_row, 0.0)
        sel = hit.astype(BF16)
        hb = jnp.dot(sel, h_scr[...], preferred_element_type=F32).astype(BF16)
        a = jnp.dot(hb, w1_ref[0], preferred_element_type=F32)
        g = jnp.dot(hb, w3_ref[0], preferred_element_type=F32)
        yb = jnp.dot((_silu(a) * g).astype(BF16), w2_ref[0], preferred_element_type=F32)
        gate_slot = jnp.sum(hit * gate_row, -1, keepdims=True)
        acc_scr[...] += lax.dot_general(sel, (yb * gate_slot).astype(BF16), TN, preferred_element_type=F32)
        return carry

    lax.fori_loop(0, (count + MOE_BLOCK - 1) // MOE_BLOCK, block, 0)

    @pl.when(e == N_EXPERTS - 1)
    def _():
        y = _layer_norm(alpha * x_ref[...] + m[5:6] * acc_scr[...], lng_ref[...], lnb_ref[...])
        if split_ctx_tiles is None:
            y_ref[...] = y
        else:
            @pl.when(pl.program_id(0) < split_ctx_tiles)
            def _():
                y_ref[...] = y

            @pl.when(pl.program_id(0) >= split_ctx_tiles)
            def _():
                y2_ref[...] = y


def _moe(x, mod_l, w_r, b_r, w1, w3, w2, ln_g, ln_b, mod_index_moe, alpha, split_ctx_tiles=None):
    n = x.shape[0]
    row = pl.BlockSpec((1, D_MODEL), lambda i, e: (0, 0))
    expert = lambda a: pl.BlockSpec((1,) + a.shape[1:], lambda i, e: (e, 0, 0))
    routing = pltpu.VMEM((N_EXPERTS, TM_MOE), F32)
    if split_ctx_tiles is None:
        out_shape = jax.ShapeDtypeStruct((n, D_MODEL), F32)
        out_specs = pl.BlockSpec((TM_MOE, D_MODEL), lambda i, e: (i, 0))
    else:
        n_ctx_rows = split_ctx_tiles * TM_MOE
        out_shape = [jax.ShapeDtypeStruct((n_ctx_rows, D_MODEL), F32), jax.ShapeDtypeStruct((n - n_ctx_rows, D_MODEL), F32)]
        out_specs = [pl.BlockSpec((TM_MOE, D_MODEL), lambda i, e: (jnp.minimum(i, split_ctx_tiles - 1), 0)),
                     pl.BlockSpec((TM_MOE, D_MODEL), lambda i, e: (jnp.maximum(i - split_ctx_tiles, 0), 0))]
    return pl.pallas_call(
        functools.partial(_moe_routed_kernel, alpha=alpha, split_ctx_tiles=split_ctx_tiles),
        out_shape=out_shape,
        grid=(n // TM_MOE, N_EXPERTS),
        in_specs=[pl.BlockSpec((TM_MOE, D_MODEL), lambda i, e: (i, 0)),
                  pl.BlockSpec((1, 6, D_MODEL), lambda i, e: (mod_index_moe(i), 0, 0)),
                  pl.BlockSpec((N_EXPERTS, D_MODEL), lambda i, e: (0, 0)),
                  pl.BlockSpec((N_EXPERTS, 1), lambda i, e: (0, 0)),
                  expert(w1), expert(w3), expert(w2), row, row],
        out_specs=out_specs,
        scratch_shapes=[pltpu.VMEM((TM_MOE, D_MODEL), BF16), routing, routing, routing,
                        pltpu.VMEM((TM_MOE, D_MODEL), F32), pltpu.VMEM((TM_MOE, TM_MOE), BF16)],
        compiler_params=pltpu.CompilerParams(dimension_semantics=("arbitrary", "arbitrary"),
                                             vmem_limit_bytes=VMEM_LIMIT),
        name="moe_ln",
    )(x, mod_l, w_r, b_r, w1, w3, w2, ln_g, ln_b)


def _moe_dense(x, mod_l, w_r, b_r, w1, w3, w2, ln_g, ln_b, mod_index_moe, alpha):
    n = x.shape[0]
    row = pl.BlockSpec((1, D_MODEL), lambda i, e: (0, 0))
    expert = lambda a: pl.BlockSpec((1,) + a.shape[1:], lambda i, e: (e, 0, 0))
    return pl.pallas_call(
        functools.partial(_moe_kernel, alpha=alpha),
        out_shape=jax.ShapeDtypeStruct((n, D_MODEL), F32),
        grid=(n // TM_MOE, N_EXPERTS),
        in_specs=[pl.BlockSpec((TM_MOE, D_MODEL), lambda i, e: (i, 0)),
                  pl.BlockSpec((1, 6, D_MODEL), lambda i, e: (mod_index_moe(i), 0, 0)),
                  pl.BlockSpec((D_MODEL, 128), lambda i, e: (0, 0)),
                  pl.BlockSpec((1, 128), lambda i, e: (0, 0)),
                  expert(w1), expert(w3), expert(w2), row, row],
        out_specs=pl.BlockSpec((TM_MOE, D_MODEL), lambda i, e: (i, 0)),
        scratch_shapes=[pltpu.VMEM((TM_MOE, D_MODEL), BF16), pltpu.VMEM((TM_MOE, 128), F32),
                        pltpu.VMEM((TM_MOE, D_MODEL), F32)],
        compiler_params=pltpu.CompilerParams(dimension_semantics=("arbitrary", "arbitrary"),
                                             vmem_limit_bytes=VMEM_LIMIT),
        name="moe_ln",
    )(x, mod_l, w_r, b_r, w1, w3, w2, ln_g, ln_b)


def _rope_tables(lat_len):
    t = jnp.arange(lat_len)
    pos = jnp.stack([t // GRID_W, t % GRID_W], axis=1).astype(F32)
    half, quarter = HEAD_DIM // 2, HEAD_DIM // 4
    inv_freq = ROPE_THETA ** (-jnp.arange(quarter, dtype=F32) * 2.0 / half)
    lane = np.arange(HEAD_DIM)
    ang = pos[:, lane // half] * inv_freq[lane % quarter][None, :]
    sign = jnp.asarray(np.where((lane % half) < quarter, -1.0, 1.0), F32)
    cos = jnp.tile(jnp.cos(ang), (1, 128 // HEAD_DIM))
    sin = jnp.tile(jnp.sin(ang) * sign, (1, 128 // HEAD_DIM))
    return cos, sin


def kernel(x_prompt, x_sample, cache_k, cache_v, state_gdn, state_hgrn, c, c_ctx, w_mod, b_mod, w_in, conv_w, attn_sink, gdn_a_log, gdn_dt_bias, gdn_norm_w, hgrn_lb, hgrn_norm_w, w_out, ln_g, ln_b, ffn_w1, ffn_w3, ffn_w2, moe_router, moe_router_b, moe_w1, moe_w3, moe_w2):
    depth = w_in.shape[0]
    n_ctx, ctx_len, _ = x_prompt.shape
    lat_batch, lat_len, _ = x_sample.shape
    n_ctx_rows = n_ctx * ctx_len
    alpha = (2.0 * depth) ** 0.25

    x_parts = [x_prompt.reshape(-1, D_MODEL), x_sample.reshape(-1, D_MODEL)]

    def mod_index_for(tile):
        ctx_tiles, per_seq = n_ctx_rows // tile, lat_len // tile
        return lambda i: jnp.where(i < ctx_tiles, 0, 1 + (i - ctx_tiles) // per_seq)

    cvecs = jnp.concatenate([c_ctx[None], c], axis=0)
    mods = _modulation(cvecs, w_mod, b_mod).reshape(depth, cvecs.shape[0], 6, D_MODEL)

    cos_tab, sin_tab = _rope_tables(lat_len)
    sm = jax.nn.softmax(hgrn_lb.astype(F32), axis=0)
    lower = jnp.cumsum(sm, axis=0) - sm[0:1]

    w_out_b = w_out.astype(BF16)

    zeros_state = jnp.zeros((n_ctx, HEAD_DIM, D_REC), F32)
    sched = [_scan_schedule(n_ctx, ctx_len, lat_batch, lat_len, reverse) for reverse in (False, True)]
    new_k, new_v, new_sg, new_sh = [], [], [], []

    for l in range(depth):
        alog_row = jnp.zeros((1, 128), F32).at[0, 8:16].set(gdn_a_log[l].reshape(-1))
        dtb_row = jnp.zeros((1, 128), F32).at[0, 8:16].set(gdn_dt_bias[l].reshape(-1))
        qa, ka, va, qkv, g_b, hg, bl = _in_proj(x_parts, mods[l], w_in[l], cos_tab, sin_tab, conv_w[l], alog_row,
                                                dtb_row, mod_index_for(TM), n_ctx_rows // TM, ctx_len, lat_len)
        new_k.append(ka[:n_ctx_rows].reshape(n_ctx, ctx_len, A_KV_HEADS, HEAD_DIM))
        new_v.append(va[:n_ctx_rows].reshape(n_ctx, ctx_len, A_KV_HEADS, HEAD_DIM))

        o_a = _attention(qa, ka, va, cache_k[:, l].reshape(lat_batch, -1, D_KV), cache_v[:, l].reshape(lat_batch, -1, D_KV),
                         attn_sink[l], n_ctx, ctx_len, lat_batch, lat_len)

        sg0 = state_gdn[:, l].transpose(0, 1, 3, 2, 4).reshape(lat_batch, 2, HEAD_DIM, D_REC)
        sh0 = state_hgrn[:, l].transpose(0, 1, 4, 2, 3).reshape(lat_batch, 2, HEAD_DIM, D_REC)
        o_f, s_f, o_b, s_b = _gdn_scan_bidir(qkv, bl, jnp.concatenate([zeros_state, sg0[:, 0]], axis=0),
                                             jnp.concatenate([zeros_state, sg0[:, 1]], axis=0),
                                             jnp.concatenate(sched, axis=0))
        ob = [o_f, o_b]
        sg_fin = [s[:n_ctx].reshape(n_ctx, HEAD_DIM, N_HEADS_REC, HEAD_DIM).transpose(0, 2, 1, 3) for s in (s_f, s_b)]
        o_f, s_f, o_b, s_b = _hgrn_scan_bidir(hg, lower[l], jnp.concatenate([zeros_state, sh0[:, 0]], axis=0),
                                              jnp.concatenate([zeros_state, sh0[:, 1]], axis=0),
                                              jnp.concatenate(sched, axis=0))
        oc = [o_f, o_b]
        sh_fin = [s[:n_ctx].reshape(n_ctx, HEAD_DIM, N_HEADS_REC, HEAD_DIM).transpose(0, 2, 3, 1) for s in (s_f, s_b)]
        new_sg.append(jnp.stack(sg_fin, axis=1))
        new_sh.append(jnp.stack(sh_fin, axis=1))

        nw_b = jnp.tile(gdn_norm_w[l], N_HEADS_REC)[None]
        nw_c = jnp.tile(hgrn_norm_w[l], N_HEADS_REC)[None]
        x = _out_proj(x_parts, mods[l], o_a, ob[0], ob[1], oc[0], oc[1], g_b, hg, nw_b, nw_c, w_out_b[l],
                      ln_g[l, 0][None], ln_b[l, 0][None], mod_index_for(TM), alpha, n_ctx_rows // TM)

        i = l // 2
        if l % 2 == 0:
            x = _ffn(x, mods[l], ffn_w1[i].astype(BF16), ffn_w3[i].astype(BF16), ffn_w2[i].astype(BF16),
                     ln_g[l, 1][None], ln_b[l, 1][None], mod_index_for(TM_FFN), alpha)
        else:
            split = n_ctx_rows // TM_MOE if l == depth - 1 else None
            x = _moe(x, mods[l], moe_router[i].T, moe_router_b[i][:, None],
                     moe_w1[i].astype(BF16), moe_w3[i].astype(BF16), moe_w2[i].astype(BF16),
                     ln_g[l, 1][None], ln_b[l, 1][None], mod_index_for(TM_MOE), alpha, split)
        x_parts = [x]

    if isinstance(x, (list, tuple)):
        y_prompt, y_sample = x[0].reshape(x_prompt.shape), x[1].reshape(x_sample.shape)
    else:
        y_prompt = x[:n_ctx_rows].reshape(x_prompt.shape)
        y_sample = x[n_ctx_rows:].reshape(x_sample.shape)
    return (y_prompt, y_sample, jnp.stack(new_k, axis=1), jnp.stack(new_v, axis=1),
            jnp.stack(new_sg, axis=1), jnp.stack(new_sh, axis=1))
```

```python
import functools

import numpy as np
import jax
import jax.numpy as jnp
from jax import lax
from jax.experimental import pallas as pl
from jax.experimental.pallas import tpu as pltpu

F32 = jnp.float32
BF16 = jnp.bfloat16

D_MODEL = 1024
HEAD_DIM = 64
A_HEADS = 8
A_KV_HEADS = 2
GRID_W = 64
ATTN_BLOCK = 128
N_HEADS_REC = 4
D_REC = N_HEADS_REC * HEAD_DIM
CONV_W = 5
CHUNK = 64
N_LEVELS = 6
D_A = A_HEADS * HEAD_DIM
D_KV = A_KV_HEADS * HEAD_DIM
D_FF = 2816
N_EXPERTS = 8
EXPERT_FF = 1024
ROPE_THETA = 10000.0
NORM_EPS = 1e-5
TINY = 1e-30
NEG_BIG = -1e30

C_QA, C_KA, C_VA, C_QKVB, C_GB, C_HG = 0, 512, 640, 768, 1536, 1792
C_BA_END = C_HG + 16
D_HG = 1280
HALO = 8

TM = 512
TS = 256
TM_FFN = 512
FF_CHUNK = 256
TM_MOE = 1024
MOE_BLOCK = 128
VMEM_LIMIT = 56 * 1024 * 1024

NN = (((1,), (0,)), ((), ()))
NT = (((1,), (1,)), ((), ()))
TN = (((0,), (0,)), ((), ()))


def _dot(a, b, dims=NN):
    return lax.dot_general(a.astype(BF16), b.astype(BF16), dims, preferred_element_type=F32)


def _split2(x):
    hi = x.astype(BF16)
    lo = (x - hi.astype(F32)).astype(BF16)
    return hi, lo


def _split3(x):
    hi = x.astype(BF16)
    r = x - hi.astype(F32)
    mid = r.astype(BF16)
    lo = (r - mid.astype(F32)).astype(BF16)
    return hi, mid, lo


def _dot_sel(sel, x, dims=NN, sel_is_lhs=True):
    out = None
    for part in _split3(x):
        a, b = (sel, part) if sel_is_lhs else (part, sel)
        t = lax.dot_general(a, b, dims, preferred_element_type=F32)
        out = t if out is None else out + t
    return out


def _sigmoid(x):
    return 1.0 / (1.0 + jnp.exp(-x))


def _silu(x):
    return x * _sigmoid(x)


def _layer_norm(x, g, b):
    mu = jnp.mean(x, -1, keepdims=True)
    xc = x - mu
    var = jnp.mean(xc * xc, -1, keepdims=True)
    return xc * lax.rsqrt(var + NORM_EPS) * g + b


def _iota(shape, dim):
    return lax.broadcasted_iota(jnp.int32, shape, dim)


def _head_ones():
    return jnp.where(_iota((D_REC, D_REC), 0) // HEAD_DIM == _iota((D_REC, D_REC), 1) // HEAD_DIM, 1.0, 0.0).astype(BF16)


def _block_diag(x):
    head = _iota(x.shape, 1) // HEAD_DIM
    return jnp.concatenate([jnp.where(head == h, x, jnp.zeros_like(x)) for h in range(N_HEADS_REC)], axis=0)


def _block_diag_mask():
    return _iota((D_REC, D_REC), 0) // HEAD_DIM == _iota((D_REC, D_REC), 1) // HEAD_DIM


def _collapse_block_diag(s):
    out = s[0:HEAD_DIM]
    for h in range(1, N_HEADS_REC):
        out = out + s[h * HEAD_DIM:(h + 1) * HEAD_DIM]
    return out


def _mod_kernel(ct_ref, w_ref, b_ref, o_ref):
    s = _silu(ct_ref[...])
    w = w_ref[0]
    rows = [jnp.sum(s[:, r:r + 1] * w, axis=0, keepdims=True) for r in range(o_ref.shape[1])]
    o_ref[0] = jnp.concatenate(rows, axis=0) + b_ref[0]


def _modulation(cvecs, w_mod, b_mod):
    depth, d, n6 = w_mod.shape
    r = cvecs.shape[0]
    tn = 1024
    ct = jnp.zeros((d, 8), F32).at[:, :r].set(cvecs.T)
    return pl.pallas_call(
        _mod_kernel,
        out_shape=jax.ShapeDtypeStruct((depth, r, n6), F32),
        grid=(depth, n6 // tn),
        in_specs=[pl.BlockSpec((d, 8), lambda l, j: (0, 0)),
                  pl.BlockSpec((1, d, tn), lambda l, j: (l, 0, j)),
                  pl.BlockSpec((1, 1, tn), lambda l, j: (l, 0, j))],
        out_specs=pl.BlockSpec((1, r, tn), lambda l, j: (l, 0, j)),
        compiler_params=pltpu.CompilerParams(dimension_semantics=("arbitrary", "arbitrary"),
                                             vmem_limit_bytes=VMEM_LIMIT),
        name="modulation",
    )(ct, w_mod, b_mod.reshape(depth, 1, n6))


def _rope(x, cos, sin):
    outs = []
    lane = _iota(cos.shape, 1)
    first = (lane % 32) < 16
    for m in range(x.shape[1] // 128):
        xs = x[:, m * 128:(m + 1) * 128]
        swapped = jnp.where(first, pltpu.roll(xs, 128 - 16, 1), pltpu.roll(xs, 16, 1))
        outs.append(xs * cos + swapped * sin)
    return outs[0] if len(outs) == 1 else jnp.concatenate(outs, axis=1)


def _token_specs(x_parts, n_ctx_tiles):
    if len(x_parts) == 1:
        return [pl.BlockSpec((TM, D_MODEL), lambda i: (i, 0))]
    return [pl.BlockSpec((TM, D_MODEL), lambda i: (jnp.minimum(i, n_ctx_tiles - 1), 0)),
            pl.BlockSpec((TM, D_MODEL), lambda i: (jnp.maximum(i - n_ctx_tiles, 0), 0))]


def _read_tokens(x_refs, n_ctx_tiles):
    if len(x_refs) == 1:
        return x_refs[0][...]
    return jnp.where(pl.program_id(0) < n_ctx_tiles, x_refs[0][...], x_refs[1][...])


def _halo_specs(x_parts, n_ctx_tiles):
    per = TM // HALO
    specs = []
    for p, part in enumerate(x_parts):
        last_block = part.shape[0] // HALO - 1
        if len(x_parts) == 1:
            local = lambda i: i
        elif p == 0:
            local = lambda i: jnp.minimum(i, n_ctx_tiles - 1)
        else:
            local = lambda i: jnp.maximum(i - n_ctx_tiles, 0)
        specs.append(pl.BlockSpec((HALO, D_MODEL), lambda i, local=local: (jnp.maximum(local(i) * per - 1, 0), 0)))
        specs.append(pl.BlockSpec((HALO, D_MODEL),
                                  lambda i, local=local, last_block=last_block:
                                  (jnp.minimum((local(i) + 1) * per, last_block), 0)))
    return specs


def _in_proj_kernel(*refs, n_ctx_tiles, n_x, ctx_len, lat_len):
    x_refs, halo_refs = refs[:n_x], refs[n_x:3 * n_x]
    (mod_ref, w_ref, whg_ref, wba_ref, cos_ref, sin_ref, convw_ref, alog_ref, dtb_ref,
     qa_ref, ka_ref, va_ref, qkv_ref, gb_ref, hg_ref, bl_ref) = refs[3 * n_x:]
    i = pl.program_id(0)
    in_ctx = i < n_ctx_tiles
    m = mod_ref[0]
    h = (_read_tokens(x_refs, n_ctx_tiles) * (1.0 + m[1:2]) + m[0:1]).astype(BF16)

    def proj(lo, hi):
        return jnp.dot(h, w_ref[:, lo:hi], preferred_element_type=F32)

    if n_x == 1:
        x_halo = jnp.concatenate([halo_refs[0][...], halo_refs[1][...]], axis=0)
    else:
        x_halo = jnp.concatenate([jnp.where(in_ctx, halo_refs[0][...], halo_refs[2][...]),
                                  jnp.where(in_ctx, halo_refs[1][...], halo_refs[3][...])], axis=0)
    h_halo = (x_halo * (1.0 + m[1:2]) + m[0:1]).astype(BF16)
    p_halo = jnp.dot(h_halo, w_ref[:, C_QKVB:C_GB], preferred_element_type=F32)
    xh = jnp.concatenate([p_halo[:HALO], proj(C_QKVB, C_GB), p_halo[HALO:]], axis=0)

    r = _iota((TM, 1), 0)
    pos_ctx = (i * TM + r) % ctx_len
    pos_lat = ((i - n_ctx_tiles) * TM + r) % lat_len
    cw = convw_ref[...]
    y = None
    for t in range(CONV_W):
        lo = HALO - CONV_W // 2 + t
        term = xh[lo:lo + TM] * cw[t:t + 1]
        off = t - CONV_W // 2
        if off != 0:
            ok_ctx = jnp.logical_and(pos_ctx + off >= 0, pos_ctx + off < ctx_len)
            ok_lat = jnp.logical_and(pos_lat + off >= 0, pos_lat + off < lat_len)
            keep = jnp.where(in_ctx, jnp.where(ok_ctx, 1.0, 0.0), jnp.where(ok_lat, 1.0, 0.0))
            term = term * keep
        y = term if y is None else y + term
        if t == 1:
            qa_ref[...] = proj(C_QA, C_KA)
        if t == 3:
            hg_ref[:, :2 * D_REC] = jnp.dot(h, whg_ref[:, :2 * D_REC], preferred_element_type=F32)
    ka_ref[...] = proj(C_KA, C_VA)
    va_ref[...] = proj(C_VA, C_QKVB)
    y = _silu(y)
    hg_ref[:, 2 * D_REC:4 * D_REC] = jnp.dot(h, whg_ref[:, 2 * D_REC:4 * D_REC], preferred_element_type=F32)
    q, k, v = y[:, :D_REC], y[:, D_REC:2 * D_REC], y[:, 2 * D_REC:]
    ones = _head_ones()
    qq_hi, qq_lo = _split2(q * q)
    kk_hi, kk_lo = _split2(k * k)
    sums = jnp.dot(jnp.concatenate([qq_hi, qq_lo, kk_hi, kk_lo], axis=0), ones, preferred_element_type=F32)
    hg_ref[:, 4 * D_REC:] = jnp.dot(h, whg_ref[:, 4 * D_REC:], preferred_element_type=F32)
    gb_ref[...] = proj(C_GB, C_HG)
    ba = jnp.dot(h, wba_ref[...], preferred_element_type=F32)
    q_ss = sums[0:TM] + sums[TM:2 * TM]
    k_ss = sums[2 * TM:3 * TM] + sums[3 * TM:]
    qkv_ref[:, :D_REC] = q * lax.rsqrt(q_ss + 1e-6) * (HEAD_DIM ** -0.5)
    qkv_ref[:, D_REC:2 * D_REC] = k * lax.rsqrt(k_ss + 1e-6)
    qkv_ref[:, 2 * D_REC:] = v

    za = ba + dtb_ref[...]
    softplus = jnp.maximum(za, 0.0) + jnp.log(1.0 + jnp.exp(-jnp.abs(za)))
    la = -jnp.exp(alog_ref[...]) * softplus
    bl_ref[...] = jnp.where(_iota(ba.shape, 1) < 2 * N_HEADS_REC, _sigmoid(ba), la)

    @pl.when(i >= n_ctx_tiles)
    def _():
        cos, sin = cos_ref[...], sin_ref[...]
        qa_ref[...] = _rope(qa_ref[...], cos, sin)
        ka_ref[...] = _rope(ka_ref[...], cos, sin)


def _in_proj(x_parts, mod_l, w_in_l, cos_tab, sin_tab, conv_w, alog_row, dtb_row, mod_index, n_ctx_tiles,
             ctx_len, lat_len):
    n = sum(p.shape[0] for p in x_parts)
    lat_tiles_per_seq = lat_len // TM
    w_main = w_in_l[:, :C_HG].astype(BF16)
    w_hg = w_in_l[:, C_BA_END:].astype(BF16)
    w_ba = jnp.pad(w_in_l[:, C_HG:C_BA_END], ((0, 0), (0, 128 - (C_BA_END - C_HG)))).astype(BF16)
    whole = lambda a: pl.BlockSpec(a.shape, lambda i: (0,) * a.ndim)

    def tab_index(i):
        return (jnp.where(i < n_ctx_tiles, 0, (i - n_ctx_tiles) % lat_tiles_per_seq), 0)

    def out(width):
        return jax.ShapeDtypeStruct((n, width), F32), pl.BlockSpec((TM, width), lambda i: (i, 0))

    outs = [out(D_A), out(D_KV), out(D_KV), out(3 * D_REC), out(D_REC), out(D_HG), out(128)]
    return pl.pallas_call(
        functools.partial(_in_proj_kernel, n_ctx_tiles=n_ctx_tiles, n_x=len(x_parts), ctx_len=ctx_len,
                          lat_len=lat_len),
        out_shape=[o[0] for o in outs],
        grid=(n // TM,),
        in_specs=_token_specs(x_parts, n_ctx_tiles) + _halo_specs(x_parts, n_ctx_tiles) + [
            pl.BlockSpec((1, 6, D_MODEL), lambda i: (mod_index(i), 0, 0)),
            whole(w_main), whole(w_hg), whole(w_ba),
            pl.BlockSpec((TM, 128), tab_index),
            pl.BlockSpec((TM, 128), tab_index),
            whole(conv_w), whole(alog_row), whole(dtb_row)],
        out_specs=[o[1] for o in outs],
        compiler_params=pltpu.CompilerParams(dimension_semantics=("arbitrary",), vmem_limit_bytes=VMEM_LIMIT),
        name="in_proj",
    )(*x_parts, *[p for part in x_parts for p in (part, part)], mod_l, w_main, w_hg, w_ba, cos_tab, sin_tab,
      conv_w, alog_row, dtb_row)


def _attn_heads(q_ref, k, v, bias, sink_ref, o_ref):
    k_same, k_swap = k.astype(BF16), pltpu.roll(k, HEAD_DIM, 1).astype(BF16)
    v_same, v_swap = v.astype(BF16), pltpu.roll(v, HEAD_DIM, 1).astype(BF16)
    nq = q_ref.shape[0]
    low_half = _iota((nq, 128), 1) < HEAD_DIM
    rep = A_HEADS // A_KV_HEADS
    for g in range(A_KV_HEADS):
        heads = range(g * rep, (g + 1) * rep)
        scores, exps, dens, outs = {}, {}, {}, {}
        for hq in heads:
            p, e = hq // 2, hq % 2
            q2 = q_ref[:, p * 128:(p + 1) * 128] * (HEAD_DIM ** -0.5)
            qm = jnp.where(low_half if e == 0 else jnp.logical_not(low_half), q2, 0.0)
            s = _dot(qm, k_same if e == g else k_swap, NT)
            if bias is not None:
                b = ATTN_BLOCK
                s = jnp.concatenate([s[:, :b] + bias[0], s[:, b:2 * b], s[:, 2 * b:3 * b] + bias[1], s[:, 3 * b:]], axis=1)
            scores[hq] = s
        for hq in heads:
            snk = sink_ref[hq]
            mx = jnp.maximum(jnp.max(scores[hq], -1, keepdims=True), snk)
            exps[hq] = jnp.exp(scores[hq] - mx)
            dens[hq] = jnp.sum(exps[hq], -1, keepdims=True) + jnp.exp(snk - mx)
        for hq in heads:
            outs[hq] = _dot(exps[hq], v_same if hq % 2 == g else v_swap) / dens[hq]
        for p in range(g * rep // 2, (g + 1) * rep // 2):
            o_ref[:, p * 128:(p + 1) * 128] = jnp.where(low_half, outs[2 * p], outs[2 * p + 1])


def _attn_kernel(q_ref, kx_ref, vx_ref, kp_ref, kc_ref, kn_ref, vp_ref, vc_ref, vn_ref, ck_ref, cv_ref, sink_ref,
                 o_ref, *, ctx_steps, n_blocks):
    i = pl.program_id(0)

    @pl.when(i < ctx_steps)
    def _():
        _attn_heads(q_ref, kx_ref[...], vx_ref[...], None, sink_ref, o_ref)

    @pl.when(i >= ctx_steps)
    def _():
        n = (i - ctx_steps) % n_blocks
        k = jnp.concatenate([kp_ref[...], kc_ref[...], kn_ref[...], ck_ref[0]], axis=0)
        v = jnp.concatenate([vp_ref[...], vc_ref[...], vn_ref[...], cv_ref[0]], axis=0)
        qi = _iota((ATTN_BLOCK, ATTN_BLOCK), 0)
        jj = _iota((ATTN_BLOCK, ATTN_BLOCK), 1)
        bias_prev = jnp.where(jj >= qi, 0.0, NEG_BIG) + jnp.where(n > 0, 0.0, NEG_BIG)
        bias_next = jnp.where(jj <= qi, 0.0, NEG_BIG) + jnp.where(n < n_blocks - 1, 0.0, NEG_BIG)
        _attn_heads(q_ref, k, v, (bias_prev, bias_next), sink_ref, o_ref)


def _attention(qa, ka, va, ck, cv, sink, n_ctx, ctx_len, lat_batch, lat_len):
    n = qa.shape[0]
    per_seq = ctx_len // ATTN_BLOCK
    ctx_steps = n_ctx * per_seq
    nb = lat_len // ATTN_BLOCK
    past = ck.shape[1]

    def lat(i):
        return jnp.maximum(i - ctx_steps, 0)

    def cur(i):
        return (ctx_steps + lat(i), 0)

    def prev(i):
        return (ctx_steps + (lat(i) // nb) * nb + jnp.maximum(lat(i) % nb - 1, 0), 0)

    def nxt(i):
        return (ctx_steps + (lat(i) // nb) * nb + jnp.minimum(lat(i) % nb + 1, nb - 1), 0)

    def own_seq(i):
        return (jnp.minimum(i // per_seq, n_ctx - 1), 0)

    kv = lambda f: pl.BlockSpec((ATTN_BLOCK, D_KV), f)
    cache = pl.BlockSpec((1, past, D_KV), lambda i: (lat(i) // nb, 0, 0))
    return pl.pallas_call(
        functools.partial(_attn_kernel, ctx_steps=ctx_steps, n_blocks=nb),
        out_shape=jax.ShapeDtypeStruct((n, D_A), F32),
        grid=(ctx_steps + lat_batch * nb,),
        in_specs=[pl.BlockSpec((ATTN_BLOCK, D_A), lambda i: (i, 0)),
                  pl.BlockSpec((ctx_len, D_KV), own_seq), pl.BlockSpec((ctx_len, D_KV), own_seq),
                  kv(prev), kv(cur), kv(nxt), kv(prev), kv(cur), kv(nxt), cache, cache,
                  pl.BlockSpec(memory_space=pltpu.SMEM)],
        out_specs=pl.BlockSpec((ATTN_BLOCK, D_A), lambda i: (i, 0)),
        compiler_params=pltpu.CompilerParams(dimension_semantics=("arbitrary",), vmem_limit_bytes=VMEM_LIMIT),
        name="attention",
    )(qa, ka, va, ka, ka, ka, va, va, va, ck, cv, sink)


def _tri(reverse, strict=False):
    i, j = _iota((CHUNK, CHUNK), 0), _iota((CHUNK, CHUNK), 1)
    if strict:
        return (i < j) if reverse else (i > j)
    return (i <= j) if reverse else (i >= j)


def _scan_schedule(n_ctx, ctx_len, lat_batch, lat_len, reverse):
    tiles_ctx, tiles_lat = ctx_len // TS, lat_len // TS
    seqs = [[s * tiles_ctx + t for t in range(tiles_ctx)] for s in range(n_ctx)]
    seqs += [[n_ctx * tiles_ctx + b * tiles_lat + t for t in range(tiles_lat)] for b in range(lat_batch)]
    cols = []
    for s, tiles in enumerate(seqs):
        order = tiles[::-1] if reverse else tiles
        for n, t in enumerate(order):
            cols.append((t, s, int(n == 0), int(n == len(order) - 1)))
    return jnp.asarray(np.array(cols, np.int32).T)


def _mm3_heads(lhs_list, rhs):
    r_hi, r_lo = _split2(rhs)
    w_hi, w_lo = _block_diag(r_hi), _block_diag(r_lo)
    parts = [_split2(a) for a in lhs_list]
    his = [p[0] for p in parts]
    los = [p[1] for p in parts]
    n = len(lhs_list) * CHUNK
    big = jnp.dot(jnp.concatenate(his + los, axis=0), w_hi, preferred_element_type=F32)
    small = jnp.dot(his[0] if len(his) == 1 else jnp.concatenate(his, axis=0), w_lo, preferred_element_type=F32)
    out = big[:n] + big[n:] + small
    return [out[m * CHUNK:(m + 1) * CHUNK] for m in range(len(lhs_list))]


def _pick_heads(vals):
    shape = (CHUNK, D_REC)
    head = _iota(shape, 1) // HEAD_DIM
    out = jnp.broadcast_to(vals[-1], shape)
    for h in range(N_HEADS_REC - 2, -1, -1):
        out = jnp.where(head == h, jnp.broadcast_to(vals[h], shape), out)
    return out


def _gdn_bidir_kernel(sch_ref, qkvf_ref, blf_ref, qkvb_ref, blb_ref, s0f_ref, s0b_ref,
                      of_ref, sff_ref, ob_ref, sfb_ref, s_scr, *sets):
    n = pl.program_id(0)
    n_chunks = TS // CHUNK
    prev = jnp.maximum(n - 1, 0)
    chain_live = n >= 1
    half = len(sets) // 2
    lanes = ((False, qkvf_ref, blf_ref, s0f_ref, of_ref, sff_ref), (True, qkvb_ref, blb_ref, s0b_ref, ob_ref, sfb_ref))

    @pl.when(n == 0)
    def _():
        for ref in sets:
            ref[...] = jnp.zeros(ref.shape, ref.dtype)
        s_scr[...] = jnp.zeros(s_scr.shape, F32)

    for ln, lane_refs in enumerate(lanes):
        @pl.when(jnp.logical_and(chain_live, sch_ref[4 * ln + 2, prev] == 1))
        def _(ln=ln, s0_ref=lane_refs[3]):
            s_scr[ln] = _block_diag(s0_ref[0])

    row, lane = _iota((CHUNK, D_REC), 0), _iota((CHUNK, D_REC), 1) % HEAD_DIM
    eye = jnp.where(row == lane, 1.0, 0.0)
    ones8 = jnp.ones((8, CHUNK), BF16)
    bd_mask = _block_diag_mask()

    def step(wr, rd):
        u_w, wq_w, at_w, kd_w, et_w = wr
        u_r, wq_r, at_r, kd_r, et_r = rd
        state = [s_scr[0], s_scr[1]]
        inst = []

        def chain(m):
            for ln, (reverse, _, _, _, o_ref, _) in enumerate(lanes):
                c = (n_chunks - 1 - m) if reverse else m
                rs = slice(c * CHUNK, (c + 1) * CHUNK)
                k = ln * n_chunks + m
                s = state[ln]
                ws_qs = _dot(wq_r[k], s)
                v_new = u_r[k] - ws_qs[:CHUNK]
                o_ref[rs, :] = ws_qs[CHUNK:] + _dot(at_r[k], _block_diag(v_new.astype(BF16)))
                state[ln] = s * et_r[k][0:1] + jnp.where(bd_mask, _dot(kd_r[k], v_new, TN), 0.0)

        for ln, (reverse, qkv_ref, bl_ref, _, _, _) in enumerate(lanes):
            incl = (row <= lane) if reverse else (row >= lane)
            tri = jnp.where(_tri(reverse), 1.0, 0.0).astype(BF16)
            t_before_j = (row >= lane) if reverse else (row <= lane)
            last = 0 if reverse else CHUNK - 1
            cb, ca = ln * N_HEADS_REC, 2 * N_HEADS_REC + ln * N_HEADS_REC
            for m in range(n_chunks):
                c = (n_chunks - 1 - m) if reverse else m
                rs = slice(c * CHUNK, (c + 1) * CHUNK)
                bl = bl_ref[rs, :]
                d = dict(q=qkv_ref[rs, :D_REC], k=qkv_ref[rs, D_REC:2 * D_REC], v=qkv_ref[rs, 2 * D_REC:],
                         strict=(row < lane) if reverse else (row > lane), slot=ln * n_chunks + m)
                d["beta"] = _pick_heads([bl[:, cb + h:cb + h + 1] for h in range(N_HEADS_REC)])
                la = _pick_heads([bl[:, ca + h:ca + h + 1] for h in range(N_HEADS_REC)])
                g_i, g_j = None, None
                for p_i, p_j in zip(_split3(la), _split3(jnp.where(t_before_j, la, 0.0))):
                    a = jnp.dot(tri, p_i, preferred_element_type=F32)
                    b = jnp.dot(ones8, p_j, preferred_element_type=F32)
                    g_i = a if g_i is None else g_i + a
                    g_j = b if g_j is None else g_j + b
                d["g_i"] = g_i
                d["g_tot"] = g_i[last:last + 1]
                d["decay"] = jnp.where(incl, jnp.exp(jnp.where(incl, g_i - g_j[0:1], 0.0)), 0.0)
                inst.append(d)
        chain(0)

        for d in inst:
            gram = _dot(jnp.concatenate([d["k"], d["q"]], axis=0), _block_diag(d["k"].astype(BF16)), NT)
            d["low"] = jnp.where(d["strict"], d["beta"] * gram[:CHUNK] * d["decay"], 0.0)
            d["attn"] = gram[CHUNK:] * d["decay"]

        for d in inst:
            d["t"] = eye - d["low"]
            (d["p"],) = _mm3_heads([d["low"]], d["low"])
        for it in range(4):
            for d in inst:
                p_next, tp = _mm3_heads([d["p"], d["t"]], d["p"])
                d["t"] = d["t"] + tp
                d["p"] = p_next
            if 1 + it < n_chunks:
                chain(1 + it)
        for d in inst:
            k = d["slot"]
            t_mat = d["t"] + _mm3_heads([d["t"]], d["p"])[0]
            eg = jnp.exp(d["g_i"])
            vb = d["v"] * d["beta"]
            kbg = d["k"] * d["beta"] * eg
            u_w[k] = _dot(t_mat, _block_diag(vb.astype(BF16)))
            wq_w[k] = jnp.concatenate([_dot(t_mat, _block_diag(kbg.astype(BF16))), d["q"] * eg], axis=0).astype(BF16)
            at_w[k] = d["attn"]
            kd_w[k] = (d["k"] * jnp.exp(d["g_tot"] - d["g_i"])).astype(BF16)
            et_w[k] = jnp.broadcast_to(jnp.exp(d["g_tot"]), (8, D_REC))

        s_scr[0] = state[0]
        s_scr[1] = state[1]

    @pl.when(n % 2 == 0)
    def _():
        step(sets[:half], sets[half:])

    @pl.when(n % 2 == 1)
    def _():
        step(sets[half:], sets[:half])

    for ln, lane_refs in enumerate(lanes):
        @pl.when(jnp.logical_and(chain_live, sch_ref[4 * ln + 3, prev] == 1))
        def _(ln=ln, sfin_ref=lane_refs[5]):
            sfin_ref[0] = _collapse_block_diag(s_scr[ln])


def _gdn_scan_bidir(qkv, bl, s0_f, s0_b, sched):
    n = qkv.shape[0]
    n_seq = s0_f.shape[0]
    n_steps = sched.shape[1]
    n_inst = 2 * (TS // CHUNK)
    ahead = lambda i: jnp.minimum(i, n_steps - 1)
    behind = lambda i: jnp.maximum(i - 1, 0)
    scratch_set = [pltpu.VMEM((n_inst, CHUNK, D_REC), F32), pltpu.VMEM((n_inst, 2 * CHUNK, D_REC), BF16),
                   pltpu.VMEM((n_inst, CHUNK, D_REC), F32), pltpu.VMEM((n_inst, CHUNK, D_REC), BF16),
                   pltpu.VMEM((n_inst, 8, D_REC), F32)]

    def tile_in(width, ln):
        return pl.BlockSpec((TS, width), lambda i, sch: (sch[4 * ln, ahead(i)], 0))

    def tile_out(ln):
        return pl.BlockSpec((TS, D_REC), lambda i, sch: (sch[4 * ln, behind(i)], 0))

    def state_spec(ln):
        return pl.BlockSpec((1, HEAD_DIM, D_REC), lambda i, sch: (sch[4 * ln + 1, behind(i)], 0, 0))

    grid_spec = pltpu.PrefetchScalarGridSpec(
        num_scalar_prefetch=1,
        grid=(n_steps + 1,),
        in_specs=[tile_in(3 * D_REC, 0), tile_in(128, 0), tile_in(3 * D_REC, 1), tile_in(128, 1),
                  state_spec(0), state_spec(1)],
        out_specs=[tile_out(0), state_spec(0), tile_out(1), state_spec(1)],
        scratch_shapes=[pltpu.VMEM((2, D_REC, D_REC), F32)] + scratch_set + scratch_set,
    )
    o_sds = jax.ShapeDtypeStruct((n, D_REC), F32)
    s_sds = jax.ShapeDtypeStruct((n_seq, HEAD_DIM, D_REC), F32)
    return pl.pallas_call(
        _gdn_bidir_kernel,
        out_shape=[o_sds, s_sds, o_sds, s_sds],
        grid_spec=grid_spec,
        compiler_params=pltpu.CompilerParams(dimension_semantics=("arbitrary",), vmem_limit_bytes=VMEM_LIMIT),
        name="gdn_scan",
    )(sched, qkv, bl, qkv, bl, s0_f, s0_b)


def _hgrn_bidir_kernel(sch_ref, ff_ref, if_ref, qf_ref, fb_ref, ib_ref, qb_ref, lb_ref, sel_ref, s0f_ref, s0b_ref,
                       of_ref, sff_ref, ob_ref, sfb_ref, s_scr):
    n = pl.program_id(0)
    n_chunks = TS // CHUNK
    lanes = ((False, ff_ref, if_ref, qf_ref, s0f_ref, of_ref, sff_ref), (True, fb_ref, ib_ref, qb_ref, s0b_ref, ob_ref, sfb_ref))

    for ln, lane_refs in enumerate(lanes):
        @pl.when(sch_ref[4 * ln + 2, n] == 1)
        def _(ln=ln, s0_ref=lane_refs[4]):
            s_scr[ln] = _block_diag(s0_ref[0])

    row = _iota((CHUNK, D_REC), 0)
    col_tok = _iota((CHUNK, D_REC), 1) % HEAD_DIM
    ones = _head_ones()
    bd_mask = _block_diag_mask()

    inst = []
    for ln, (reverse, f_ref, i_ref, q_ref, _, o_ref, _) in enumerate(lanes):
        lb = lb_ref[ln:ln + 1, :]
        sel = sel_ref[ln]
        for m in range(n_chunks):
            c = (n_chunks - 1 - m) if reverse else m
            rs = slice(c * CHUNK, (c + 1) * CHUNK)
            f = lb + (1.0 - lb) * _sigmoid(f_ref[rs, :])
            d = dict(rs=rs, kk=1.0 - f, q=q_ref[rs, :], v=i_ref[rs, :], reverse=reverse, ln=ln, o_ref=o_ref)
            cums = _dot_sel(sel, jnp.log(jnp.maximum(f, TINY)))
            d["b"] = cums[:CHUNK]
            d["b_mid_small"] = [cums[CHUNK:2 * CHUNK], cums[2 * CHUNK:]]
            inst.append(d)

    for lvl in range(N_LEVELS):
        s_half = (CHUNK // 2) >> lvl
        in_block = row % (2 * s_half)
        same_block = (row // (2 * s_half)) == (col_tok // (2 * s_half))
        for d in inst:
            b, reverse = d["b"], d["reverse"]
            later = (in_block < s_half) if reverse else (in_block >= s_half)
            if s_half >= 4:
                pieces = []
                for blk in range(CHUNK // (2 * s_half)):
                    r = blk * 2 * s_half + (s_half if reverse else s_half - 1)
                    pieces.append(jnp.broadcast_to(b[r:r + 1], (2 * s_half, D_REC)))
                b_mid = pieces[0] if len(pieces) == 1 else jnp.concatenate(pieces, axis=0)
            else:
                b_mid = d["b_mid_small"][lvl - (N_LEVELS - 2)]
            e = jnp.exp(-jnp.abs(b - b_mid))
            q_hat = jnp.where(later, d["q"] * e, 0.0)
            k_hat = jnp.where(later, 0.0, d["kk"] * e)
            res = _dot(q_hat, _block_diag(k_hat.astype(BF16)), NT)
            d["a"] = res if lvl == 0 else d["a"] + jnp.where(same_block, res, 0.0)

    for d in inst:
        sums = _dot(d["q"] * d["kk"], ones)
        d["a"] = d["a"] + jnp.where(col_tok == row, sums, 0.0)

    for d in inst:
        b = d["b"]
        last = 0 if d["reverse"] else CHUNK - 1
        b_last = b[last:last + 1]
        d["av"] = _dot(d["a"], _block_diag(d["v"].astype(BF16)))
        d["kv"] = jnp.where(bd_mask, _dot(d["v"], d["kk"] * jnp.exp(b_last - b), TN), 0.0)
        d["qb"] = (d["q"] * jnp.exp(b)).astype(BF16)
        d["e_last"] = jnp.exp(b_last)

    state = [s_scr[0], s_scr[1]]
    for d in inst:
        s = state[d["ln"]]
        d["o_ref"][d["rs"], :] = d["av"] + _dot(d["qb"], s, NT)
        state[d["ln"]] = s * d["e_last"] + d["kv"]
    s_scr[0] = state[0]
    s_scr[1] = state[1]

    for ln, lane_refs in enumerate(lanes):
        @pl.when(sch_ref[4 * ln + 3, n] == 1)
        def _(ln=ln, sfin_ref=lane_refs[6]):
            sfin_ref[0] = _collapse_block_diag(s_scr[ln])


def _hgrn_scan_bidir(hg, lb_e, s0_f, s0_b, sched):
    n = hg.shape[0]
    n_seq = s0_f.shape[0]
    i = np.arange(CHUNK)
    sels = []
    for reverse in (False, True):
        tri = (i[None, :] >= i[:, None]) if reverse else (i[None, :] <= i[:, None])
        rows = [tri]
        for s_half in (2, 1):
            mid = (i // (2 * s_half)) * (2 * s_half) + (s_half if reverse else s_half - 1)
            rows.append(tri[mid])
        sels.append(np.concatenate(rows, axis=0))
    sel = jnp.asarray(np.stack(sels).astype(np.float32), BF16)

    def col(c, ln):
        return pl.BlockSpec((TS, D_REC), lambda i, sch: (sch[4 * ln, i], c))

    def state_spec(ln):
        return pl.BlockSpec((1, HEAD_DIM, D_REC), lambda i, sch: (sch[4 * ln + 1, i], 0, 0))

    def tile_out(ln):
        return pl.BlockSpec((TS, D_REC), lambda i, sch: (sch[4 * ln, i], 0))

    grid_spec = pltpu.PrefetchScalarGridSpec(
        num_scalar_prefetch=1,
        grid=(sched.shape[1],),
        in_specs=[col(0, 0), col(2, 0), col(3, 0), col(1, 1), col(2, 1), col(3, 1),
                  pl.BlockSpec((2, D_REC), lambda i, sch: (0, 0)),
                  pl.BlockSpec((2, 3 * CHUNK, CHUNK), lambda i, sch: (0, 0, 0)),
                  state_spec(0), state_spec(1)],
        out_specs=[tile_out(0), state_spec(0), tile_out(1), state_spec(1)],
        scratch_shapes=[pltpu.VMEM((2, D_REC, D_REC), F32)],
    )
    o_sds = jax.ShapeDtypeStruct((n, D_REC), F32)
    s_sds = jax.ShapeDtypeStruct((n_seq, HEAD_DIM, D_REC), F32)
    return pl.pallas_call(
        _hgrn_bidir_kernel,
        out_shape=[o_sds, s_sds, o_sds, s_sds],
        grid_spec=grid_spec,
        compiler_params=pltpu.CompilerParams(dimension_semantics=("arbitrary",), vmem_limit_bytes=VMEM_LIMIT),
        name="hgrn_scan",
    )(sched, hg, hg, hg, hg, hg, hg, lb_e, sel, s0_f, s0_b)


def _gated_readout(o, gate, w_e, ones):
    ms = _dot_sel(ones, o * o, sel_is_lhs=False) * (1.0 / HEAD_DIM)
    return o * lax.rsqrt(ms + NORM_EPS) * w_e * _silu(gate)


def _out_proj_kernel(*refs, alpha, n_ctx_tiles, n_x):
    x_refs = refs[:n_x]
    (mod_ref, oa_ref, bf_ref, bb_ref, cf_ref, cb_ref, gb_ref, gc_ref, nwb_ref, nwc_ref,
     w_ref, lng_ref, lnb_ref, y_ref) = refs[n_x:]
    x = _read_tokens(x_refs, n_ctx_tiles)
    m = mod_ref[0]
    ones = _head_ones()
    o_b = _gated_readout(bf_ref[...] + bb_ref[...], gb_ref[...], nwb_ref[...], ones)
    o_c = _gated_readout(cf_ref[...] + cb_ref[...], gc_ref[...], nwc_ref[...], ones)
    mix = jnp.dot(oa_ref[...].astype(BF16), w_ref[0:D_A], preferred_element_type=F32)
    mix = mix + jnp.dot(o_b.astype(BF16), w_ref[D_A:D_A + D_REC], preferred_element_type=F32)
    mix = mix + jnp.dot(o_c.astype(BF16), w_ref[D_A + D_REC:], preferred_element_type=F32)
    y_ref[...] = _layer_norm(alpha * x + m[2:3] * mix, lng_ref[...], lnb_ref[...])


def _out_proj(x_parts, mod_l, o_a, ob_f, ob_b, oc_f, oc_b, g_b, hg, nw_b, nw_c, w_out, ln_g, ln_b, mod_index, alpha,
              n_ctx_tiles):
    n = sum(p.shape[0] for p in x_parts)
    tile = lambda w: pl.BlockSpec((TM, w), lambda i: (i, 0))
    row = lambda w: pl.BlockSpec((1, w), lambda i: (0, 0))
    return pl.pallas_call(
        functools.partial(_out_proj_kernel, alpha=alpha, n_ctx_tiles=n_ctx_tiles, n_x=len(x_parts)),
        out_shape=jax.ShapeDtypeStruct((n, D_MODEL), F32),
        grid=(n // TM,),
        in_specs=_token_specs(x_parts, n_ctx_tiles) + [
                  pl.BlockSpec((1, 6, D_MODEL), lambda i: (mod_index(i), 0, 0)),
                  tile(D_A), tile(D_REC), tile(D_REC), tile(D_REC), tile(D_REC), tile(D_REC),
                  pl.BlockSpec((TM, D_REC), lambda i: (i, 4)),
                  row(D_REC), row(D_REC),
                  pl.BlockSpec((D_MODEL, D_MODEL), lambda i: (0, 0)),
                  row(D_MODEL), row(D_MODEL)],
        out_specs=tile(D_MODEL),
        compiler_params=pltpu.CompilerParams(dimension_semantics=("arbitrary",), vmem_limit_bytes=VMEM_LIMIT),
        name="out_proj_ln",
    )(*x_parts, mod_l, o_a, ob_f, ob_b, oc_f, oc_b, g_b, hg, nw_b, nw_c, w_out, ln_g, ln_b)


def _ffn_kernel(x_ref, mod_ref, w1_ref, w3_ref, w2_ref, lng_ref, lnb_ref, y_ref, *, alpha):
    m = mod_ref[0]
    x = x_ref[...]
    h = (x * (1.0 + m[4:5]) + m[3:4]).astype(BF16)
    acc = jnp.zeros((x.shape[0], D_MODEL), F32)
    for c in range(D_FF // FF_CHUNK):
        sl = slice(c * FF_CHUNK, (c + 1) * FF_CHUNK)
        a = jnp.dot(h, w1_ref[:, sl], preferred_element_type=F32)
        g = jnp.dot(h, w3_ref[:, sl], preferred_element_type=F32)
        acc = acc + jnp.dot((_silu(a) * g).astype(BF16), w2_ref[sl, :], preferred_element_type=F32)
    y_ref[...] = _layer_norm(alpha * x + m[5:6] * acc, lng_ref[...], lnb_ref[...])


def _ffn(x, mod_l, w1, w3, w2, ln_g, ln_b, mod_index_ffn, alpha):
    n = x.shape[0]
    whole = lambda a: pl.BlockSpec(a.shape, lambda i: (0, 0), pipeline_mode=pl.Buffered(1))
    row = pl.BlockSpec((1, D_MODEL), lambda i: (0, 0))
    return pl.pallas_call(
        functools.partial(_ffn_kernel, alpha=alpha),
        out_shape=jax.ShapeDtypeStruct((n, D_MODEL), F32),
        grid=(n // TM_FFN,),
        in_specs=[pl.BlockSpec((TM_FFN, D_MODEL), lambda i: (i, 0)),
                  pl.BlockSpec((1, 6, D_MODEL), lambda i: (mod_index_ffn(i), 0, 0)),
                  whole(w1), whole(w3), whole(w2), row, row],
        out_specs=pl.BlockSpec((TM_FFN, D_MODEL), lambda i: (i, 0)),
        compiler_params=pltpu.CompilerParams(dimension_semantics=("arbitrary",), vmem_limit_bytes=VMEM_LIMIT),
        name="ffn_ln",
    )(x, mod_l, w1, w3, w2, ln_g, ln_b)


def _moe_routed_kernel(x_ref, mod_ref, wr_ref, br_ref, w1_ref, w3_ref, w2_ref, lng_ref, lnb_ref, *rest,
                       alpha, split_ctx_tiles):
    if split_ctx_tiles is None:
        y_ref, h_scr, mem_scr, gate_scr, rank_scr, acc_scr, earlier_scr = rest
    else:
        y_ref, y2_ref, h_scr, mem_scr, gate_scr, rank_scr, acc_scr, earlier_scr = rest
    _moe_routed_body(x_ref, mod_ref, wr_ref, br_ref, w1_ref, w3_ref, w2_ref, lng_ref, lnb_ref, y_ref,
                     None if split_ctx_tiles is None else y2_ref, h_scr, mem_scr, gate_scr, rank_scr, acc_scr,
                     earlier_scr, alpha=alpha, split_ctx_tiles=split_ctx_tiles)


def _moe_routed_body(x_ref, mod_ref, wr_ref, br_ref, w1_ref, w3_ref, w2_ref, lng_ref, lnb_ref, y_ref, y2_ref,
                     h_scr, mem_scr, gate_scr, rank_scr, acc_scr, earlier_scr, *, alpha, split_ctx_tiles):
    e = pl.program_id(1)
    m = mod_ref[0]
    t_tile = x_ref.shape[0]

    @pl.when(jnp.logical_and(pl.program_id(0) == 0, e == 0))
    def _():
        earlier_scr[...] = jnp.where(_iota((t_tile, t_tile), 0) < _iota((t_tile, t_tile), 1), 1.0, 0.0).astype(BF16)

    @pl.when(e == 0)
    def _():
        h = x_ref[...] * (1.0 + m[4:5]) + m[3:4]
        h_hi, h_lo = _split2(h)
        h_scr[...] = h_hi
        w_hi, w_lo = _split2(wr_ref[...])
        both = lax.dot_general(jnp.concatenate([w_hi, w_lo], axis=0), h_hi, NT, preferred_element_type=F32)
        lt = (both[:N_EXPERTS] + both[N_EXPERTS:] + lax.dot_general(w_hi, h_lo, NT, preferred_element_type=F32)
              + br_ref[...])
        eidx = _iota(lt.shape, 0)
        m1 = jnp.max(lt, 0, keepdims=True)
        i1 = jnp.min(jnp.where(lt == m1, eidx, N_EXPERTS), 0, keepdims=True)
        rest = jnp.where(eidx == i1, NEG_BIG, lt)
        m2 = jnp.max(rest, 0, keepdims=True)
        i2 = jnp.min(jnp.where(rest == m2, eidx, N_EXPERTS), 0, keepdims=True)
        e2 = jnp.exp(m2 - m1)
        g1 = 1.0 / (1.0 + e2)
        mem = jnp.where(eidx == i1, 1.0, jnp.where(eidx == i2, 1.0, 0.0))
        mem_scr[...] = mem
        gate_scr[...] = jnp.where(eidx == i1, g1, jnp.where(eidx == i2, e2 * g1, 0.0))
        rank_scr[...] = jnp.dot(mem.astype(BF16), earlier_scr[...], preferred_element_type=F32)
        acc_scr[...] = jnp.zeros_like(acc_scr)

    mem_row = mem_scr[pl.ds(e, 1), :]
    gate_row = gate_scr[pl.ds(e, 1), :]
    rank_row = rank_scr[pl.ds(e, 1), :]
    count = jnp.sum(mem_row).astype(jnp.int32)
    slot0 = _iota((MOE_BLOCK, t_tile), 0).astype(F32)

    def block(j, carry):
        slot = slot0 + (j * MOE_BLOCK).astype(F32)
        hit = jnp.where(rank_row == slot, mem_row, 0.0)
        sel = hit.astype(BF16)
        hb = jnp.dot(sel, h_scr[...], preferred_element_type=F32).astype(BF16)
        a = jnp.dot(hb, w1_ref[0], preferred_element_type=F32)
        g = jnp.dot(hb, w3_ref[0], preferred_element_type=F32)
        yb = jnp.dot((_silu(a) * g).astype(BF16), w2_ref[0], preferred_element_type=F32)
        gate_slot = jnp.sum(hit * gate_row, -1, keepdims=True)
        acc_scr[...] += lax.dot_general(sel, (yb * gate_slot).astype(BF16), TN, preferred_element_type=F32)
        return carry

    lax.fori_loop(0, (count + MOE_BLOCK - 1) // MOE_BLOCK, block, 0)

    @pl.when(e == N_EXPERTS - 1)
    def _():
        y = _layer_norm(alpha * x_ref[...] + m[5:6] * acc_scr[...], lng_ref[...], lnb_ref[...])
        if split_ctx_tiles is None:
            y_ref[...] = y
        else:
            @pl.when(pl.program_id(0) < split_ctx_tiles)
            def _():
                y_ref[...] = y

            @pl.when(pl.program_id(0) >= split_ctx_tiles)
            def _():
                y2_ref[...] = y


def _moe(x, mod_l, w_r, b_r, w1, w3, w2, ln_g, ln_b, mod_index_moe, alpha, split_ctx_tiles=None):
    n = x.shape[0]
    row = pl.BlockSpec((1, D_MODEL), lambda i, e: (0, 0))
    expert = lambda a: pl.BlockSpec((1,) + a.shape[1:], lambda i, e: (e, 0, 0))
    routing = pltpu.VMEM((N_EXPERTS, TM_MOE), F32)
    if split_ctx_tiles is None:
        out_shape = jax.ShapeDtypeStruct((n, D_MODEL), F32)
        out_specs = pl.BlockSpec((TM_MOE, D_MODEL), lambda i, e: (i, 0))
    else:
        n_ctx_rows = split_ctx_tiles * TM_MOE
        out_shape = [jax.ShapeDtypeStruct((n_ctx_rows, D_MODEL), F32), jax.ShapeDtypeStruct((n - n_ctx_rows, D_MODEL), F32)]
        out_specs = [pl.BlockSpec((TM_MOE, D_MODEL), lambda i, e: (jnp.minimum(i, split_ctx_tiles - 1), 0)),
                     pl.BlockSpec((TM_MOE, D_MODEL), lambda i, e: (jnp.maximum(i - split_ctx_tiles, 0), 0))]
    return pl.pallas_call(
        functools.partial(_moe_routed_kernel, alpha=alpha, split_ctx_tiles=split_ctx_tiles),
        out_shape=out_shape,
        grid=(n // TM_MOE, N_EXPERTS),
        in_specs=[pl.BlockSpec((TM_MOE, D_MODEL), lambda i, e: (i, 0)),
                  pl.BlockSpec((1, 6, D_MODEL), lambda i, e: (mod_index_moe(i), 0, 0)),
                  pl.BlockSpec((N_EXPERTS, D_MODEL), lambda i, e: (0, 0)),
                  pl.BlockSpec((N_EXPERTS, 1), lambda i, e: (0, 0)),
                  expert(w1), expert(w3), expert(w2), row, row],
        out_specs=out_specs,
        scratch_shapes=[pltpu.VMEM((TM_MOE, D_MODEL), BF16), routing, routing, routing,
                        pltpu.VMEM((TM_MOE, D_MODEL), F32), pltpu.VMEM((TM_MOE, TM_MOE), BF16)],
        compiler_params=pltpu.CompilerParams(dimension_semantics=("arbitrary", "arbitrary"),
                                             vmem_limit_bytes=VMEM_LIMIT),
        name="moe_ln",
    )(x, mod_l, w_r, b_r, w1, w3, w2, ln_g, ln_b)


def _rope_tables(lat_len):
    t = jnp.arange(lat_len)
    pos = jnp.stack([t // GRID_W, t % GRID_W], axis=1).astype(F32)
    half, quarter = HEAD_DIM // 2, HEAD_DIM // 4
    inv_freq = ROPE_THETA ** (-jnp.arange(quarter, dtype=F32) * 2.0 / half)
    lane = np.arange(HEAD_DIM)
    ang = pos[:, lane // half] * inv_freq[lane % quarter][None, :]
    sign = jnp.asarray(np.where((lane % half) < quarter, -1.0, 1.0), F32)
    cos = jnp.tile(jnp.cos(ang), (1, 128 // HEAD_DIM))
    sin = jnp.tile(jnp.sin(ang) * sign, (1, 128 // HEAD_DIM))
    return cos, sin


def kernel(x_prompt, x_sample, cache_k, cache_v, state_gdn, state_hgrn, c, c_ctx, w_mod, b_mod, w_in, conv_w, attn_sink, gdn_a_log, gdn_dt_bias, gdn_norm_w, hgrn_lb, hgrn_norm_w, w_out, ln_g, ln_b, ffn_w1, ffn_w3, ffn_w2, moe_router, moe_router_b, moe_w1, moe_w3, moe_w2):
    depth = w_in.shape[0]
    n_ctx, ctx_len, _ = x_prompt.shape
    lat_batch, lat_len, _ = x_sample.shape
    n_ctx_rows = n_ctx * ctx_len
    alpha = (2.0 * depth) ** 0.25

    x_parts = [x_prompt.reshape(-1, D_MODEL), x_sample.reshape(-1, D_MODEL)]

    def mod_index_for(tile):
        ctx_tiles, per_seq = n_ctx_rows // tile, lat_len // tile
        return lambda i: jnp.where(i < ctx_tiles, 0, 1 + (i - ctx_tiles) // per_seq)

    cvecs = jnp.concatenate([c_ctx[None], c], axis=0)
    mods = _modulation(cvecs, w_mod, b_mod).reshape(depth, cvecs.shape[0], 6, D_MODEL)

    cos_tab, sin_tab = _rope_tables(lat_len)
    sm = jax.nn.softmax(hgrn_lb.astype(F32), axis=0)
    lower = jnp.cumsum(sm, axis=0) - sm[0:1]

    w_out_b = w_out.astype(BF16)

    zeros_state = jnp.zeros((n_ctx, HEAD_DIM, D_REC), F32)
    sched = [_scan_schedule(n_ctx, ctx_len, lat_batch, lat_len, reverse) for reverse in (False, True)]
    new_k, new_v, new_sg, new_sh = [], [], [], []

    for l in range(depth):
        alog_row = jnp.zeros((1, 128), F32).at[0, 8:16].set(gdn_a_log[l].reshape(-1))
        dtb_row = jnp.zeros((1, 128), F32).at[0, 8:16].set(gdn_dt_bias[l].reshape(-1))
        qa, ka, va, qkv, g_b, hg, bl = _in_proj(x_parts, mods[l], w_in[l], cos_tab, sin_tab, conv_w[l], alog_row,
                                                dtb_row, mod_index_for(TM), n_ctx_rows // TM, ctx_len, lat_len)
        new_k.append(ka[:n_ctx_rows].reshape(n_ctx, ctx_len, A_KV_HEADS, HEAD_DIM))
        new_v.append(va[:n_ctx_rows].reshape(n_ctx, ctx_len, A_KV_HEADS, HEAD_DIM))

        o_a = _attention(qa, ka, va, cache_k[:, l].reshape(lat_batch, -1, D_KV), cache_v[:, l].reshape(lat_batch, -1, D_KV),
                         attn_sink[l], n_ctx, ctx_len, lat_batch, lat_len)

        sg0 = state_gdn[:, l].transpose(0, 1, 3, 2, 4).reshape(lat_batch, 2, HEAD_DIM, D_REC)
        sh0 = state_hgrn[:, l].transpose(0, 1, 4, 2, 3).reshape(lat_batch, 2, HEAD_DIM, D_REC)
        o_f, s_f, o_b, s_b = _gdn_scan_bidir(qkv, bl, jnp.concatenate([zeros_state, sg0[:, 0]], axis=0),
                                             jnp.concatenate([zeros_state, sg0[:, 1]], axis=0),
                                             jnp.concatenate(sched, axis=0))
        ob = [o_f, o_b]
        sg_fin = [s[:n_ctx].reshape(n_ctx, HEAD_DIM, N_HEADS_REC, HEAD_DIM).transpose(0, 2, 1, 3) for s in (s_f, s_b)]
        o_f, s_f, o_b, s_b = _hgrn_scan_bidir(hg, lower[l], jnp.concatenate([zeros_state, sh0[:, 0]], axis=0),
                                              jnp.concatenate([zeros_state, sh0[:, 1]], axis=0),
                                              jnp.concatenate(sched, axis=0))
        oc = [o_f, o_b]
        sh_fin = [s[:n_ctx].reshape(n_ctx, HEAD_DIM, N_HEADS_REC, HEAD_DIM).transpose(0, 2, 3, 1) for s in (s_f, s_b)]
        new_sg.append(jnp.stack(sg_fin, axis=1))
        new_sh.append(jnp.stack(sh_fin, axis=1))

        nw_b = jnp.tile(gdn_norm_w[l], N_HEADS_REC)[None]
        nw_c = jnp.tile(hgrn_norm_w[l], N_HEADS_REC)[None]
        x = _out_proj(x_parts, mods[l], o_a, ob[0], ob[1], oc[0], oc[1], g_b, hg, nw_b, nw_c, w_out_b[l],
                      ln_g[l, 0][None], ln_b[l, 0][None], mod_index_for(TM), alpha, n_ctx_rows // TM)

        i = l // 2
        if l % 2 == 0:
            x = _ffn(x, mods[l], ffn_w1[i].astype(BF16), ffn_w3[i].astype(BF16), ffn_w2[i].astype(BF16),
                     ln_g[l, 1][None], ln_b[l, 1][None], mod_index_for(TM_FFN), alpha)
        else:
            split = n_ctx_rows // TM_MOE if l == depth - 1 else None
            x = _moe(x, mods[l], moe_router[i].T, moe_router_b[i][:, None],
                     moe_w1[i].astype(BF16), moe_w3[i].astype(BF16), moe_w2[i].astype(BF16),
                     ln_g[l, 1][None], ln_b[l, 1][None], mod_index_for(TM_MOE), alpha, split)
        x_parts = [x]

    if isinstance(x, (list, tuple)):
        y_prompt, y_sample = x[0].reshape(x_prompt.shape), x[1].reshape(x_sample.shape)
    else:
        y_prompt = x[:n_ctx_rows].reshape(x_prompt.shape)
        y_sample = x[n_ctx_rows:].reshape(x_sample.shape)
    return (y_prompt, y_sample, jnp.stack(new_k, axis=1), jnp.stack(new_v, axis=1),
            jnp.stack(new_sg, axis=1), jnp.stack(new_sh, axis=1))
```

```python
import functools

import numpy as np
import jax
import jax.numpy as jnp
from jax import lax
from jax.experimental import pallas as pl
from jax.experimental.pallas import tpu as pltpu

F32 = jnp.float32
BF16 = jnp.bfloat16

D_MODEL = 1024
HEAD_DIM = 64
A_HEADS = 8
A_KV_HEADS = 2
GRID_W = 64
ATTN_BLOCK = 128
N_HEADS_REC = 4
D_REC = N_HEADS_REC * HEAD_DIM
CONV_W = 5
CHUNK = 64
N_LEVELS = 6
D_A = A_HEADS * HEAD_DIM
D_KV = A_KV_HEADS * HEAD_DIM
D_FF = 2816
N_EXPERTS = 8
EXPERT_FF = 1024
ROPE_THETA = 10000.0
NORM_EPS = 1e-5
TINY = 1e-30
NEG_BIG = -1e30

C_QA, C_KA, C_VA, C_QKVB, C_GB, C_HG = 0, 512, 640, 768, 1536, 1792
C_BA_END = C_HG + 16
D_HG = 1280
HALO = 8

TM = 512
TS = 256
TM_FFN = 512
FF_CHUNK = 256
TM_MOE = 1024
MOE_BLOCK = 288
VMEM_LIMIT = 56 * 1024 * 1024

NN = (((1,), (0,)), ((), ()))
NT = (((1,), (1,)), ((), ()))
TN = (((0,), (0,)), ((), ()))


def _dot(a, b, dims=NN):
    return lax.dot_general(a.astype(BF16), b.astype(BF16), dims, preferred_element_type=F32)


def _split2(x):
    hi = x.astype(BF16)
    lo = (x - hi.astype(F32)).astype(BF16)
    return hi, lo


def _split3(x):
    hi = x.astype(BF16)
    r = x - hi.astype(F32)
    mid = r.astype(BF16)
    lo = (r - mid.astype(F32)).astype(BF16)
    return hi, mid, lo


def _dot_sel(sel, x, dims=NN, sel_is_lhs=True):
    out = None
    for part in _split3(x):
        a, b = (sel, part) if sel_is_lhs else (part, sel)
        t = lax.dot_general(a, b, dims, preferred_element_type=F32)
        out = t if out is None else out + t
    return out


def _sigmoid(x):
    return 1.0 / (1.0 + jnp.exp(-x))


def _silu(x):
    return x * _sigmoid(x)


def _layer_norm(x, g, b):
    mu = jnp.mean(x, -1, keepdims=True)
    xc = x - mu
    var = jnp.mean(xc * xc, -1, keepdims=True)
    return xc * lax.rsqrt(var + NORM_EPS) * g + b


def _iota(shape, dim):
    return lax.broadcasted_iota(jnp.int32, shape, dim)


def _head_ones():
    return jnp.where(_iota((D_REC, D_REC), 0) // HEAD_DIM == _iota((D_REC, D_REC), 1) // HEAD_DIM, 1.0, 0.0).astype(BF16)


def _block_diag(x):
    head = _iota(x.shape, 1) // HEAD_DIM
    return jnp.concatenate([jnp.where(head == h, x, jnp.zeros_like(x)) for h in range(N_HEADS_REC)], axis=0)


def _block_diag_mask():
    return _iota((D_REC, D_REC), 0) // HEAD_DIM == _iota((D_REC, D_REC), 1) // HEAD_DIM


def _collapse_block_diag(s):
    out = s[0:HEAD_DIM]
    for h in range(1, N_HEADS_REC):
        out = out + s[h * HEAD_DIM:(h + 1) * HEAD_DIM]
    return out


def _mod_kernel(ct_ref, w_ref, b_ref, o_ref):
    s = _silu(ct_ref[...])
    w = w_ref[0]
    rows = [jnp.sum(s[:, r:r + 1] * w, axis=0, keepdims=True) for r in range(o_ref.shape[1])]
    o_ref[0] = jnp.concatenate(rows, axis=0) + b_ref[0]


def _modulation(cvecs, w_mod, b_mod):
    depth, d, n6 = w_mod.shape
    r = cvecs.shape[0]
    tn = 1024
    ct = jnp.zeros((d, 8), F32).at[:, :r].set(cvecs.T)
    return pl.pallas_call(
        _mod_kernel,
        out_shape=jax.ShapeDtypeStruct((depth, r, n6), F32),
        grid=(depth, n6 // tn),
        in_specs=[pl.BlockSpec((d, 8), lambda l, j: (0, 0)),
                  pl.BlockSpec((1, d, tn), lambda l, j: (l, 0, j)),
                  pl.BlockSpec((1, 1, tn), lambda l, j: (l, 0, j))],
        out_specs=pl.BlockSpec((1, r, tn), lambda l, j: (l, 0, j)),
        compiler_params=pltpu.CompilerParams(dimension_semantics=("arbitrary", "arbitrary"),
                                             vmem_limit_bytes=VMEM_LIMIT),
        name="modulation",
    )(ct, w_mod, b_mod.reshape(depth, 1, n6))


def _rope(x, cos, sin):
    outs = []
    lane = _iota(cos.shape, 1)
    first = (lane % 32) < 16
    for m in range(x.shape[1] // 128):
        xs = x[:, m * 128:(m + 1) * 128]
        swapped = jnp.where(first, pltpu.roll(xs, 128 - 16, 1), pltpu.roll(xs, 16, 1))
        outs.append(xs * cos + swapped * sin)
    return outs[0] if len(outs) == 1 else jnp.concatenate(outs, axis=1)


def _token_specs(x_parts, n_ctx_tiles):
    if len(x_parts) == 1:
        return [pl.BlockSpec((TM, D_MODEL), lambda i: (i, 0))]
    return [pl.BlockSpec((TM, D_MODEL), lambda i: (jnp.minimum(i, n_ctx_tiles - 1), 0)),
            pl.BlockSpec((TM, D_MODEL), lambda i: (jnp.maximum(i - n_ctx_tiles, 0), 0))]


def _read_tokens(x_refs, n_ctx_tiles):
    if len(x_refs) == 1:
        return x_refs[0][...]
    return jnp.where(pl.program_id(0) < n_ctx_tiles, x_refs[0][...], x_refs[1][...])


def _halo_specs(x_parts, n_ctx_tiles):
    per = TM // HALO
    specs = []
    for p, part in enumerate(x_parts):
        last_block = part.shape[0] // HALO - 1
        if len(x_parts) == 1:
            local = lambda i: i
        elif p == 0:
            local = lambda i: jnp.minimum(i, n_ctx_tiles - 1)
        else:
            local = lambda i: jnp.maximum(i - n_ctx_tiles, 0)
        specs.append(pl.BlockSpec((HALO, D_MODEL), lambda i, local=local: (jnp.maximum(local(i) * per - 1, 0), 0)))
        specs.append(pl.BlockSpec((HALO, D_MODEL),
                                  lambda i, local=local, last_block=last_block:
                                  (jnp.minimum((local(i) + 1) * per, last_block), 0)))
    return specs


def _in_proj_kernel(*refs, n_ctx_tiles, n_x, ctx_len, lat_len):
    x_refs, halo_refs = refs[:n_x], refs[n_x:3 * n_x]
    (mod_ref, w_ref, whg_ref, wba_ref, cos_ref, sin_ref, convw_ref, alog_ref, dtb_ref,
     qa_ref, ka_ref, va_ref, qkv_ref, gb_ref, hg_ref, bl_ref) = refs[3 * n_x:]
    i = pl.program_id(0)
    in_ctx = i < n_ctx_tiles
    m = mod_ref[0]
    h = (_read_tokens(x_refs, n_ctx_tiles) * (1.0 + m[1:2]) + m[0:1]).astype(BF16)

    def proj(lo, hi):
        return jnp.dot(h, w_ref[:, lo:hi], preferred_element_type=F32)

    if n_x == 1:
        x_halo = jnp.concatenate([halo_refs[0][...], halo_refs[1][...]], axis=0)
    else:
        x_halo = jnp.concatenate([jnp.where(in_ctx, halo_refs[0][...], halo_refs[2][...]),
                                  jnp.where(in_ctx, halo_refs[1][...], halo_refs[3][...])], axis=0)
    h_halo = (x_halo * (1.0 + m[1:2]) + m[0:1]).astype(BF16)
    p_halo = jnp.dot(h_halo, w_ref[:, C_QKVB:C_GB], preferred_element_type=F32)
    xh = jnp.concatenate([p_halo[:HALO], proj(C_QKVB, C_GB), p_halo[HALO:]], axis=0)

    r = _iota((TM, 1), 0)
    pos_ctx = (i * TM + r) % ctx_len
    pos_lat = ((i - n_ctx_tiles) * TM + r) % lat_len
    cw = convw_ref[...]
    y = None
    for t in range(CONV_W):
        lo = HALO - CONV_W // 2 + t
        term = xh[lo:lo + TM] * cw[t:t + 1]
        off = t - CONV_W // 2
        if off != 0:
            ok_ctx = jnp.logical_and(pos_ctx + off >= 0, pos_ctx + off < ctx_len)
            ok_lat = jnp.logical_and(pos_lat + off >= 0, pos_lat + off < lat_len)
            keep = jnp.where(in_ctx, jnp.where(ok_ctx, 1.0, 0.0), jnp.where(ok_lat, 1.0, 0.0))
            term = term * keep
        y = term if y is None else y + term
        if t == 1:
            qa_ref[...] = proj(C_QA, C_KA)
        if t == 3:
            hg_ref[:, :2 * D_REC] = jnp.dot(h, whg_ref[:, :2 * D_REC], preferred_element_type=F32)
    ka_ref[...] = proj(C_KA, C_VA)
    va_ref[...] = proj(C_VA, C_QKVB)
    y = _silu(y)
    hg_ref[:, 2 * D_REC:4 * D_REC] = jnp.dot(h, whg_ref[:, 2 * D_REC:4 * D_REC], preferred_element_type=F32)
    q, k, v = y[:, :D_REC], y[:, D_REC:2 * D_REC], y[:, 2 * D_REC:]
    ones = _head_ones()
    qq_hi, qq_lo = _split2(q * q)
    kk_hi, kk_lo = _split2(k * k)
    sums = jnp.dot(jnp.concatenate([qq_hi, qq_lo, kk_hi, kk_lo], axis=0), ones, preferred_element_type=F32)
    hg_ref[:, 4 * D_REC:] = jnp.dot(h, whg_ref[:, 4 * D_REC:], preferred_element_type=F32)
    gb_ref[...] = proj(C_GB, C_HG)
    ba = jnp.dot(h, wba_ref[...], preferred_element_type=F32)
    q_ss = sums[0:TM] + sums[TM:2 * TM]
    k_ss = sums[2 * TM:3 * TM] + sums[3 * TM:]
    qkv_ref[:, :D_REC] = q * lax.rsqrt(q_ss + 1e-6) * (HEAD_DIM ** -0.5)
    qkv_ref[:, D_REC:2 * D_REC] = k * lax.rsqrt(k_ss + 1e-6)
    qkv_ref[:, 2 * D_REC:] = v

    za = ba + dtb_ref[...]
    softplus = jnp.maximum(za, 0.0) + jnp.log(1.0 + jnp.exp(-jnp.abs(za)))
    la = -jnp.exp(alog_ref[...]) * softplus
    bl_ref[...] = jnp.where(_iota(ba.shape, 1) < 2 * N_HEADS_REC, _sigmoid(ba), la)

    @pl.when(i >= n_ctx_tiles)
    def _():
        cos, sin = cos_ref[...], sin_ref[...]
        qa_ref[...] = _rope(qa_ref[...], cos, sin)
        ka_ref[...] = _rope(ka_ref[...], cos, sin)


def _in_proj(x_parts, mod_l, w_in_l, cos_tab, sin_tab, conv_w, alog_row, dtb_row, mod_index, n_ctx_tiles,
             ctx_len, lat_len):
    n = sum(p.shape[0] for p in x_parts)
    lat_tiles_per_seq = lat_len // TM
    w_main = w_in_l[:, :C_HG].astype(BF16)
    w_hg = w_in_l[:, C_BA_END:].astype(BF16)
    w_ba = jnp.pad(w_in_l[:, C_HG:C_BA_END], ((0, 0), (0, 128 - (C_BA_END - C_HG)))).astype(BF16)
    whole = lambda a: pl.BlockSpec(a.shape, lambda i: (0,) * a.ndim)

    def tab_index(i):
        return (jnp.where(i < n_ctx_tiles, 0, (i - n_ctx_tiles) % lat_tiles_per_seq), 0)

    def out(width):
        return jax.ShapeDtypeStruct((n, width), F32), pl.BlockSpec((TM, width), lambda i: (i, 0))

    outs = [out(D_A), out(D_KV), out(D_KV), out(3 * D_REC), out(D_REC), out(D_HG), out(128)]
    return pl.pallas_call(
        functools.partial(_in_proj_kernel, n_ctx_tiles=n_ctx_tiles, n_x=len(x_parts), ctx_len=ctx_len,
                          lat_len=lat_len),
        out_shape=[o[0] for o in outs],
        grid=(n // TM,),
        in_specs=_token_specs(x_parts, n_ctx_tiles) + _halo_specs(x_parts, n_ctx_tiles) + [
            pl.BlockSpec((1, 6, D_MODEL), lambda i: (mod_index(i), 0, 0)),
            whole(w_main), whole(w_hg), whole(w_ba),
            pl.BlockSpec((TM, 128), tab_index),
            pl.BlockSpec((TM, 128), tab_index),
            whole(conv_w), whole(alog_row), whole(dtb_row)],
        out_specs=[o[1] for o in outs],
        compiler_params=pltpu.CompilerParams(dimension_semantics=("arbitrary",), vmem_limit_bytes=VMEM_LIMIT),
        name="in_proj",
    )(*x_parts, *[p for part in x_parts for p in (part, part)], mod_l, w_main, w_hg, w_ba, cos_tab, sin_tab,
      conv_w, alog_row, dtb_row)


def _attn_heads(q_ref, k, v, bias, sink_ref, o_ref):
    k_same, k_swap = k.astype(BF16), pltpu.roll(k, HEAD_DIM, 1).astype(BF16)
    v_same, v_swap = v.astype(BF16), pltpu.roll(v, HEAD_DIM, 1).astype(BF16)
    nq = q_ref.shape[0]
    low_half = _iota((nq, 128), 1) < HEAD_DIM
    rep = A_HEADS // A_KV_HEADS
    for g in range(A_KV_HEADS):
        heads = range(g * rep, (g + 1) * rep)
        scores, exps, dens, outs = {}, {}, {}, {}
        for hq in heads:
            p, e = hq // 2, hq % 2
            q2 = q_ref[:, p * 128:(p + 1) * 128] * (HEAD_DIM ** -0.5)
            qm = jnp.where(low_half if e == 0 else jnp.logical_not(low_half), q2, 0.0)
            s = _dot(qm, k_same if e == g else k_swap, NT)
            if bias is not None:
                b = ATTN_BLOCK
                s = jnp.concatenate([s[:, :b] + bias[0], s[:, b:2 * b], s[:, 2 * b:3 * b] + bias[1], s[:, 3 * b:]], axis=1)
            scores[hq] = s
        for hq in heads:
            snk = sink_ref[hq]
            mx = jnp.maximum(jnp.max(scores[hq], -1, keepdims=True), snk)
            exps[hq] = jnp.exp(scores[hq] - mx)
            dens[hq] = jnp.sum(exps[hq], -1, keepdims=True) + jnp.exp(snk - mx)
        for hq in heads:
            outs[hq] = _dot(exps[hq], v_same if hq % 2 == g else v_swap) / dens[hq]
        for p in range(g * rep // 2, (g + 1) * rep // 2):
            o_ref[:, p * 128:(p + 1) * 128] = jnp.where(low_half, outs[2 * p], outs[2 * p + 1])


def _attn_kernel(q_ref, kx_ref, vx_ref, kp_ref, kc_ref, kn_ref, vp_ref, vc_ref, vn_ref, ck_ref, cv_ref, sink_ref,
                 o_ref, *, ctx_steps, n_blocks):
    i = pl.program_id(0)

    @pl.when(i < ctx_steps)
    def _():
        _attn_heads(q_ref, kx_ref[...], vx_ref[...], None, sink_ref, o_ref)

    @pl.when(i >= ctx_steps)
    def _():
        n = (i - ctx_steps) % n_blocks
        k = jnp.concatenate([kp_ref[...], kc_ref[...], kn_ref[...], ck_ref[0]], axis=0)
        v = jnp.concatenate([vp_ref[...], vc_ref[...], vn_ref[...], cv_ref[0]], axis=0)
        qi = _iota((ATTN_BLOCK, ATTN_BLOCK), 0)
        jj = _iota((ATTN_BLOCK, ATTN_BLOCK), 1)
        bias_prev = jnp.where(jj >= qi, 0.0, NEG_BIG) + jnp.where(n > 0, 0.0, NEG_BIG)
        bias_next = jnp.where(jj <= qi, 0.0, NEG_BIG) + jnp.where(n < n_blocks - 1, 0.0, NEG_BIG)
        _attn_heads(q_ref, k, v, (bias_prev, bias_next), sink_ref, o_ref)


def _attention(qa, ka, va, ck, cv, sink, n_ctx, ctx_len, lat_batch, lat_len):
    n = qa.shape[0]
    per_seq = ctx_len // ATTN_BLOCK
    ctx_steps = n_ctx * per_seq
    nb = lat_len // ATTN_BLOCK
    past = ck.shape[1]

    def lat(i):
        return jnp.maximum(i - ctx_steps, 0)

    def cur(i):
        return (ctx_steps + lat(i), 0)

    def prev(i):
        return (ctx_steps + (lat(i) // nb) * nb + jnp.maximum(lat(i) % nb - 1, 0), 0)

    def nxt(i):
        return (ctx_steps + (lat(i) // nb) * nb + jnp.minimum(lat(i) % nb + 1, nb - 1), 0)

    def own_seq(i):
        return (jnp.minimum(i // per_seq, n_ctx - 1), 0)

    kv = lambda f: pl.BlockSpec((ATTN_BLOCK, D_KV), f)
    cache = pl.BlockSpec((1, past, D_KV), lambda i: (lat(i) // nb, 0, 0))
    return pl.pallas_call(
        functools.partial(_attn_kernel, ctx_steps=ctx_steps, n_blocks=nb),
        out_shape=jax.ShapeDtypeStruct((n, D_A), F32),
        grid=(ctx_steps + lat_batch * nb,),
        in_specs=[pl.BlockSpec((ATTN_BLOCK, D_A), lambda i: (i, 0)),
                  pl.BlockSpec((ctx_len, D_KV), own_seq), pl.BlockSpec((ctx_len, D_KV), own_seq),
                  kv(prev), kv(cur), kv(nxt), kv(prev), kv(cur), kv(nxt), cache, cache,
                  pl.BlockSpec(memory_space=pltpu.SMEM)],
        out_specs=pl.BlockSpec((ATTN_BLOCK, D_A), lambda i: (i, 0)),
        compiler_params=pltpu.CompilerParams(dimension_semantics=("arbitrary",), vmem_limit_bytes=VMEM_LIMIT),
        name="attention",
    )(qa, ka, va, ka, ka, ka, va, va, va, ck, cv, sink)


def _tri(reverse, strict=False):
    i, j = _iota((CHUNK, CHUNK), 0), _iota((CHUNK, CHUNK), 1)
    if strict:
        return (i < j) if reverse else (i > j)
    return (i <= j) if reverse else (i >= j)


def _scan_schedule(n_ctx, ctx_len, lat_batch, lat_len, reverse):
    tiles_ctx, tiles_lat = ctx_len // TS, lat_len // TS
    seqs = [[s * tiles_ctx + t for t in range(tiles_ctx)] for s in range(n_ctx)]
    seqs += [[n_ctx * tiles_ctx + b * tiles_lat + t for t in range(tiles_lat)] for b in range(lat_batch)]
    cols = []
    for s, tiles in enumerate(seqs):
        order = tiles[::-1] if reverse else tiles
        for n, t in enumerate(order):
            cols.append((t, s, int(n == 0), int(n == len(order) - 1)))
    return jnp.asarray(np.array(cols, np.int32).T)


def _mm3_heads(lhs_list, rhs):
    r_hi, r_lo = _split2(rhs)
    w_hi, w_lo = _block_diag(r_hi), _block_diag(r_lo)
    parts = [_split2(a) for a in lhs_list]
    his = [p[0] for p in parts]
    los = [p[1] for p in parts]
    n = len(lhs_list) * CHUNK
    big = jnp.dot(jnp.concatenate(his + los, axis=0), w_hi, preferred_element_type=F32)
    small = jnp.dot(his[0] if len(his) == 1 else jnp.concatenate(his, axis=0), w_lo, preferred_element_type=F32)
    out = big[:n] + big[n:] + small
    return [out[m * CHUNK:(m + 1) * CHUNK] for m in range(len(lhs_list))]


def _pick_heads(vals):
    shape = (CHUNK, D_REC)
    head = _iota(shape, 1) // HEAD_DIM
    out = jnp.broadcast_to(vals[-1], shape)
    for h in range(N_HEADS_REC - 2, -1, -1):
        out = jnp.where(head == h, jnp.broadcast_to(vals[h], shape), out)
    return out


def _gdn_bidir_kernel(sch_ref, qkvf_ref, blf_ref, qkvb_ref, blb_ref, s0f_ref, s0b_ref,
                      of_ref, sff_ref, ob_ref, sfb_ref, s_scr, *sets):
    n = pl.program_id(0)
    n_chunks = TS // CHUNK
    prev = jnp.maximum(n - 1, 0)
    chain_live = n >= 1
    half = len(sets) // 2
    lanes = ((False, qkvf_ref, blf_ref, s0f_ref, of_ref, sff_ref), (True, qkvb_ref, blb_ref, s0b_ref, ob_ref, sfb_ref))

    @pl.when(n == 0)
    def _():
        for ref in sets:
            ref[...] = jnp.zeros(ref.shape, ref.dtype)
        s_scr[...] = jnp.zeros(s_scr.shape, F32)

    for ln, lane_refs in enumerate(lanes):
        @pl.when(jnp.logical_and(chain_live, sch_ref[4 * ln + 2, prev] == 1))
        def _(ln=ln, s0_ref=lane_refs[3]):
            s_scr[ln] = _block_diag(s0_ref[0])

    row, lane = _iota((CHUNK, D_REC), 0), _iota((CHUNK, D_REC), 1) % HEAD_DIM
    eye = jnp.where(row == lane, 1.0, 0.0)
    ones8 = jnp.ones((8, CHUNK), BF16)
    bd_mask = _block_diag_mask()

    def step(wr, rd):
        u_w, wq_w, at_w, kd_w, et_w = wr
        u_r, wq_r, at_r, kd_r, et_r = rd
        state = [s_scr[0], s_scr[1]]
        inst = []

        def chain(m):
            for ln, (reverse, _, _, _, o_ref, _) in enumerate(lanes):
                c = (n_chunks - 1 - m) if reverse else m
                rs = slice(c * CHUNK, (c + 1) * CHUNK)
                k = ln * n_chunks + m
                s = state[ln]
                ws_qs = _dot(wq_r[k], s)
                v_new = u_r[k] - ws_qs[:CHUNK]
                o_ref[rs, :] = ws_qs[CHUNK:] + _dot(at_r[k], _block_diag(v_new.astype(BF16)))
                state[ln] = s * et_r[k][0:1] + jnp.where(bd_mask, _dot(kd_r[k], v_new, TN), 0.0)

        for ln, (reverse, qkv_ref, bl_ref, _, _, _) in enumerate(lanes):
            incl = (row <= lane) if reverse else (row >= lane)
            tri = jnp.where(_tri(reverse), 1.0, 0.0).astype(BF16)
            t_before_j = (row >= lane) if reverse else (row <= lane)
            last = 0 if reverse else CHUNK - 1
            cb, ca = ln * N_HEADS_REC, 2 * N_HEADS_REC + ln * N_HEADS_REC
            for m in range(n_chunks):
                c = (n_chunks - 1 - m) if reverse else m
                rs = slice(c * CHUNK, (c + 1) * CHUNK)
                bl = bl_ref[rs, :]
                d = dict(q=qkv_ref[rs, :D_REC], k=qkv_ref[rs, D_REC:2 * D_REC], v=qkv_ref[rs, 2 * D_REC:],
                         strict=(row < lane) if reverse else (row > lane), slot=ln * n_chunks + m)
                d["beta"] = _pick_heads([bl[:, cb + h:cb + h + 1] for h in range(N_HEADS_REC)])
                la = _pick_heads([bl[:, ca + h:ca + h + 1] for h in range(N_HEADS_REC)])
                g_i, g_j = None, None
                for p_i, p_j in zip(_split3(la), _split3(jnp.where(t_before_j, la, 0.0))):
                    a = jnp.dot(tri, p_i, preferred_element_type=F32)
                    b = jnp.dot(ones8, p_j, preferred_element_type=F32)
                    g_i = a if g_i is None else g_i + a
                    g_j = b if g_j is None else g_j + b
                d["g_i"] = g_i
                d["g_tot"] = g_i[last:last + 1]
                d["decay"] = jnp.where(incl, jnp.exp(jnp.where(incl, g_i - g_j[0:1], 0.0)), 0.0)
                inst.append(d)
        chain(0)

        for d in inst:
            gram = _dot(jnp.concatenate([d["k"], d["q"]], axis=0), _block_diag(d["k"].astype(BF16)), NT)
            d["low"] = jnp.where(d["strict"], d["beta"] * gram[:CHUNK] * d["decay"], 0.0)
            d["attn"] = gram[CHUNK:] * d["decay"]

        for d in inst:
            d["t"] = eye - d["low"]
            (d["p"],) = _mm3_heads([d["low"]], d["low"])
        for it in range(4):
            for d in inst:
                p_next, tp = _mm3_heads([d["p"], d["t"]], d["p"])
                d["t"] = d["t"] + tp
                d["p"] = p_next
            if 1 + it < n_chunks:
                chain(1 + it)
        for d in inst:
            k = d["slot"]
            t_mat = d["t"] + _mm3_heads([d["t"]], d["p"])[0]
            eg = jnp.exp(d["g_i"])
            vb = d["v"] * d["beta"]
            kbg = d["k"] * d["beta"] * eg
            u_w[k] = _dot(t_mat, _block_diag(vb.astype(BF16)))
            wq_w[k] = jnp.concatenate([_dot(t_mat, _block_diag(kbg.astype(BF16))), d["q"] * eg], axis=0).astype(BF16)
            at_w[k] = d["attn"]
            kd_w[k] = (d["k"] * jnp.exp(d["g_tot"] - d["g_i"])).astype(BF16)
            et_w[k] = jnp.broadcast_to(jnp.exp(d["g_tot"]), (8, D_REC))

        s_scr[0] = state[0]
        s_scr[1] = state[1]

    @pl.when(n % 2 == 0)
    def _():
        step(sets[:half], sets[half:])

    @pl.when(n % 2 == 1)
    def _():
        step(sets[half:], sets[:half])

    for ln, lane_refs in enumerate(lanes):
        @pl.when(jnp.logical_and(chain_live, sch_ref[4 * ln + 3, prev] == 1))
        def _(ln=ln, sfin_ref=lane_refs[5]):
            sfin_ref[0] = _collapse_block_diag(s_scr[ln])


def _gdn_scan_bidir(qkv, bl, s0_f, s0_b, sched):
    n = qkv.shape[0]
    n_seq = s0_f.shape[0]
    n_steps = sched.shape[1]
    n_inst = 2 * (TS // CHUNK)
    ahead = lambda i: jnp.minimum(i, n_steps - 1)
    behind = lambda i: jnp.maximum(i - 1, 0)
    scratch_set = [pltpu.VMEM((n_inst, CHUNK, D_REC), F32), pltpu.VMEM((n_inst, 2 * CHUNK, D_REC), BF16),
                   pltpu.VMEM((n_inst, CHUNK, D_REC), F32), pltpu.VMEM((n_inst, CHUNK, D_REC), BF16),
                   pltpu.VMEM((n_inst, 8, D_REC), F32)]

    def tile_in(width, ln):
        return pl.BlockSpec((TS, width), lambda i, sch: (sch[4 * ln, ahead(i)], 0))

    def tile_out(ln):
        return pl.BlockSpec((TS, D_REC), lambda i, sch: (sch[4 * ln, behind(i)], 0))

    def state_spec(ln):
        return pl.BlockSpec((1, HEAD_DIM, D_REC), lambda i, sch: (sch[4 * ln + 1, behind(i)], 0, 0))

    grid_spec = pltpu.PrefetchScalarGridSpec(
        num_scalar_prefetch=1,
        grid=(n_steps + 1,),
        in_specs=[tile_in(3 * D_REC, 0), tile_in(128, 0), tile_in(3 * D_REC, 1), tile_in(128, 1),
                  state_spec(0), state_spec(1)],
        out_specs=[tile_out(0), state_spec(0), tile_out(1), state_spec(1)],
        scratch_shapes=[pltpu.VMEM((2, D_REC, D_REC), F32)] + scratch_set + scratch_set,
    )
    o_sds = jax.ShapeDtypeStruct((n, D_REC), F32)
    s_sds = jax.ShapeDtypeStruct((n_seq, HEAD_DIM, D_REC), F32)
    return pl.pallas_call(
        _gdn_bidir_kernel,
        out_shape=[o_sds, s_sds, o_sds, s_sds],
        grid_spec=grid_spec,
        compiler_params=pltpu.CompilerParams(dimension_semantics=("arbitrary",), vmem_limit_bytes=VMEM_LIMIT),
        name="gdn_scan",
    )(sched, qkv, bl, qkv, bl, s0_f, s0_b)


def _hgrn_bidir_kernel(sch_ref, ff_ref, if_ref, qf_ref, fb_ref, ib_ref, qb_ref, lb_ref, sel_ref, s0f_ref, s0b_ref,
                       of_ref, sff_ref, ob_ref, sfb_ref, s_scr):
    n = pl.program_id(0)
    n_chunks = TS // CHUNK
    lanes = ((False, ff_ref, if_ref, qf_ref, s0f_ref, of_ref, sff_ref), (True, fb_ref, ib_ref, qb_ref, s0b_ref, ob_ref, sfb_ref))

    for ln, lane_refs in enumerate(lanes):
        @pl.when(sch_ref[4 * ln + 2, n] == 1)
        def _(ln=ln, s0_ref=lane_refs[4]):
            s_scr[ln] = _block_diag(s0_ref[0])

    row = _iota((CHUNK, D_REC), 0)
    col_tok = _iota((CHUNK, D_REC), 1) % HEAD_DIM
    ones = _head_ones()
    bd_mask = _block_diag_mask()

    inst = []
    for ln, (reverse, f_ref, i_ref, q_ref, _, o_ref, _) in enumerate(lanes):
        lb = lb_ref[ln:ln + 1, :]
        sel = sel_ref[ln]
        for m in range(n_chunks):
            c = (n_chunks - 1 - m) if reverse else m
            rs = slice(c * CHUNK, (c + 1) * CHUNK)
            f = lb + (1.0 - lb) * _sigmoid(f_ref[rs, :])
            d = dict(rs=rs, kk=1.0 - f, q=q_ref[rs, :], v=i_ref[rs, :], reverse=reverse, ln=ln, o_ref=o_ref)
            cums = _dot_sel(sel, jnp.log(jnp.maximum(f, TINY)))
            d["b"] = cums[:CHUNK]
            d["b_mid_small"] = [cums[CHUNK:2 * CHUNK], cums[2 * CHUNK:]]
            inst.append(d)

    for lvl in range(N_LEVELS):
        s_half = (CHUNK // 2) >> lvl
        in_block = row % (2 * s_half)
        same_block = (row // (2 * s_half)) == (col_tok // (2 * s_half))
        for d in inst:
            b, reverse = d["b"], d["reverse"]
            later = (in_block < s_half) if reverse else (in_block >= s_half)
            if s_half >= 4:
                pieces = []
                for blk in range(CHUNK // (2 * s_half)):
                    r = blk * 2 * s_half + (s_half if reverse else s_half - 1)
                    pieces.append(jnp.broadcast_to(b[r:r + 1], (2 * s_half, D_REC)))
                b_mid = pieces[0] if len(pieces) == 1 else jnp.concatenate(pieces, axis=0)
            else:
                b_mid = d["b_mid_small"][lvl - (N_LEVELS - 2)]
            e = jnp.exp(-jnp.abs(b - b_mid))
            q_hat = jnp.where(later, d["q"] * e, 0.0)
            k_hat = jnp.where(later, 0.0, d["kk"] * e)
            res = _dot(q_hat, _block_diag(k_hat.astype(BF16)), NT)
            d["a"] = res if lvl == 0 else d["a"] + jnp.where(same_block, res, 0.0)

    for d in inst:
        sums = _dot(d["q"] * d["kk"], ones)
        d["a"] = d["a"] + jnp.where(col_tok == row, sums, 0.0)

    for d in inst:
        b = d["b"]
        last = 0 if d["reverse"] else CHUNK - 1
        b_last = b[last:last + 1]
        d["av"] = _dot(d["a"], _block_diag(d["v"].astype(BF16)))
        d["kv"] = jnp.where(bd_mask, _dot(d["v"], d["kk"] * jnp.exp(b_last - b), TN), 0.0)
        d["qb"] = (d["q"] * jnp.exp(b)).astype(BF16)
        d["e_last"] = jnp.exp(b_last)

    state = [s_scr[0], s_scr[1]]
    for d in inst:
        s = state[d["ln"]]
        d["o_ref"][d["rs"], :] = d["av"] + _dot(d["qb"], s, NT)
        state[d["ln"]] = s * d["e_last"] + d["kv"]
    s_scr[0] = state[0]
    s_scr[1] = state[1]

    for ln, lane_refs in enumerate(lanes):
        @pl.when(sch_ref[4 * ln + 3, n] == 1)
        def _(ln=ln, sfin_ref=lane_refs[6]):
            sfin_ref[0] = _collapse_block_diag(s_scr[ln])


def _hgrn_scan_bidir(hg, lb_e, s0_f, s0_b, sched):
    n = hg.shape[0]
    n_seq = s0_f.shape[0]
    i = np.arange(CHUNK)
    sels = []
    for reverse in (False, True):
        tri = (i[None, :] >= i[:, None]) if reverse else (i[None, :] <= i[:, None])
        rows = [tri]
        for s_half in (2, 1):
            mid = (i // (2 * s_half)) * (2 * s_half) + (s_half if reverse else s_half - 1)
            rows.append(tri[mid])
        sels.append(np.concatenate(rows, axis=0))
    sel = jnp.asarray(np.stack(sels).astype(np.float32), BF16)

    def col(c, ln):
        return pl.BlockSpec((TS, D_REC), lambda i, sch: (sch[4 * ln, i], c))

    def state_spec(ln):
        return pl.BlockSpec((1, HEAD_DIM, D_REC), lambda i, sch: (sch[4 * ln + 1, i], 0, 0))

    def tile_out(ln):
        return pl.BlockSpec((TS, D_REC), lambda i, sch: (sch[4 * ln, i], 0))

    grid_spec = pltpu.PrefetchScalarGridSpec(
        num_scalar_prefetch=1,
        grid=(sched.shape[1],),
        in_specs=[col(0, 0), col(2, 0), col(3, 0), col(1, 1), col(2, 1), col(3, 1),
                  pl.BlockSpec((2, D_REC), lambda i, sch: (0, 0)),
                  pl.BlockSpec((2, 3 * CHUNK, CHUNK), lambda i, sch: (0, 0, 0)),
                  state_spec(0), state_spec(1)],
        out_specs=[tile_out(0), state_spec(0), tile_out(1), state_spec(1)],
        scratch_shapes=[pltpu.VMEM((2, D_REC, D_REC), F32)],
    )
    o_sds = jax.ShapeDtypeStruct((n, D_REC), F32)
    s_sds = jax.ShapeDtypeStruct((n_seq, HEAD_DIM, D_REC), F32)
    return pl.pallas_call(
        _hgrn_bidir_kernel,
        out_shape=[o_sds, s_sds, o_sds, s_sds],
        grid_spec=grid_spec,
        compiler_params=pltpu.CompilerParams(dimension_semantics=("arbitrary",), vmem_limit_bytes=VMEM_LIMIT),
        name="hgrn_scan",
    )(sched, hg, hg, hg, hg, hg, hg, lb_e, sel, s0_f, s0_b)


def _gated_readout(o, gate, w_e, ones):
    ms = _dot_sel(ones, o * o, sel_is_lhs=False) * (1.0 / HEAD_DIM)
    return o * lax.rsqrt(ms + NORM_EPS) * w_e * _silu(gate)


def _out_proj_kernel(*refs, alpha, n_ctx_tiles, n_x):
    x_refs = refs[:n_x]
    (mod_ref, oa_ref, bf_ref, bb_ref, cf_ref, cb_ref, gb_ref, gc_ref, nwb_ref, nwc_ref,
     w_ref, lng_ref, lnb_ref, y_ref) = refs[n_x:]
    x = _read_tokens(x_refs, n_ctx_tiles)
    m = mod_ref[0]
    ones = _head_ones()
    o_b = _gated_readout(bf_ref[...] + bb_ref[...], gb_ref[...], nwb_ref[...], ones)
    o_c = _gated_readout(cf_ref[...] + cb_ref[...], gc_ref[...], nwc_ref[...], ones)
    mix = jnp.dot(oa_ref[...].astype(BF16), w_ref[0:D_A], preferred_element_type=F32)
    mix = mix + jnp.dot(o_b.astype(BF16), w_ref[D_A:D_A + D_REC], preferred_element_type=F32)
    mix = mix + jnp.dot(o_c.astype(BF16), w_ref[D_A + D_REC:], preferred_element_type=F32)
    y_ref[...] = _layer_norm(alpha * x + m[2:3] * mix, lng_ref[...], lnb_ref[...])


def _out_proj(x_parts, mod_l, o_a, ob_f, ob_b, oc_f, oc_b, g_b, hg, nw_b, nw_c, w_out, ln_g, ln_b, mod_index, alpha,
              n_ctx_tiles):
    n = sum(p.shape[0] for p in x_parts)
    tile = lambda w: pl.BlockSpec((TM, w), lambda i: (i, 0))
    row = lambda w: pl.BlockSpec((1, w), lambda i: (0, 0))
    return pl.pallas_call(
        functools.partial(_out_proj_kernel, alpha=alpha, n_ctx_tiles=n_ctx_tiles, n_x=len(x_parts)),
        out_shape=jax.ShapeDtypeStruct((n, D_MODEL), F32),
        grid=(n // TM,),
        in_specs=_token_specs(x_parts, n_ctx_tiles) + [
                  pl.BlockSpec((1, 6, D_MODEL), lambda i: (mod_index(i), 0, 0)),
                  tile(D_A), tile(D_REC), tile(D_REC), tile(D_REC), tile(D_REC), tile(D_REC),
                  pl.BlockSpec((TM, D_REC), lambda i: (i, 4)),
                  row(D_REC), row(D_REC),
                  pl.BlockSpec((D_MODEL, D_MODEL), lambda i: (0, 0)),
                  row(D_MODEL), row(D_MODEL)],
        out_specs=tile(D_MODEL),
        compiler_params=pltpu.CompilerParams(dimension_semantics=("arbitrary",), vmem_limit_bytes=VMEM_LIMIT),
        name="out_proj_ln",
    )(*x_parts, mod_l, o_a, ob_f, ob_b, oc_f, oc_b, g_b, hg, nw_b, nw_c, w_out, ln_g, ln_b)


def _ffn_kernel(x_ref, mod_ref, w1_ref, w3_ref, w2_ref, lng_ref, lnb_ref, y_ref, *, alpha):
    m = mod_ref[0]
    x = x_ref[...]
    h = (x * (1.0 + m[4:5]) + m[3:4]).astype(BF16)
    acc = jnp.zeros((x.shape[0], D_MODEL), F32)
    for c in range(D_FF // FF_CHUNK):
        sl = slice(c * FF_CHUNK, (c + 1) * FF_CHUNK)
        a = jnp.dot(h, w1_ref[:, sl], preferred_element_type=F32)
        g = jnp.dot(h, w3_ref[:, sl], preferred_element_type=F32)
        acc = acc + jnp.dot((_silu(a) * g).astype(BF16), w2_ref[sl, :], preferred_element_type=F32)
    y_ref[...] = _layer_norm(alpha * x + m[5:6] * acc, lng_ref[...], lnb_ref[...])


def _ffn(x, mod_l, w1, w3, w2, ln_g, ln_b, mod_index_ffn, alpha):
    n = x.shape[0]
    whole = lambda a: pl.BlockSpec(a.shape, lambda i: (0, 0), pipeline_mode=pl.Buffered(1))
    row = pl.BlockSpec((1, D_MODEL), lambda i: (0, 0))
    return pl.pallas_call(
        functools.partial(_ffn_kernel, alpha=alpha),
        out_shape=jax.ShapeDtypeStruct((n, D_MODEL), F32),
        grid=(n // TM_FFN,),
        in_specs=[pl.BlockSpec((TM_FFN, D_MODEL), lambda i: (i, 0)),
                  pl.BlockSpec((1, 6, D_MODEL), lambda i: (mod_index_ffn(i), 0, 0)),
                  whole(w1), whole(w3), whole(w2), row, row],
        out_specs=pl.BlockSpec((TM_FFN, D_MODEL), lambda i: (i, 0)),
        compiler_params=pltpu.CompilerParams(dimension_semantics=("arbitrary",), vmem_limit_bytes=VMEM_LIMIT),
        name="ffn_ln",
    )(x, mod_l, w1, w3, w2, ln_g, ln_b)


def _moe_routed_kernel(x_ref, mod_ref, wr_ref, br_ref, w1_ref, w3_ref, w2_ref, lng_ref, lnb_ref, *rest,
                       alpha, split_ctx_tiles):
    if split_ctx_tiles is None:
        y_ref, h_scr, mem_scr, gate_scr, rank_scr, acc_scr, earlier_scr = rest
    else:
        y_ref, y2_ref, h_scr, mem_scr, gate_scr, rank_scr, acc_scr, earlier_scr = rest
    _moe_routed_body(x_ref, mod_ref, wr_ref, br_ref, w1_ref, w3_ref, w2_ref, lng_ref, lnb_ref, y_ref,
                     None if split_ctx_tiles is None else y2_ref, h_scr, mem_scr, gate_scr, rank_scr, acc_scr,
                     earlier_scr, alpha=alpha, split_ctx_tiles=split_ctx_tiles)


def _moe_routed_body(x_ref, mod_ref, wr_ref, br_ref, w1_ref, w3_ref, w2_ref, lng_ref, lnb_ref, y_ref, y2_ref,
                     h_scr, mem_scr, gate_scr, rank_scr, acc_scr, earlier_scr, *, alpha, split_ctx_tiles):
    e = pl.program_id(1)
    m = mod_ref[0]
    t_tile = x_ref.shape[0]

    @pl.when(jnp.logical_and(pl.program_id(0) == 0, e == 0))
    def _():
        earlier_scr[...] = jnp.where(_iota((t_tile, t_tile), 0) < _iota((t_tile, t_tile), 1), 1.0, 0.0).astype(BF16)

    @pl.when(e == 0)
    def _():
        h = x_ref[...] * (1.0 + m[4:5]) + m[3:4]
        h_hi, h_lo = _split2(h)
        h_scr[...] = h_hi
        w_hi, w_lo = _split2(wr_ref[...])
        both = lax.dot_general(jnp.concatenate([w_hi, w_lo], axis=0), h_hi, NT, preferred_element_type=F32)
        lt = (both[:N_EXPERTS] + both[N_EXPERTS:] + lax.dot_general(w_hi, h_lo, NT, preferred_element_type=F32)
              + br_ref[...])
        eidx = _iota(lt.shape, 0)
        m1 = jnp.max(lt, 0, keepdims=True)
        i1 = jnp.min(jnp.where(lt == m1, eidx, N_EXPERTS), 0, keepdims=True)
        rest = jnp.where(eidx == i1, NEG_BIG, lt)
        m2 = jnp.max(rest, 0, keepdims=True)
        i2 = jnp.min(jnp.where(rest == m2, eidx, N_EXPERTS), 0, keepdims=True)
        e2 = jnp.exp(m2 - m1)
        g1 = 1.0 / (1.0 + e2)
        mem = jnp.where(eidx == i1, 1.0, jnp.where(eidx == i2, 1.0, 0.0))
        mem_scr[...] = mem
        gate_scr[...] = jnp.where(eidx == i1, g1, jnp.where(eidx == i2, e2 * g1, 0.0))
        rank_scr[...] = jnp.dot(mem.astype(BF16), earlier_scr[...], preferred_element_type=F32)
        acc_scr[...] = jnp.zeros_like(acc_scr)

    mem_row = mem_scr[pl.ds(e, 1), :]
    gate_row = gate_scr[pl.ds(e, 1), :]
    rank_row = rank_scr[pl.ds(e, 1), :]
    count = jnp.sum(mem_row).astype(jnp.int32)
    slot0 = _iota((MOE_BLOCK, t_tile), 0).astype(F32)

    def block(j, carry):
        slot = slot0 + (j * MOE_BLOCK).astype(F32)
        hit = jnp.where(rank_row == slot, mem_row, 0.0)
        sel = hit.astype(BF16)
        hb = jnp.dot(sel, h_scr[...], preferred_element_type=F32).astype(BF16)
        a = jnp.dot(hb, w1_ref[0], preferred_element_type=F32)
        g = jnp.dot(hb, w3_ref[0], preferred_element_type=F32)
        yb = jnp.dot((_silu(a) * g).astype(BF16), w2_ref[0], preferred_element_type=F32)
        gate_slot = jnp.sum(hit * gate_row, -1, keepdims=True)
        acc_scr[...] += lax.dot_general(sel, (yb * gate_slot).astype(BF16), TN, preferred_element_type=F32)
        return carry

    lax.fori_loop(0, (count + MOE_BLOCK - 1) // MOE_BLOCK, block, 0)

    @pl.when(e == N_EXPERTS - 1)
    def _():
        y = _layer_norm(alpha * x_ref[...] + m[5:6] * acc_scr[...], lng_ref[...], lnb_ref[...])
        if split_ctx_tiles is None:
            y_ref[...] = y
        else:
            @pl.when(pl.program_id(0) < split_ctx_tiles)
            def _():
                y_ref[...] = y

            @pl.when(pl.program_id(0) >= split_ctx_tiles)
            def _():
                y2_ref[...] = y


def _moe(x, mod_l, w_r, b_r, w1, w3, w2, ln_g, ln_b, mod_index_moe, alpha, split_ctx_tiles=None):
    n = x.shape[0]
    row = pl.BlockSpec((1, D_MODEL), lambda i, e: (0, 0))
    expert = lambda a: pl.BlockSpec((1,) + a.shape[1:], lambda i, e: (e, 0, 0))
    routing = pltpu.VMEM((N_EXPERTS, TM_MOE), F32)
    if split_ctx_tiles is None:
        out_shape = jax.ShapeDtypeStruct((n, D_MODEL), F32)
        out_specs = pl.BlockSpec((TM_MOE, D_MODEL), lambda i, e: (i, 0))
    else:
        n_ctx_rows = split_ctx_tiles * TM_MOE
        out_shape = [jax.ShapeDtypeStruct((n_ctx_rows, D_MODEL), F32), jax.ShapeDtypeStruct((n - n_ctx_rows, D_MODEL), F32)]
        out_specs = [pl.BlockSpec((TM_MOE, D_MODEL), lambda i, e: (jnp.minimum(i, split_ctx_tiles - 1), 0)),
                     pl.BlockSpec((TM_MOE, D_MODEL), lambda i, e: (jnp.maximum(i - split_ctx_tiles, 0), 0))]
    return pl.pallas_call(
        functools.partial(_moe_routed_kernel, alpha=alpha, split_ctx_tiles=split_ctx_tiles),
        out_shape=out_shape,
        grid=(n // TM_MOE, N_EXPERTS),
        in_specs=[pl.BlockSpec((TM_MOE, D_MODEL), lambda i, e: (i, 0)),
                  pl.BlockSpec((1, 6, D_MODEL), lambda i, e: (mod_index_moe(i), 0, 0)),
                  pl.BlockSpec((N_EXPERTS, D_MODEL), lambda i, e: (0, 0)),
                  pl.BlockSpec((N_EXPERTS, 1), lambda i, e: (0, 0)),
                  expert(w1), expert(w3), expert(w2), row, row],
        out_specs=out_specs,
        scratch_shapes=[pltpu.VMEM((TM_MOE, D_MODEL), BF16), routing, routing, routing,
                        pltpu.VMEM((TM_MOE, D_MODEL), F32), pltpu.VMEM((TM_MOE, TM_MOE), BF16)],
        compiler_params=pltpu.CompilerParams(dimension_semantics=("arbitrary", "arbitrary"),
                                             vmem_limit_bytes=VMEM_LIMIT),
        name="moe_ln",
    )(x, mod_l, w_r, b_r, w1, w3, w2, ln_g, ln_b)


def _rope_tables(lat_len):
    t = jnp.arange(lat_len)
    pos = jnp.stack([t // GRID_W, t % GRID_W], axis=1).astype(F32)
    half, quarter = HEAD_DIM // 2, HEAD_DIM // 4
    inv_freq = ROPE_THETA ** (-jnp.arange(quarter, dtype=F32) * 2.0 / half)
    lane = np.arange(HEAD_DIM)
    ang = pos[:, lane // half] * inv_freq[lane % quarter][None, :]
    sign = jnp.asarray(np.where((lane % half) < quarter, -1.0, 1.0), F32)
    cos = jnp.tile(jnp.cos(ang), (1, 128 // HEAD_DIM))
    sin = jnp.tile(jnp.sin(ang) * sign, (1, 128 // HEAD_DIM))
    return cos, sin


def kernel(x_prompt, x_sample, cache_k, cache_v, state_gdn, state_hgrn, c, c_ctx, w_mod, b_mod, w_in, conv_w, attn_sink, gdn_a_log, gdn_dt_bias, gdn_norm_w, hgrn_lb, hgrn_norm_w, w_out, ln_g, ln_b, ffn_w1, ffn_w3, ffn_w2, moe_router, moe_router_b, moe_w1, moe_w3, moe_w2):
    depth = w_in.shape[0]
    n_ctx, ctx_len, _ = x_prompt.shape
    lat_batch, lat_len, _ = x_sample.shape
    n_ctx_rows = n_ctx * ctx_len
    alpha = (2.0 * depth) ** 0.25

    x_parts = [x_prompt.reshape(-1, D_MODEL), x_sample.reshape(-1, D_MODEL)]

    def mod_index_for(tile):
        ctx_tiles, per_seq = n_ctx_rows // tile, lat_len // tile
        return lambda i: jnp.where(i < ctx_tiles, 0, 1 + (i - ctx_tiles) // per_seq)

    cvecs = jnp.concatenate([c_ctx[None], c], axis=0)
    mods = _modulation(cvecs, w_mod, b_mod).reshape(depth, cvecs.shape[0], 6, D_MODEL)

    cos_tab, sin_tab = _rope_tables(lat_len)
    sm = jax.nn.softmax(hgrn_lb.astype(F32), axis=0)
    lower = jnp.cumsum(sm, axis=0) - sm[0:1]

    w_out_b = w_out.astype(BF16)

    zeros_state = jnp.zeros((n_ctx, HEAD_DIM, D_REC), F32)
    sched = [_scan_schedule(n_ctx, ctx_len, lat_batch, lat_len, reverse) for reverse in (False, True)]
    new_k, new_v, new_sg, new_sh = [], [], [], []

    for l in range(depth):
        alog_row = jnp.zeros((1, 128), F32).at[0, 8:16].set(gdn_a_log[l].reshape(-1))
        dtb_row = jnp.zeros((1, 128), F32).at[0, 8:16].set(gdn_dt_bias[l].reshape(-1))
        qa, ka, va, qkv, g_b, hg, bl = _in_proj(x_parts, mods[l], w_in[l], cos_tab, sin_tab, conv_w[l], alog_row,
                                                dtb_row, mod_index_for(TM), n_ctx_rows // TM, ctx_len, lat_len)
        new_k.append(ka[:n_ctx_rows].reshape(n_ctx, ctx_len, A_KV_HEADS, HEAD_DIM))
        new_v.append(va[:n_ctx_rows].reshape(n_ctx, ctx_len, A_KV_HEADS, HEAD_DIM))

        o_a = _attention(qa, ka, va, cache_k[:, l].reshape(lat_batch, -1, D_KV), cache_v[:, l].reshape(lat_batch, -1, D_KV),
                         attn_sink[l], n_ctx, ctx_len, lat_batch, lat_len)

        sg0 = state_gdn[:, l].transpose(0, 1, 3, 2, 4).reshape(lat_batch, 2, HEAD_DIM, D_REC)
        sh0 = state_hgrn[:, l].transpose(0, 1, 4, 2, 3).reshape(lat_batch, 2, HEAD_DIM, D_REC)
        o_f, s_f, o_b, s_b = _gdn_scan_bidir(qkv, bl, jnp.concatenate([zeros_state, sg0[:, 0]], axis=0),
                                             jnp.concatenate([zeros_state, sg0[:, 1]], axis=0),
                                             jnp.concatenate(sched, axis=0))
        ob = [o_f, o_b]
        sg_fin = [s[:n_ctx].reshape(n_ctx, HEAD_DIM, N_HEADS_REC, HEAD_DIM).transpose(0, 2, 1, 3) for s in (s_f, s_b)]
        o_f, s_f, o_b, s_b = _hgrn_scan_bidir(hg, lower[l], jnp.concatenate([zeros_state, sh0[:, 0]], axis=0),
                                              jnp.concatenate([zeros_state, sh0[:, 1]], axis=0),
                                              jnp.concatenate(sched, axis=0))
        oc = [o_f, o_b]
        sh_fin = [s[:n_ctx].reshape(n_ctx, HEAD_DIM, N_HEADS_REC, HEAD_DIM).transpose(0, 2, 3, 1) for s in (s_f, s_b)]
        new_sg.append(jnp.stack(sg_fin, axis=1))
        new_sh.append(jnp.stack(sh_fin, axis=1))

        nw_b = jnp.tile(gdn_norm_w[l], N_HEADS_REC)[None]
        nw_c = jnp.tile(hgrn_norm_w[l], N_HEADS_REC)[None]
        x = _out_proj(x_parts, mods[l], o_a, ob[0], ob[1], oc[0], oc[1], g_b, hg, nw_b, nw_c, w_out_b[l],
                      ln_g[l, 0][None], ln_b[l, 0][None], mod_index_for(TM), alpha, n_ctx_rows // TM)

        i = l // 2
        if l % 2 == 0:
            x = _ffn(x, mods[l], ffn_w1[i].astype(BF16), ffn_w3[i].astype(BF16), ffn_w2[i].astype(BF16),
                     ln_g[l, 1][None], ln_b[l, 1][None], mod_index_for(TM_FFN), alpha)
        else:
            split = n_ctx_rows // TM_MOE if l == depth - 1 else None
            x = _moe(x, mods[l], moe_router[i].T, moe_router_b[i][:, None],
                     moe_w1[i].astype(BF16), moe_w3[i].astype(BF16), moe_w2[i].astype(BF16),
                     ln_g[l, 1][None], ln_b[l, 1][None], mod_index_for(TM_MOE), alpha, split)
        x_parts = [x]

    if isinstance(x, (list, tuple)):
        y_prompt, y_sample = x[0].reshape(x_prompt.shape), x[1].reshape(x_sample.shape)
    else:
        y_prompt = x[:n_ctx_rows].reshape(x_prompt.shape)
        y_sample = x[n_ctx_rows:].reshape(x_sample.shape)
    return (y_prompt, y_sample, jnp.stack(new_k, axis=1), jnp.stack(new_v, axis=1),
            jnp.stack(new_sg, axis=1), jnp.stack(new_sh, axis=1))
```

```python
import functools

import numpy as np
import jax
import jax.numpy as jnp
from jax import lax
from jax.experimental import pallas as pl
from jax.experimental.pallas import tpu as pltpu

F32 = jnp.float32
BF16 = jnp.bfloat16

D_MODEL = 1024
HEAD_DIM = 64
A_HEADS = 8
A_KV_HEADS = 2
GRID_W = 64
ATTN_BLOCK = 128
ATTN_STAGE = 8
N_HEADS_REC = 4
D_REC = N_HEADS_REC * HEAD_DIM
CONV_W = 5
CHUNK = 64
N_LEVELS = 6
D_A = A_HEADS * HEAD_DIM
D_KV = A_KV_HEADS * HEAD_DIM
D_FF = 2816
N_EXPERTS = 8
EXPERT_FF = 1024
ROPE_THETA = 10000.0
NORM_EPS = 1e-5
TINY = 1e-30
NEG_BIG = -1e30

C_QA, C_KA, C_VA, C_QKVB, C_GB, C_HG = 0, 512, 640, 768, 1536, 1792
C_BA_END = C_HG + 16
D_HG = 1280
HALO = 8

TM = 512
TS = 256
TM_FFN = 512
FF_CHUNK = 256
TM_MOE = 1024
MOE_BLOCK = 128
VMEM_LIMIT = 56 * 1024 * 1024

NN = (((1,), (0,)), ((), ()))
NT = (((1,), (1,)), ((), ()))
TN = (((0,), (0,)), ((), ()))


def _dot(a, b, dims=NN):
    return lax.dot_general(a.astype(BF16), b.astype(BF16), dims, preferred_element_type=F32)


def _split2(x):
    hi = x.astype(BF16)
    lo = (x - hi.astype(F32)).astype(BF16)
    return hi, lo


def _split3(x):
    hi = x.astype(BF16)
    r = x - hi.astype(F32)
    mid = r.astype(BF16)
    lo = (r - mid.astype(F32)).astype(BF16)
    return hi, mid, lo


def _dot_sel(sel, x, dims=NN, sel_is_lhs=True):
    out = None
    for part in _split3(x):
        a, b = (sel, part) if sel_is_lhs else (part, sel)
        t = lax.dot_general(a, b, dims, preferred_element_type=F32)
        out = t if out is None else out + t
    return out


def _sigmoid(x):
    return 1.0 / (1.0 + jnp.exp(-x))


def _silu(x):
    return x * _sigmoid(x)


def _layer_norm(x, g, b):
    mu = jnp.mean(x, -1, keepdims=True)
    xc = x - mu
    var = jnp.mean(xc * xc, -1, keepdims=True)
    return xc * lax.rsqrt(var + NORM_EPS) * g + b


def _iota(shape, dim):
    return lax.broadcasted_iota(jnp.int32, shape, dim)


def _head_ones():
    return jnp.where(_iota((D_REC, D_REC), 0) // HEAD_DIM == _iota((D_REC, D_REC), 1) // HEAD_DIM, 1.0, 0.0).astype(BF16)


def _block_diag(x):
    head = _iota(x.shape, 1) // HEAD_DIM
    return jnp.concatenate([jnp.where(head == h, x, jnp.zeros_like(x)) for h in range(N_HEADS_REC)], axis=0)


def _block_diag_mask():
    return _iota((D_REC, D_REC), 0) // HEAD_DIM == _iota((D_REC, D_REC), 1) // HEAD_DIM


def _collapse_block_diag(s):
    out = s[0:HEAD_DIM]
    for h in range(1, N_HEADS_REC):
        out = out + s[h * HEAD_DIM:(h + 1) * HEAD_DIM]
    return out


def _mod_kernel(ct_ref, w_ref, b_ref, o_ref):
    s = _silu(ct_ref[...])
    w = w_ref[0]
    rows = [jnp.sum(s[:, r:r + 1] * w, axis=0, keepdims=True) for r in range(o_ref.shape[1])]
    o_ref[0] = jnp.concatenate(rows, axis=0) + b_ref[0]


def _modulation(cvecs, w_mod, b_mod):
    depth, d, n6 = w_mod.shape
    r = cvecs.shape[0]
    tn = 1024
    ct = jnp.zeros((d, 8), F32).at[:, :r].set(cvecs.T)
    return pl.pallas_call(
        _mod_kernel,
        out_shape=jax.ShapeDtypeStruct((depth, r, n6), F32),
        grid=(depth, n6 // tn),
        in_specs=[pl.BlockSpec((d, 8), lambda l, j: (0, 0)),
                  pl.BlockSpec((1, d, tn), lambda l, j: (l, 0, j)),
                  pl.BlockSpec((1, 1, tn), lambda l, j: (l, 0, j))],
        out_specs=pl.BlockSpec((1, r, tn), lambda l, j: (l, 0, j)),
        compiler_params=pltpu.CompilerParams(dimension_semantics=("arbitrary", "arbitrary"),
                                             vmem_limit_bytes=VMEM_LIMIT),
        name="modulation",
    )(ct, w_mod, b_mod.reshape(depth, 1, n6))


def _rope(x, cos, sin):
    outs = []
    lane = _iota(cos.shape, 1)
    first = (lane % 32) < 16
    for m in range(x.shape[1] // 128):
        xs = x[:, m * 128:(m + 1) * 128]
        swapped = jnp.where(first, pltpu.roll(xs, 128 - 16, 1), pltpu.roll(xs, 16, 1))
        outs.append(xs * cos + swapped * sin)
    return outs[0] if len(outs) == 1 else jnp.concatenate(outs, axis=1)


def _token_specs(x_parts, n_ctx_tiles):
    if len(x_parts) == 1:
        return [pl.BlockSpec((TM, D_MODEL), lambda i: (i, 0))]
    return [pl.BlockSpec((TM, D_MODEL), lambda i: (jnp.minimum(i, n_ctx_tiles - 1), 0)),
            pl.BlockSpec((TM, D_MODEL), lambda i: (jnp.maximum(i - n_ctx_tiles, 0), 0))]


def _read_tokens(x_refs, n_ctx_tiles):
    if len(x_refs) == 1:
        return x_refs[0][...]
    return jnp.where(pl.program_id(0) < n_ctx_tiles, x_refs[0][...], x_refs[1][...])


def _halo_specs(x_parts, n_ctx_tiles):
    per = TM // HALO
    specs = []
    for p, part in enumerate(x_parts):
        last_block = part.shape[0] // HALO - 1
        if len(x_parts) == 1:
            local = lambda i: i
        elif p == 0:
            local = lambda i: jnp.minimum(i, n_ctx_tiles - 1)
        else:
            local = lambda i: jnp.maximum(i - n_ctx_tiles, 0)
        specs.append(pl.BlockSpec((HALO, D_MODEL), lambda i, local=local: (jnp.maximum(local(i) * per - 1, 0), 0)))
        specs.append(pl.BlockSpec((HALO, D_MODEL),
                                  lambda i, local=local, last_block=last_block:
                                  (jnp.minimum((local(i) + 1) * per, last_block), 0)))
    return specs


def _in_proj_kernel(*refs, n_ctx_tiles, n_x, ctx_len, lat_len):
    x_refs, halo_refs = refs[:n_x], refs[n_x:3 * n_x]
    (mod_ref, w_ref, whg_ref, wba_ref, cos_ref, sin_ref, convw_ref, alog_ref, dtb_ref,
     qa_ref, ka_ref, va_ref, qkv_ref, gb_ref, hg_ref, bl_ref) = refs[3 * n_x:]
    i = pl.program_id(0)
    in_ctx = i < n_ctx_tiles
    m = mod_ref[0]
    h = (_read_tokens(x_refs, n_ctx_tiles) * (1.0 + m[1:2]) + m[0:1]).astype(BF16)

    def proj(lo, hi):
        return jnp.dot(h, w_ref[:, lo:hi], preferred_element_type=F32)

    if n_x == 1:
        x_halo = jnp.concatenate([halo_refs[0][...], halo_refs[1][...]], axis=0)
    else:
        x_halo = jnp.concatenate([jnp.where(in_ctx, halo_refs[0][...], halo_refs[2][...]),
                                  jnp.where(in_ctx, halo_refs[1][...], halo_refs[3][...])], axis=0)
    h_halo = (x_halo * (1.0 + m[1:2]) + m[0:1]).astype(BF16)
    p_halo = jnp.dot(h_halo, w_ref[:, C_QKVB:C_GB], preferred_element_type=F32)
    xh = jnp.concatenate([p_halo[:HALO], proj(C_QKVB, C_GB), p_halo[HALO:]], axis=0)

    r = _iota((TM, 1), 0)
    pos_ctx = (i * TM + r) % ctx_len
    pos_lat = ((i - n_ctx_tiles) * TM + r) % lat_len
    cw = convw_ref[...]
    y = None
    for t in range(CONV_W):
        lo = HALO - CONV_W // 2 + t
        term = xh[lo:lo + TM] * cw[t:t + 1]
        off = t - CONV_W // 2
        if off != 0:
            ok_ctx = jnp.logical_and(pos_ctx + off >= 0, pos_ctx + off < ctx_len)
            ok_lat = jnp.logical_and(pos_lat + off >= 0, pos_lat + off < lat_len)
            keep = jnp.where(in_ctx, jnp.where(ok_ctx, 1.0, 0.0), jnp.where(ok_lat, 1.0, 0.0))
            term = term * keep
        y = term if y is None else y + term
        if t == 1:
            qa_ref[...] = proj(C_QA, C_KA)
        if t == 3:
            hg_ref[:, :2 * D_REC] = jnp.dot(h, whg_ref[:, :2 * D_REC], preferred_element_type=F32)
    ka_ref[...] = proj(C_KA, C_VA)
    va_ref[...] = proj(C_VA, C_QKVB)
    y = _silu(y)
    hg_ref[:, 2 * D_REC:4 * D_REC] = jnp.dot(h, whg_ref[:, 2 * D_REC:4 * D_REC], preferred_element_type=F32)
    q, k, v = y[:, :D_REC], y[:, D_REC:2 * D_REC], y[:, 2 * D_REC:]
    ones = _head_ones()
    qq_hi, qq_lo = _split2(q * q)
    kk_hi, kk_lo = _split2(k * k)
    sums = jnp.dot(jnp.concatenate([qq_hi, qq_lo, kk_hi, kk_lo], axis=0), ones, preferred_element_type=F32)
    hg_ref[:, 4 * D_REC:] = jnp.dot(h, whg_ref[:, 4 * D_REC:], preferred_element_type=F32)
    gb_ref[...] = proj(C_GB, C_HG)
    ba = jnp.dot(h, wba_ref[...], preferred_element_type=F32)
    q_ss = sums[0:TM] + sums[TM:2 * TM]
    k_ss = sums[2 * TM:3 * TM] + sums[3 * TM:]
    qkv_ref[:, :D_REC] = q * lax.rsqrt(q_ss + 1e-6) * (HEAD_DIM ** -0.5)
    qkv_ref[:, D_REC:2 * D_REC] = k * lax.rsqrt(k_ss + 1e-6)
    qkv_ref[:, 2 * D_REC:] = v

    za = ba + dtb_ref[...]
    softplus = jnp.maximum(za, 0.0) + jnp.log(1.0 + jnp.exp(-jnp.abs(za)))
    la = -jnp.exp(alog_ref[...]) * softplus
    bl_ref[...] = jnp.where(_iota(ba.shape, 1) < 2 * N_HEADS_REC, _sigmoid(ba), la)

    @pl.when(i >= n_ctx_tiles)
    def _():
        cos, sin = cos_ref[...], sin_ref[...]
        qa_ref[...] = _rope(qa_ref[...], cos, sin)
        ka_ref[...] = _rope(ka_ref[...], cos, sin)


def _in_proj(x_parts, mod_l, w_in_l, cos_tab, sin_tab, conv_w, alog_row, dtb_row, mod_index, n_ctx_tiles,
             ctx_len, lat_len):
    n = sum(p.shape[0] for p in x_parts)
    lat_tiles_per_seq = lat_len // TM
    w_main = w_in_l[:, :C_HG].astype(BF16)
    w_hg = w_in_l[:, C_BA_END:].astype(BF16)
    w_ba = jnp.pad(w_in_l[:, C_HG:C_BA_END], ((0, 0), (0, 128 - (C_BA_END - C_HG)))).astype(BF16)
    whole = lambda a: pl.BlockSpec(a.shape, lambda i: (0,) * a.ndim)

    def tab_index(i):
        return (jnp.where(i < n_ctx_tiles, 0, (i - n_ctx_tiles) % lat_tiles_per_seq), 0)

    def out(width):
        return jax.ShapeDtypeStruct((n, width), F32), pl.BlockSpec((TM, width), lambda i: (i, 0))

    outs = [out(D_A), out(D_KV), out(D_KV), out(3 * D_REC), out(D_REC), out(D_HG), out(128)]
    return pl.pallas_call(
        functools.partial(_in_proj_kernel, n_ctx_tiles=n_ctx_tiles, n_x=len(x_parts), ctx_len=ctx_len,
                          lat_len=lat_len),
        out_shape=[o[0] for o in outs],
        grid=(n // TM,),
        in_specs=_token_specs(x_parts, n_ctx_tiles) + _halo_specs(x_parts, n_ctx_tiles) + [
            pl.BlockSpec((1, 6, D_MODEL), lambda i: (mod_index(i), 0, 0)),
            whole(w_main), whole(w_hg), whole(w_ba),
            pl.BlockSpec((TM, 128), tab_index),
            pl.BlockSpec((TM, 128), tab_index),
            whole(conv_w), whole(alog_row), whole(dtb_row)],
        out_specs=[o[1] for o in outs],
        compiler_params=pltpu.CompilerParams(dimension_semantics=("arbitrary",), vmem_limit_bytes=VMEM_LIMIT),
        name="in_proj",
    )(*x_parts, *[p for part in x_parts for p in (part, part)], mod_l, w_main, w_hg, w_ba, cos_tab, sin_tab,
      conv_w, alog_row, dtb_row)


def _attn_heads(q_ref, k, v, bias, sink_ref, o_ref):
    k_same, k_swap = k.astype(BF16), pltpu.roll(k, HEAD_DIM, 1).astype(BF16)
    v_same, v_swap = v.astype(BF16), pltpu.roll(v, HEAD_DIM, 1).astype(BF16)
    nq = q_ref.shape[0]
    low_half = _iota((nq, 128), 1) < HEAD_DIM
    rep = A_HEADS // A_KV_HEADS
    for first in range(0, A_HEADS, ATTN_STAGE):
        heads = range(first, first + ATTN_STAGE)
        scores, exps, dens, outs = {}, {}, {}, {}
        for hq in heads:
            p, e, g = hq // 2, hq % 2, hq // rep
            q2 = q_ref[:, p * 128:(p + 1) * 128] * (HEAD_DIM ** -0.5)
            qm = jnp.where(low_half if e == 0 else jnp.logical_not(low_half), q2, 0.0)
            s = _dot(qm, k_same if e == g else k_swap, NT)
            if bias is not None:
                b = ATTN_BLOCK
                s = jnp.concatenate([s[:, :b] + bias[0], s[:, b:2 * b], s[:, 2 * b:3 * b] + bias[1], s[:, 3 * b:]], axis=1)
            scores[hq] = s
        for hq in heads:
            snk = sink_ref[hq]
            mx = jnp.maximum(jnp.max(scores[hq], -1, keepdims=True), snk)
            exps[hq] = jnp.exp(scores[hq] - mx)
            dens[hq] = jnp.sum(exps[hq], -1, keepdims=True) + jnp.exp(snk - mx)
        for hq in heads:
            outs[hq] = _dot(exps[hq], v_same if hq % 2 == hq // rep else v_swap) / dens[hq]
        for p in range(first // 2, (first + ATTN_STAGE) // 2):
            o_ref[:, p * 128:(p + 1) * 128] = jnp.where(low_half, outs[2 * p], outs[2 * p + 1])


def _attn_kernel(q_ref, kx_ref, vx_ref, kp_ref, kc_ref, kn_ref, vp_ref, vc_ref, vn_ref, ck_ref, cv_ref, sink_ref,
                 o_ref, *, ctx_steps, n_blocks):
    i = pl.program_id(0)

    @pl.when(i < ctx_steps)
    def _():
        _attn_heads(q_ref, kx_ref[...], vx_ref[...], None, sink_ref, o_ref)

    @pl.when(i >= ctx_steps)
    def _():
        n = (i - ctx_steps) % n_blocks
        k = jnp.concatenate([kp_ref[...], kc_ref[...], kn_ref[...], ck_ref[0]], axis=0)
        v = jnp.concatenate([vp_ref[...], vc_ref[...], vn_ref[...], cv_ref[0]], axis=0)
        qi = _iota((ATTN_BLOCK, ATTN_BLOCK), 0)
        jj = _iota((ATTN_BLOCK, ATTN_BLOCK), 1)
        bias_prev = jnp.where(jj >= qi, 0.0, NEG_BIG) + jnp.where(n > 0, 0.0, NEG_BIG)
        bias_next = jnp.where(jj <= qi, 0.0, NEG_BIG) + jnp.where(n < n_blocks - 1, 0.0, NEG_BIG)
        _attn_heads(q_ref, k, v, (bias_prev, bias_next), sink_ref, o_ref)


def _attention(qa, ka, va, ck, cv, sink, n_ctx, ctx_len, lat_batch, lat_len):
    n = qa.shape[0]
    per_seq = ctx_len // ATTN_BLOCK
    ctx_steps = n_ctx * per_seq
    nb = lat_len // ATTN_BLOCK
    past = ck.shape[1]

    def lat(i):
        return jnp.maximum(i - ctx_steps, 0)

    def cur(i):
        return (ctx_steps + lat(i), 0)

    def prev(i):
        return (ctx_steps + (lat(i) // nb) * nb + jnp.maximum(lat(i) % nb - 1, 0), 0)

    def nxt(i):
        return (ctx_steps + (lat(i) // nb) * nb + jnp.minimum(lat(i) % nb + 1, nb - 1), 0)

    def own_seq(i):
        return (jnp.minimum(i // per_seq, n_ctx - 1), 0)

    kv = lambda f: pl.BlockSpec((ATTN_BLOCK, D_KV), f)
    cache = pl.BlockSpec((1, past, D_KV), lambda i: (lat(i) // nb, 0, 0))
    return pl.pallas_call(
        functools.partial(_attn_kernel, ctx_steps=ctx_steps, n_blocks=nb),
        out_shape=jax.ShapeDtypeStruct((n, D_A), F32),
        grid=(ctx_steps + lat_batch * nb,),
        in_specs=[pl.BlockSpec((ATTN_BLOCK, D_A), lambda i: (i, 0)),
                  pl.BlockSpec((ctx_len, D_KV), own_seq), pl.BlockSpec((ctx_len, D_KV), own_seq),
                  kv(prev), kv(cur), kv(nxt), kv(prev), kv(cur), kv(nxt), cache, cache,
                  pl.BlockSpec(memory_space=pltpu.SMEM)],
        out_specs=pl.BlockSpec((ATTN_BLOCK, D_A), lambda i: (i, 0)),
        compiler_params=pltpu.CompilerParams(dimension_semantics=("arbitrary",), vmem_limit_bytes=VMEM_LIMIT),
        name="attention",
    )(qa, ka, va, ka, ka, ka, va, va, va, ck, cv, sink)


def _tri(reverse, strict=False):
    i, j = _iota((CHUNK, CHUNK), 0), _iota((CHUNK, CHUNK), 1)
    if strict:
        return (i < j) if reverse else (i > j)
    return (i <= j) if reverse else (i >= j)


def _scan_schedule(n_ctx, ctx_len, lat_batch, lat_len, reverse):
    tiles_ctx, tiles_lat = ctx_len // TS, lat_len // TS
    seqs = [[s * tiles_ctx + t for t in range(tiles_ctx)] for s in range(n_ctx)]
    seqs += [[n_ctx * tiles_ctx + b * tiles_lat + t for t in range(tiles_lat)] for b in range(lat_batch)]
    cols = []
    for s, tiles in enumerate(seqs):
        order = tiles[::-1] if reverse else tiles
        for n, t in enumerate(order):
            cols.append((t, s, int(n == 0), int(n == len(order) - 1)))
    return jnp.asarray(np.array(cols, np.int32).T)


def _mm3_heads(lhs_list, rhs):
    r_hi, r_lo = _split2(rhs)
    w_hi, w_lo = _block_diag(r_hi), _block_diag(r_lo)
    parts = [_split2(a) for a in lhs_list]
    his = [p[0] for p in parts]
    los = [p[1] for p in parts]
    n = len(lhs_list) * CHUNK
    big = jnp.dot(jnp.concatenate(his + los, axis=0), w_hi, preferred_element_type=F32)
    small = jnp.dot(his[0] if len(his) == 1 else jnp.concatenate(his, axis=0), w_lo, preferred_element_type=F32)
    out = big[:n] + big[n:] + small
    return [out[m * CHUNK:(m + 1) * CHUNK] for m in range(len(lhs_list))]


def _pick_heads(vals):
    shape = (CHUNK, D_REC)
    head = _iota(shape, 1) // HEAD_DIM
    out = jnp.broadcast_to(vals[-1], shape)
    for h in range(N_HEADS_REC - 2, -1, -1):
        out = jnp.where(head == h, jnp.broadcast_to(vals[h], shape), out)
    return out


def _gdn_bidir_kernel(sch_ref, qkvf_ref, blf_ref, qkvb_ref, blb_ref, s0f_ref, s0b_ref,
                      of_ref, sff_ref, ob_ref, sfb_ref, s_scr, *sets):
    n = pl.program_id(0)
    n_chunks = TS // CHUNK
    prev = jnp.maximum(n - 1, 0)
    chain_live = n >= 1
    half = len(sets) // 2
    lanes = ((False, qkvf_ref, blf_ref, s0f_ref, of_ref, sff_ref), (True, qkvb_ref, blb_ref, s0b_ref, ob_ref, sfb_ref))

    @pl.when(n == 0)
    def _():
        for ref in sets:
            ref[...] = jnp.zeros(ref.shape, ref.dtype)
        s_scr[...] = jnp.zeros(s_scr.shape, F32)

    for ln, lane_refs in enumerate(lanes):
        @pl.when(jnp.logical_and(chain_live, sch_ref[4 * ln + 2, prev] == 1))
        def _(ln=ln, s0_ref=lane_refs[3]):
            s_scr[ln] = _block_diag(s0_ref[0])

    row, lane = _iota((CHUNK, D_REC), 0), _iota((CHUNK, D_REC), 1) % HEAD_DIM
    eye = jnp.where(row == lane, 1.0, 0.0)
    ones8 = jnp.ones((8, CHUNK), BF16)
    bd_mask = _block_diag_mask()

    def step(wr, rd):
        u_w, wq_w, at_w, kd_w, et_w = wr
        u_r, wq_r, at_r, kd_r, et_r = rd
        state = [s_scr[0], s_scr[1]]
        inst = []

        def chain(m):
            for ln, (reverse, _, _, _, o_ref, _) in enumerate(lanes):
                c = (n_chunks - 1 - m) if reverse else m
                rs = slice(c * CHUNK, (c + 1) * CHUNK)
                k = ln * n_chunks + m
                s = state[ln]
                ws_qs = _dot(wq_r[k], s)
                v_new = u_r[k] - ws_qs[:CHUNK]
                o_ref[rs, :] = ws_qs[CHUNK:] + _dot(at_r[k], _block_diag(v_new.astype(BF16)))
                state[ln] = s * et_r[k][0:1] + jnp.where(bd_mask, _dot(kd_r[k], v_new, TN), 0.0)

        for ln, (reverse, qkv_ref, bl_ref, _, _, _) in enumerate(lanes):
            incl = (row <= lane) if reverse else (row >= lane)
            tri = jnp.where(_tri(reverse), 1.0, 0.0).astype(BF16)
            t_before_j = (row >= lane) if reverse else (row <= lane)
            last = 0 if reverse else CHUNK - 1
            cb, ca = ln * N_HEADS_REC, 2 * N_HEADS_REC + ln * N_HEADS_REC
            for m in range(n_chunks):
                c = (n_chunks - 1 - m) if reverse else m
                rs = slice(c * CHUNK, (c + 1) * CHUNK)
                bl = bl_ref[rs, :]
                d = dict(q=qkv_ref[rs, :D_REC], k=qkv_ref[rs, D_REC:2 * D_REC], v=qkv_ref[rs, 2 * D_REC:],
                         strict=(row < lane) if reverse else (row > lane), slot=ln * n_chunks + m)
                d["beta"] = _pick_heads([bl[:, cb + h:cb + h + 1] for h in range(N_HEADS_REC)])
                la = _pick_heads([bl[:, ca + h:ca + h + 1] for h in range(N_HEADS_REC)])
                g_i, g_j = None, None
                for p_i, p_j in zip(_split3(la), _split3(jnp.where(t_before_j, la, 0.0))):
                    a = jnp.dot(tri, p_i, preferred_element_type=F32)
                    b = jnp.dot(ones8, p_j, preferred_element_type=F32)
                    g_i = a if g_i is None else g_i + a
                    g_j = b if g_j is None else g_j + b
                d["g_i"] = g_i
                d["g_tot"] = g_i[last:last + 1]
                d["decay"] = jnp.where(incl, jnp.exp(jnp.where(incl, g_i - g_j[0:1], 0.0)), 0.0)
                inst.append(d)
        chain(0)

        for d in inst:
            gram = _dot(jnp.concatenate([d["k"], d["q"]], axis=0), _block_diag(d["k"].astype(BF16)), NT)
            d["low"] = jnp.where(d["strict"], d["beta"] * gram[:CHUNK] * d["decay"], 0.0)
            d["attn"] = gram[CHUNK:] * d["decay"]

        for d in inst:
            d["t"] = eye - d["low"]
            (d["p"],) = _mm3_heads([d["low"]], d["low"])
        for it in range(4):
            for d in inst:
                p_next, tp = _mm3_heads([d["p"], d["t"]], d["p"])
                d["t"] = d["t"] + tp
                d["p"] = p_next
            if 1 + it < n_chunks:
                chain(1 + it)
        for d in inst:
            k = d["slot"]
            t_mat = d["t"] + _mm3_heads([d["t"]], d["p"])[0]
            eg = jnp.exp(d["g_i"])
            vb = d["v"] * d["beta"]
            kbg = d["k"] * d["beta"] * eg
            u_w[k] = _dot(t_mat, _block_diag(vb.astype(BF16)))
            wq_w[k] = jnp.concatenate([_dot(t_mat, _block_diag(kbg.astype(BF16))), d["q"] * eg], axis=0).astype(BF16)
            at_w[k] = d["attn"]
            kd_w[k] = (d["k"] * jnp.exp(d["g_tot"] - d["g_i"])).astype(BF16)
            et_w[k] = jnp.broadcast_to(jnp.exp(d["g_tot"]), (8, D_REC))

        s_scr[0] = state[0]
        s_scr[1] = state[1]

    @pl.when(n % 2 == 0)
    def _():
        step(sets[:half], sets[half:])

    @pl.when(n % 2 == 1)
    def _():
        step(sets[half:], sets[:half])

    for ln, lane_refs in enumerate(lanes):
        @pl.when(jnp.logical_and(chain_live, sch_ref[4 * ln + 3, prev] == 1))
        def _(ln=ln, sfin_ref=lane_refs[5]):
            sfin_ref[0] = _collapse_block_diag(s_scr[ln])


def _gdn_scan_bidir(qkv, bl, s0_f, s0_b, sched):
    n = qkv.shape[0]
    n_seq = s0_f.shape[0]
    n_steps = sched.shape[1]
    n_inst = 2 * (TS // CHUNK)
    ahead = lambda i: jnp.minimum(i, n_steps - 1)
    behind = lambda i: jnp.maximum(i - 1, 0)
    scratch_set = [pltpu.VMEM((n_inst, CHUNK, D_REC), F32), pltpu.VMEM((n_inst, 2 * CHUNK, D_REC), BF16),
                   pltpu.VMEM((n_inst, CHUNK, D_REC), F32), pltpu.VMEM((n_inst, CHUNK, D_REC), BF16),
                   pltpu.VMEM((n_inst, 8, D_REC), F32)]

    def tile_in(width, ln):
        return pl.BlockSpec((TS, width), lambda i, sch: (sch[4 * ln, ahead(i)], 0))

    def tile_out(ln):
        return pl.BlockSpec((TS, D_REC), lambda i, sch: (sch[4 * ln, behind(i)], 0))

    def state_spec(ln):
        return pl.BlockSpec((1, HEAD_DIM, D_REC), lambda i, sch: (sch[4 * ln + 1, behind(i)], 0, 0))

    grid_spec = pltpu.PrefetchScalarGridSpec(
        num_scalar_prefetch=1,
        grid=(n_steps + 1,),
        in_specs=[tile_in(3 * D_REC, 0), tile_in(128, 0), tile_in(3 * D_REC, 1), tile_in(128, 1),
                  state_spec(0), state_spec(1)],
        out_specs=[tile_out(0), state_spec(0), tile_out(1), state_spec(1)],
        scratch_shapes=[pltpu.VMEM((2, D_REC, D_REC), F32)] + scratch_set + scratch_set,
    )
    o_sds = jax.ShapeDtypeStruct((n, D_REC), F32)
    s_sds = jax.ShapeDtypeStruct((n_seq, HEAD_DIM, D_REC), F32)
    return pl.pallas_call(
        _gdn_bidir_kernel,
        out_shape=[o_sds, s_sds, o_sds, s_sds],
        grid_spec=grid_spec,
        compiler_params=pltpu.CompilerParams(dimension_semantics=("arbitrary",), vmem_limit_bytes=VMEM_LIMIT),
        name="gdn_scan",
    )(sched, qkv, bl, qkv, bl, s0_f, s0_b)


def _hgrn_bidir_kernel(sch_ref, ff_ref, if_ref, qf_ref, fb_ref, ib_ref, qb_ref, lb_ref, sel_ref, s0f_ref, s0b_ref,
                       of_ref, sff_ref, ob_ref, sfb_ref, s_scr):
    n = pl.program_id(0)
    n_chunks = TS // CHUNK
    lanes = ((False, ff_ref, if_ref, qf_ref, s0f_ref, of_ref, sff_ref), (True, fb_ref, ib_ref, qb_ref, s0b_ref, ob_ref, sfb_ref))

    for ln, lane_refs in enumerate(lanes):
        @pl.when(sch_ref[4 * ln + 2, n] == 1)
        def _(ln=ln, s0_ref=lane_refs[4]):
            s_scr[ln] = _block_diag(s0_ref[0])

    row = _iota((CHUNK, D_REC), 0)
    col_tok = _iota((CHUNK, D_REC), 1) % HEAD_DIM
    ones = _head_ones()
    bd_mask = _block_diag_mask()

    inst = []
    for ln, (reverse, f_ref, i_ref, q_ref, _, o_ref, _) in enumerate(lanes):
        lb = lb_ref[ln:ln + 1, :]
        sel = sel_ref[ln]
        for m in range(n_chunks):
            c = (n_chunks - 1 - m) if reverse else m
            rs = slice(c * CHUNK, (c + 1) * CHUNK)
            f = lb + (1.0 - lb) * _sigmoid(f_ref[rs, :])
            d = dict(rs=rs, kk=1.0 - f, q=q_ref[rs, :], v=i_ref[rs, :], reverse=reverse, ln=ln, o_ref=o_ref)
            cums = _dot_sel(sel, jnp.log(jnp.maximum(f, TINY)))
            d["b"] = cums[:CHUNK]
            d["b_mid_small"] = [cums[CHUNK:2 * CHUNK], cums[2 * CHUNK:]]
            inst.append(d)

    for lvl in range(N_LEVELS):
        s_half = (CHUNK // 2) >> lvl
        in_block = row % (2 * s_half)
        same_block = (row // (2 * s_half)) == (col_tok // (2 * s_half))
        for d in inst:
            b, reverse = d["b"], d["reverse"]
            later = (in_block < s_half) if reverse else (in_block >= s_half)
            if s_half >= 4:
                pieces = []
                for blk in range(CHUNK // (2 * s_half)):
                    r = blk * 2 * s_half + (s_half if reverse else s_half - 1)
                    pieces.append(jnp.broadcast_to(b[r:r + 1], (2 * s_half, D_REC)))
                b_mid = pieces[0] if len(pieces) == 1 else jnp.concatenate(pieces, axis=0)
            else:
                b_mid = d["b_mid_small"][lvl - (N_LEVELS - 2)]
            e = jnp.exp(-jnp.abs(b - b_mid))
            q_hat = jnp.where(later, d["q"] * e, 0.0)
            k_hat = jnp.where(later, 0.0, d["kk"] * e)
            res = _dot(q_hat, _block_diag(k_hat.astype(BF16)), NT)
            d["a"] = res if lvl == 0 else d["a"] + jnp.where(same_block, res, 0.0)

    for d in inst:
        sums = _dot(d["q"] * d["kk"], ones)
        d["a"] = d["a"] + jnp.where(col_tok == row, sums, 0.0)

    for d in inst:
        b = d["b"]
        last = 0 if d["reverse"] else CHUNK - 1
        b_last = b[last:last + 1]
        d["av"] = _dot(d["a"], _block_diag(d["v"].astype(BF16)))
        d["kv"] = jnp.where(bd_mask, _dot(d["v"], d["kk"] * jnp.exp(b_last - b), TN), 0.0)
        d["qb"] = (d["q"] * jnp.exp(b)).astype(BF16)
        d["e_last"] = jnp.exp(b_last)

    state = [s_scr[0], s_scr[1]]
    for d in inst:
        s = state[d["ln"]]
        d["o_ref"][d["rs"], :] = d["av"] + _dot(d["qb"], s, NT)
        state[d["ln"]] = s * d["e_last"] + d["kv"]
    s_scr[0] = state[0]
    s_scr[1] = state[1]

    for ln, lane_refs in enumerate(lanes):
        @pl.when(sch_ref[4 * ln + 3, n] == 1)
        def _(ln=ln, sfin_ref=lane_refs[6]):
            sfin_ref[0] = _collapse_block_diag(s_scr[ln])


def _hgrn_scan_bidir(hg, lb_e, s0_f, s0_b, sched):
    n = hg.shape[0]
    n_seq = s0_f.shape[0]
    i = np.arange(CHUNK)
    sels = []
    for reverse in (False, True):
        tri = (i[None, :] >= i[:, None]) if reverse else (i[None, :] <= i[:, None])
        rows = [tri]
        for s_half in (2, 1):
            mid = (i // (2 * s_half)) * (2 * s_half) + (s_half if reverse else s_half - 1)
            rows.append(tri[mid])
        sels.append(np.concatenate(rows, axis=0))
    sel = jnp.asarray(np.stack(sels).astype(np.float32), BF16)

    def col(c, ln):
        return pl.BlockSpec((TS, D_REC), lambda i, sch: (sch[4 * ln, i], c))

    def state_spec(ln):
        return pl.BlockSpec((1, HEAD_DIM, D_REC), lambda i, sch: (sch[4 * ln + 1, i], 0, 0))

    def tile_out(ln):
        return pl.BlockSpec((TS, D_REC), lambda i, sch: (sch[4 * ln, i], 0))

    grid_spec = pltpu.PrefetchScalarGridSpec(
        num_scalar_prefetch=1,
        grid=(sched.shape[1],),
        in_specs=[col(0, 0), col(2, 0), col(3, 0), col(1, 1), col(2, 1), col(3, 1),
                  pl.BlockSpec((2, D_REC), lambda i, sch: (0, 0)),
                  pl.BlockSpec((2, 3 * CHUNK, CHUNK), lambda i, sch: (0, 0, 0)),
                  state_spec(0), state_spec(1)],
        out_specs=[tile_out(0), state_spec(0), tile_out(1), state_spec(1)],
        scratch_shapes=[pltpu.VMEM((2, D_REC, D_REC), F32)],
    )
    o_sds = jax.ShapeDtypeStruct((n, D_REC), F32)
    s_sds = jax.ShapeDtypeStruct((n_seq, HEAD_DIM, D_REC), F32)
    return pl.pallas_call(
        _hgrn_bidir_kernel,
        out_shape=[o_sds, s_sds, o_sds, s_sds],
        grid_spec=grid_spec,
        compiler_params=pltpu.CompilerParams(dimension_semantics=("arbitrary",), vmem_limit_bytes=VMEM_LIMIT),
        name="hgrn_scan",
    )(sched, hg, hg, hg, hg, hg, hg, lb_e, sel, s0_f, s0_b)


def _gated_readout(o, gate, w_e, ones):
    ms = _dot_sel(ones, o * o, sel_is_lhs=False) * (1.0 / HEAD_DIM)
    return o * lax.rsqrt(ms + NORM_EPS) * w_e * _silu(gate)


def _out_proj_kernel(*refs, alpha, n_ctx_tiles, n_x):
    x_refs = refs[:n_x]
    (mod_ref, oa_ref, bf_ref, bb_ref, cf_ref, cb_ref, gb_ref, gc_ref, nwb_ref, nwc_ref,
     w_ref, lng_ref, lnb_ref, y_ref) = refs[n_x:]
    x = _read_tokens(x_refs, n_ctx_tiles)
    m = mod_ref[0]
    ones = _head_ones()
    o_b = _gated_readout(bf_ref[...] + bb_ref[...], gb_ref[...], nwb_ref[...], ones)
    o_c = _gated_readout(cf_ref[...] + cb_ref[...], gc_ref[...], nwc_ref[...], ones)
    mix = jnp.dot(oa_ref[...].astype(BF16), w_ref[0:D_A], preferred_element_type=F32)
    mix = mix + jnp.dot(o_b.astype(BF16), w_ref[D_A:D_A + D_REC], preferred_element_type=F32)
    mix = mix + jnp.dot(o_c.astype(BF16), w_ref[D_A + D_REC:], preferred_element_type=F32)
    y_ref[...] = _layer_norm(alpha * x + m[2:3] * mix, lng_ref[...], lnb_ref[...])


def _out_proj(x_parts, mod_l, o_a, ob_f, ob_b, oc_f, oc_b, g_b, hg, nw_b, nw_c, w_out, ln_g, ln_b, mod_index, alpha,
              n_ctx_tiles):
    n = sum(p.shape[0] for p in x_parts)
    tile = lambda w: pl.BlockSpec((TM, w), lambda i: (i, 0))
    row = lambda w: pl.BlockSpec((1, w), lambda i: (0, 0))
    return pl.pallas_call(
        functools.partial(_out_proj_kernel, alpha=alpha, n_ctx_tiles=n_ctx_tiles, n_x=len(x_parts)),
        out_shape=jax.ShapeDtypeStruct((n, D_MODEL), F32),
        grid=(n // TM,),
        in_specs=_token_specs(x_parts, n_ctx_tiles) + [
                  pl.BlockSpec((1, 6, D_MODEL), lambda i: (mod_index(i), 0, 0)),
                  tile(D_A), tile(D_REC), tile(D_REC), tile(D_REC), tile(D_REC), tile(D_REC),
                  pl.BlockSpec((TM, D_REC), lambda i: (i, 4)),
                  row(D_REC), row(D_REC),
                  pl.BlockSpec((D_MODEL, D_MODEL), lambda i: (0, 0)),
                  row(D_MODEL), row(D_MODEL)],
        out_specs=tile(D_MODEL),
        compiler_params=pltpu.CompilerParams(dimension_semantics=("arbitrary",), vmem_limit_bytes=VMEM_LIMIT),
        name="out_proj_ln",
    )(*x_parts, mod_l, o_a, ob_f, ob_b, oc_f, oc_b, g_b, hg, nw_b, nw_c, w_out, ln_g, ln_b)


def _ffn_kernel(x_ref, mod_ref, w1_ref, w3_ref, w2_ref, lng_ref, lnb_ref, y_ref, *, alpha):
    m = mod_ref[0]
    x = x_ref[...]
    h = (x * (1.0 + m[4:5]) + m[3:4]).astype(BF16)
    acc = jnp.zeros((x.shape[0], D_MODEL), F32)
    for c in range(D_FF // FF_CHUNK):
        sl = slice(c * FF_CHUNK, (c + 1) * FF_CHUNK)
        a = jnp.dot(h, w1_ref[:, sl], preferred_element_type=F32)
        g = jnp.dot(h, w3_ref[:, sl], preferred_element_type=F32)
        acc = acc + jnp.dot((_silu(a) * g).astype(BF16), w2_ref[sl, :], preferred_element_type=F32)
    y_ref[...] = _layer_norm(alpha * x + m[5:6] * acc, lng_ref[...], lnb_ref[...])


def _ffn(x, mod_l, w1, w3, w2, ln_g, ln_b, mod_index_ffn, alpha):
    n = x.shape[0]
    whole = lambda a: pl.BlockSpec(a.shape, lambda i: (0, 0), pipeline_mode=pl.Buffered(1))
    row = pl.BlockSpec((1, D_MODEL), lambda i: (0, 0))
    return pl.pallas_call(
        functools.partial(_ffn_kernel, alpha=alpha),
        out_shape=jax.ShapeDtypeStruct((n, D_MODEL), F32),
        grid=(n // TM_FFN,),
        in_specs=[pl.BlockSpec((TM_FFN, D_MODEL), lambda i: (i, 0)),
                  pl.BlockSpec((1, 6, D_MODEL), lambda i: (mod_index_ffn(i), 0, 0)),
                  whole(w1), whole(w3), whole(w2), row, row],
        out_specs=pl.BlockSpec((TM_FFN, D_MODEL), lambda i: (i, 0)),
        compiler_params=pltpu.CompilerParams(dimension_semantics=("arbitrary",), vmem_limit_bytes=VMEM_LIMIT),
        name="ffn_ln",
    )(x, mod_l, w1, w3, w2, ln_g, ln_b)


def _moe_routed_kernel(x_ref, mod_ref, wr_ref, br_ref, w1_ref, w3_ref, w2_ref, lng_ref, lnb_ref, *rest,
                       alpha, split_ctx_tiles):
    if split_ctx_tiles is None:
        y_ref, h_scr, mem_scr, gate_scr, rank_scr, acc_scr, earlier_scr = rest
    else:
        y_ref, y2_ref, h_scr, mem_scr, gate_scr, rank_scr, acc_scr, earlier_scr = rest
    _moe_routed_body(x_ref, mod_ref, wr_ref, br_ref, w1_ref, w3_ref, w2_ref, lng_ref, lnb_ref, y_ref,
                     None if split_ctx_tiles is None else y2_ref, h_scr, mem_scr, gate_scr, rank_scr, acc_scr,
                     earlier_scr, alpha=alpha, split_ctx_tiles=split_ctx_tiles)


def _moe_routed_body(x_ref, mod_ref, wr_ref, br_ref, w1_ref, w3_ref, w2_ref, lng_ref, lnb_ref, y_ref, y2_ref,
                     h_scr, mem_scr, gate_scr, rank_scr, acc_scr, earlier_scr, *, alpha, split_ctx_tiles):
    e = pl.program_id(1)
    m = mod_ref[0]
    t_tile = x_ref.shape[0]

    @pl.when(jnp.logical_and(pl.program_id(0) == 0, e == 0))
    def _():
        earlier_scr[...] = jnp.where(_iota((t_tile, t_tile), 0) < _iota((t_tile, t_tile), 1), 1.0, 0.0).astype(BF16)

    @pl.when(e == 0)
    def _():
        h = x_ref[...] * (1.0 + m[4:5]) + m[3:4]
        h_hi, h_lo = _split2(h)
        h_scr[...] = h_hi
        w_hi, w_lo = _split2(wr_ref[...])
        both = lax.dot_general(jnp.concatenate([w_hi, w_lo], axis=0), h_hi, NT, preferred_element_type=F32)
        lt = (both[:N_EXPERTS] + both[N_EXPERTS:] + lax.dot_general(w_hi, h_lo, NT, preferred_element_type=F32)
              + br_ref[...])
        eidx = _iota(lt.shape, 0)
        m1 = jnp.max(lt, 0, keepdims=True)
        i1 = jnp.min(jnp.where(lt == m1, eidx, N_EXPERTS), 0, keepdims=True)
        rest = jnp.where(eidx == i1, NEG_BIG, lt)
        m2 = jnp.max(rest, 0, keepdims=True)
        i2 = jnp.min(jnp.where(rest == m2, eidx, N_EXPERTS), 0, keepdims=True)
        e2 = jnp.exp(m2 - m1)
        g1 = 1.0 / (1.0 + e2)
        mem = jnp.where(eidx == i1, 1.0, jnp.where(eidx == i2, 1.0, 0.0))
        mem_scr[...] = mem
        gate_scr[...] = jnp.where(eidx == i1, g1, jnp.where(eidx == i2, e2 * g1, 0.0))
        rank_scr[...] = jnp.dot(mem.astype(BF16), earlier_scr[...], preferred_element_type=F32)
        acc_scr[...] = jnp.zeros_like(acc_scr)

    mem_row = mem_scr[pl.ds(e, 1), :]
    gate_row = gate_scr[pl.ds(e, 1), :]
    rank_row = rank_scr[pl.ds(e, 1), :]
    count = jnp.sum(mem_row).astype(jnp.int32)
    slot0 = _iota((MOE_BLOCK, t_tile), 0).astype(F32)

    def block(j, carry):
        slot = slot0 + (j * MOE_BLOCK).astype(F32)
        hit = jnp.where(rank_row == slot, mem_row, 0.0)
        sel = hit.astype(BF16)
        hb = jnp.dot(sel, h_scr[...], preferred_element_type=F32).astype(BF16)
        a = jnp.dot(hb, w1_ref[0], preferred_element_type=F32)
        g = jnp.dot(hb, w3_ref[0], preferred_element_type=F32)
        yb = jnp.dot((_silu(a) * g).astype(BF16), w2_ref[0], preferred_element_type=F32)
        gate_slot = jnp.sum(hit * gate_row, -1, keepdims=True)
        acc_scr[...] += lax.dot_general(sel, (yb * gate_slot).astype(BF16), TN, preferred_element_type=F32)
        return carry

    lax.fori_loop(0, (count + MOE_BLOCK - 1) // MOE_BLOCK, block, 0)

    @pl.when(e == N_EXPERTS - 1)
    def _():
        y = _layer_norm(alpha * x_ref[...] + m[5:6] * acc_scr[...], lng_ref[...], lnb_ref[...])
        if split_ctx_tiles is None:
            y_ref[...] = y
        else:
            @pl.when(pl.program_id(0) < split_ctx_tiles)
            def _():
                y_ref[...] = y

            @pl.when(pl.program_id(0) >= split_ctx_tiles)
            def _():
                y2_ref[...] = y


def _moe(x, mod_l, w_r, b_r, w1, w3, w2, ln_g, ln_b, mod_index_moe, alpha, split_ctx_tiles=None):
    n = x.shape[0]
    row = pl.BlockSpec((1, D_MODEL), lambda i, e: (0, 0))
    expert = lambda a: pl.BlockSpec((1,) + a.shape[1:], lambda i, e: (e, 0, 0))
    routing = pltpu.VMEM((N_EXPERTS, TM_MOE), F32)
    if split_ctx_tiles is None:
        out_shape = jax.ShapeDtypeStruct((n, D_MODEL), F32)
        out_specs = pl.BlockSpec((TM_MOE, D_MODEL), lambda i, e: (i, 0))
    else:
        n_ctx_rows = split_ctx_tiles * TM_MOE
        out_shape = [jax.ShapeDtypeStruct((n_ctx_rows, D_MODEL), F32), jax.ShapeDtypeStruct((n - n_ctx_rows, D_MODEL), F32)]
        out_specs = [pl.BlockSpec((TM_MOE, D_MODEL), lambda i, e: (jnp.minimum(i, split_ctx_tiles - 1), 0)),
                     pl.BlockSpec((TM_MOE, D_MODEL), lambda i, e: (jnp.maximum(i - split_ctx_tiles, 0), 0))]
    return pl.pallas_call(
        functools.partial(_moe_routed_kernel, alpha=alpha, split_ctx_tiles=split_ctx_tiles),
        out_shape=out_shape,
        grid=(n // TM_MOE, N_EXPERTS),
        in_specs=[pl.BlockSpec((TM_MOE, D_MODEL), lambda i, e: (i, 0)),
                  pl.BlockSpec((1, 6, D_MODEL), lambda i, e: (mod_index_moe(i), 0, 0)),
                  pl.BlockSpec((N_EXPERTS, D_MODEL), lambda i, e: (0, 0)),
                  pl.BlockSpec((N_EXPERTS, 1), lambda i, e: (0, 0)),
                  expert(w1), expert(w3), expert(w2), row, row],
        out_specs=out_specs,
        scratch_shapes=[pltpu.VMEM((TM_MOE, D_MODEL), BF16), routing, routing, routing,
                        pltpu.VMEM((TM_MOE, D_MODEL), F32), pltpu.VMEM((TM_MOE, TM_MOE), BF16)],
        compiler_params=pltpu.CompilerParams(dimension_semantics=("arbitrary", "arbitrary"),
                                             vmem_limit_bytes=VMEM_LIMIT),
        name="moe_ln",
    )(x, mod_l, w_r, b_r, w1, w3, w2, ln_g, ln_b)


def _rope_tables(lat_len):
    t = jnp.arange(lat_len)
    pos = jnp.stack([t // GRID_W, t % GRID_W], axis=1).astype(F32)
    half, quarter = HEAD_DIM // 2, HEAD_DIM // 4
    inv_freq = ROPE_THETA ** (-jnp.arange(quarter, dtype=F32) * 2.0 / half)
    lane = np.arange(HEAD_DIM)
    ang = pos[:, lane // half] * inv_freq[lane % quarter][None, :]
    sign = jnp.asarray(np.where((lane % half) < quarter, -1.0, 1.0), F32)
    cos = jnp.tile(jnp.cos(ang), (1, 128 // HEAD_DIM))
    sin = jnp.tile(jnp.sin(ang) * sign, (1, 128 // HEAD_DIM))
    return cos, sin


def kernel(x_prompt, x_sample, cache_k, cache_v, state_gdn, state_hgrn, c, c_ctx, w_mod, b_mod, w_in, conv_w, attn_sink, gdn_a_log, gdn_dt_bias, gdn_norm_w, hgrn_lb, hgrn_norm_w, w_out, ln_g, ln_b, ffn_w1, ffn_w3, ffn_w2, moe_router, moe_router_b, moe_w1, moe_w3, moe_w2):
    depth = w_in.shape[0]
    n_ctx, ctx_len, _ = x_prompt.shape
    lat_batch, lat_len, _ = x_sample.shape
    n_ctx_rows = n_ctx * ctx_len
    alpha = (2.0 * depth) ** 0.25

    x_parts = [x_prompt.reshape(-1, D_MODEL), x_sample.reshape(-1, D_MODEL)]

    def mod_index_for(tile):
        ctx_tiles, per_seq = n_ctx_rows // tile, lat_len // tile
        return lambda i: jnp.where(i < ctx_tiles, 0, 1 + (i - ctx_tiles) // per_seq)

    cvecs = jnp.concatenate([c_ctx[None], c], axis=0)
    mods = _modulation(cvecs, w_mod, b_mod).reshape(depth, cvecs.shape[0], 6, D_MODEL)

    cos_tab, sin_tab = _rope_tables(lat_len)
    sm = jax.nn.softmax(hgrn_lb.astype(F32), axis=0)
    lower = jnp.cumsum(sm, axis=0) - sm[0:1]

    w_out_b = w_out.astype(BF16)

    zeros_state = jnp.zeros((n_ctx, HEAD_DIM, D_REC), F32)
    sched = [_scan_schedule(n_ctx, ctx_len, lat_batch, lat_len, reverse) for reverse in (False, True)]
    new_k, new_v, new_sg, new_sh = [], [], [], []

    for l in range(depth):
        alog_row = jnp.zeros((1, 128), F32).at[0, 8:16].set(gdn_a_log[l].reshape(-1))
        dtb_row = jnp.zeros((1, 128), F32).at[0, 8:16].set(gdn_dt_bias[l].reshape(-1))
        qa, ka, va, qkv, g_b, hg, bl = _in_proj(x_parts, mods[l], w_in[l], cos_tab, sin_tab, conv_w[l], alog_row,
                                                dtb_row, mod_index_for(TM), n_ctx_rows // TM, ctx_len, lat_len)
        new_k.append(ka[:n_ctx_rows].reshape(n_ctx, ctx_len, A_KV_HEADS, HEAD_DIM))
        new_v.append(va[:n_ctx_rows].reshape(n_ctx, ctx_len, A_KV_HEADS, HEAD_DIM))

        o_a = _attention(qa, ka, va, cache_k[:, l].reshape(lat_batch, -1, D_KV), cache_v[:, l].reshape(lat_batch, -1, D_KV),
                         attn_sink[l], n_ctx, ctx_len, lat_batch, lat_len)

        sg0 = state_gdn[:, l].transpose(0, 1, 3, 2, 4).reshape(lat_batch, 2, HEAD_DIM, D_REC)
        sh0 = state_hgrn[:, l].transpose(0, 1, 4, 2, 3).reshape(lat_batch, 2, HEAD_DIM, D_REC)
        o_f, s_f, o_b, s_b = _gdn_scan_bidir(qkv, bl, jnp.concatenate([zeros_state, sg0[:, 0]], axis=0),
                                             jnp.concatenate([zeros_state, sg0[:, 1]], axis=0),
                                             jnp.concatenate(sched, axis=0))
        ob = [o_f, o_b]
        sg_fin = [s[:n_ctx].reshape(n_ctx, HEAD_DIM, N_HEADS_REC, HEAD_DIM).transpose(0, 2, 1, 3) for s in (s_f, s_b)]
        o_f, s_f, o_b, s_b = _hgrn_scan_bidir(hg, lower[l], jnp.concatenate([zeros_state, sh0[:, 0]], axis=0),
                                              jnp.concatenate([zeros_state, sh0[:, 1]], axis=0),
                                              jnp.concatenate(sched, axis=0))
        oc = [o_f, o_b]
        sh_fin = [s[:n_ctx].reshape(n_ctx, HEAD_DIM, N_HEADS_REC, HEAD_DIM).transpose(0, 2, 3, 1) for s in (s_f, s_b)]
        new_sg.append(jnp.stack(sg_fin, axis=1))
        new_sh.append(jnp.stack(sh_fin, axis=1))

        nw_b = jnp.tile(gdn_norm_w[l], N_HEADS_REC)[None]
        nw_c = jnp.tile(hgrn_norm_w[l], N_HEADS_REC)[None]
        x = _out_proj(x_parts, mods[l], o_a, ob[0], ob[1], oc[0], oc[1], g_b, hg, nw_b, nw_c, w_out_b[l],
                      ln_g[l, 0][None], ln_b[l, 0][None], mod_index_for(TM), alpha, n_ctx_rows // TM)

        i = l // 2
        if l % 2 == 0:
            x = _ffn(x, mods[l], ffn_w1[i].astype(BF16), ffn_w3[i].astype(BF16), ffn_w2[i].astype(BF16),
                     ln_g[l, 1][None], ln_b[l, 1][None], mod_index_for(TM_FFN), alpha)
        else:
            split = n_ctx_rows // TM_MOE if l == depth - 1 else None
            x = _moe(x, mods[l], moe_router[i].T, moe_router_b[i][:, None],
                     moe_w1[i].astype(BF16), moe_w3[i].astype(BF16), moe_w2[i].astype(BF16),
                     ln_g[l, 1][None], ln_b[l, 1][None], mod_index_for(TM_MOE), alpha, split)
        x_parts = [x]

    if isinstance(x, (list, tuple)):
        y_prompt, y_sample = x[0].reshape(x_prompt.shape), x[1].reshape(x_sample.shape)
    else:
        y_prompt = x[:n_ctx_rows].reshape(x_prompt.shape)
        y_sample = x[n_ctx_rows:].reshape(x_sample.shape)
    return (y_prompt, y_sample, jnp.stack(new_k, axis=1), jnp.stack(new_v, axis=1),
            jnp.stack(new_sg, axis=1), jnp.stack(new_sh, axis=1))
```
